```python
import math
import jax, jax.numpy as jnp
from jax import lax
import numpy as np

D_MODEL = 1024
BATCH = 8
SEQ = 4096
DEPTH = 1
DEC_BATCH = 8
DEC_SEQ = 16
PAST_LEN = 4096

CHUNK = 64
QBLOCK = 128
DA_HEADS = 8
DA_HEAD_DIM = 32
DA_V_DIM = 2 * DA_HEAD_DIM
DA_QK_WIDTH = DA_HEADS * 2 * DA_HEAD_DIM
DA_WIDTH = DA_HEADS * DA_V_DIM
HG_HEADS = 4
HG_KEY_DIM = 128
HG_V_DIM = 128
HG_K_WIDTH = HG_HEADS * HG_KEY_DIM
HG_WIDTH = HG_HEADS * HG_V_DIM
MIX_WIDTH = DA_WIDTH + HG_WIDTH
IN_WIDTH = 2 * DA_QK_WIDTH + DA_WIDTH + 2 * HG_K_WIDTH + 2 * HG_WIDTH
REL_BUCKETS = 32
REL_MAX_DIST = 128
PEER_HEADS = 8
PEER_KEYS = 128
PEER_EXPERTS = PEER_KEYS * PEER_KEYS
PEER_KEY_DIM = 128
PEER_TOPK = 16
PEER_BLOCK = 128
EPS = 1e-6
NEG = -1e30

kernel_name = 'hybrid_diffattn_hgrn2_peer_stream_step'


def rmsnorm(x, g):
    xf = x.astype(jnp.float32)
    y = xf * lax.rsqrt(jnp.mean(xf * xf, axis=-1, keepdims=True) + EPS)
    return (y * g.astype(jnp.float32)).astype(x.dtype)


def split_proj(z):
    sizes = (DA_QK_WIDTH, DA_QK_WIDTH, DA_WIDTH, HG_K_WIDTH, HG_K_WIDTH, HG_WIDTH, HG_WIDTH)
    offs, c = [], 0
    for s in sizes[:-1]:
        c += s
        offs.append(c)
    return jnp.split(z, offs, axis=-1)


def rel_bucket(rel):
    nb = REL_BUCKETS // 2
    max_exact = nb // 2
    ret = jnp.where(rel > 0, nb, 0)
    n = jnp.abs(rel)
    nf = jnp.maximum(n, 1).astype(jnp.float32)
    large = max_exact + (jnp.log(nf / max_exact) / math.log(REL_MAX_DIST / max_exact)
                         * (nb - max_exact)).astype(jnp.int32)
    large = jnp.minimum(large, nb - 1)
    return ret + jnp.where(n < max_exact, n, large)


def diff_attn(q, k, v, q_pos, k_pos, rel_bias, lam):
    scale = DA_HEAD_DIM ** -0.5
    bias = jnp.transpose(rel_bias[rel_bucket(k_pos[None, :] - q_pos[:, None])], (2, 0, 1)).astype(jnp.float32)
    mask = (k_pos[None, :] // CHUNK) <= (q_pos[:, None] // CHUNK)

    def probs(qc, kc):
        s = jnp.einsum('bqhd,bkhd->bhqk', qc, kc).astype(jnp.float32) * scale + bias
        return jax.nn.softmax(jnp.where(mask, s, NEG), axis=-1)

    a = probs(q[..., :DA_HEAD_DIM], k[..., :DA_HEAD_DIM]) - lam * probs(q[..., DA_HEAD_DIM:], k[..., DA_HEAD_DIM:])
    return jnp.einsum('bhqk,bkhd->bqhd', a.astype(v.dtype), v)


def hgrn2_chunk(s0, q, k, v, logf):
    L = q.shape[1]
    b = jnp.cumsum(logf, axis=1)
    causal = jnp.tril(jnp.ones((L, L), bool))[None, :, :, None, None]
    decay = jnp.exp(jnp.where(causal, b[:, :, None] - b[:, None, :], -jnp.inf))
    attn = jnp.einsum('bthd,btshd,bshd->bhts', q, decay, k)
    o = jnp.einsum('bhts,bshe->bthe', attn, v) + jnp.einsum('bthd,bhde->bthe', q * jnp.exp(b), s0)
    b_last = b[:, -1]
    s_new = jnp.exp(b_last)[..., None] * s0 + jnp.einsum('bshd,bshe->bhde', k * jnp.exp(b_last[:, None] - b), v)
    return s_new, o


def hgrn2_scan(q, k, v, logf):
    B, L = q.shape[:2]
    n = L // CHUNK

    def chunks(t):
        return t.reshape(B, n, CHUNK, *t.shape[2:]).swapaxes(0, 1)

    s0 = jnp.zeros((B, HG_HEADS, HG_KEY_DIM, HG_V_DIM), jnp.float32)
    s_fin, o = lax.scan(lambda s, xs: hgrn2_chunk(s, *xs), s0,
                        (chunks(q), chunks(k), chunks(v), chunks(logf)))
    return s_fin, o.swapaxes(0, 1).reshape(B, L, HG_HEADS, HG_V_DIM)


def token_mixer(hn, past_k, past_v, s0, w_in, w_out, lam, lam_init, da_norm, lb, hg_norm, rel_bias):
    B, L, _ = hn.shape
    z = jnp.einsum('bld,de->ble', hn, w_in)
    q, k, v, hq, hf, hi, hg = split_proj(z)
    q = q.reshape(B, L, DA_HEADS, 2 * DA_HEAD_DIM)
    k = k.reshape(B, L, DA_HEADS, 2 * DA_HEAD_DIM)
    v = v.reshape(B, L, DA_HEADS, DA_V_DIM)
    if past_k is None:
        pos = jnp.arange(L)
        nb = L // QBLOCK
        q_blocks = q.reshape(B, nb, QBLOCK, DA_HEADS, 2 * DA_HEAD_DIM).swapaxes(0, 1)

        def one_block(args):
            qb, pb = args
            return diff_attn(qb, k, v, pb, pos, rel_bias, lam)

        o_a = lax.map(one_block, (q_blocks, pos.reshape(nb, QBLOCK)))
        o_a = o_a.swapaxes(0, 1).reshape(B, L, DA_HEADS, DA_V_DIM)
    else:
        q_pos = PAST_LEN + jnp.arange(L)
        k_all = jnp.concatenate([past_k.astype(k.dtype), k], axis=1)
        v_all = jnp.concatenate([past_v.astype(v.dtype), v], axis=1)
        o_a = diff_attn(q, k_all, v_all, q_pos, jnp.arange(PAST_LEN + L), rel_bias, lam)
    o_a = (rmsnorm(o_a, da_norm) * (1.0 - lam_init)).reshape(B, L, DA_WIDTH).astype(hn.dtype)
    f = lb + (1.0 - lb) * jax.nn.sigmoid(hf.astype(jnp.float32))
    shp = (B, L, HG_HEADS, HG_KEY_DIM)
    logf = jnp.log(f).reshape(shp)
    kk = (1.0 - f).reshape(shp)
    qq = hq.astype(jnp.float32).reshape(shp)
    vv = hi.astype(jnp.float32).reshape(B, L, HG_HEADS, HG_V_DIM)
    if s0 is None:
        s_new, o_b = hgrn2_scan(qq, kk, vv, logf)
    else:
        s_new, o_b = hgrn2_chunk(s0.astype(jnp.float32), qq, kk, vv, logf)
    gate = jax.nn.silu(hg.astype(jnp.float32)).reshape(B, L, HG_HEADS, HG_V_DIM)
    o_b = (rmsnorm(o_b, hg_norm) * gate).reshape(B, L, HG_WIDTH).astype(hn.dtype)
    y = jnp.einsum('ble,ed->bld', jnp.concatenate([o_a, o_b], axis=-1), w_out)
    return y, k, v, s_new


def peer(xn, w_q, sub_keys, u_tab, v_tab):
    B, L, D = xn.shape
    T = B * L
    nblk = -(-T // PEER_BLOCK)
    xt = jnp.pad(xn.reshape(T, D), ((0, nblk * PEER_BLOCK - T), (0, 0))).reshape(nblk, PEER_BLOCK, D)

    def block(xb):
        qh = jnp.einsum('td,de->te', xb, w_q).reshape(PEER_BLOCK, PEER_HEADS, 2, PEER_KEY_DIM // 2)
        s = jnp.einsum('thcd,hcnd->thcn', qh, sub_keys).astype(jnp.float32)
        s1, i1 = lax.top_k(s[:, :, 0], PEER_TOPK)
        s2, i2 = lax.top_k(s[:, :, 1], PEER_TOPK)
        cand = (s1[..., :, None] + s2[..., None, :]).reshape(PEER_BLOCK, PEER_HEADS, PEER_TOPK * PEER_TOPK)
        cidx = (i1[..., :, None] * PEER_KEYS + i2[..., None, :]).reshape(PEER_BLOCK, PEER_HEADS, PEER_TOPK * PEER_TOPK)
        top_s, sel = lax.top_k(cand, PEER_TOPK)
        eidx = jnp.take_along_axis(cidx, sel, axis=-1)
        g = jax.nn.softmax(top_s, axis=-1)
        act = jax.nn.gelu(jnp.einsum('td,thkd->thk', xb, u_tab[eidx]).astype(jnp.float32), approximate=False)
        return jnp.einsum('thk,thkd->td', (g * act).astype(xb.dtype), v_tab[eidx])

    y = lax.map(block, xt).reshape(nblk * PEER_BLOCK, D)[:T]
    return y.reshape(B, L, D)


def setup_inputs(seed: int = 0) -> dict:
    key = jax.random.key(seed)
    ks = jax.random.split(key, 24)
    f32 = jnp.float32

    def nrm(k, shape, s):
        return jax.random.normal(k, shape, f32) * s

    def gain(k, shape):
        return 1.0 + 0.02 * jax.random.normal(k, shape, f32)

    return {
        'x_prompt': nrm(ks[0], (BATCH, SEQ, D_MODEL), 1.0),
        'x_sample': nrm(ks[1], (DEC_BATCH, DEC_SEQ, D_MODEL), 1.0),
        'cache_k': nrm(ks[2], (DEPTH, DEC_BATCH, PAST_LEN, DA_HEADS, 2 * DA_HEAD_DIM), 1.0),
        'cache_v': nrm(ks[3], (DEPTH, DEC_BATCH, PAST_LEN, DA_HEADS, DA_V_DIM), 1.0),
        'state_hgrn': nrm(ks[4], (DEPTH, DEC_BATCH, HG_HEADS, HG_KEY_DIM, HG_V_DIM), 0.5),
        'norm1': gain(ks[5], (DEPTH, D_MODEL)),
        'w_in': nrm(ks[6], (DEPTH, D_MODEL, IN_WIDTH), D_MODEL ** -0.5),
        'da_lq1': nrm(ks[7], (DEPTH, DA_HEAD_DIM), 0.1),
        'da_lk1': nrm(ks[8], (DEPTH, DA_HEAD_DIM), 0.1),
        'da_lq2': nrm(ks[9], (DEPTH, DA_HEAD_DIM), 0.1),
        'da_lk2': nrm(ks[10], (DEPTH, DA_HEAD_DIM), 0.1),
        'da_out_norm': gain(ks[11], (DEPTH, DA_V_DIM)),
        'hg_lb_logits': nrm(ks[12], (DEPTH + 1, HG_K_WIDTH), 0.5),
        'hg_out_norm': gain(ks[13], (DEPTH, HG_V_DIM)),
        'w_out': nrm(ks[14], (DEPTH, MIX_WIDTH, D_MODEL), MIX_WIDTH ** -0.5),
        'rel_bias': nrm(ks[15], (REL_BUCKETS, DA_HEADS), 0.1),
        'norm2': gain(ks[16], (DEPTH, D_MODEL)),
        'peer_w_q': nrm(ks[17], (DEPTH, D_MODEL, PEER_HEADS * PEER_KEY_DIM), D_MODEL ** -0.5),
        'peer_sub_keys': nrm(ks[18], (DEPTH, PEER_HEADS, 2, PEER_KEYS, PEER_KEY_DIM // 2), (PEER_KEY_DIM // 2) ** -0.5),
        'peer_u': nrm(ks[19], (DEPTH, PEER_EXPERTS, D_MODEL), D_MODEL ** -0.5),
        'peer_v': nrm(ks[20], (DEPTH, PEER_EXPERTS, D_MODEL), PEER_HEADS ** -0.5),
        'final_norm': gain(ks[21], (D_MODEL,)),
    }


def reference(x_prompt, x_sample, cache_k, cache_v, state_hgrn, norm1, w_in, da_lq1, da_lk1, da_lq2, da_lk2,
              da_out_norm, hg_lb_logits, hg_out_norm, w_out, rel_bias, norm2, peer_w_q, peer_sub_keys,
              peer_u, peer_v, final_norm):
    f32 = jnp.float32
    lb_all = jnp.cumsum(jax.nn.softmax(hg_lb_logits.astype(f32), axis=0), axis=0)
    hp, hs = x_prompt, x_sample
    kp, vp, sp, ksm, vsm, ssm = [], [], [], [], [], []
    for l in range(DEPTH):
        lam_init = 0.8 - 0.6 * math.exp(-0.3 * l)
        lam = (jnp.exp(jnp.sum(da_lq1[l].astype(f32) * da_lk1[l].astype(f32)))
               - jnp.exp(jnp.sum(da_lq2[l].astype(f32) * da_lk2[l].astype(f32))) + lam_init)
        common = (w_in[l], w_out[l], lam, lam_init, da_out_norm[l], lb_all[l], hg_out_norm[l], rel_bias)
        y, k_new, v_new, s_new = token_mixer(rmsnorm(hp, norm1[l]), None, None, None, *common)
        hp = hp + y
        hp = hp + peer(rmsnorm(hp, norm2[l]), peer_w_q[l], peer_sub_keys[l], peer_u[l], peer_v[l])
        kp.append(k_new)
        vp.append(v_new)
        sp.append(s_new)
        y, k_new, v_new, s_new = token_mixer(rmsnorm(hs, norm1[l]), cache_k[l], cache_v[l], state_hgrn[l], *common)
        hs = hs + y
        hs = hs + peer(rmsnorm(hs, norm2[l]), peer_w_q[l], peer_sub_keys[l], peer_u[l], peer_v[l])
        ksm.append(k_new)
        vsm.append(v_new)
        ssm.append(s_new)
    y_prompt = rmsnorm(hp, final_norm)
    y_sample = rmsnorm(hs, final_norm)
    k_prompt = jnp.stack(kp)
    v_prompt = jnp.stack(vp)
    state_prompt = jnp.stack(sp).astype(state_hgrn.dtype)
    k_sample = jnp.stack(ksm)
    v_sample = jnp.stack(vsm)
    state_sample = jnp.stack(ssm).astype(state_hgrn.dtype)
    return (y_prompt, y_sample, k_prompt, v_prompt, state_prompt, k_sample, v_sample, state_sample)
```

```python
import functools
import math

import jax
import jax.numpy as jnp
from jax import lax
from jax.experimental import pallas as pl
from jax.experimental.pallas import tpu as pltpu
from jax.experimental.pallas import tpu_sc as plsc

CHUNK = 64
DA_HEAD_DIM = 32
DA_V_DIM = 2 * DA_HEAD_DIM
HG_DIM = 128
REL_BUCKETS = 32
REL_MAX_DIST = 128
PEER_HEADS = 8
PEER_KEYS = 128
PEER_TOPK = 16
PEER_SUB_DIM = 64
EPS = 1e-6
NEG = -1e30
LOG2E = 1.4426950408889634

LANES = 128
ATT_BLOCK = 128
ATT_FAR = 512
HG_SUB = 16
GATHER_WINDOW = 64
PEER_TOKENS = 16
VMEM_LIMIT = 56 * 1024 * 1024

f32 = jnp.float32
bf16 = jnp.bfloat16


def _cparams(*sem):
    return pltpu.CompilerParams(dimension_semantics=sem, vmem_limit_bytes=VMEM_LIMIT)


def _inproj_kernel(x_ref, g_ref, w_ref, q_ref, k_ref, v_ref, kt_ref, vb_ref, hz_ref):
    x = x_ref[...]
    xn = x * lax.rsqrt(jnp.mean(x * x, axis=-1, keepdims=True) + EPS) * g_ref[...]
    xb = xn.astype(bf16)
    wq = q_ref.shape[1]
    q_ref[...] = jnp.dot(xb, w_ref[:, 0:wq], preferred_element_type=f32).astype(bf16)
    k = jnp.dot(xb, w_ref[:, wq:2 * wq], preferred_element_type=f32)
    k_ref[...] = k
    kt_ref[...] = k.T.astype(bf16)
    v = jnp.dot(xb, w_ref[:, 2 * wq:3 * wq], preferred_element_type=f32)
    v_ref[...] = v
    vb_ref[...] = v.astype(bf16)
    hz_ref[...] = jnp.dot(xb, w_ref[:, 3 * wq:], preferred_element_type=f32)


def _inproj(x, g, w_bf, tm):
    t, d = x.shape
    e = w_bf.shape[1]
    wq = 512
    return pl.pallas_call(
        _inproj_kernel,
        grid=(t // tm,),
        in_specs=[
            pl.BlockSpec((tm, d), lambda i: (i, 0)),
            pl.BlockSpec((1, d), lambda i: (0, 0)),
            pl.BlockSpec((d, e), lambda i: (0, 0)),
        ],
        out_specs=[
            pl.BlockSpec((tm, wq), lambda i: (i, 0)),
            pl.BlockSpec((tm, wq), lambda i: (i, 0)),
            pl.BlockSpec((tm, wq), lambda i: (i, 0)),
            pl.BlockSpec((wq, tm), lambda i: (0, i)),
            pl.BlockSpec((tm, wq), lambda i: (i, 0)),
            pl.BlockSpec((tm, e - 3 * wq), lambda i: (i, 0)),
        ],
        out_shape=[
            jax.ShapeDtypeStruct((t, wq), bf16),
            jax.ShapeDtypeStruct((t, wq), f32),
            jax.ShapeDtypeStruct((t, wq), f32),
            jax.ShapeDtypeStruct((wq, t), bf16),
            jax.ShapeDtypeStruct((t, wq), bf16),
            jax.ShapeDtypeStruct((t, e - 3 * wq), f32),
        ],
        compiler_params=_cparams("parallel"),
        name="inproj",
    )(x, g, w_bf)


def _rel_bucket(rel):
    nb = REL_BUCKETS // 2
    max_exact = nb // 2
    ret = jnp.where(rel > 0, nb, 0)
    n = jnp.abs(rel)
    nf = jnp.maximum(n, 1).astype(f32)
    large = max_exact + (jnp.log(nf / max_exact) / math.log(REL_MAX_DIST / max_exact)
                         * (nb - max_exact)).astype(jnp.int32)
    large = jnp.minimum(large, nb - 1)
    return ret + jnp.where(n < max_exact, n, large)


def _bias_of(rel, rel_bias):
    return jnp.transpose(rel_bias[_rel_bucket(rel)], (2, 0, 1)).astype(f32)


def _attn_prompt_kernel(lam_ref, q_ref, kt_ref, v_ref, bias_ref, gain_ref, o_ref, m_sc, acc_sc):
    qb = pl.program_id(2)
    tq = q_ref.shape[0]
    lane = lax.broadcasted_iota(jnp.int32, (tq, LANES), 1)
    qs = q_ref[...].astype(f32) * (DA_HEAD_DIM ** -0.5 * LOG2E)
    q4 = jnp.concatenate(
        [jnp.where((lane // DA_HEAD_DIM) == i, qs, 0.0).astype(bf16) for i in range(4)], axis=0)

    m_sc[...] = jnp.full(m_sc.shape, -jnp.inf, f32)
    acc_sc[...] = jnp.zeros(acc_sc.shape, f32)

    def step(start, width, bias):
        kt = kt_ref[:, pl.ds(start, width)]
        vv = v_ref[pl.ds(start, width), :]
        lane_k = lax.broadcasted_iota(jnp.int32, (width, LANES), 1)
        s = jnp.dot(q4, kt, preferred_element_type=f32)
        if bias is not None:
            s = s + bias
        m_old = m_sc[...]
        m_new = jnp.maximum(m_old, jnp.max(s, axis=-1, keepdims=True))
        p = jnp.exp2(s - m_new[:, 0:1]).astype(bf16)
        alpha = jnp.exp2(m_old - m_new)
        for hh in range(2):
            rows = slice(2 * hh * tq, (2 * hh + 2) * tq)
            vaug = jnp.where((lane_k // DA_V_DIM) == hh, vv, jnp.ones_like(vv))
            acc_sc[rows] = acc_sc[rows] * alpha[rows] + jnp.dot(p[rows], vaug, preferred_element_type=f32)
        m_sc[...] = m_new

    n_far = jnp.maximum(qb - 1, 0)
    n_full = n_far // (ATT_FAR // ATT_BLOCK)

    def far(j, carry):
        step(pl.multiple_of(j * ATT_FAR, ATT_FAR), ATT_FAR, None)
        return carry

    lax.fori_loop(0, n_full, far, 0)

    @pl.when(n_far > n_full * (ATT_FAR // ATT_BLOCK))
    def _():
        start = pl.multiple_of(n_full * ATT_FAR, ATT_FAR)
        key = start + lax.broadcasted_iota(jnp.int32, (1, ATT_FAR), 1)
        step(start, ATT_FAR, jnp.where(key < n_far * ATT_BLOCK, 0.0, NEG).astype(f32))

    first = (qb == 0).astype(jnp.int32)
    step(pl.multiple_of(jnp.maximum(qb - 1, 0) * ATT_BLOCK, ATT_BLOCK), 2 * ATT_BLOCK, bias_ref[0, first])

    lam = lam_ref[0]
    outs = []
    for hh in range(2):
        own = (lane // DA_V_DIM) == hh
        a1 = acc_sc[2 * hh * tq:(2 * hh + 1) * tq]
        a2 = acc_sc[(2 * hh + 1) * tq:(2 * hh + 2) * tq]
        l1 = jnp.max(jnp.where(own, 0.0, a1), axis=-1, keepdims=True)
        l2 = jnp.max(jnp.where(own, 0.0, a2), axis=-1, keepdims=True)
        o = a1 / l1 - lam * (a2 / l2)
        ssq = jnp.sum(jnp.where(own, o * o, 0.0), axis=-1, keepdims=True)
        outs.append(o * lax.rsqrt(ssq * (1.0 / DA_V_DIM) + EPS))
    o = jnp.where((lane // DA_V_DIM) == 0, outs[0], outs[1])
    o_ref[...] = (o * gain_ref[...]).astype(o_ref.dtype)


def _attn_prompt(q, kt, vb, bias_tiles, gain, lam, batch, seq):
    t, w = q.shape
    pairs = w // LANES
    nq = seq // ATT_BLOCK
    assert seq % ATT_FAR == 0
    return pl.pallas_call(
        _attn_prompt_kernel,
        grid=(batch, pairs, nq),
        in_specs=[
            pl.BlockSpec(memory_space=pltpu.SMEM),
            pl.BlockSpec((ATT_BLOCK, LANES), lambda b, p, i: (b * nq + i, p)),
            pl.BlockSpec((LANES, seq), lambda b, p, i: (p, b)),
            pl.BlockSpec((seq, LANES), lambda b, p, i: (b, p)),
            pl.BlockSpec((1, 2, 4 * ATT_BLOCK, 2 * ATT_BLOCK), lambda b, p, i: (p, 0, 0, 0)),
            pl.BlockSpec((1, LANES), lambda b, p, i: (0, 0)),
        ],
        out_specs=pl.BlockSpec((ATT_BLOCK, LANES), lambda b, p, i: (b * nq + i, p)),
        out_shape=jax.ShapeDtypeStruct((t, w), bf16),
        scratch_shapes=[
            pltpu.VMEM((4 * ATT_BLOCK, LANES), f32),
            pltpu.VMEM((4 * ATT_BLOCK, LANES), f32),
        ],
        compiler_params=_cparams("parallel", "parallel", "arbitrary"),
        name="attn_prompt",
    )(lam, q, kt, vb, bias_tiles, gain)


def _prompt_bias_tiles(rel_bias):
    h = rel_bias.shape[1]
    i = jnp.arange(ATT_BLOCK)
    rel_diag = i[None, :] - i[:, None]
    rel_prev = rel_diag - ATT_BLOCK
    far = rel_bias[REL_BUCKETS // 2 - 1].astype(f32)
    b_diag = _bias_of(rel_diag, rel_bias) - far[:, None, None]
    b_prev = _bias_of(rel_prev, rel_bias) - far[:, None, None]
    mask = (i[None, :] // CHUNK) <= (i[:, None] // CHUNK)
    b_diag = jnp.where(mask[None], b_diag * LOG2E, NEG)
    general = jnp.concatenate([b_prev * LOG2E, b_diag], axis=-1)
    first = jnp.concatenate([b_diag, jnp.full_like(b_diag, NEG)], axis=-1)
    tiles = jnp.stack([general, first], axis=1)
    tiles = tiles.reshape(h // 2, 2, 2, ATT_BLOCK, 2 * ATT_BLOCK)
    tiles = jnp.transpose(tiles, (0, 2, 1, 3, 4))
    tiles = jnp.repeat(tiles, 2, axis=2)
    return tiles.reshape(h // 2, 2, 4 * ATT_BLOCK, 2 * ATT_BLOCK)


def _attn_sample_kernel(lam_ref, q_ref, kp_ref, vp_ref, kn_ref, vn_ref, bp_ref, bn_ref, gain_ref, o_ref):
    nq = q_ref.shape[0]
    pairs = q_ref.shape[1] // LANES
    lane = lax.broadcasted_iota(jnp.int32, (nq, LANES), 1)
    lam = lam_ref[0]
    nt = (((1,), (1,)), ((), ()))
    for p in range(pairs):
        sl = slice(p * LANES, (p + 1) * LANES)
        qs = q_ref[:, sl].astype(f32) * (DA_HEAD_DIM ** -0.5 * LOG2E)
        kp = kp_ref[:, sl].astype(bf16)
        kn = kn_ref[:, sl].astype(bf16)
        vp = vp_ref[:, sl].astype(bf16)
        vn = vn_ref[:, sl].astype(bf16)
        outs = []
        for hh in range(2):
            h = 2 * p + hh
            own = (lane // DA_V_DIM) == hh
            res = []
            for c in range(2):
                qm = jnp.where((lane // DA_HEAD_DIM) == 2 * hh + c, qs, 0.0).astype(bf16)
                sp = lax.dot_general(qm, kp, nt, preferred_element_type=f32) + bp_ref[h]
                sn = lax.dot_general(qm, kn, nt, preferred_element_type=f32) + bn_ref[h]
                m = jnp.maximum(jnp.max(sp, axis=-1, keepdims=True), jnp.max(sn, axis=-1, keepdims=True))
                pp = jnp.exp2(sp - m)
                pn = jnp.exp2(sn - m)
                l = jnp.sum(pp, axis=-1, keepdims=True) + jnp.sum(pn, axis=-1, keepdims=True)
                pv = (jnp.dot(pp.astype(bf16), vp, preferred_element_type=f32)
                      + jnp.dot(pn.astype(bf16), vn, preferred_element_type=f32))
                res.append(pv / l)
            o = res[0] - lam * res[1]
            ssq = jnp.sum(jnp.where(own, o * o, 0.0), axis=-1, keepdims=True)
            outs.append(o * lax.rsqrt(ssq * (1.0 / DA_V_DIM) + EPS))
        o = jnp.where((lane // DA_V_DIM) == 0, outs[0], outs[1])
        o_ref[:, sl] = (o * gain_ref[...]).astype(o_ref.dtype)


def _attn_sample(q, k_new, v_new, k_past, v_past, bias_past, bias_new, gain, lam, batch, nq, past):
    w = q.shape[1]
    h = bias_past.shape[0]
    return pl.pallas_call(
        _attn_sample_kernel,
        grid=(batch,),
        in_specs=[
            pl.BlockSpec(memory_space=pltpu.SMEM),
            pl.BlockSpec((nq, w), lambda b: (b, 0)),
            pl.BlockSpec((past, w), lambda b: (b, 0)),
            pl.BlockSpec((past, w), lambda b: (b, 0)),
            pl.BlockSpec((nq, w), lambda b: (b, 0)),
            pl.BlockSpec((nq, w), lambda b: (b, 0)),
            pl.BlockSpec((h, nq, past), lambda b: (0, 0, 0)),
            pl.BlockSpec((h, nq, nq), lambda b: (0, 0, 0)),
            pl.BlockSpec((1, LANES), lambda b: (0, 0)),
        ],
        out_specs=pl.BlockSpec((nq, w), lambda b: (b, 0)),
        out_shape=jax.ShapeDtypeStruct((batch * nq, w), bf16),
        compiler_params=_cparams("parallel"),
        name="attn_sample",
    )(lam, q, k_past, v_past, k_new, v_new, bias_past, bias_new, gain)


def _hgrn_chunk(q, f_logit, vv, g_logit, lb, gain, s0, lc):
    n_sub = lc // HG_SUB
    f = lb + (1.0 - lb) * jax.nn.sigmoid(f_logit)
    logf = jnp.log(f)
    k = 1.0 - f
    row = lax.broadcasted_iota(jnp.int32, (lc, lc), 0)
    col = lax.broadcasted_iota(jnp.int32, (lc, lc), 1)
    tril = (col <= row).astype(f32)
    b = jnp.dot(tril, logf, precision=lax.Precision.HIGHEST, preferred_element_type=f32)
    b_last = b[lc - 1:lc, :]

    o = jnp.dot((q * jnp.exp(b)).astype(bf16), s0.astype(bf16), preferred_element_type=f32)
    k_hat = k * jnp.exp(b_last - b)
    bt = b.T
    decay_col = jnp.exp(bt[:, lc - 1:lc])
    s_new = decay_col * s0 + jnp.dot(k_hat.T.astype(bf16), vv.astype(bf16), preferred_element_type=f32)

    if n_sub > 1:
        rows = []
        nt = (((1,), (1,)), ((), ()))
        for i in range(n_sub):
            lo = i * HG_SUB
            if i == 0:
                rows.append(jnp.zeros((HG_SUB, lc), f32))
                continue
            beta = b[lo - 1:lo, :]
            q_t = q[lo:lo + HG_SUB, :] * jnp.exp(b[lo:lo + HG_SUB, :] - beta)
            k_t = k * jnp.exp(jnp.minimum(beta - b, 0.0))
            rows.append(lax.dot_general(q_t.astype(bf16), k_t.astype(bf16), nt, preferred_element_type=f32))
        a_off = jnp.concatenate(rows, axis=0)
        a_off = jnp.where(col < (row // HG_SUB) * HG_SUB, a_off, 0.0)
        o = o + jnp.dot(a_off.astype(bf16), vv.astype(bf16), preferred_element_type=f32)

    ones = jnp.ones((HG_DIM, LANES), bf16)
    srow = lax.broadcasted_iota(jnp.int32, (HG_SUB, HG_DIM), 0)
    orow = lax.broadcasted_iota(jnp.int32, (HG_SUB, HG_DIM), 0)
    blocks = []
    for i in range(n_sub):
        lo = i * HG_SUB
        b_i = b[lo:lo + HG_SUB, :]
        k_i = k[lo:lo + HG_SUB, :]
        v_i = vv[lo:lo + HG_SUB, :]
        q_i = q[lo:lo + HG_SUB, :]
        d_rows = []
        for t in range(HG_SUB):
            arg = jnp.where(srow <= t, b_i[t:t + 1, :] - b_i, -jnp.inf)
            d_rows.append(q_i[t:t + 1, :] * k_i * jnp.exp(arg))
        d3 = jnp.concatenate(d_rows, axis=0)
        a_rep = jnp.dot(d3.astype(bf16), ones, preferred_element_type=f32)
        o_i = jnp.zeros((HG_SUB, HG_DIM), f32)
        for t in range(HG_SUB):
            o_t = jnp.sum(a_rep[t * HG_SUB:(t + 1) * HG_SUB, :] * v_i, axis=0, keepdims=True)
            o_i = jnp.where(orow == t, o_t, o_i)
        blocks.append(o_i)
    o = o + (jnp.concatenate(blocks, axis=0) if n_sub > 1 else blocks[0])

    on = o * lax.rsqrt(jnp.mean(o * o, axis=-1, keepdims=True) + EPS) * gain
    gate = g_logit * jax.nn.sigmoid(g_logit)
    return on * gate, s_new


def _hgrn_kernel(hz_ref, lb_ref, gain_ref, s0_ref, o_ref, s_ref, st_sc, *, lc, heads):
    j = pl.program_id(1)
    w = heads * HG_DIM

    @pl.when(j == 0)
    def _():
        st_sc[...] = s0_ref[0]

    n_chunks = hz_ref.shape[0] // lc
    for c in range(n_chunks):
        rs = slice(c * lc, (c + 1) * lc)
        for h in range(heads):
            cs = slice(h * HG_DIM, (h + 1) * HG_DIM)
            o, s_new = _hgrn_chunk(
                hz_ref[rs, h * HG_DIM:(h + 1) * HG_DIM],
                hz_ref[rs, w + h * HG_DIM:w + (h + 1) * HG_DIM],
                hz_ref[rs, 2 * w + h * HG_DIM:2 * w + (h + 1) * HG_DIM],
                hz_ref[rs, 3 * w + h * HG_DIM:3 * w + (h + 1) * HG_DIM],
                lb_ref[:, cs], gain_ref[...], st_sc[h], lc)
            st_sc[h] = s_new
            o_ref[rs, cs] = o.astype(o_ref.dtype)

    @pl.when(j == pl.num_programs(1) - 1)
    def _():
        s_ref[0] = st_sc[...]


def _hgrn(hz, lb, gain, s0, batch, seq, lc, chunks_per_step):
    t = hz.shape[0]
    w = hz.shape[1] // 4
    heads = w // HG_DIM
    rows = lc * chunks_per_step
    steps = seq // rows
    return pl.pallas_call(
        functools.partial(_hgrn_kernel, lc=lc, heads=heads),
        grid=(batch, steps),
        in_specs=[
            pl.BlockSpec((rows, 4 * w), lambda b, j: (b * steps + j, 0)),
            pl.BlockSpec((1, w), lambda b, j: (0, 0)),
            pl.BlockSpec((1, HG_DIM), lambda b, j: (0, 0)),
            pl.BlockSpec((1, heads, HG_DIM, HG_DIM), lambda b, j: (b, 0, 0, 0)),
        ],
        out_specs=[
            pl.BlockSpec((rows, w), lambda b, j: (b * steps + j, 0)),
            pl.BlockSpec((1, heads, HG_DIM, HG_DIM), lambda b, j: (b, 0, 0, 0)),
        ],
        out_shape=[
            jax.ShapeDtypeStruct((t, w), bf16),
            jax.ShapeDtypeStruct((batch, heads, HG_DIM, HG_DIM), f32),
        ],
        scratch_shapes=[pltpu.VMEM((heads, HG_DIM, HG_DIM), f32)],
        compiler_params=_cparams("parallel", "arbitrary"),
        name="hgrn2",
    )(hz, lb, gain, s0)


def _top_rows(vals, k, payload=None):
    n_rows = vals.shape[0]
    row = lax.broadcasted_iota(jnp.int32, vals.shape, 0)
    out_v, out_i = [], []
    for _ in range(k):
        m = jnp.max(vals, axis=0, keepdims=True)
        first = jnp.min(jnp.where(vals == m, row, n_rows), axis=0, keepdims=True)
        hit = row == first
        out_v.append(m)
        if payload is None:
            out_i.append(first)
        else:
            out_i.append(jnp.max(jnp.where(hit, payload, -1), axis=0, keepdims=True))
        vals = jnp.where(hit, -jnp.inf, vals)
    return jnp.concatenate(out_v, axis=0), jnp.concatenate(out_i, axis=0)


def _retrieve_kernel(oa_ref, ob_ref, x_ref, wo_ref, g_ref, wqt_ref, sk_ref,
                     hp_ref, xn_ref, idx_ref, gate_ref, qt_sc, s1v_sc, s1i_sc, idx_sc):
    half = oa_ref.shape[1]
    y = (jnp.dot(oa_ref[...], wo_ref[0:half, :], preferred_element_type=f32)
         + jnp.dot(ob_ref[...], wo_ref[half:, :], preferred_element_type=f32))
    hp = x_ref[...] + y
    hp_ref[...] = hp
    xn = hp * lax.rsqrt(jnp.mean(hp * hp, axis=-1, keepdims=True) + EPS) * g_ref[...]
    xn_ref[...] = xn
    nt = (((1,), (1,)), ((), ()))
    qt_sc[...] = lax.dot_general(wqt_ref[...], xn.astype(bf16), nt, preferred_element_type=f32).astype(bf16)

    n_half = sk_ref.shape[0]

    def stage1(hc, carry):
        start = pl.multiple_of(hc * PEER_SUB_DIM, PEER_SUB_DIM)
        s = jnp.dot(sk_ref[hc], qt_sc[pl.ds(start, PEER_SUB_DIM), :], preferred_element_type=f32)
        v, i = _top_rows(s, PEER_TOPK)
        s1v_sc[hc] = v
        s1i_sc[hc] = i
        return carry

    lax.fori_loop(0, n_half, stage1, 0)

    tm = x_ref.shape[0]
    sub = lax.broadcasted_iota(jnp.int32, (8, tm), 0)

    def stage2(h, carry):
        v1, i1 = s1v_sc[2 * h], s1i_sc[2 * h]
        v2, i2 = s1v_sc[2 * h + 1], s1i_sc[2 * h + 1]
        cv, ci = [], []
        for i in range(PEER_TOPK):
            nj = PEER_TOPK // (i + 1)
            rows = PEER_TOPK if nj > 8 else 8
            a = v1[i:i + 1, :] + v2[0:rows, :]
            e = i1[i:i + 1, :] * PEER_KEYS + i2[0:rows, :]
            if nj < rows:
                a = jnp.where(sub < nj, a, -jnp.inf)
            cv.append(a)
            ci.append(e)
        top_s, eidx = _top_rows(jnp.concatenate(cv, axis=0), PEER_TOPK, jnp.concatenate(ci, axis=0))
        p = jnp.exp(top_s - top_s[0:1, :])
        g = p / jnp.sum(p, axis=0, keepdims=True)
        r0 = pl.multiple_of(h * PEER_TOPK, PEER_TOPK)
        gate_ref[pl.ds(r0, PEER_TOPK), :] = g
        idx_sc[pl.ds(r0, PEER_TOPK), :] = eidx
        return carry

    lax.fori_loop(0, n_half // 2, stage2, 0)
    idx_ref[...] = pltpu.bitcast(pltpu.bitcast(idx_sc[...], f32).T, jnp.int32)


def _retrieve(oa, ob, x, wo_bf, g2, wqt_bf, sk_bf, tm):
    t, d = x.shape
    half = oa.shape[1]
    e = wqt_bf.shape[0]
    n_half = sk_bf.shape[0]
    n_sel = (n_half // 2) * PEER_TOPK
    return pl.pallas_call(
        _retrieve_kernel,
        grid=(t // tm,),
        in_specs=[
            pl.BlockSpec((tm, half), lambda i: (i, 0)),
            pl.BlockSpec((tm, half), lambda i: (i, 0)),
            pl.BlockSpec((tm, d), lambda i: (i, 0)),
            pl.BlockSpec((2 * half, d), lambda i: (0, 0)),
            pl.BlockSpec((1, d), lambda i: (0, 0)),
            pl.BlockSpec((e, d), lambda i: (0, 0)),
            pl.BlockSpec((n_half, PEER_KEYS, PEER_SUB_DIM), lambda i: (0, 0, 0)),
        ],
        out_specs=[
            pl.BlockSpec((tm, d), lambda i: (i, 0)),
            pl.BlockSpec((tm, d), lambda i: (i, 0)),
            pl.BlockSpec((tm, n_sel), lambda i: (i, 0)),
            pl.BlockSpec((n_sel, tm), lambda i: (0, i)),
        ],
        out_shape=[
            jax.ShapeDtypeStruct((t, d), f32),
            jax.ShapeDtypeStruct((t, d), f32),
            jax.ShapeDtypeStruct((t, n_sel), jnp.int32),
            jax.ShapeDtypeStruct((n_sel, t), f32),
        ],
        scratch_shapes=[
            pltpu.VMEM((e, tm), bf16),
            pltpu.VMEM((n_half, PEER_TOPK, tm), f32),
            pltpu.VMEM((n_half, PEER_TOPK, tm), jnp.int32),
            pltpu.VMEM((n_sel, tm), jnp.int32),
        ],
        compiler_params=_cparams("parallel"),
        name="retrieve",
    )(oa, ob, x, wo_bf, g2, wqt_bf, sk_bf)


def _gather_rows(table, idx):
    n = idx.shape[0]
    width = table.shape[1]
    idx2 = idx.reshape(n // GATHER_WINDOW, GATHER_WINDOW)
    mesh = plsc.VectorSubcoreMesh(core_axis_name="c", subcore_axis_name="s")

    @functools.partial(pl.kernel, out_type=jax.ShapeDtypeStruct((n, width), table.dtype), mesh=mesh)
    def gather(t_hbm, i_hbm, o_hbm):
        def body(i_vmem, o_vmem):
            pltpu.sync_copy(t_hbm.at[i_vmem.at[0]], o_vmem)

        pltpu.emit_pipeline(
            body,
            grid=(n // GATHER_WINDOW,),
            in_specs=[pl.BlockSpec((1, GATHER_WINDOW), lambda i: (i, 0))],
            out_specs=[pl.BlockSpec((GATHER_WINDOW, width), lambda i: (i, 0))],
            core_axis_name=("c", "s"),
            dimension_semantics=(pltpu.PARALLEL,),
            trace_scopes=False,
        )(i_hbm, o_hbm)

    return gather(table, idx2)


def _pack_table(tab):
    half = tab.shape[1] // 2
    b = lax.bitcast_convert_type(tab.astype(bf16), jnp.uint16).astype(jnp.uint32)
    return lax.bitcast_convert_type(b[:, :half] | (b[:, half:] << 16), jnp.int32)


def _split_bf16(x):
    hi = x.astype(bf16)
    return hi, (x - hi.astype(f32)).astype(bf16)


def _mix_kernel(ug_ref, vg_ref, xn_ref, hp_ref, gate_ref, gf_ref, y_ref, wt_sc, peer_sc):
    i = pl.program_id(0)
    n_tok = xn_ref.shape[0]
    n_sel = gate_ref.shape[0]
    half = ug_ref.shape[1]
    lane0 = (i % (LANES // n_tok)) * n_tok
    nt = (((1,), (1,)), ((), ()))

    xb = xn_ref[...].astype(bf16)
    place = (lax.broadcasted_iota(jnp.int32, (LANES, n_tok), 0)
             == lane0 + lax.broadcasted_iota(jnp.int32, (LANES, n_tok), 1)).astype(bf16)
    xp = jnp.concatenate([jnp.dot(place, xb[:, :half], preferred_element_type=f32).astype(bf16),
                          jnp.dot(place, xb[:, half:], preferred_element_type=f32).astype(bf16)], axis=0)

    row2 = lax.broadcasted_iota(jnp.int32, (2 * n_sel, LANES), 0)
    lane2 = lax.broadcasted_iota(jnp.int32, (2 * n_sel, LANES), 1)
    even = (row2 % 2) == 0
    act2 = jnp.zeros((2 * n_sel, LANES), f32)
    for t in range(n_tok):
        rows = pltpu.bitcast(ug_ref[t * n_sel:(t + 1) * n_sel, :], bf16)
        r = lax.dot_general(rows, xp, nt, preferred_element_type=f32)
        act2 = jnp.where(lane2 == lane0 + t, jnp.where(even, r[:, :LANES], r[:, LANES:]), act2)

    pair = (lax.broadcasted_iota(jnp.int32, (n_sel, 2 * n_sel), 1) // 2
            == lax.broadcasted_iota(jnp.int32, (n_sel, 2 * n_sel), 0)).astype(bf16)
    a_hi, a_lo = _split_bf16(act2)
    act = jnp.dot(pair, a_hi, preferred_element_type=f32) + jnp.dot(pair, a_lo, preferred_element_type=f32)
    w = gate_ref[...] * (0.5 * act * (1.0 + lax.erf(act * (2.0 ** -0.5))))

    w_hi, w_lo = _split_bf16(w.T)
    wt_sc[0] = jnp.dot(w_hi, pair, preferred_element_type=f32)
    wt_sc[1] = jnp.dot(w_lo, pair, preferred_element_type=f32)
    rowl = lax.broadcasted_iota(jnp.int32, (16, 2 * n_sel), 0)
    keep = (rowl < 4) & ((lax.broadcasted_iota(jnp.int32, (16, 2 * n_sel), 1) % 2) == (rowl % 2))
    for t in range(n_tok):
        whi = wt_sc[0, pl.ds(lane0 + t, 1), :]
        wlo = wt_sc[1, pl.ds(lane0 + t, 1), :]
        lhs = jnp.where(keep, jnp.where(rowl < 2, whi, wlo), 0.0).astype(bf16)
        rows = pltpu.bitcast(vg_ref[t * n_sel:(t + 1) * n_sel, :], bf16)
        res = jnp.dot(lhs, rows, preferred_element_type=f32)
        peer_sc[t:t + 1, 0:half] = res[0:1] + res[2:3]
        peer_sc[t:t + 1, half:] = res[1:2] + res[3:4]

    h = hp_ref[...] + peer_sc[...]
    y_ref[...] = h * lax.rsqrt(jnp.mean(h * h, axis=-1, keepdims=True) + EPS) * gf_ref[...]


def _mix(ug, vg, xn, hp, gate, gf):
    t, d = xn.shape
    n_sel = gate.shape[0]
    half = ug.shape[1]
    nt = PEER_TOKENS
    per_lane_block = LANES // nt
    return pl.pallas_call(
        _mix_kernel,
        grid=(t // nt,),
        in_specs=[
            pl.BlockSpec((nt * n_sel, half), lambda i: (i, 0)),
            pl.BlockSpec((nt * n_sel, half), lambda i: (i, 0)),
            pl.BlockSpec((nt, d), lambda i: (i, 0)),
            pl.BlockSpec((nt, d), lambda i: (i, 0)),
            pl.BlockSpec((n_sel, LANES), lambda i: (0, i // per_lane_block)),
            pl.BlockSpec((1, d), lambda i: (0, 0)),
        ],
        out_specs=pl.BlockSpec((nt, d), lambda i: (i, 0)),
        out_shape=jax.ShapeDtypeStruct((t, d), f32),
        scratch_shapes=[
            pltpu.VMEM((2, LANES, 2 * n_sel), f32),
            pltpu.VMEM((nt, d), f32),
        ],
        compiler_params=_cparams("arbitrary"),
        name="peer_mix",
    )(ug, vg, xn, hp, gate, gf)


def _channel_mixer(oa, ob, x, wo_bf, g2, wqt_bf, sk_bf, u_pk, v_pk, gf, tm):
    hp, xn, eidx, gate = _retrieve(oa, ob, x, wo_bf, g2, wqt_bf, sk_bf, tm)
    flat = eidx.reshape(-1)
    ug = _gather_rows(u_pk, flat)
    vg = _gather_rows(v_pk, flat)
    return _mix(ug, vg, xn, hp, gate, gf)


def kernel(x_prompt, x_sample, cache_k, cache_v, state_hgrn, norm1, w_in, da_lq1, da_lk1, da_lq2, da_lk2,
           da_out_norm, hg_lb_logits, hg_out_norm, w_out, rel_bias, norm2, peer_w_q, peer_sub_keys,
           peer_u, peer_v, final_norm):
    batch, seq, d = x_prompt.shape
    dbatch, dseq, _ = x_sample.shape
    past = cache_k.shape[2]
    da_heads = cache_k.shape[3]
    hg_heads = state_hgrn.shape[2]
    depth = w_in.shape[0]
    assert depth == 1 and seq % ATT_BLOCK == 0 and seq % CHUNK == 0 and dseq % HG_SUB == 0 and dseq <= CHUNK
    assert past % CHUNK == 0 and (past + dseq - 1) // CHUNK == past // CHUNK

    l = 0
    lam_init = 0.8 - 0.6 * math.exp(-0.3 * l)
    lam = (jnp.exp(jnp.sum(da_lq1[l].astype(f32) * da_lk1[l].astype(f32)))
           - jnp.exp(jnp.sum(da_lq2[l].astype(f32) * da_lk2[l].astype(f32))) + lam_init).reshape(1)
    lb = jnp.cumsum(jax.nn.softmax(hg_lb_logits.astype(f32), axis=0), axis=0)[l].reshape(1, -1)
    da_gain = (jnp.tile(da_out_norm[l].astype(f32), 2) * (1.0 - lam_init)).reshape(1, LANES)
    hg_gain = hg_out_norm[l].astype(f32).reshape(1, HG_DIM)
    g1 = norm1[l].astype(f32).reshape(1, d)
    g2 = norm2[l].astype(f32).reshape(1, d)
    gf = final_norm.astype(f32).reshape(1, d)
    w_in_bf = w_in[l].astype(bf16)
    wo_bf = w_out[l].astype(bf16)
    wqt_bf = peer_w_q[l].T.astype(bf16)
    sk_bf = peer_sub_keys[l].reshape(-1, PEER_KEYS, PEER_SUB_DIM).astype(bf16)
    u_pk = _pack_table(peer_u[l])
    v_pk = _pack_table(peer_v[l])

    bias_tiles = _prompt_bias_tiles(rel_bias)
    q_pos = past + jnp.arange(dseq)
    bias_s = _bias_of(jnp.arange(past + dseq)[None, :] - q_pos[:, None], rel_bias) * LOG2E
    bias_past, bias_new = bias_s[:, :, :past], bias_s[:, :, past:]
    zero_state = jnp.zeros((1, hg_heads, HG_DIM, HG_DIM), f32)

    y_p, k_p, v_p, s_p = [], [], [], []
    for b in range(batch):
        xb = x_prompt[b]
        q, k, v, kt, vb, hz = _inproj(xb, g1, w_in_bf, 256)
        oa = _attn_prompt(q, kt, vb, bias_tiles, da_gain, lam, 1, seq)
        ob, s_new = _hgrn(hz, lb, hg_gain, zero_state, 1, seq, CHUNK, 4)
        y_p.append(_channel_mixer(oa, ob, xb, wo_bf, g2, wqt_bf, sk_bf, u_pk, v_pk, gf, 256))
        k_p.append(k)
        v_p.append(v)
        s_p.append(s_new)

    xs = x_sample.reshape(dbatch * dseq, d)
    q, k_s, v_s, _, _, hz = _inproj(xs, g1, w_in_bf, dbatch * dseq)
    oa = _attn_sample(q, k_s, v_s, cache_k[l].reshape(dbatch * past, -1), cache_v[l].reshape(dbatch * past, -1),
                      bias_past, bias_new, da_gain, lam, dbatch, dseq, past)
    ob, s_s = _hgrn(hz, lb, hg_gain, state_hgrn[l].astype(f32), dbatch, dseq, dseq, 1)
    y_s = _channel_mixer(oa, ob, xs, wo_bf, g2, wqt_bf, sk_bf, u_pk, v_pk, gf, 128)

    dh2 = 2 * DA_HEAD_DIM
    y_prompt = jnp.stack(y_p)
    y_sample = y_s.reshape(dbatch, dseq, d)
    k_prompt = jnp.stack(k_p).reshape(1, batch, seq, da_heads, dh2)
    v_prompt = jnp.stack(v_p).reshape(1, batch, seq, da_heads, DA_V_DIM)
    state_prompt = jnp.concatenate(s_p, axis=0)[None].astype(state_hgrn.dtype)
    k_sample = k_s.reshape(1, dbatch, dseq, da_heads, dh2)
    v_sample = v_s.reshape(1, dbatch, dseq, da_heads, DA_V_DIM)
    state_sample = s_s[None].astype(state_hgrn.dtype)
    return (y_prompt, y_sample, k_prompt, v_prompt, state_prompt, k_sample, v_sample, state_sample)
```

```python
import functools
import math

import jax
import jax.numpy as jnp
from jax import lax
from jax.experimental import pallas as pl
from jax.experimental.pallas import tpu as pltpu
from jax.experimental.pallas import tpu_sc as plsc

CHUNK = 64
DA_HEAD_DIM = 32
DA_V_DIM = 2 * DA_HEAD_DIM
HG_DIM = 128
REL_BUCKETS = 32
REL_MAX_DIST = 128
PEER_HEADS = 8
PEER_KEYS = 128
PEER_TOPK = 16
PEER_SUB_DIM = 64
EPS = 1e-6
NEG = -1e30
LOG2E = 1.4426950408889634

LANES = 128
ATT_BLOCK = 128
ATT_STEP = 512
HG_SUB = 16
GATHER_WINDOW = 64
PEER_TOKENS = 16
VMEM_LIMIT = 56 * 1024 * 1024

f32 = jnp.float32
bf16 = jnp.bfloat16


def _cparams(*sem):
    return pltpu.CompilerParams(dimension_semantics=sem, vmem_limit_bytes=VMEM_LIMIT)


def _inproj_prompt_kernel(x_ref, g_ref, w_ref, wkv_t_ref, kin_ref, vin_ref,
                          q_ref, kt_ref, vt_ref, hz_ref, ktf_ref, vtf_ref):
    del kin_ref, vin_ref
    x = x_ref[...]
    xn = x * lax.rsqrt(jnp.mean(x * x, axis=-1, keepdims=True) + EPS) * g_ref[...]
    xb = xn.astype(bf16)
    wq = q_ref.shape[1]
    nt = (((1,), (1,)), ((), ()))
    q_ref[...] = jnp.dot(xb, w_ref[:, 0:wq], preferred_element_type=f32).astype(bf16)
    kt = lax.dot_general(wkv_t_ref[0:wq, :], xb, nt, preferred_element_type=f32)
    ktf_ref[...] = kt
    kt_ref[...] = kt.astype(bf16)
    vt = lax.dot_general(wkv_t_ref[wq:2 * wq, :], xb, nt, preferred_element_type=f32)
    vtf_ref[...] = vt
    vt_ref[...] = vt.astype(bf16)
    hz_ref[...] = jnp.dot(xb, w_ref[:, 3 * wq:], preferred_element_type=f32)


def _inproj_prompt(x, g, w_bf, wkv_t_bf, kt_all, vt_all, b, seq, tm):
    d = x.shape[1]
    e = w_bf.shape[1]
    wq = wkv_t_bf.shape[0] // 2
    steps = seq // tm
    return pl.pallas_call(
        _inproj_prompt_kernel,
        grid=(steps,),
        in_specs=[
            pl.BlockSpec((tm, d), lambda i: (b * steps + i, 0)),
            pl.BlockSpec((1, d), lambda i: (0, 0)),
            pl.BlockSpec((d, e), lambda i: (0, 0)),
            pl.BlockSpec((2 * wq, d), lambda i: (0, 0)),
            pl.BlockSpec(memory_space=pl.ANY),
            pl.BlockSpec(memory_space=pl.ANY),
        ],
        out_specs=[
            pl.BlockSpec((tm, wq), lambda i: (i, 0)),
            pl.BlockSpec((wq, tm), lambda i: (0, i)),
            pl.BlockSpec((wq, tm), lambda i: (0, i)),
            pl.BlockSpec((tm, e - 3 * wq), lambda i: (i, 0)),
            pl.BlockSpec((wq, tm), lambda i: (b, i)),
            pl.BlockSpec((wq, tm), lambda i: (b, i)),
        ],
        out_shape=[
            jax.ShapeDtypeStruct((seq, wq), bf16),
            jax.ShapeDtypeStruct((wq, seq), bf16),
            jax.ShapeDtypeStruct((wq, seq), bf16),
            jax.ShapeDtypeStruct((seq, e - 3 * wq), f32),
            jax.ShapeDtypeStruct(kt_all.shape, f32),
            jax.ShapeDtypeStruct(vt_all.shape, f32),
        ],
        input_output_aliases={4: 4, 5: 5},
        compiler_params=_cparams("parallel"),
        name="inproj",
    )(x, g, w_bf, wkv_t_bf, kt_all, vt_all)


def _inproj_sample_kernel(x_ref, g_ref, w_ref, q_ref, k_ref, v_ref, hz_ref):
    x = x_ref[...]
    xn = x * lax.rsqrt(jnp.mean(x * x, axis=-1, keepdims=True) + EPS) * g_ref[...]
    xb = xn.astype(bf16)
    wq = q_ref.shape[1]
    q_ref[...] = jnp.dot(xb, w_ref[:, 0:wq], preferred_element_type=f32).astype(bf16)
    k_ref[...] = jnp.dot(xb, w_ref[:, wq:2 * wq], preferred_element_type=f32)
    v_ref[...] = jnp.dot(xb, w_ref[:, 2 * wq:3 * wq], preferred_element_type=f32)
    hz_ref[...] = jnp.dot(xb, w_ref[:, 3 * wq:], preferred_element_type=f32)


def _inproj_sample(x, g, w_bf, wq):
    t, d = x.shape
    e = w_bf.shape[1]
    return pl.pallas_call(
        _inproj_sample_kernel,
        grid=(1,),
        in_specs=[
            pl.BlockSpec((t, d), lambda i: (0, 0)),
            pl.BlockSpec((1, d), lambda i: (0, 0)),
            pl.BlockSpec((d, e), lambda i: (0, 0)),
        ],
        out_specs=[
            pl.BlockSpec((t, wq), lambda i: (0, 0)),
            pl.BlockSpec((t, wq), lambda i: (0, 0)),
            pl.BlockSpec((t, wq), lambda i: (0, 0)),
            pl.BlockSpec((t, e - 3 * wq), lambda i: (0, 0)),
        ],
        out_shape=[
            jax.ShapeDtypeStruct((t, wq), bf16),
            jax.ShapeDtypeStruct((t, wq), f32),
            jax.ShapeDtypeStruct((t, wq), f32),
            jax.ShapeDtypeStruct((t, e - 3 * wq), f32),
        ],
        compiler_params=_cparams("arbitrary"),
        name="inproj_sample",
    )(x, g, w_bf)


def _rel_bucket(rel):
    nb = REL_BUCKETS // 2
    max_exact = nb // 2
    ret = jnp.where(rel > 0, nb, 0)
    n = jnp.abs(rel)
    nf = jnp.maximum(n, 1).astype(f32)
    large = max_exact + (jnp.log(nf / max_exact) / math.log(REL_MAX_DIST / max_exact)
                         * (nb - max_exact)).astype(jnp.int32)
    large = jnp.minimum(large, nb - 1)
    return ret + jnp.where(n < max_exact, n, large)


def _bias_of(rel, rel_bias):
    bucket = _rel_bucket(rel)
    out = jnp.zeros((rel_bias.shape[1],) + rel.shape, f32)
    for b in range(REL_BUCKETS):
        out = out + jnp.where(bucket[None] == b, rel_bias[b].astype(f32)[:, None, None], 0.0)
    return out


def _attn_prompt_kernel(lam_ref, q_ref, kt_ref, vt_ref, bias_ref, gain_ref, o_ref, m_sc, acc_sc, sa_sc, sb_sc):
    qb = pl.program_id(2)
    tq = q_ref.shape[0]
    seq = kt_ref.shape[1]
    per_step = ATT_STEP // ATT_BLOCK
    lane = lax.broadcasted_iota(jnp.int32, (tq, LANES), 1)
    feat_k = lax.broadcasted_iota(jnp.int32, (LANES, ATT_STEP), 0)
    qs = q_ref[...].astype(f32) * (DA_HEAD_DIM ** -0.5 * LOG2E)
    q4 = jnp.concatenate(
        [jnp.where((lane // DA_HEAD_DIM) == i, qs, 0.0).astype(bf16) for i in range(4)], axis=0)

    m_sc[...] = jnp.full(m_sc.shape, -jnp.inf, f32)
    acc_sc[...] = jnp.zeros(acc_sc.shape, f32)

    n_steps = qb // per_step + 1
    last = n_steps - 1
    r = qb % per_step

    def scores(k):
        start = pl.multiple_of(jnp.minimum(k * ATT_STEP, seq - ATT_STEP), ATT_STEP)
        return jnp.dot(q4, kt_ref[:, pl.ds(start, ATT_STEP)], preferred_element_type=f32)

    def half(k, cur_ref, nxt_ref):
        nxt_ref[...] = scores(k + 1)
        tile = jnp.where(k == last, 1 + r, jnp.where((k == last - 1) & (r == 0), per_step + 1, 0))
        vv = vt_ref[:, pl.ds(pl.multiple_of(k * ATT_STEP, ATT_STEP), ATT_STEP)]
        for hh in range(2):
            bias = bias_ref[0, tile, hh * tq:(hh + 1) * tq, :]
            ps, alphas = [], []
            for c in range(2):
                rows = slice((2 * hh + c) * tq, (2 * hh + c + 1) * tq)
                s = cur_ref[rows, :] + bias
                m_old = m_sc[rows]
                m_new = jnp.maximum(m_old, jnp.max(s, axis=-1, keepdims=True))
                ps.append(jnp.exp2(s - m_new[:, 0:1]).astype(bf16))
                alphas.append(jnp.exp2(m_old - m_new))
                m_sc[rows] = m_new
            rows2 = slice(2 * hh * tq, (2 * hh + 2) * tq)
            vaug = jnp.where((feat_k // DA_V_DIM) == hh, vv, jnp.ones_like(vv))
            acc_sc[rows2] = (acc_sc[rows2] * jnp.concatenate(alphas, axis=0)
                             + lax.dot_general(jnp.concatenate(ps, axis=0), vaug, (((1,), (1,)), ((), ())),
                                               preferred_element_type=f32))

    sa_sc[...] = scores(0)

    def two_steps(kk, carry):
        half(2 * kk, sa_sc, sb_sc)

        @pl.when(2 * kk + 1 < n_steps)
        def _():
            half(2 * kk + 1, sb_sc, sa_sc)

        return carry

    lax.fori_loop(0, (n_steps + 1) // 2, two_steps, 0)

    lam = lam_ref[0]
    outs = []
    for hh in range(2):
        own = (lane // DA_V_DIM) == hh
        a1 = acc_sc[2 * hh * tq:(2 * hh + 1) * tq]
        a2 = acc_sc[(2 * hh + 1) * tq:(2 * hh + 2) * tq]
        l1 = jnp.max(jnp.where(own, 0.0, a1), axis=-1, keepdims=True)
        l2 = jnp.max(jnp.where(own, 0.0, a2), axis=-1, keepdims=True)
        o = a1 / l1 - lam * (a2 / l2)
        ssq = jnp.sum(jnp.where(own, o * o, 0.0), axis=-1, keepdims=True)
        outs.append(o * lax.rsqrt(ssq * (1.0 / DA_V_DIM) + EPS))
    o = jnp.where((lane // DA_V_DIM) == 0, outs[0], outs[1])
    o_ref[...] = (o * gain_ref[...]).astype(o_ref.dtype)


def _attn_prompt(q, kt, vt, bias_tiles, gain, lam, batch, seq):
    t, w = q.shape
    pairs = w // LANES
    nq = seq // ATT_BLOCK
    assert seq % ATT_STEP == 0
    n_tiles = ATT_STEP // ATT_BLOCK + 2
    return pl.pallas_call(
        _attn_prompt_kernel,
        grid=(batch, pairs, nq),
        in_specs=[
            pl.BlockSpec(memory_space=pltpu.SMEM),
            pl.BlockSpec((ATT_BLOCK, LANES), lambda b, p, i: (b * nq + i, p)),
            pl.BlockSpec((LANES, seq), lambda b, p, i: (p, b)),
            pl.BlockSpec((LANES, seq), lambda b, p, i: (p, b)),
            pl.BlockSpec((1, n_tiles, 2 * ATT_BLOCK, ATT_STEP), lambda b, p, i: (p, 0, 0, 0)),
            pl.BlockSpec((1, LANES), lambda b, p, i: (0, 0)),
        ],
        out_specs=pl.BlockSpec((ATT_BLOCK, LANES), lambda b, p, i: (b * nq + i, p)),
        out_shape=jax.ShapeDtypeStruct((t, w), bf16),
        scratch_shapes=[
            pltpu.VMEM((4 * ATT_BLOCK, LANES), f32),
            pltpu.VMEM((4 * ATT_BLOCK, LANES), f32),
            pltpu.VMEM((4 * ATT_BLOCK, ATT_STEP), f32),
            pltpu.VMEM((4 * ATT_BLOCK, ATT_STEP), f32),
        ],
        compiler_params=_cparams("parallel", "parallel", "arbitrary"),
        name="attn_prompt",
    )(lam, q, kt, vt, bias_tiles, gain)


def _prompt_bias_tiles(rel_bias):
    h = rel_bias.shape[1]
    per_step = ATT_STEP // ATT_BLOCK
    i = jnp.arange(ATT_BLOCK)
    rel_diag = i[None, :] - i[:, None]
    rel_prev = rel_diag - ATT_BLOCK
    far = rel_bias[REL_BUCKETS // 2 - 1].astype(f32)
    b_diag = (_bias_of(rel_diag, rel_bias) - far[:, None, None]) * LOG2E
    b_prev = (_bias_of(rel_prev, rel_bias) - far[:, None, None]) * LOG2E
    mask = (i[None, :] // CHUNK) <= (i[:, None] // CHUNK)
    b_diag = jnp.where(mask[None], b_diag, NEG)
    zero = jnp.zeros_like(b_diag)
    dead = jnp.full_like(b_diag, NEG)
    tiles = [jnp.concatenate([zero] * per_step, axis=-1)]
    for r in range(per_step):
        blocks = [zero if j < r - 1 else b_prev if j == r - 1 else b_diag if j == r else dead
                  for j in range(per_step)]
        tiles.append(jnp.concatenate(blocks, axis=-1))
    tiles.append(jnp.concatenate([zero] * (per_step - 1) + [b_prev], axis=-1))
    tiles = jnp.stack(tiles, axis=1)
    tiles = tiles.reshape(h // 2, 2, per_step + 2, ATT_BLOCK, ATT_STEP)
    return jnp.transpose(tiles, (0, 2, 1, 3, 4)).reshape(h // 2, per_step + 2, 2 * ATT_BLOCK, ATT_STEP)


def _attn_sample_kernel(lam_ref, q_ref, kp_ref, vp_ref, kn_ref, vn_ref, bp_ref, bn_ref, gain_ref, o_ref):
    nq = q_ref.shape[0]
    pairs = q_ref.shape[1] // LANES
    lane = lax.broadcasted_iota(jnp.int32, (nq, LANES), 1)
    lam = lam_ref[0]
    nt = (((1,), (1,)), ((), ()))
    for p in range(pairs):
        sl = slice(p * LANES, (p + 1) * LANES)
        qs = q_ref[:, sl].astype(f32) * (DA_HEAD_DIM ** -0.5 * LOG2E)
        kp = kp_ref[sl, :].astype(bf16)
        vp = vp_ref[sl, :].astype(bf16)
        kn = kn_ref[:, sl].astype(bf16)
        vn = vn_ref[:, sl].astype(bf16)
        outs = []
        for hh in range(2):
            h = 2 * p + hh
            own = (lane // DA_V_DIM) == hh
            res = []
            for c in range(2):
                qm = jnp.where((lane // DA_HEAD_DIM) == 2 * hh + c, qs, 0.0).astype(bf16)
                sp = jnp.dot(qm, kp, preferred_element_type=f32) + bp_ref[h]
                sn = lax.dot_general(qm, kn, nt, preferred_element_type=f32) + bn_ref[h]
                m = jnp.maximum(jnp.max(sp, axis=-1, keepdims=True), jnp.max(sn, axis=-1, keepdims=True))
                pp = jnp.exp2(sp - m)
                pn = jnp.exp2(sn - m)
                l = jnp.sum(pp, axis=-1, keepdims=True) + jnp.sum(pn, axis=-1, keepdims=True)
                pv = (lax.dot_general(pp.astype(bf16), vp, nt, preferred_element_type=f32)
                      + jnp.dot(pn.astype(bf16), vn, preferred_element_type=f32))
                res.append(pv / l)
            o = res[0] - lam * res[1]
            ssq = jnp.sum(jnp.where(own, o * o, 0.0), axis=-1, keepdims=True)
            outs.append(o * lax.rsqrt(ssq * (1.0 / DA_V_DIM) + EPS))
        o = jnp.where((lane // DA_V_DIM) == 0, outs[0], outs[1])
        o_ref[:, sl] = (o * gain_ref[...]).astype(o_ref.dtype)


def _attn_sample(q, k_new, v_new, k_past, v_past, bias_past, bias_new, gain, lam, batch, nq, past):
    w = q.shape[1]
    h = bias_past.shape[0]
    return pl.pallas_call(
        _attn_sample_kernel,
        grid=(batch,),
        in_specs=[
            pl.BlockSpec(memory_space=pltpu.SMEM),
            pl.BlockSpec((nq, w), lambda b: (b, 0)),
            pl.BlockSpec((w, past), lambda b: (b, 0)),
            pl.BlockSpec((w, past), lambda b: (b, 0)),
            pl.BlockSpec((nq, w), lambda b: (b, 0)),
            pl.BlockSpec((nq, w), lambda b: (b, 0)),
            pl.BlockSpec((h, nq, past), lambda b: (0, 0, 0)),
            pl.BlockSpec((h, nq, nq), lambda b: (0, 0, 0)),
            pl.BlockSpec((1, LANES), lambda b: (0, 0)),
        ],
        out_specs=pl.BlockSpec((nq, w), lambda b: (b, 0)),
        out_shape=jax.ShapeDtypeStruct((batch * nq, w), bf16),
        compiler_params=_cparams("parallel"),
        name="attn_sample",
    )(lam, q, k_past, v_past, k_new, v_new, bias_past, bias_new, gain)


def _hgrn_chunk(q, f_logit, vv, g_logit, lb, gain, s0, lc):
    n_sub = lc // HG_SUB
    f = lb + (1.0 - lb) * jax.nn.sigmoid(f_logit)
    logf = jnp.log(f)
    k = 1.0 - f
    row = lax.broadcasted_iota(jnp.int32, (lc, lc), 0)
    col = lax.broadcasted_iota(jnp.int32, (lc, lc), 1)
    tril = (col <= row).astype(f32)
    b = jnp.dot(tril, logf, precision=lax.Precision.HIGHEST, preferred_element_type=f32)
    b_last = b[lc - 1:lc, :]

    o = jnp.dot((q * jnp.exp(b)).astype(bf16), s0.astype(bf16), preferred_element_type=f32)
    k_hat = k * jnp.exp(b_last - b)
    bt = b.T
    decay_col = jnp.exp(bt[:, lc - 1:lc])
    s_new = decay_col * s0 + jnp.dot(k_hat.T.astype(bf16), vv.astype(bf16), preferred_element_type=f32)

    if n_sub > 1:
        rows = []
        nt = (((1,), (1,)), ((), ()))
        for i in range(n_sub):
            lo = i * HG_SUB
            if i == 0:
                rows.append(jnp.zeros((HG_SUB, lc), f32))
                continue
            beta = b[lo - 1:lo, :]
            q_t = q[lo:lo + HG_SUB, :] * jnp.exp(b[lo:lo + HG_SUB, :] - beta)
            k_t = k * jnp.exp(jnp.minimum(beta - b, 0.0))
            rows.append(lax.dot_general(q_t.astype(bf16), k_t.astype(bf16), nt, preferred_element_type=f32))
        a_off = jnp.concatenate(rows, axis=0)
        a_off = jnp.where(col < (row // HG_SUB) * HG_SUB, a_off, 0.0)
        o = o + jnp.dot(a_off.astype(bf16), vv.astype(bf16), preferred_element_type=f32)

    ones = jnp.ones((HG_DIM, LANES), bf16)
    srow = lax.broadcasted_iota(jnp.int32, (HG_SUB, HG_DIM), 0)
    orow = lax.broadcasted_iota(jnp.int32, (HG_SUB, HG_DIM), 0)
    blocks = []
    for i in range(n_sub):
        lo = i * HG_SUB
        b_i = b[lo:lo + HG_SUB, :]
        k_i = k[lo:lo + HG_SUB, :]
        v_i = vv[lo:lo + HG_SUB, :]
        q_i = q[lo:lo + HG_SUB, :]
        d_rows = []
        for t in range(HG_SUB):
            arg = jnp.where(srow <= t, b_i[t:t + 1, :] - b_i, -jnp.inf)
            d_rows.append(q_i[t:t + 1, :] * k_i * jnp.exp(arg))
        d3 = jnp.concatenate(d_rows, axis=0)
        a_rep = jnp.dot(d3.astype(bf16), ones, preferred_element_type=f32)
        o_i = jnp.zeros((HG_SUB, HG_DIM), f32)
        for t in range(HG_SUB):
            o_t = jnp.sum(a_rep[t * HG_SUB:(t + 1) * HG_SUB, :] * v_i, axis=0, keepdims=True)
            o_i = jnp.where(orow == t, o_t, o_i)
        blocks.append(o_i)
    o = o + (jnp.concatenate(blocks, axis=0) if n_sub > 1 else blocks[0])

    on = o * lax.rsqrt(jnp.mean(o * o, axis=-1, keepdims=True) + EPS) * gain
    gate = g_logit * jax.nn.sigmoid(g_logit)
    return on * gate, s_new


def _hgrn_kernel(hz_ref, lb_ref, gain_ref, s0_ref, o_ref, s_ref, st_sc, *, lc, heads):
    j = pl.program_id(1)
    w = heads * HG_DIM

    @pl.when(j == 0)
    def _():
        st_sc[...] = s0_ref[0]

    n_chunks = hz_ref.shape[0] // lc
    for c in range(n_chunks):
        rs = slice(c * lc, (c + 1) * lc)
        for h in range(heads):
            cs = slice(h * HG_DIM, (h + 1) * HG_DIM)
            o, s_new = _hgrn_chunk(
                hz_ref[rs, h * HG_DIM:(h + 1) * HG_DIM],
                hz_ref[rs, w + h * HG_DIM:w + (h + 1) * HG_DIM],
                hz_ref[rs, 2 * w + h * HG_DIM:2 * w + (h + 1) * HG_DIM],
                hz_ref[rs, 3 * w + h * HG_DIM:3 * w + (h + 1) * HG_DIM],
                lb_ref[:, cs], gain_ref[...], st_sc[h], lc)
            st_sc[h] = s_new
            o_ref[rs, cs] = o.astype(o_ref.dtype)

    @pl.when(j == pl.num_programs(1) - 1)
    def _():
        s_ref[0] = st_sc[...]


def _hgrn(hz, lb, gain, s0, batch, seq, lc, chunks_per_step):
    t = hz.shape[0]
    w = hz.shape[1] // 4
    heads = w // HG_DIM
    rows = lc * chunks_per_step
    steps = seq // rows
    return pl.pallas_call(
        functools.partial(_hgrn_kernel, lc=lc, heads=heads),
        grid=(batch, steps),
        in_specs=[
            pl.BlockSpec((rows, 4 * w), lambda b, j: (b * steps + j, 0)),
            pl.BlockSpec((1, w), lambda b, j: (0, 0)),
            pl.BlockSpec((1, HG_DIM), lambda b, j: (0, 0)),
            pl.BlockSpec((1, heads, HG_DIM, HG_DIM), lambda b, j: (b, 0, 0, 0)),
        ],
        out_specs=[
            pl.BlockSpec((rows, w), lambda b, j: (b * steps + j, 0)),
            pl.BlockSpec((1, heads, HG_DIM, HG_DIM), lambda b, j: (b, 0, 0, 0)),
        ],
        out_shape=[
            jax.ShapeDtypeStruct((t, w), bf16),
            jax.ShapeDtypeStruct((batch, heads, HG_DIM, HG_DIM), f32),
        ],
        scratch_shapes=[pltpu.VMEM((heads, HG_DIM, HG_DIM), f32)],
        compiler_params=_cparams("parallel", "arbitrary"),
        name="hgrn2",
    )(hz, lb, gain, s0)


def _top_rows(vals, k, payload=None):
    n_rows = vals.shape[0]
    row = lax.broadcasted_iota(jnp.int32, vals.shape, 0)
    out_v, out_i = [], []
    for _ in range(k):
        m = jnp.max(vals, axis=0, keepdims=True)
        first = jnp.min(jnp.where(vals == m, row, n_rows), axis=0, keepdims=True)
        hit = row == first
        out_v.append(m)
        if payload is None:
            out_i.append(first)
        else:
            out_i.append(jnp.max(jnp.where(hit, payload, -1), axis=0, keepdims=True))
        vals = jnp.where(hit, -jnp.inf, vals)
    return jnp.concatenate(out_v, axis=0), jnp.concatenate(out_i, axis=0)


def _retrieve_kernel(oa_ref, ob_ref, x_ref, wo_ref, g_ref, wqt_ref, sk_ref,
                     hp_ref, xn_ref, idx_ref, gate_ref, qt_sc, s1v_sc, s1i_sc, idx_sc):
    half = oa_ref.shape[1]
    y = (jnp.dot(oa_ref[...], wo_ref[0:half, :], preferred_element_type=f32)
         + jnp.dot(ob_ref[...], wo_ref[half:, :], preferred_element_type=f32))
    hp = x_ref[...] + y
    hp_ref[...] = hp
    xn = hp * lax.rsqrt(jnp.mean(hp * hp, axis=-1, keepdims=True) + EPS) * g_ref[...]
    xn_ref[...] = xn
    nt = (((1,), (1,)), ((), ()))
    qt_sc[...] = lax.dot_general(wqt_ref[...], xn.astype(bf16), nt, preferred_element_type=f32).astype(bf16)

    n_half = sk_ref.shape[0]

    def stage1(hc, carry):
        start = pl.multiple_of(hc * PEER_SUB_DIM, PEER_SUB_DIM)
        s = jnp.dot(sk_ref[hc], qt_sc[pl.ds(start, PEER_SUB_DIM), :], preferred_element_type=f32)
        v, i = _top_rows(s, PEER_TOPK)
        s1v_sc[hc] = v
        s1i_sc[hc] = i
        return carry

    lax.fori_loop(0, n_half, stage1, 0)

    tm = x_ref.shape[0]
    sub = lax.broadcasted_iota(jnp.int32, (8, tm), 0)

    def stage2(h, carry):
        v1, i1 = s1v_sc[2 * h], s1i_sc[2 * h]
        v2, i2 = s1v_sc[2 * h + 1], s1i_sc[2 * h + 1]
        cv, ci = [], []
        for i in range(PEER_TOPK):
            nj = PEER_TOPK // (i + 1)
            rows = PEER_TOPK if nj > 8 else 8
            a = v1[i:i + 1, :] + v2[0:rows, :]
            e = i1[i:i + 1, :] * PEER_KEYS + i2[0:rows, :]
            if nj < rows:
                a = jnp.where(sub < nj, a, -jnp.inf)
            cv.append(a)
            ci.append(e)
        top_s, eidx = _top_rows(jnp.concatenate(cv, axis=0), PEER_TOPK, jnp.concatenate(ci, axis=0))
        p = jnp.exp(top_s - top_s[0:1, :])
        g = p / jnp.sum(p, axis=0, keepdims=True)
        r0 = pl.multiple_of(h * PEER_TOPK, PEER_TOPK)
        gate_ref[pl.ds(r0, PEER_TOPK), :] = g
        idx_sc[pl.ds(r0, PEER_TOPK), :] = eidx
        return carry

    lax.fori_loop(0, n_half // 2, stage2, 0)
    idx_ref[...] = pltpu.bitcast(pltpu.bitcast(idx_sc[...], f32).T, jnp.int32)


def _retrieve(oa, ob, x, wo_bf, g2, wqt_bf, sk_bf, tm, x_block0=0):
    t, half = oa.shape
    d = x.shape[1]
    e = wqt_bf.shape[0]
    n_half = sk_bf.shape[0]
    n_sel = (n_half // 2) * PEER_TOPK
    return pl.pallas_call(
        _retrieve_kernel,
        grid=(t // tm,),
        in_specs=[
            pl.BlockSpec((tm, half), lambda i: (i, 0)),
            pl.BlockSpec((tm, half), lambda i: (i, 0)),
            pl.BlockSpec((tm, d), lambda i: (x_block0 + i, 0)),
            pl.BlockSpec((2 * half, d), lambda i: (0, 0)),
            pl.BlockSpec((1, d), lambda i: (0, 0)),
            pl.BlockSpec((e, d), lambda i: (0, 0)),
            pl.BlockSpec((n_half, PEER_KEYS, PEER_SUB_DIM), lambda i: (0, 0, 0)),
        ],
        out_specs=[
            pl.BlockSpec((tm, d), lambda i: (i, 0)),
            pl.BlockSpec((tm, d), lambda i: (i, 0)),
            pl.BlockSpec((tm, n_sel), lambda i: (i, 0)),
            pl.BlockSpec((n_sel, tm), lambda i: (0, i)),
        ],
        out_shape=[
            jax.ShapeDtypeStruct((t, d), f32),
            jax.ShapeDtypeStruct((t, d), f32),
            jax.ShapeDtypeStruct((t, n_sel), jnp.int32),
            jax.ShapeDtypeStruct((n_sel, t), f32),
        ],
        scratch_shapes=[
            pltpu.VMEM((e, tm), bf16),
            pltpu.VMEM((n_half, PEER_TOPK, tm), f32),
            pltpu.VMEM((n_half, PEER_TOPK, tm), jnp.int32),
            pltpu.VMEM((n_sel, tm), jnp.int32),
        ],
        compiler_params=_cparams("parallel"),
        name="retrieve",
    )(oa, ob, x, wo_bf, g2, wqt_bf, sk_bf)


def _gather_rows(table, idx):
    n = idx.shape[0]
    width = table.shape[1]
    idx2 = idx.reshape(n // GATHER_WINDOW, GATHER_WINDOW)
    mesh = plsc.VectorSubcoreMesh(core_axis_name="c", subcore_axis_name="s")

    @functools.partial(pl.kernel, out_type=jax.ShapeDtypeStruct((n, width), table.dtype), mesh=mesh)
    def gather(t_hbm, i_hbm, o_hbm):
        def body(i_vmem, o_vmem):
            pltpu.sync_copy(t_hbm.at[i_vmem.at[0]], o_vmem)

        pltpu.emit_pipeline(
            body,
            grid=(n // GATHER_WINDOW,),
            in_specs=[pl.BlockSpec((1, GATHER_WINDOW), lambda i: (i, 0))],
            out_specs=[pl.BlockSpec((GATHER_WINDOW, width), lambda i: (i, 0))],
            core_axis_name=("c", "s"),
            dimension_semantics=(pltpu.PARALLEL,),
            trace_scopes=False,
        )(i_hbm, o_hbm)

    return gather(table, idx2)


def _pack_table(tab):
    half = tab.shape[1] // 2
    lo = lax.bitcast_convert_type(tab[:, :half].astype(bf16), jnp.uint16).astype(jnp.uint32)
    hi = lax.bitcast_convert_type(tab[:, half:].astype(bf16), jnp.uint16).astype(jnp.uint32)
    return lax.bitcast_convert_type(lo | (hi << 16), jnp.int32)


def _split_bf16(x):
    hi = x.astype(bf16)
    return hi, (x - hi.astype(f32)).astype(bf16)


def _mix_kernel(ug_ref, vg_ref, xn_ref, hp_ref, gate_ref, gf_ref, yin_ref, y_ref, wt_sc, peer_sc):
    del yin_ref
    i = pl.program_id(0)
    n_tok = xn_ref.shape[0]
    n_sel = gate_ref.shape[0]
    half = ug_ref.shape[1]
    lane0 = (i % (LANES // n_tok)) * n_tok
    nt = (((1,), (1,)), ((), ()))

    xb = xn_ref[...].astype(bf16)
    place = (lax.broadcasted_iota(jnp.int32, (LANES, n_tok), 0)
             == lane0 + lax.broadcasted_iota(jnp.int32, (LANES, n_tok), 1)).astype(bf16)
    xp = jnp.concatenate([jnp.dot(place, xb[:, :half], preferred_element_type=f32).astype(bf16),
                          jnp.dot(place, xb[:, half:], preferred_element_type=f32).astype(bf16)], axis=0)

    row2 = lax.broadcasted_iota(jnp.int32, (2 * n_sel, LANES), 0)
    lane2 = lax.broadcasted_iota(jnp.int32, (2 * n_sel, LANES), 1)
    even = (row2 % 2) == 0
    act2 = jnp.zeros((2 * n_sel, LANES), f32)
    for t in range(n_tok):
        rows = pltpu.bitcast(ug_ref[t * n_sel:(t + 1) * n_sel, :], bf16)
        r = lax.dot_general(rows, xp, nt, preferred_element_type=f32)
        act2 = jnp.where(lane2 == lane0 + t, jnp.where(even, r[:, :LANES], r[:, LANES:]), act2)

    pair = (lax.broadcasted_iota(jnp.int32, (n_sel, 2 * n_sel), 1) // 2
            == lax.broadcasted_iota(jnp.int32, (n_sel, 2 * n_sel), 0)).astype(bf16)
    a_hi, a_lo = _split_bf16(act2)
    act = jnp.dot(pair, a_hi, preferred_element_type=f32) + jnp.dot(pair, a_lo, preferred_element_type=f32)
    w = gate_ref[...] * (0.5 * act * (1.0 + lax.erf(act * (2.0 ** -0.5))))

    w_hi, w_lo = _split_bf16(w.T)
    wt_sc[0] = jnp.dot(w_hi, pair, preferred_element_type=f32)
    wt_sc[1] = jnp.dot(w_lo, pair, preferred_element_type=f32)
    rowl = lax.broadcasted_iota(jnp.int32, (16, 2 * n_sel), 0)
    keep = (rowl < 4) & ((lax.broadcasted_iota(jnp.int32, (16, 2 * n_sel), 1) % 2) == (rowl % 2))
    for t in range(n_tok):
        whi = wt_sc[0, pl.ds(lane0 + t, 1), :]
        wlo = wt_sc[1, pl.ds(lane0 + t, 1), :]
        lhs = jnp.where(keep, jnp.where(rowl < 2, whi, wlo), 0.0).astype(bf16)
        rows = pltpu.bitcast(vg_ref[t * n_sel:(t + 1) * n_sel, :], bf16)
        res = jnp.dot(lhs, rows, preferred_element_type=f32)
        peer_sc[t:t + 1, 0:half] = res[0:1] + res[2:3]
        peer_sc[t:t + 1, half:] = res[1:2] + res[3:4]

    h = hp_ref[...] + peer_sc[...]
    y_ref[...] = h * lax.rsqrt(jnp.mean(h * h, axis=-1, keepdims=True) + EPS) * gf_ref[...]


def _mix(ug, vg, xn, hp, gate, gf, y_all, y_block0):
    t, d = xn.shape
    n_sel = gate.shape[0]
    half = ug.shape[1]
    nt = PEER_TOKENS
    per_lane_block = LANES // nt
    return pl.pallas_call(
        _mix_kernel,
        grid=(t // nt,),
        in_specs=[
            pl.BlockSpec((nt * n_sel, half), lambda i: (i, 0)),
            pl.BlockSpec((nt * n_sel, half), lambda i: (i, 0)),
            pl.BlockSpec((nt, d), lambda i: (i, 0)),
            pl.BlockSpec((nt, d), lambda i: (i, 0)),
            pl.BlockSpec((n_sel, LANES), lambda i: (0, i // per_lane_block)),
            pl.BlockSpec((1, d), lambda i: (0, 0)),
            pl.BlockSpec(memory_space=pl.ANY),
        ],
        out_specs=pl.BlockSpec((nt, d), lambda i: (y_block0 + i, 0)),
        out_shape=jax.ShapeDtypeStruct(y_all.shape, f32),
        input_output_aliases={6: 0},
        scratch_shapes=[
            pltpu.VMEM((2, LANES, 2 * n_sel), f32),
            pltpu.VMEM((nt, d), f32),
        ],
        compiler_params=_cparams("arbitrary"),
        name="peer_mix",
    )(ug, vg, xn, hp, gate, gf, y_all)


def _channel_mixer(oa, ob, x, x_block0, wo_bf, g2, wqt_bf, sk_bf, u_pk, v_pk, gf, tm, y_all, y_block0):
    hp, xn, eidx, gate = _retrieve(oa, ob, x, wo_bf, g2, wqt_bf, sk_bf, tm, x_block0)
    flat = eidx.reshape(-1)
    ug = _gather_rows(u_pk, flat)
    vg = _gather_rows(v_pk, flat)
    return _mix(ug, vg, xn, hp, gate, gf, y_all, y_block0)


def kernel(x_prompt, x_sample, cache_k, cache_v, state_hgrn, norm1, w_in, da_lq1, da_lk1, da_lq2, da_lk2,
           da_out_norm, hg_lb_logits, hg_out_norm, w_out, rel_bias, norm2, peer_w_q, peer_sub_keys,
           peer_u, peer_v, final_norm):
    batch, seq, d = x_prompt.shape
    dbatch, dseq, _ = x_sample.shape
    past = cache_k.shape[2]
    da_heads = cache_k.shape[3]
    hg_heads = state_hgrn.shape[2]
    depth = w_in.shape[0]
    assert depth == 1 and seq % ATT_BLOCK == 0 and seq % CHUNK == 0 and dseq % HG_SUB == 0 and dseq <= CHUNK
    assert past % CHUNK == 0 and (past + dseq - 1) // CHUNK == past // CHUNK

    l = 0
    lam_init = 0.8 - 0.6 * math.exp(-0.3 * l)
    lam = (jnp.exp(jnp.sum(da_lq1[l].astype(f32) * da_lk1[l].astype(f32)))
           - jnp.exp(jnp.sum(da_lq2[l].astype(f32) * da_lk2[l].astype(f32))) + lam_init).reshape(1)
    lb = jnp.cumsum(jax.nn.softmax(hg_lb_logits.astype(f32), axis=0), axis=0)[l].reshape(1, -1)
    da_gain = (jnp.tile(da_out_norm[l].astype(f32), 2) * (1.0 - lam_init)).reshape(1, LANES)
    hg_gain = hg_out_norm[l].astype(f32).reshape(1, HG_DIM)
    g1 = norm1[l].astype(f32).reshape(1, d)
    g2 = norm2[l].astype(f32).reshape(1, d)
    gf = final_norm.astype(f32).reshape(1, d)
    w_in_bf = w_in[l].astype(bf16)
    wkv_t_bf = w_in[l][:, cache_k.shape[3] * cache_k.shape[4]:3 * cache_k.shape[3] * cache_k.shape[4]].T.astype(bf16)
    wo_bf = w_out[l].astype(bf16)
    wqt_bf = peer_w_q[l].T.astype(bf16)
    sk_bf = peer_sub_keys[l].reshape(-1, PEER_KEYS, PEER_SUB_DIM).astype(bf16)
    u_pk = _pack_table(peer_u[l])
    v_pk = _pack_table(peer_v[l])

    bias_tiles = _prompt_bias_tiles(rel_bias)
    q_pos = past + jnp.arange(dseq)
    bias_s = _bias_of(jnp.arange(past + dseq)[None, :] - q_pos[:, None], rel_bias) * LOG2E
    bias_past, bias_new = bias_s[:, :, :past], bias_s[:, :, past:]
    zero_state = jnp.zeros((1, hg_heads, HG_DIM, HG_DIM), f32)

    dh2 = 2 * DA_HEAD_DIM
    wq = da_heads * dh2
    tm = 256
    x_all = x_prompt.reshape(batch * seq, d)
    kt_all = jnp.zeros((batch * wq, seq), f32)
    vt_all = jnp.zeros((batch * wq, seq), f32)
    y_all = jnp.zeros((batch * seq, d), f32)
    s_p = []
    for b in range(batch):
        q, kt, vt, hz, kt_all, vt_all = _inproj_prompt(x_all, g1, w_in_bf, wkv_t_bf, kt_all, vt_all, b, seq, tm)
        oa = _attn_prompt(q, kt, vt, bias_tiles, da_gain, lam, 1, seq)
        ob, s_new = _hgrn(hz, lb, hg_gain, zero_state, 1, seq, CHUNK, 4)
        y_all = _channel_mixer(oa, ob, x_all, b * (seq // tm), wo_bf, g2, wqt_bf, sk_bf, u_pk, v_pk, gf, tm,
                               y_all, b * (seq // PEER_TOKENS))
        s_p.append(s_new)

    xs = x_sample.reshape(dbatch * dseq, d)
    q, k_s, v_s, hz = _inproj_sample(xs, g1, w_in_bf, wq)
    kp_t = jnp.transpose(cache_k[l], (0, 2, 3, 1)).reshape(dbatch * wq, past)
    vp_t = jnp.transpose(cache_v[l], (0, 2, 3, 1)).reshape(dbatch * wq, past)
    oa = _attn_sample(q, k_s, v_s, kp_t, vp_t, bias_past, bias_new, da_gain, lam, dbatch, dseq, past)
    ob, s_s = _hgrn(hz, lb, hg_gain, state_hgrn[l].astype(f32), dbatch, dseq, dseq, 1)
    y_s = _channel_mixer(oa, ob, xs, 0, wo_bf, g2, wqt_bf, sk_bf, u_pk, v_pk, gf, dbatch * dseq,
                         jnp.zeros((dbatch * dseq, d), f32), 0)

    y_prompt = y_all.reshape(batch, seq, d)
    y_sample = y_s.reshape(dbatch, dseq, d)
    k_prompt = jnp.transpose(kt_all.reshape(batch, da_heads, dh2, seq), (0, 3, 1, 2))[None]
    v_prompt = jnp.transpose(vt_all.reshape(batch, da_heads, DA_V_DIM, seq), (0, 3, 1, 2))[None]
    state_prompt = jnp.concatenate(s_p, axis=0)[None].astype(state_hgrn.dtype)
    k_sample = k_s.reshape(1, dbatch, dseq, da_heads, dh2)
    v_sample = v_s.reshape(1, dbatch, dseq, da_heads, DA_V_DIM)
    state_sample = s_s[None].astype(state_hgrn.dtype)
    return (y_prompt, y_sample, k_prompt, v_prompt, state_prompt, k_sample, v_sample, state_sample)
```

```python
import functools
import math

import jax
import jax.numpy as jnp
from jax import lax
from jax.experimental import pallas as pl
from jax.experimental.pallas import tpu as pltpu
from jax.experimental.pallas import tpu_sc as plsc

CHUNK = 64
DA_HEAD_DIM = 32
DA_V_DIM = 2 * DA_HEAD_DIM
HG_DIM = 128
REL_BUCKETS = 32
REL_MAX_DIST = 128
PEER_HEADS = 8
PEER_KEYS = 128
PEER_TOPK = 16
PEER_SUB_DIM = 64
EPS = 1e-6
NEG = -1e30
LOG2E = 1.4426950408889634

LANES = 128
ATT_BLOCK = 128
ATT_STEP = 512
HG_SUB = 16
GATHER_WINDOW = 64
PEER_TOKENS = 16
VMEM_LIMIT = 56 * 1024 * 1024

f32 = jnp.float32
bf16 = jnp.bfloat16


def _cparams(*sem):
    return pltpu.CompilerParams(dimension_semantics=sem, vmem_limit_bytes=VMEM_LIMIT)


def _inproj_prompt_kernel(x_ref, g_ref, w_ref, wkv_t_ref, kin_ref, vin_ref, after_ref,
                          q_ref, kt_ref, vt_ref, hz_ref, ktf_ref, vtf_ref):
    del kin_ref, vin_ref
    del after_ref
    x = x_ref[...]
    xn = x * lax.rsqrt(jnp.mean(x * x, axis=-1, keepdims=True) + EPS) * g_ref[...]
    xb = xn.astype(bf16)
    wq = q_ref.shape[1]
    nt = (((1,), (1,)), ((), ()))
    q_ref[...] = jnp.dot(xb, w_ref[:, 0:wq], preferred_element_type=f32).astype(bf16)
    kt = lax.dot_general(wkv_t_ref[0:wq, :], xb, nt, preferred_element_type=f32)
    ktf_ref[...] = kt
    kt_ref[...] = kt.astype(bf16)
    vt = lax.dot_general(wkv_t_ref[wq:2 * wq, :], xb, nt, preferred_element_type=f32)
    vtf_ref[...] = vt
    vt_ref[...] = vt.astype(bf16)
    hz_ref[...] = jnp.dot(xb, w_ref[:, 3 * wq:], preferred_element_type=f32)


def _inproj_prompt(x, g, w_bf, wkv_t_bf, kt_all, vt_all, after, b, seq, tm):
    d = x.shape[1]
    e = w_bf.shape[1]
    wq = wkv_t_bf.shape[0] // 2
    steps = seq // tm
    return pl.pallas_call(
        _inproj_prompt_kernel,
        grid=(steps,),
        in_specs=[
            pl.BlockSpec((tm, d), lambda i: (b * steps + i, 0)),
            pl.BlockSpec((1, d), lambda i: (0, 0)),
            pl.BlockSpec((d, e), lambda i: (0, 0)),
            pl.BlockSpec((2 * wq, d), lambda i: (0, 0)),
            pl.BlockSpec(memory_space=pl.ANY),
            pl.BlockSpec(memory_space=pl.ANY),
            pl.BlockSpec(memory_space=pl.ANY),
        ],
        out_specs=[
            pl.BlockSpec((tm, wq), lambda i: (i, 0)),
            pl.BlockSpec((wq, tm), lambda i: (0, i)),
            pl.BlockSpec((wq, tm), lambda i: (0, i)),
            pl.BlockSpec((tm, e - 3 * wq), lambda i: (i, 0)),
            pl.BlockSpec((wq, tm), lambda i: (b, i)),
            pl.BlockSpec((wq, tm), lambda i: (b, i)),
        ],
        out_shape=[
            jax.ShapeDtypeStruct((seq, wq), bf16),
            jax.ShapeDtypeStruct((wq, seq), bf16),
            jax.ShapeDtypeStruct((wq, seq), bf16),
            jax.ShapeDtypeStruct((seq, e - 3 * wq), f32),
            jax.ShapeDtypeStruct(kt_all.shape, f32),
            jax.ShapeDtypeStruct(vt_all.shape, f32),
        ],
        input_output_aliases={4: 4, 5: 5},
        compiler_params=_cparams("parallel"),
        name="inproj",
    )(x, g, w_bf, wkv_t_bf, kt_all, vt_all, after)


def _inproj_sample_kernel(x_ref, g_ref, w_ref, q_ref, k_ref, v_ref, hz_ref):
    x = x_ref[...]
    xn = x * lax.rsqrt(jnp.mean(x * x, axis=-1, keepdims=True) + EPS) * g_ref[...]
    xb = xn.astype(bf16)
    wq = q_ref.shape[1]
    q_ref[...] = jnp.dot(xb, w_ref[:, 0:wq], preferred_element_type=f32).astype(bf16)
    k_ref[...] = jnp.dot(xb, w_ref[:, wq:2 * wq], preferred_element_type=f32)
    v_ref[...] = jnp.dot(xb, w_ref[:, 2 * wq:3 * wq], preferred_element_type=f32)
    hz_ref[...] = jnp.dot(xb, w_ref[:, 3 * wq:], preferred_element_type=f32)


def _inproj_sample(x, g, w_bf, wq):
    t, d = x.shape
    e = w_bf.shape[1]
    return pl.pallas_call(
        _inproj_sample_kernel,
        grid=(1,),
        in_specs=[
            pl.BlockSpec((t, d), lambda i: (0, 0)),
            pl.BlockSpec((1, d), lambda i: (0, 0)),
            pl.BlockSpec((d, e), lambda i: (0, 0)),
        ],
        out_specs=[
            pl.BlockSpec((t, wq), lambda i: (0, 0)),
            pl.BlockSpec((t, wq), lambda i: (0, 0)),
            pl.BlockSpec((t, wq), lambda i: (0, 0)),
            pl.BlockSpec((t, e - 3 * wq), lambda i: (0, 0)),
        ],
        out_shape=[
            jax.ShapeDtypeStruct((t, wq), bf16),
            jax.ShapeDtypeStruct((t, wq), f32),
            jax.ShapeDtypeStruct((t, wq), f32),
            jax.ShapeDtypeStruct((t, e - 3 * wq), f32),
        ],
        compiler_params=_cparams("arbitrary"),
        name="inproj_sample",
    )(x, g, w_bf)


def _rel_bucket(rel):
    nb = REL_BUCKETS // 2
    max_exact = nb // 2
    ret = jnp.where(rel > 0, nb, 0)
    n = jnp.abs(rel)
    nf = jnp.maximum(n, 1).astype(f32)
    large = max_exact + (jnp.log(nf / max_exact) / math.log(REL_MAX_DIST / max_exact)
                         * (nb - max_exact)).astype(jnp.int32)
    large = jnp.minimum(large, nb - 1)
    return ret + jnp.where(n < max_exact, n, large)


def _bias_of(rel, rel_bias):
    bucket = _rel_bucket(rel)
    out = jnp.zeros((rel_bias.shape[1],) + rel.shape, f32)
    for b in range(REL_BUCKETS):
        out = out + jnp.where(bucket[None] == b, rel_bias[b].astype(f32)[:, None, None], 0.0)
    return out


def _attn_prompt_kernel(lam_ref, q_ref, kt_ref, vt_ref, bias_ref, gain_ref, o_ref, m_sc, acc_sc, sa_sc, sb_sc):
    qb = pl.program_id(2)
    tq = q_ref.shape[0]
    seq = kt_ref.shape[1]
    per_step = ATT_STEP // ATT_BLOCK
    lane = lax.broadcasted_iota(jnp.int32, (tq, LANES), 1)
    feat_k = lax.broadcasted_iota(jnp.int32, (LANES, ATT_STEP), 0)
    qs = q_ref[...].astype(f32) * (DA_HEAD_DIM ** -0.5 * LOG2E)
    q4 = jnp.concatenate(
        [jnp.where((lane // DA_HEAD_DIM) == i, qs, 0.0).astype(bf16) for i in range(4)], axis=0)

    m_sc[...] = jnp.full(m_sc.shape, -jnp.inf, f32)
    acc_sc[...] = jnp.zeros(acc_sc.shape, f32)

    n_steps = qb // per_step + 1
    last = n_steps - 1
    r = qb % per_step

    def scores(k):
        start = pl.multiple_of(jnp.minimum(k * ATT_STEP, seq - ATT_STEP), ATT_STEP)
        return jnp.dot(q4, kt_ref[:, pl.ds(start, ATT_STEP)], preferred_element_type=f32)

    def half(k, cur_ref, nxt_ref):
        nxt_ref[...] = scores(k + 1)
        tile = jnp.where(k == last, 1 + r, jnp.where((k == last - 1) & (r == 0), per_step + 1, 0))
        vv = vt_ref[:, pl.ds(pl.multiple_of(k * ATT_STEP, ATT_STEP), ATT_STEP)]
        for hh in range(2):
            bias = bias_ref[0, tile, hh * tq:(hh + 1) * tq, :]
            ps, alphas = [], []
            for c in range(2):
                rows = slice((2 * hh + c) * tq, (2 * hh + c + 1) * tq)
                s = cur_ref[rows, :] + bias
                m_old = m_sc[rows]
                m_new = jnp.maximum(m_old, jnp.max(s, axis=-1, keepdims=True))
                ps.append(jnp.exp2(s - m_new[:, 0:1]).astype(bf16))
                alphas.append(jnp.exp2(m_old - m_new))
                m_sc[rows] = m_new
            rows2 = slice(2 * hh * tq, (2 * hh + 2) * tq)
            vaug = jnp.where((feat_k // DA_V_DIM) == hh, vv, jnp.ones_like(vv))
            acc_sc[rows2] = (acc_sc[rows2] * jnp.concatenate(alphas, axis=0)
                             + lax.dot_general(jnp.concatenate(ps, axis=0), vaug, (((1,), (1,)), ((), ())),
                                               preferred_element_type=f32))

    sa_sc[...] = scores(0)

    def two_steps(kk, carry):
        half(2 * kk, sa_sc, sb_sc)

        @pl.when(2 * kk + 1 < n_steps)
        def _():
            half(2 * kk + 1, sb_sc, sa_sc)

        return carry

    lax.fori_loop(0, (n_steps + 1) // 2, two_steps, 0)

    lam = lam_ref[0]
    outs = []
    for hh in range(2):
        own = (lane // DA_V_DIM) == hh
        a1 = acc_sc[2 * hh * tq:(2 * hh + 1) * tq]
        a2 = acc_sc[(2 * hh + 1) * tq:(2 * hh + 2) * tq]
        l1 = jnp.max(jnp.where(own, 0.0, a1), axis=-1, keepdims=True)
        l2 = jnp.max(jnp.where(own, 0.0, a2), axis=-1, keepdims=True)
        o = a1 / l1 - lam * (a2 / l2)
        ssq = jnp.sum(jnp.where(own, o * o, 0.0), axis=-1, keepdims=True)
        outs.append(o * lax.rsqrt(ssq * (1.0 / DA_V_DIM) + EPS))
    o = jnp.where((lane // DA_V_DIM) == 0, outs[0], outs[1])
    o_ref[...] = (o * gain_ref[...]).astype(o_ref.dtype)


def _attn_prompt(q, kt, vt, bias_tiles, gain, lam, batch, seq):
    t, w = q.shape
    pairs = w // LANES
    nq = seq // ATT_BLOCK
    assert seq % ATT_STEP == 0
    n_tiles = ATT_STEP // ATT_BLOCK + 2
    return pl.pallas_call(
        _attn_prompt_kernel,
        grid=(batch, pairs, nq),
        in_specs=[
            pl.BlockSpec(memory_space=pltpu.SMEM),
            pl.BlockSpec((ATT_BLOCK, LANES), lambda b, p, i: (b * nq + i, p)),
            pl.BlockSpec((LANES, seq), lambda b, p, i: (p, b)),
            pl.BlockSpec((LANES, seq), lambda b, p, i: (p, b)),
            pl.BlockSpec((1, n_tiles, 2 * ATT_BLOCK, ATT_STEP), lambda b, p, i: (p, 0, 0, 0)),
            pl.BlockSpec((1, LANES), lambda b, p, i: (0, 0)),
        ],
        out_specs=pl.BlockSpec((ATT_BLOCK, LANES), lambda b, p, i: (b * nq + i, p)),
        out_shape=jax.ShapeDtypeStruct((t, w), bf16),
        scratch_shapes=[
            pltpu.VMEM((4 * ATT_BLOCK, LANES), f32),
            pltpu.VMEM((4 * ATT_BLOCK, LANES), f32),
            pltpu.VMEM((4 * ATT_BLOCK, ATT_STEP), f32),
            pltpu.VMEM((4 * ATT_BLOCK, ATT_STEP), f32),
        ],
        compiler_params=_cparams("parallel", "parallel", "arbitrary"),
        name="attn_prompt",
    )(lam, q, kt, vt, bias_tiles, gain)


def _prompt_bias_tiles(rel_bias):
    h = rel_bias.shape[1]
    per_step = ATT_STEP // ATT_BLOCK
    i = jnp.arange(ATT_BLOCK)
    rel_diag = i[None, :] - i[:, None]
    rel_prev = rel_diag - ATT_BLOCK
    far = rel_bias[REL_BUCKETS // 2 - 1].astype(f32)
    b_diag = (_bias_of(rel_diag, rel_bias) - far[:, None, None]) * LOG2E
    b_prev = (_bias_of(rel_prev, rel_bias) - far[:, None, None]) * LOG2E
    mask = (i[None, :] // CHUNK) <= (i[:, None] // CHUNK)
    b_diag = jnp.where(mask[None], b_diag, NEG)
    zero = jnp.zeros_like(b_diag)
    dead = jnp.full_like(b_diag, NEG)
    tiles = [jnp.concatenate([zero] * per_step, axis=-1)]
    for r in range(per_step):
        blocks = [zero if j < r - 1 else b_prev if j == r - 1 else b_diag if j == r else dead
                  for j in range(per_step)]
        tiles.append(jnp.concatenate(blocks, axis=-1))
    tiles.append(jnp.concatenate([zero] * (per_step - 1) + [b_prev], axis=-1))
    tiles = jnp.stack(tiles, axis=1)
    tiles = tiles.reshape(h // 2, 2, per_step + 2, ATT_BLOCK, ATT_STEP)
    return jnp.transpose(tiles, (0, 2, 1, 3, 4)).reshape(h // 2, per_step + 2, 2 * ATT_BLOCK, ATT_STEP)


def _attn_sample_kernel(lam_ref, q_ref, kp_ref, vp_ref, kn_ref, vn_ref, bp_ref, bn_ref, gain_ref, o_ref):
    nq = q_ref.shape[0]
    pairs = q_ref.shape[1] // LANES
    lane = lax.broadcasted_iota(jnp.int32, (nq, LANES), 1)
    lam = lam_ref[0]
    nt = (((1,), (1,)), ((), ()))
    for p in range(pairs):
        sl = slice(p * LANES, (p + 1) * LANES)
        qs = q_ref[:, sl].astype(f32) * (DA_HEAD_DIM ** -0.5 * LOG2E)
        kp = kp_ref[sl, :].astype(bf16)
        vp = vp_ref[sl, :].astype(bf16)
        kn = kn_ref[:, sl].astype(bf16)
        vn = vn_ref[:, sl].astype(bf16)
        outs = []
        for hh in range(2):
            h = 2 * p + hh
            own = (lane // DA_V_DIM) == hh
            res = []
            for c in range(2):
                qm = jnp.where((lane // DA_HEAD_DIM) == 2 * hh + c, qs, 0.0).astype(bf16)
                sp = jnp.dot(qm, kp, preferred_element_type=f32) + bp_ref[h]
                sn = lax.dot_general(qm, kn, nt, preferred_element_type=f32) + bn_ref[h]
                m = jnp.maximum(jnp.max(sp, axis=-1, keepdims=True), jnp.max(sn, axis=-1, keepdims=True))
                pp = jnp.exp2(sp - m)
                pn = jnp.exp2(sn - m)
                l = jnp.sum(pp, axis=-1, keepdims=True) + jnp.sum(pn, axis=-1, keepdims=True)
                pv = (lax.dot_general(pp.astype(bf16), vp, nt, preferred_element_type=f32)
                      + jnp.dot(pn.astype(bf16), vn, preferred_element_type=f32))
                res.append(pv / l)
            o = res[0] - lam * res[1]
            ssq = jnp.sum(jnp.where(own, o * o, 0.0), axis=-1, keepdims=True)
            outs.append(o * lax.rsqrt(ssq * (1.0 / DA_V_DIM) + EPS))
        o = jnp.where((lane // DA_V_DIM) == 0, outs[0], outs[1])
        o_ref[:, sl] = (o * gain_ref[...]).astype(o_ref.dtype)


def _attn_sample(q, k_new, v_new, k_past, v_past, bias_past, bias_new, gain, lam, batch, nq, past):
    w = q.shape[1]
    h = bias_past.shape[0]
    return pl.pallas_call(
        _attn_sample_kernel,
        grid=(batch,),
        in_specs=[
            pl.BlockSpec(memory_space=pltpu.SMEM),
            pl.BlockSpec((nq, w), lambda b: (b, 0)),
            pl.BlockSpec((w, past), lambda b: (b, 0)),
            pl.BlockSpec((w, past), lambda b: (b, 0)),
            pl.BlockSpec((nq, w), lambda b: (b, 0)),
            pl.BlockSpec((nq, w), lambda b: (b, 0)),
            pl.BlockSpec((h, nq, past), lambda b: (0, 0, 0)),
            pl.BlockSpec((h, nq, nq), lambda b: (0, 0, 0)),
            pl.BlockSpec((1, LANES), lambda b: (0, 0)),
        ],
        out_specs=pl.BlockSpec((nq, w), lambda b: (b, 0)),
        out_shape=jax.ShapeDtypeStruct((batch * nq, w), bf16),
        compiler_params=_cparams("parallel"),
        name="attn_sample",
    )(lam, q, k_past, v_past, k_new, v_new, bias_past, bias_new, gain)


def _hgrn_chunk(q, f_logit, vv, g_logit, lb, gain, s0, lc):
    n_sub = lc // HG_SUB
    f = lb + (1.0 - lb) * jax.nn.sigmoid(f_logit)
    logf = jnp.log(f)
    k = 1.0 - f
    row = lax.broadcasted_iota(jnp.int32, (lc, lc), 0)
    col = lax.broadcasted_iota(jnp.int32, (lc, lc), 1)
    tril = (col <= row).astype(f32)
    b = jnp.dot(tril, logf, precision=lax.Precision.HIGHEST, preferred_element_type=f32)
    b_last = b[lc - 1:lc, :]

    o = jnp.dot((q * jnp.exp(b)).astype(bf16), s0.astype(bf16), preferred_element_type=f32)
    k_hat = k * jnp.exp(b_last - b)
    bt = b.T
    decay_col = jnp.exp(bt[:, lc - 1:lc])
    s_new = decay_col * s0 + jnp.dot(k_hat.T.astype(bf16), vv.astype(bf16), preferred_element_type=f32)

    if n_sub > 1:
        rows = []
        nt = (((1,), (1,)), ((), ()))
        for i in range(n_sub):
            lo = i * HG_SUB
            if i == 0:
                rows.append(jnp.zeros((HG_SUB, lc), f32))
                continue
            beta = b[lo - 1:lo, :]
            q_t = q[lo:lo + HG_SUB, :] * jnp.exp(b[lo:lo + HG_SUB, :] - beta)
            k_t = k * jnp.exp(jnp.minimum(beta - b, 0.0))
            rows.append(lax.dot_general(q_t.astype(bf16), k_t.astype(bf16), nt, preferred_element_type=f32))
        a_off = jnp.concatenate(rows, axis=0)
        a_off = jnp.where(col < (row // HG_SUB) * HG_SUB, a_off, 0.0)
        o = o + jnp.dot(a_off.astype(bf16), vv.astype(bf16), preferred_element_type=f32)

    ones = jnp.ones((HG_DIM, LANES), bf16)
    srow = lax.broadcasted_iota(jnp.int32, (HG_SUB, HG_DIM), 0)
    orow = lax.broadcasted_iota(jnp.int32, (HG_SUB, HG_DIM), 0)
    blocks = []
    for i in range(n_sub):
        lo = i * HG_SUB
        b_i = b[lo:lo + HG_SUB, :]
        k_i = k[lo:lo + HG_SUB, :]
        v_i = vv[lo:lo + HG_SUB, :]
        q_i = q[lo:lo + HG_SUB, :]
        d_rows = []
        for t in range(HG_SUB):
            arg = jnp.where(srow <= t, b_i[t:t + 1, :] - b_i, -jnp.inf)
            d_rows.append(q_i[t:t + 1, :] * k_i * jnp.exp(arg))
        d3 = jnp.concatenate(d_rows, axis=0)
        a_rep = jnp.dot(d3.astype(bf16), ones, preferred_element_type=f32)
        o_i = jnp.zeros((HG_SUB, HG_DIM), f32)
        for t in range(HG_SUB):
            o_t = jnp.sum(a_rep[t * HG_SUB:(t + 1) * HG_SUB, :] * v_i, axis=0, keepdims=True)
            o_i = jnp.where(orow == t, o_t, o_i)
        blocks.append(o_i)
    o = o + (jnp.concatenate(blocks, axis=0) if n_sub > 1 else blocks[0])

    on = o * lax.rsqrt(jnp.mean(o * o, axis=-1, keepdims=True) + EPS) * gain
    gate = g_logit * jax.nn.sigmoid(g_logit)
    return on * gate, s_new


def _hgrn_kernel(hz_ref, lb_ref, gain_ref, s0_ref, o_ref, s_ref, st_sc, *, lc, heads):
    j = pl.program_id(1)
    w = heads * HG_DIM

    @pl.when(j == 0)
    def _():
        st_sc[...] = s0_ref[0]

    n_chunks = hz_ref.shape[0] // lc
    for c in range(n_chunks):
        rs = slice(c * lc, (c + 1) * lc)
        for h in range(heads):
            cs = slice(h * HG_DIM, (h + 1) * HG_DIM)
            o, s_new = _hgrn_chunk(
                hz_ref[rs, h * HG_DIM:(h + 1) * HG_DIM],
                hz_ref[rs, w + h * HG_DIM:w + (h + 1) * HG_DIM],
                hz_ref[rs, 2 * w + h * HG_DIM:2 * w + (h + 1) * HG_DIM],
                hz_ref[rs, 3 * w + h * HG_DIM:3 * w + (h + 1) * HG_DIM],
                lb_ref[:, cs], gain_ref[...], st_sc[h], lc)
            st_sc[h] = s_new
            o_ref[rs, cs] = o.astype(o_ref.dtype)

    @pl.when(j == pl.num_programs(1) - 1)
    def _():
        s_ref[0] = st_sc[...]


def _hgrn(hz, lb, gain, s0, batch, seq, lc, chunks_per_step):
    t = hz.shape[0]
    w = hz.shape[1] // 4
    heads = w // HG_DIM
    rows = lc * chunks_per_step
    steps = seq // rows
    return pl.pallas_call(
        functools.partial(_hgrn_kernel, lc=lc, heads=heads),
        grid=(batch, steps),
        in_specs=[
            pl.BlockSpec((rows, 4 * w), lambda b, j: (b * steps + j, 0)),
            pl.BlockSpec((1, w), lambda b, j: (0, 0)),
            pl.BlockSpec((1, HG_DIM), lambda b, j: (0, 0)),
            pl.BlockSpec((1, heads, HG_DIM, HG_DIM), lambda b, j: (b, 0, 0, 0)),
        ],
        out_specs=[
            pl.BlockSpec((rows, w), lambda b, j: (b * steps + j, 0)),
            pl.BlockSpec((1, heads, HG_DIM, HG_DIM), lambda b, j: (b, 0, 0, 0)),
        ],
        out_shape=[
            jax.ShapeDtypeStruct((t, w), bf16),
            jax.ShapeDtypeStruct((batch, heads, HG_DIM, HG_DIM), f32),
        ],
        scratch_shapes=[pltpu.VMEM((heads, HG_DIM, HG_DIM), f32)],
        compiler_params=_cparams("parallel", "arbitrary"),
        name="hgrn2",
    )(hz, lb, gain, s0)


def _top_rows(vals, k, payload=None):
    n_rows = vals.shape[0]
    row = lax.broadcasted_iota(jnp.int32, vals.shape, 0)
    out_v, out_i = [], []
    for _ in range(k):
        m = jnp.max(vals, axis=0, keepdims=True)
        first = jnp.min(jnp.where(vals == m, row, n_rows), axis=0, keepdims=True)
        hit = row == first
        out_v.append(m)
        if payload is None:
            out_i.append(first)
        else:
            out_i.append(jnp.max(jnp.where(hit, payload, -1), axis=0, keepdims=True))
        vals = jnp.where(hit, -jnp.inf, vals)
    return jnp.concatenate(out_v, axis=0), jnp.concatenate(out_i, axis=0)


def _retrieve_kernel(oa_ref, ob_ref, x_ref, wo_ref, g_ref, wqt_ref, sk_ref,
                     hp_ref, xn_ref, idx_ref, gate_ref, qt_sc, s1v_sc, s1i_sc, idx_sc):
    half = oa_ref.shape[1]
    y = (jnp.dot(oa_ref[...], wo_ref[0:half, :], preferred_element_type=f32)
         + jnp.dot(ob_ref[...], wo_ref[half:, :], preferred_element_type=f32))
    hp = x_ref[...] + y
    hp_ref[...] = hp
    xn = hp * lax.rsqrt(jnp.mean(hp * hp, axis=-1, keepdims=True) + EPS) * g_ref[...]
    xn_ref[...] = xn
    nt = (((1,), (1,)), ((), ()))
    qt_sc[...] = lax.dot_general(wqt_ref[...], xn.astype(bf16), nt, preferred_element_type=f32).astype(bf16)

    n_half = sk_ref.shape[0]

    def stage1(hc, carry):
        start = pl.multiple_of(hc * PEER_SUB_DIM, PEER_SUB_DIM)
        s = jnp.dot(sk_ref[hc], qt_sc[pl.ds(start, PEER_SUB_DIM), :], preferred_element_type=f32)
        v, i = _top_rows(s, PEER_TOPK)
        s1v_sc[hc] = v
        s1i_sc[hc] = i
        return carry

    lax.fori_loop(0, n_half, stage1, 0)

    tm = x_ref.shape[0]
    sub = lax.broadcasted_iota(jnp.int32, (8, tm), 0)

    def stage2(h, carry):
        v1, i1 = s1v_sc[2 * h], s1i_sc[2 * h]
        v2, i2 = s1v_sc[2 * h + 1], s1i_sc[2 * h + 1]
        cv, ci = [], []
        for i in range(PEER_TOPK):
            nj = PEER_TOPK // (i + 1)
            rows = PEER_TOPK if nj > 8 else 8
            a = v1[i:i + 1, :] + v2[0:rows, :]
            e = i1[i:i + 1, :] * PEER_KEYS + i2[0:rows, :]
            if nj < rows:
                a = jnp.where(sub < nj, a, -jnp.inf)
            cv.append(a)
            ci.append(e)
        top_s, eidx = _top_rows(jnp.concatenate(cv, axis=0), PEER_TOPK, jnp.concatenate(ci, axis=0))
        p = jnp.exp(top_s - top_s[0:1, :])
        g = p / jnp.sum(p, axis=0, keepdims=True)
        r0 = pl.multiple_of(h * PEER_TOPK, PEER_TOPK)
        gate_ref[pl.ds(r0, PEER_TOPK), :] = g
        idx_sc[pl.ds(r0, PEER_TOPK), :] = eidx
        return carry

    lax.fori_loop(0, n_half // 2, stage2, 0)
    idx_ref[...] = pltpu.bitcast(pltpu.bitcast(idx_sc[...], f32).T, jnp.int32)


def _retrieve(oa, ob, x, wo_bf, g2, wqt_bf, sk_bf, tm, x_block0=0):
    t, half = oa.shape
    d = x.shape[1]
    e = wqt_bf.shape[0]
    n_half = sk_bf.shape[0]
    n_sel = (n_half // 2) * PEER_TOPK
    return pl.pallas_call(
        _retrieve_kernel,
        grid=(t // tm,),
        in_specs=[
            pl.BlockSpec((tm, half), lambda i: (i, 0)),
            pl.BlockSpec((tm, half), lambda i: (i, 0)),
            pl.BlockSpec((tm, d), lambda i: (x_block0 + i, 0)),
            pl.BlockSpec((2 * half, d), lambda i: (0, 0)),
            pl.BlockSpec((1, d), lambda i: (0, 0)),
            pl.BlockSpec((e, d), lambda i: (0, 0)),
            pl.BlockSpec((n_half, PEER_KEYS, PEER_SUB_DIM), lambda i: (0, 0, 0)),
        ],
        out_specs=[
            pl.BlockSpec((tm, d), lambda i: (i, 0)),
            pl.BlockSpec((tm, d), lambda i: (i, 0)),
            pl.BlockSpec((tm, n_sel), lambda i: (i, 0)),
            pl.BlockSpec((n_sel, tm), lambda i: (0, i)),
        ],
        out_shape=[
            jax.ShapeDtypeStruct((t, d), f32),
            jax.ShapeDtypeStruct((t, d), f32),
            jax.ShapeDtypeStruct((t, n_sel), jnp.int32),
            jax.ShapeDtypeStruct((n_sel, t), f32),
        ],
        scratch_shapes=[
            pltpu.VMEM((e, tm), bf16),
            pltpu.VMEM((n_half, PEER_TOPK, tm), f32),
            pltpu.VMEM((n_half, PEER_TOPK, tm), jnp.int32),
            pltpu.VMEM((n_sel, tm), jnp.int32),
        ],
        compiler_params=_cparams("parallel"),
        name="retrieve",
    )(oa, ob, x, wo_bf, g2, wqt_bf, sk_bf)


def _gather_rows(table, idx):
    n = idx.shape[0]
    width = table.shape[1]
    idx2 = idx.reshape(n // GATHER_WINDOW, GATHER_WINDOW)
    mesh = plsc.VectorSubcoreMesh(core_axis_name="c", subcore_axis_name="s")

    @functools.partial(pl.kernel, out_type=jax.ShapeDtypeStruct((n, width), table.dtype), mesh=mesh)
    def gather(t_hbm, i_hbm, o_hbm):
        def body(i_vmem, o_vmem):
            pltpu.sync_copy(t_hbm.at[i_vmem.at[0]], o_vmem)

        pltpu.emit_pipeline(
            body,
            grid=(n // GATHER_WINDOW,),
            in_specs=[pl.BlockSpec((1, GATHER_WINDOW), lambda i: (i, 0))],
            out_specs=[pl.BlockSpec((GATHER_WINDOW, width), lambda i: (i, 0))],
            core_axis_name=("c", "s"),
            dimension_semantics=(pltpu.PARALLEL,),
            trace_scopes=False,
        )(i_hbm, o_hbm)

    return gather(table, idx2)


def _pack_table(tab):
    half = tab.shape[1] // 2
    lo = lax.bitcast_convert_type(tab[:, :half].astype(bf16), jnp.uint16).astype(jnp.uint32)
    hi = lax.bitcast_convert_type(tab[:, half:].astype(bf16), jnp.uint16).astype(jnp.uint32)
    return lax.bitcast_convert_type(lo | (hi << 16), jnp.int32)


def _split_bf16(x):
    hi = x.astype(bf16)
    return hi, (x - hi.astype(f32)).astype(bf16)


def _mix_kernel(ug_ref, vg_ref, xn_ref, hp_ref, gate_ref, gf_ref, yin_ref, y_ref, done_ref, wt_sc, peer_sc):
    del yin_ref
    done_ref[...] = jnp.zeros(done_ref.shape, f32)
    i = pl.program_id(0)
    n_tok = xn_ref.shape[0]
    n_sel = gate_ref.shape[0]
    half = ug_ref.shape[1]
    lane0 = (i % (LANES // n_tok)) * n_tok
    nt = (((1,), (1,)), ((), ()))

    xb = xn_ref[...].astype(bf16)
    place = (lax.broadcasted_iota(jnp.int32, (LANES, n_tok), 0)
             == lane0 + lax.broadcasted_iota(jnp.int32, (LANES, n_tok), 1)).astype(bf16)
    xp = jnp.concatenate([jnp.dot(place, xb[:, :half], preferred_element_type=f32).astype(bf16),
                          jnp.dot(place, xb[:, half:], preferred_element_type=f32).astype(bf16)], axis=0)

    row2 = lax.broadcasted_iota(jnp.int32, (2 * n_sel, LANES), 0)
    lane2 = lax.broadcasted_iota(jnp.int32, (2 * n_sel, LANES), 1)
    even = (row2 % 2) == 0
    act2 = jnp.zeros((2 * n_sel, LANES), f32)
    for t in range(n_tok):
        rows = pltpu.bitcast(ug_ref[t * n_sel:(t + 1) * n_sel, :], bf16)
        r = lax.dot_general(rows, xp, nt, preferred_element_type=f32)
        act2 = jnp.where(lane2 == lane0 + t, jnp.where(even, r[:, :LANES], r[:, LANES:]), act2)

    pair = (lax.broadcasted_iota(jnp.int32, (n_sel, 2 * n_sel), 1) // 2
            == lax.broadcasted_iota(jnp.int32, (n_sel, 2 * n_sel), 0)).astype(bf16)
    a_hi, a_lo = _split_bf16(act2)
    act = jnp.dot(pair, a_hi, preferred_element_type=f32) + jnp.dot(pair, a_lo, preferred_element_type=f32)
    w = gate_ref[...] * (0.5 * act * (1.0 + lax.erf(act * (2.0 ** -0.5))))

    w_hi, w_lo = _split_bf16(w.T)
    wt_sc[0] = jnp.dot(w_hi, pair, preferred_element_type=f32)
    wt_sc[1] = jnp.dot(w_lo, pair, preferred_element_type=f32)
    rowl = lax.broadcasted_iota(jnp.int32, (16, 2 * n_sel), 0)
    keep = (rowl < 4) & ((lax.broadcasted_iota(jnp.int32, (16, 2 * n_sel), 1) % 2) == (rowl % 2))
    for t in range(n_tok):
        whi = wt_sc[0, pl.ds(lane0 + t, 1), :]
        wlo = wt_sc[1, pl.ds(lane0 + t, 1), :]
        lhs = jnp.where(keep, jnp.where(rowl < 2, whi, wlo), 0.0).astype(bf16)
        rows = pltpu.bitcast(vg_ref[t * n_sel:(t + 1) * n_sel, :], bf16)
        res = jnp.dot(lhs, rows, preferred_element_type=f32)
        peer_sc[t:t + 1, 0:half] = res[0:1] + res[2:3]
        peer_sc[t:t + 1, half:] = res[1:2] + res[3:4]

    h = hp_ref[...] + peer_sc[...]
    y_ref[...] = h * lax.rsqrt(jnp.mean(h * h, axis=-1, keepdims=True) + EPS) * gf_ref[...]


def _mix(ug, vg, xn, hp, gate, gf, y_all, y_block0):
    t, d = xn.shape
    n_sel = gate.shape[0]
    half = ug.shape[1]
    nt = PEER_TOKENS
    per_lane_block = LANES // nt
    return pl.pallas_call(
        _mix_kernel,
        grid=(t // nt,),
        in_specs=[
            pl.BlockSpec((nt * n_sel, half), lambda i: (i, 0)),
            pl.BlockSpec((nt * n_sel, half), lambda i: (i, 0)),
            pl.BlockSpec((nt, d), lambda i: (i, 0)),
            pl.BlockSpec((nt, d), lambda i: (i, 0)),
            pl.BlockSpec((n_sel, LANES), lambda i: (0, i // per_lane_block)),
            pl.BlockSpec((1, d), lambda i: (0, 0)),
            pl.BlockSpec(memory_space=pl.ANY),
        ],
        out_specs=[pl.BlockSpec((nt, d), lambda i: (y_block0 + i, 0)),
                   pl.BlockSpec((8, LANES), lambda i: (0, 0))],
        out_shape=[jax.ShapeDtypeStruct(y_all.shape, f32), jax.ShapeDtypeStruct((8, LANES), f32)],
        input_output_aliases={6: 0},
        scratch_shapes=[
            pltpu.VMEM((2, LANES, 2 * n_sel), f32),
            pltpu.VMEM((nt, d), f32),
        ],
        compiler_params=_cparams("arbitrary"),
        name="peer_mix",
    )(ug, vg, xn, hp, gate, gf, y_all)


def _channel_mixer(oa, ob, x, x_block0, wo_bf, g2, wqt_bf, sk_bf, u_pk, v_pk, gf, tm, y_all, y_block0):
    hp, xn, eidx, gate = _retrieve(oa, ob, x, wo_bf, g2, wqt_bf, sk_bf, tm, x_block0)
    flat = eidx.reshape(-1)
    ug = _gather_rows(u_pk, flat)
    vg = _gather_rows(v_pk, flat)
    return _mix(ug, vg, xn, hp, gate, gf, y_all, y_block0)


def kernel(x_prompt, x_sample, cache_k, cache_v, state_hgrn, norm1, w_in, da_lq1, da_lk1, da_lq2, da_lk2,
           da_out_norm, hg_lb_logits, hg_out_norm, w_out, rel_bias, norm2, peer_w_q, peer_sub_keys,
           peer_u, peer_v, final_norm):
    batch, seq, d = x_prompt.shape
    dbatch, dseq, _ = x_sample.shape
    past = cache_k.shape[2]
    da_heads = cache_k.shape[3]
    hg_heads = state_hgrn.shape[2]
    depth = w_in.shape[0]
    assert depth == 1 and seq % ATT_BLOCK == 0 and seq % CHUNK == 0 and dseq % HG_SUB == 0 and dseq <= CHUNK
    assert past % CHUNK == 0 and (past + dseq - 1) // CHUNK == past // CHUNK

    l = 0
    lam_init = 0.8 - 0.6 * math.exp(-0.3 * l)
    lam = (jnp.exp(jnp.sum(da_lq1[l].astype(f32) * da_lk1[l].astype(f32)))
           - jnp.exp(jnp.sum(da_lq2[l].astype(f32) * da_lk2[l].astype(f32))) + lam_init).reshape(1)
    lb = jnp.cumsum(jax.nn.softmax(hg_lb_logits.astype(f32), axis=0), axis=0)[l].reshape(1, -1)
    da_gain = (jnp.tile(da_out_norm[l].astype(f32), 2) * (1.0 - lam_init)).reshape(1, LANES)
    hg_gain = hg_out_norm[l].astype(f32).reshape(1, HG_DIM)
    g1 = norm1[l].astype(f32).reshape(1, d)
    g2 = norm2[l].astype(f32).reshape(1, d)
    gf = final_norm.astype(f32).reshape(1, d)
    w_in_bf = w_in[l].astype(bf16)
    wkv_t_bf = w_in[l][:, cache_k.shape[3] * cache_k.shape[4]:3 * cache_k.shape[3] * cache_k.shape[4]].T.astype(bf16)
    wo_bf = w_out[l].astype(bf16)
    wqt_bf = peer_w_q[l].T.astype(bf16)
    sk_bf = peer_sub_keys[l].reshape(-1, PEER_KEYS, PEER_SUB_DIM).astype(bf16)
    u_pk = _pack_table(peer_u[l])
    v_pk = _pack_table(peer_v[l])

    bias_tiles = _prompt_bias_tiles(rel_bias)
    q_pos = past + jnp.arange(dseq)
    bias_s = _bias_of(jnp.arange(past + dseq)[None, :] - q_pos[:, None], rel_bias) * LOG2E
    bias_past, bias_new = bias_s[:, :, :past], bias_s[:, :, past:]
    zero_state = jnp.zeros((1, hg_heads, HG_DIM, HG_DIM), f32)

    dh2 = 2 * DA_HEAD_DIM
    wq = da_heads * dh2
    tm = 256
    x_all = x_prompt.reshape(batch * seq, d)
    kt_all = jnp.zeros((batch * wq, seq), f32)
    vt_all = jnp.zeros((batch * wq, seq), f32)
    y_all = jnp.zeros((batch * seq, d), f32)
    s_p = []
    done = [jnp.zeros((8, LANES), f32)] * 2
    for b in range(batch):
        q, kt, vt, hz, kt_all, vt_all = _inproj_prompt(x_all, g1, w_in_bf, wkv_t_bf, kt_all, vt_all, done[b],
                                                       b, seq, tm)
        oa = _attn_prompt(q, kt, vt, bias_tiles, da_gain, lam, 1, seq)
        ob, s_new = _hgrn(hz, lb, hg_gain, zero_state, 1, seq, CHUNK, 4)
        y_all, token = _channel_mixer(oa, ob, x_all, b * (seq // tm), wo_bf, g2, wqt_bf, sk_bf, u_pk, v_pk, gf,
                                      tm, y_all, b * (seq // PEER_TOKENS))
        done.append(token)
        s_p.append(s_new)

    xs = x_sample.reshape(dbatch * dseq, d)
    q, k_s, v_s, hz = _inproj_sample(xs, g1, w_in_bf, wq)
    kp_t = jnp.transpose(cache_k[l], (0, 2, 3, 1)).reshape(dbatch * wq, past)
    vp_t = jnp.transpose(cache_v[l], (0, 2, 3, 1)).reshape(dbatch * wq, past)
    oa = _attn_sample(q, k_s, v_s, kp_t, vp_t, bias_past, bias_new, da_gain, lam, dbatch, dseq, past)
    ob, s_s = _hgrn(hz, lb, hg_gain, state_hgrn[l].astype(f32), dbatch, dseq, dseq, 1)
    y_s, _ = _channel_mixer(oa, ob, xs, 0, wo_bf, g2, wqt_bf, sk_bf, u_pk, v_pk, gf, dbatch * dseq,
                            jnp.zeros((dbatch * dseq, d), f32), 0)

    y_prompt = y_all.reshape(batch, seq, d)
    y_sample = y_s.reshape(dbatch, dseq, d)
    k_prompt = jnp.transpose(kt_all.reshape(batch, da_heads, dh2, seq), (0, 3, 1, 2))[None]
    v_prompt = jnp.transpose(vt_all.reshape(batch, da_heads, DA_V_DIM, seq), (0, 3, 1, 2))[None]
    state_prompt = jnp.concatenate(s_p, axis=0)[None].astype(state_hgrn.dtype)
    k_sample = k_s.reshape(1, dbatch, dseq, da_heads, dh2)
    v_sample = v_s.reshape(1, dbatch, dseq, da_heads, DA_V_DIM)
    state_sample = s_s[None].astype(state_hgrn.dtype)
    return (y_prompt, y_sample, k_prompt, v_prompt, state_prompt, k_sample, v_sample, state_sample)
```

```python
import functools
import math

import jax
import jax.numpy as jnp
from jax import lax
from jax.experimental import pallas as pl
from jax.experimental.pallas import tpu as pltpu
from jax.experimental.pallas import tpu_sc as plsc

CHUNK = 64
DA_HEAD_DIM = 32
DA_V_DIM = 2 * DA_HEAD_DIM
HG_DIM = 128
REL_BUCKETS = 32
REL_MAX_DIST = 128
PEER_HEADS = 8
PEER_KEYS = 128
PEER_TOPK = 16
PEER_SUB_DIM = 64
EPS = 1e-6
NEG = -1e30
LOG2E = 1.4426950408889634

LANES = 128
ATT_BLOCK = 128
ATT_STEP = 512
HG_SUB = 16
GATHER_WINDOW = 64
PEER_TOKENS = 16
VMEM_LIMIT = 56 * 1024 * 1024

f32 = jnp.float32
bf16 = jnp.bfloat16


def _cparams(*sem):
    return pltpu.CompilerParams(dimension_semantics=sem, vmem_limit_bytes=VMEM_LIMIT)


def _inproj_prompt_kernel(x_ref, g_ref, w_ref, wkv_t_ref, kin_ref, vin_ref,
                          q_ref, kt_ref, vt_ref, hz_ref, ktf_ref, vtf_ref):
    del kin_ref, vin_ref
    x = x_ref[...]
    xn = x * lax.rsqrt(jnp.mean(x * x, axis=-1, keepdims=True) + EPS) * g_ref[...]
    xb = xn.astype(bf16)
    wq = q_ref.shape[1]
    nt = (((1,), (1,)), ((), ()))
    q_ref[...] = jnp.dot(xb, w_ref[:, 0:wq], preferred_element_type=f32).astype(bf16)
    kt = lax.dot_general(wkv_t_ref[0:wq, :], xb, nt, preferred_element_type=f32)
    ktf_ref[...] = kt
    kt_ref[...] = kt.astype(bf16)
    vt = lax.dot_general(wkv_t_ref[wq:2 * wq, :], xb, nt, preferred_element_type=f32)
    vtf_ref[...] = vt
    vt_ref[...] = vt.astype(bf16)
    hz_ref[...] = jnp.dot(xb, w_ref[:, 3 * wq:], preferred_element_type=f32)


def _inproj_prompt(x, g, w_bf, wkv_t_bf, kt_all, vt_all, b, seq, tm):
    d = x.shape[1]
    e = w_bf.shape[1]
    wq = wkv_t_bf.shape[0] // 2
    steps = seq // tm
    return pl.pallas_call(
        _inproj_prompt_kernel,
        grid=(steps,),
        in_specs=[
            pl.BlockSpec((tm, d), lambda i: (b * steps + i, 0)),
            pl.BlockSpec((1, d), lambda i: (0, 0)),
            pl.BlockSpec((d, e), lambda i: (0, 0)),
            pl.BlockSpec((2 * wq, d), lambda i: (0, 0)),
            pl.BlockSpec(memory_space=pl.ANY),
            pl.BlockSpec(memory_space=pl.ANY),
        ],
        out_specs=[
            pl.BlockSpec((tm, wq), lambda i: (i, 0)),
            pl.BlockSpec((wq, tm), lambda i: (0, i)),
            pl.BlockSpec((wq, tm), lambda i: (0, i)),
            pl.BlockSpec((tm, e - 3 * wq), lambda i: (i, 0)),
            pl.BlockSpec((wq, tm), lambda i: (b, i)),
            pl.BlockSpec((wq, tm), lambda i: (b, i)),
        ],
        out_shape=[
            jax.ShapeDtypeStruct((seq, wq), bf16),
            jax.ShapeDtypeStruct((wq, seq), bf16),
            jax.ShapeDtypeStruct((wq, seq), bf16),
            jax.ShapeDtypeStruct((seq, e - 3 * wq), f32),
            jax.ShapeDtypeStruct(kt_all.shape, f32),
            jax.ShapeDtypeStruct(vt_all.shape, f32),
        ],
        input_output_aliases={4: 4, 5: 5},
        compiler_params=_cparams("parallel"),
        name="inproj",
    )(x, g, w_bf, wkv_t_bf, kt_all, vt_all)


def _inproj_sample_kernel(x_ref, g_ref, w_ref, q_ref, k_ref, v_ref, hz_ref):
    x = x_ref[...]
    xn = x * lax.rsqrt(jnp.mean(x * x, axis=-1, keepdims=True) + EPS) * g_ref[...]
    xb = xn.astype(bf16)
    wq = q_ref.shape[1]
    q_ref[...] = jnp.dot(xb, w_ref[:, 0:wq], preferred_element_type=f32).astype(bf16)
    k_ref[...] = jnp.dot(xb, w_ref[:, wq:2 * wq], preferred_element_type=f32)
    v_ref[...] = jnp.dot(xb, w_ref[:, 2 * wq:3 * wq], preferred_element_type=f32)
    hz_ref[...] = jnp.dot(xb, w_ref[:, 3 * wq:], preferred_element_type=f32)


def _inproj_sample(x, g, w_bf, wq):
    t, d = x.shape
    e = w_bf.shape[1]
    return pl.pallas_call(
        _inproj_sample_kernel,
        grid=(1,),
        in_specs=[
            pl.BlockSpec((t, d), lambda i: (0, 0)),
            pl.BlockSpec((1, d), lambda i: (0, 0)),
            pl.BlockSpec((d, e), lambda i: (0, 0)),
        ],
        out_specs=[
            pl.BlockSpec((t, wq), lambda i: (0, 0)),
            pl.BlockSpec((t, wq), lambda i: (0, 0)),
            pl.BlockSpec((t, wq), lambda i: (0, 0)),
            pl.BlockSpec((t, e - 3 * wq), lambda i: (0, 0)),
        ],
        out_shape=[
            jax.ShapeDtypeStruct((t, wq), bf16),
            jax.ShapeDtypeStruct((t, wq), f32),
            jax.ShapeDtypeStruct((t, wq), f32),
            jax.ShapeDtypeStruct((t, e - 3 * wq), f32),
        ],
        compiler_params=_cparams("arbitrary"),
        name="inproj_sample",
    )(x, g, w_bf)


def _rel_bucket(rel):
    nb = REL_BUCKETS // 2
    max_exact = nb // 2
    ret = jnp.where(rel > 0, nb, 0)
    n = jnp.abs(rel)
    nf = jnp.maximum(n, 1).astype(f32)
    large = max_exact + (jnp.log(nf / max_exact) / math.log(REL_MAX_DIST / max_exact)
                         * (nb - max_exact)).astype(jnp.int32)
    large = jnp.minimum(large, nb - 1)
    return ret + jnp.where(n < max_exact, n, large)


def _bias_of(rel, rel_bias):
    bucket = _rel_bucket(rel)
    out = jnp.zeros((rel_bias.shape[1],) + rel.shape, f32)
    for b in range(REL_BUCKETS):
        out = out + jnp.where(bucket[None] == b, rel_bias[b].astype(f32)[:, None, None], 0.0)
    return out


def _attn_prompt_kernel(lam_ref, q_ref, kt_ref, vt_ref, bias_ref, gain_ref, o_ref, m_sc, acc_sc, sa_sc, sb_sc):
    qb = pl.program_id(2)
    tq = q_ref.shape[0]
    seq = kt_ref.shape[1]
    per_step = ATT_STEP // ATT_BLOCK
    lane = lax.broadcasted_iota(jnp.int32, (tq, LANES), 1)
    feat_k = lax.broadcasted_iota(jnp.int32, (LANES, ATT_STEP), 0)
    qs = q_ref[...].astype(f32) * (DA_HEAD_DIM ** -0.5 * LOG2E)
    q4 = jnp.concatenate(
        [jnp.where((lane // DA_HEAD_DIM) == i, qs, 0.0).astype(bf16) for i in range(4)], axis=0)

    m_sc[...] = jnp.full(m_sc.shape, -jnp.inf, f32)
    acc_sc[...] = jnp.zeros(acc_sc.shape, f32)

    n_steps = qb // per_step + 1
    last = n_steps - 1
    r = qb % per_step

    def scores(k):
        start = pl.multiple_of(jnp.minimum(k * ATT_STEP, seq - ATT_STEP), ATT_STEP)
        return jnp.dot(q4, kt_ref[:, pl.ds(start, ATT_STEP)], preferred_element_type=f32)

    def half(k, cur_ref, nxt_ref):
        nxt_ref[...] = scores(k + 1)
        tile = jnp.where(k == last, 1 + r, jnp.where((k == last - 1) & (r == 0), per_step + 1, 0))
        vv = vt_ref[:, pl.ds(pl.multiple_of(k * ATT_STEP, ATT_STEP), ATT_STEP)]
        for hh in range(2):
            bias = bias_ref[0, tile, hh * tq:(hh + 1) * tq, :]
            ps, alphas = [], []
            for c in range(2):
                rows = slice((2 * hh + c) * tq, (2 * hh + c + 1) * tq)
                s = cur_ref[rows, :] + bias
                m_old = m_sc[rows]
                m_new = jnp.maximum(m_old, jnp.max(s, axis=-1, keepdims=True))
                ps.append(jnp.exp2(s - m_new[:, 0:1]).astype(bf16))
                alphas.append(jnp.exp2(m_old - m_new))
                m_sc[rows] = m_new
            rows2 = slice(2 * hh * tq, (2 * hh + 2) * tq)
            vaug = jnp.where((feat_k // DA_V_DIM) == hh, vv, jnp.ones_like(vv))
            acc_sc[rows2] = (acc_sc[rows2] * jnp.concatenate(alphas, axis=0)
                             + lax.dot_general(jnp.concatenate(ps, axis=0), vaug, (((1,), (1,)), ((), ())),
                                               preferred_element_type=f32))

    sa_sc[...] = scores(0)

    def two_steps(kk, carry):
        half(2 * kk, sa_sc, sb_sc)

        @pl.when(2 * kk + 1 < n_steps)
        def _():
            half(2 * kk + 1, sb_sc, sa_sc)

        return carry

    lax.fori_loop(0, (n_steps + 1) // 2, two_steps, 0)

    lam = lam_ref[0]
    outs = []
    for hh in range(2):
        own = (lane // DA_V_DIM) == hh
        a1 = acc_sc[2 * hh * tq:(2 * hh + 1) * tq]
        a2 = acc_sc[(2 * hh + 1) * tq:(2 * hh + 2) * tq]
        l1 = jnp.max(jnp.where(own, 0.0, a1), axis=-1, keepdims=True)
        l2 = jnp.max(jnp.where(own, 0.0, a2), axis=-1, keepdims=True)
        o = a1 / l1 - lam * (a2 / l2)
        ssq = jnp.sum(jnp.where(own, o * o, 0.0), axis=-1, keepdims=True)
        outs.append(o * lax.rsqrt(ssq * (1.0 / DA_V_DIM) + EPS))
    o = jnp.where((lane // DA_V_DIM) == 0, outs[0], outs[1])
    o_ref[...] = (o * gain_ref[...]).astype(o_ref.dtype)


def _attn_prompt(q, kt, vt, bias_tiles, gain, lam, batch, seq):
    t, w = q.shape
    pairs = w // LANES
    nq = seq // ATT_BLOCK
    assert seq % ATT_STEP == 0
    n_tiles = ATT_STEP // ATT_BLOCK + 2
    return pl.pallas_call(
        _attn_prompt_kernel,
        grid=(batch, pairs, nq),
        in_specs=[
            pl.BlockSpec(memory_space=pltpu.SMEM),
            pl.BlockSpec((ATT_BLOCK, LANES), lambda b, p, i: (b * nq + i, p)),
            pl.BlockSpec((LANES, seq), lambda b, p, i: (p, b)),
            pl.BlockSpec((LANES, seq), lambda b, p, i: (p, b)),
            pl.BlockSpec((1, n_tiles, 2 * ATT_BLOCK, ATT_STEP), lambda b, p, i: (p, 0, 0, 0)),
            pl.BlockSpec((1, LANES), lambda b, p, i: (0, 0)),
        ],
        out_specs=pl.BlockSpec((ATT_BLOCK, LANES), lambda b, p, i: (b * nq + i, p)),
        out_shape=jax.ShapeDtypeStruct((t, w), bf16),
        scratch_shapes=[
            pltpu.VMEM((4 * ATT_BLOCK, LANES), f32),
            pltpu.VMEM((4 * ATT_BLOCK, LANES), f32),
            pltpu.VMEM((4 * ATT_BLOCK, ATT_STEP), f32),
            pltpu.VMEM((4 * ATT_BLOCK, ATT_STEP), f32),
        ],
        compiler_params=_cparams("parallel", "parallel", "arbitrary"),
        name="attn_prompt",
    )(lam, q, kt, vt, bias_tiles, gain)


def _prompt_bias_tiles(rel_bias):
    h = rel_bias.shape[1]
    per_step = ATT_STEP // ATT_BLOCK
    i = jnp.arange(ATT_BLOCK)
    rel_diag = i[None, :] - i[:, None]
    rel_prev = rel_diag - ATT_BLOCK
    far = rel_bias[REL_BUCKETS // 2 - 1].astype(f32)
    b_diag = (_bias_of(rel_diag, rel_bias) - far[:, None, None]) * LOG2E
    b_prev = (_bias_of(rel_prev, rel_bias) - far[:, None, None]) * LOG2E
    mask = (i[None, :] // CHUNK) <= (i[:, None] // CHUNK)
    b_diag = jnp.where(mask[None], b_diag, NEG)
    zero = jnp.zeros_like(b_diag)
    dead = jnp.full_like(b_diag, NEG)
    tiles = [jnp.concatenate([zero] * per_step, axis=-1)]
    for r in range(per_step):
        blocks = [zero if j < r - 1 else b_prev if j == r - 1 else b_diag if j == r else dead
                  for j in range(per_step)]
        tiles.append(jnp.concatenate(blocks, axis=-1))
    tiles.append(jnp.concatenate([zero] * (per_step - 1) + [b_prev], axis=-1))
    tiles = jnp.stack(tiles, axis=1)
    tiles = tiles.reshape(h // 2, 2, per_step + 2, ATT_BLOCK, ATT_STEP)
    return jnp.transpose(tiles, (0, 2, 1, 3, 4)).reshape(h // 2, per_step + 2, 2 * ATT_BLOCK, ATT_STEP)


def _attn_sample_kernel(lam_ref, q_ref, kp_ref, vp_ref, kn_ref, vn_ref, bp_ref, bn_ref, gain_ref, o_ref):
    nq = q_ref.shape[0]
    pairs = q_ref.shape[1] // LANES
    lane = lax.broadcasted_iota(jnp.int32, (nq, LANES), 1)
    lam = lam_ref[0]
    nt = (((1,), (1,)), ((), ()))
    for p in range(pairs):
        sl = slice(p * LANES, (p + 1) * LANES)
        qs = q_ref[:, sl].astype(f32) * (DA_HEAD_DIM ** -0.5 * LOG2E)
        kp = kp_ref[sl, :].astype(bf16)
        vp = vp_ref[sl, :].astype(bf16)
        kn = kn_ref[:, sl].astype(bf16)
        vn = vn_ref[:, sl].astype(bf16)
        outs = []
        for hh in range(2):
            h = 2 * p + hh
            own = (lane // DA_V_DIM) == hh
            res = []
            for c in range(2):
                qm = jnp.where((lane // DA_HEAD_DIM) == 2 * hh + c, qs, 0.0).astype(bf16)
                sp = jnp.dot(qm, kp, preferred_element_type=f32) + bp_ref[h]
                sn = lax.dot_general(qm, kn, nt, preferred_element_type=f32) + bn_ref[h]
                m = jnp.maximum(jnp.max(sp, axis=-1, keepdims=True), jnp.max(sn, axis=-1, keepdims=True))
                pp = jnp.exp2(sp - m)
                pn = jnp.exp2(sn - m)
                l = jnp.sum(pp, axis=-1, keepdims=True) + jnp.sum(pn, axis=-1, keepdims=True)
                pv = (lax.dot_general(pp.astype(bf16), vp, nt, preferred_element_type=f32)
                      + jnp.dot(pn.astype(bf16), vn, preferred_element_type=f32))
                res.append(pv / l)
            o = res[0] - lam * res[1]
            ssq = jnp.sum(jnp.where(own, o * o, 0.0), axis=-1, keepdims=True)
            outs.append(o * lax.rsqrt(ssq * (1.0 / DA_V_DIM) + EPS))
        o = jnp.where((lane // DA_V_DIM) == 0, outs[0], outs[1])
        o_ref[:, sl] = (o * gain_ref[...]).astype(o_ref.dtype)


def _attn_sample(q, k_new, v_new, k_past, v_past, bias_past, bias_new, gain, lam, batch, nq, past):
    w = q.shape[1]
    h = bias_past.shape[0]
    return pl.pallas_call(
        _attn_sample_kernel,
        grid=(batch,),
        in_specs=[
            pl.BlockSpec(memory_space=pltpu.SMEM),
            pl.BlockSpec((nq, w), lambda b: (b, 0)),
            pl.BlockSpec((w, past), lambda b: (b, 0)),
            pl.BlockSpec((w, past), lambda b: (b, 0)),
            pl.BlockSpec((nq, w), lambda b: (b, 0)),
            pl.BlockSpec((nq, w), lambda b: (b, 0)),
            pl.BlockSpec((h, nq, past), lambda b: (0, 0, 0)),
            pl.BlockSpec((h, nq, nq), lambda b: (0, 0, 0)),
            pl.BlockSpec((1, LANES), lambda b: (0, 0)),
        ],
        out_specs=pl.BlockSpec((nq, w), lambda b: (b, 0)),
        out_shape=jax.ShapeDtypeStruct((batch * nq, w), bf16),
        compiler_params=_cparams("parallel"),
        name="attn_sample",
    )(lam, q, k_past, v_past, k_new, v_new, bias_past, bias_new, gain)


def _hgrn_chunk(q, f_logit, vv, g_logit, lb, gain, s0, lc):
    n_sub = lc // HG_SUB
    f = lb + (1.0 - lb) * jax.nn.sigmoid(f_logit)
    logf = jnp.log(f)
    k = 1.0 - f
    row = lax.broadcasted_iota(jnp.int32, (lc, lc), 0)
    col = lax.broadcasted_iota(jnp.int32, (lc, lc), 1)
    tril = (col <= row).astype(f32)
    b = jnp.dot(tril, logf, precision=lax.Precision.HIGHEST, preferred_element_type=f32)
    b_last = b[lc - 1:lc, :]

    o = jnp.dot((q * jnp.exp(b)).astype(bf16), s0.astype(bf16), preferred_element_type=f32)
    k_hat = k * jnp.exp(b_last - b)
    bt = b.T
    decay_col = jnp.exp(bt[:, lc - 1:lc])
    s_new = decay_col * s0 + jnp.dot(k_hat.T.astype(bf16), vv.astype(bf16), preferred_element_type=f32)

    if n_sub > 1:
        rows = []
        nt = (((1,), (1,)), ((), ()))
        for i in range(n_sub):
            lo = i * HG_SUB
            if i == 0:
                rows.append(jnp.zeros((HG_SUB, lc), f32))
                continue
            beta = b[lo - 1:lo, :]
            q_t = q[lo:lo + HG_SUB, :] * jnp.exp(b[lo:lo + HG_SUB, :] - beta)
            k_t = k * jnp.exp(jnp.minimum(beta - b, 0.0))
            rows.append(lax.dot_general(q_t.astype(bf16), k_t.astype(bf16), nt, preferred_element_type=f32))
        a_off = jnp.concatenate(rows, axis=0)
        a_off = jnp.where(col < (row // HG_SUB) * HG_SUB, a_off, 0.0)
        o = o + jnp.dot(a_off.astype(bf16), vv.astype(bf16), preferred_element_type=f32)

    ones = jnp.ones((HG_DIM, LANES), bf16)
    srow = lax.broadcasted_iota(jnp.int32, (HG_SUB, HG_DIM), 0)
    orow = lax.broadcasted_iota(jnp.int32, (HG_SUB, HG_DIM), 0)
    blocks = []
    for i in range(n_sub):
        lo = i * HG_SUB
        b_i = b[lo:lo + HG_SUB, :]
        k_i = k[lo:lo + HG_SUB, :]
        v_i = vv[lo:lo + HG_SUB, :]
        q_i = q[lo:lo + HG_SUB, :]
        d_rows = []
        for t in range(HG_SUB):
            arg = jnp.where(srow <= t, b_i[t:t + 1, :] - b_i, -jnp.inf)
            d_rows.append(q_i[t:t + 1, :] * k_i * jnp.exp(arg))
        d3 = jnp.concatenate(d_rows, axis=0)
        a_rep = jnp.dot(d3.astype(bf16), ones, preferred_element_type=f32)
        o_i = jnp.zeros((HG_SUB, HG_DIM), f32)
        for t in range(HG_SUB):
            o_t = jnp.sum(a_rep[t * HG_SUB:(t + 1) * HG_SUB, :] * v_i, axis=0, keepdims=True)
            o_i = jnp.where(orow == t, o_t, o_i)
        blocks.append(o_i)
    o = o + (jnp.concatenate(blocks, axis=0) if n_sub > 1 else blocks[0])

    on = o * lax.rsqrt(jnp.mean(o * o, axis=-1, keepdims=True) + EPS) * gain
    gate = g_logit * jax.nn.sigmoid(g_logit)
    return on * gate, s_new


def _hgrn_kernel(hz_ref, lb_ref, gain_ref, s0_ref, o_ref, s_ref, st_sc, *, lc, heads):
    j = pl.program_id(1)
    w = heads * HG_DIM

    @pl.when(j == 0)
    def _():
        st_sc[...] = s0_ref[0]

    n_chunks = hz_ref.shape[0] // lc
    for c in range(n_chunks):
        rs = slice(c * lc, (c + 1) * lc)
        for h in range(heads):
            cs = slice(h * HG_DIM, (h + 1) * HG_DIM)
            o, s_new = _hgrn_chunk(
                hz_ref[rs, h * HG_DIM:(h + 1) * HG_DIM],
                hz_ref[rs, w + h * HG_DIM:w + (h + 1) * HG_DIM],
                hz_ref[rs, 2 * w + h * HG_DIM:2 * w + (h + 1) * HG_DIM],
                hz_ref[rs, 3 * w + h * HG_DIM:3 * w + (h + 1) * HG_DIM],
                lb_ref[:, cs], gain_ref[...], st_sc[h], lc)
            st_sc[h] = s_new
            o_ref[rs, cs] = o.astype(o_ref.dtype)

    @pl.when(j == pl.num_programs(1) - 1)
    def _():
        s_ref[0] = st_sc[...]


def _hgrn(hz, lb, gain, s0, batch, seq, lc, chunks_per_step):
    t = hz.shape[0]
    w = hz.shape[1] // 4
    heads = w // HG_DIM
    rows = lc * chunks_per_step
    steps = seq // rows
    return pl.pallas_call(
        functools.partial(_hgrn_kernel, lc=lc, heads=heads),
        grid=(batch, steps),
        in_specs=[
            pl.BlockSpec((rows, 4 * w), lambda b, j: (b * steps + j, 0)),
            pl.BlockSpec((1, w), lambda b, j: (0, 0)),
            pl.BlockSpec((1, HG_DIM), lambda b, j: (0, 0)),
            pl.BlockSpec((1, heads, HG_DIM, HG_DIM), lambda b, j: (b, 0, 0, 0)),
        ],
        out_specs=[
            pl.BlockSpec((rows, w), lambda b, j: (b * steps + j, 0)),
            pl.BlockSpec((1, heads, HG_DIM, HG_DIM), lambda b, j: (b, 0, 0, 0)),
        ],
        out_shape=[
            jax.ShapeDtypeStruct((t, w), bf16),
            jax.ShapeDtypeStruct((batch, heads, HG_DIM, HG_DIM), f32),
        ],
        scratch_shapes=[pltpu.VMEM((heads, HG_DIM, HG_DIM), f32)],
        compiler_params=_cparams("parallel", "arbitrary"),
        name="hgrn2",
    )(hz, lb, gain, s0)


def _top_rows(vals, k, payload=None):
    n_rows = vals.shape[0]
    row = lax.broadcasted_iota(jnp.int32, vals.shape, 0)
    out_v, out_i = [], []
    for _ in range(k):
        m = jnp.max(vals, axis=0, keepdims=True)
        first = jnp.min(jnp.where(vals == m, row, n_rows), axis=0, keepdims=True)
        hit = row == first
        out_v.append(m)
        if payload is None:
            out_i.append(first)
        else:
            out_i.append(jnp.max(jnp.where(hit, payload, -1), axis=0, keepdims=True))
        vals = jnp.where(hit, -jnp.inf, vals)
    return jnp.concatenate(out_v, axis=0), jnp.concatenate(out_i, axis=0)


def _retrieve_kernel(oa_ref, ob_ref, x_ref, wo_ref, g_ref, wqt_ref, sk_ref,
                     hp_ref, xn_ref, idx_ref, gate_ref, qt_sc, s1v_sc, s1i_sc, idx_sc, gate_sc):
    half = oa_ref.shape[1]
    y = (jnp.dot(oa_ref[...], wo_ref[0:half, :], preferred_element_type=f32)
         + jnp.dot(ob_ref[...], wo_ref[half:, :], preferred_element_type=f32))
    hp = x_ref[...] + y
    hp_ref[...] = hp
    xn = hp * lax.rsqrt(jnp.mean(hp * hp, axis=-1, keepdims=True) + EPS) * g_ref[...]
    xn_ref[...] = xn
    nt = (((1,), (1,)), ((), ()))
    qt_sc[...] = lax.dot_general(wqt_ref[...], xn.astype(bf16), nt, preferred_element_type=f32).astype(bf16)

    n_half = sk_ref.shape[0]

    def stage1(hc, carry):
        start = pl.multiple_of(hc * PEER_SUB_DIM, PEER_SUB_DIM)
        s = jnp.dot(sk_ref[hc], qt_sc[pl.ds(start, PEER_SUB_DIM), :], preferred_element_type=f32)
        v, i = _top_rows(s, PEER_TOPK)
        s1v_sc[hc] = v
        s1i_sc[hc] = i
        return carry

    lax.fori_loop(0, n_half, stage1, 0)

    tm = x_ref.shape[0]
    sub = lax.broadcasted_iota(jnp.int32, (8, tm), 0)

    def stage2(h, carry):
        v1, i1 = s1v_sc[2 * h], s1i_sc[2 * h]
        v2, i2 = s1v_sc[2 * h + 1], s1i_sc[2 * h + 1]
        cv, ci = [], []
        for i in range(PEER_TOPK):
            nj = PEER_TOPK // (i + 1)
            rows = PEER_TOPK if nj > 8 else 8
            a = v1[i:i + 1, :] + v2[0:rows, :]
            e = i1[i:i + 1, :] * PEER_KEYS + i2[0:rows, :]
            if nj < rows:
                a = jnp.where(sub < nj, a, -jnp.inf)
            cv.append(a)
            ci.append(e)
        top_s, eidx = _top_rows(jnp.concatenate(cv, axis=0), PEER_TOPK, jnp.concatenate(ci, axis=0))
        p = jnp.exp(top_s - top_s[0:1, :])
        g = p / jnp.sum(p, axis=0, keepdims=True)
        r0 = pl.multiple_of(h * PEER_TOPK, PEER_TOPK)
        gate_sc[pl.ds(r0, PEER_TOPK), :] = g
        idx_sc[pl.ds(r0, PEER_TOPK), :] = eidx
        return carry

    lax.fori_loop(0, n_half // 2, stage2, 0)
    idx_ref[...] = pltpu.bitcast(pltpu.bitcast(idx_sc[...], f32).T, jnp.int32)
    gate_ref[...] = gate_sc[...].T


def _retrieve(oa, ob, x, wo_bf, g2, wqt_bf, sk_bf, tm, x_block0=0):
    t, half = oa.shape
    d = x.shape[1]
    e = wqt_bf.shape[0]
    n_half = sk_bf.shape[0]
    n_sel = (n_half // 2) * PEER_TOPK
    return pl.pallas_call(
        _retrieve_kernel,
        grid=(t // tm,),
        in_specs=[
            pl.BlockSpec((tm, half), lambda i: (i, 0)),
            pl.BlockSpec((tm, half), lambda i: (i, 0)),
            pl.BlockSpec((tm, d), lambda i: (x_block0 + i, 0)),
            pl.BlockSpec((2 * half, d), lambda i: (0, 0)),
            pl.BlockSpec((1, d), lambda i: (0, 0)),
            pl.BlockSpec((e, d), lambda i: (0, 0)),
            pl.BlockSpec((n_half, PEER_KEYS, PEER_SUB_DIM), lambda i: (0, 0, 0)),
        ],
        out_specs=[
            pl.BlockSpec((tm, d), lambda i: (i, 0)),
            pl.BlockSpec((tm, d), lambda i: (i, 0)),
            pl.BlockSpec((tm, n_sel), lambda i: (i, 0)),
            pl.BlockSpec((tm, n_sel), lambda i: (i, 0)),
        ],
        out_shape=[
            jax.ShapeDtypeStruct((t, d), f32),
            jax.ShapeDtypeStruct((t, d), f32),
            jax.ShapeDtypeStruct((t, n_sel), jnp.int32),
            jax.ShapeDtypeStruct((t, n_sel), f32),
        ],
        scratch_shapes=[
            pltpu.VMEM((e, tm), bf16),
            pltpu.VMEM((n_half, PEER_TOPK, tm), f32),
            pltpu.VMEM((n_half, PEER_TOPK, tm), jnp.int32),
            pltpu.VMEM((n_sel, tm), jnp.int32),
            pltpu.VMEM((n_sel, tm), f32),
        ],
        compiler_params=_cparams("parallel"),
        name="retrieve",
    )(oa, ob, x, wo_bf, g2, wqt_bf, sk_bf)


def _gather_rows(table, idx):
    n = idx.shape[0]
    width = table.shape[1]
    idx2 = idx.reshape(n // GATHER_WINDOW, GATHER_WINDOW)
    mesh = plsc.VectorSubcoreMesh(core_axis_name="c", subcore_axis_name="s")

    @functools.partial(pl.kernel, out_type=jax.ShapeDtypeStruct((n, width), table.dtype), mesh=mesh)
    def gather(t_hbm, i_hbm, o_hbm):
        def body(i_vmem, o_vmem):
            pltpu.sync_copy(t_hbm.at[i_vmem.at[0]], o_vmem)

        pltpu.emit_pipeline(
            body,
            grid=(n // GATHER_WINDOW,),
            in_specs=[pl.BlockSpec((1, GATHER_WINDOW), lambda i: (i, 0))],
            out_specs=[pl.BlockSpec((GATHER_WINDOW, width), lambda i: (i, 0))],
            core_axis_name=("c", "s"),
            dimension_semantics=(pltpu.PARALLEL,),
            trace_scopes=False,
        )(i_hbm, o_hbm)

    return gather(table, idx2)


def _pack_table(tab):
    half = tab.shape[1] // 2
    lo = lax.bitcast_convert_type(tab[:, :half].astype(bf16), jnp.uint16).astype(jnp.uint32)
    hi = lax.bitcast_convert_type(tab[:, half:].astype(bf16), jnp.uint16).astype(jnp.uint32)
    return lax.bitcast_convert_type(lo | (hi << 16), jnp.int32)


def _split_bf16(x):
    hi = x.astype(bf16)
    return hi, (x - hi.astype(f32)).astype(bf16)


def _weights_kernel(ug_ref, xn_ref, gate_ref, w_ref):
    n_tok, n_sel = gate_ref.shape
    half = ug_ref.shape[1]
    nt = (((1,), (1,)), ((), ()))

    xb = xn_ref[...].astype(bf16)
    place = (lax.broadcasted_iota(jnp.int32, (LANES, n_tok), 0)
             == lax.broadcasted_iota(jnp.int32, (LANES, n_tok), 1)).astype(bf16)
    xp = jnp.concatenate([jnp.dot(place, xb[:, :half], preferred_element_type=f32).astype(bf16),
                          jnp.dot(place, xb[:, half:], preferred_element_type=f32).astype(bf16)], axis=0)

    row2 = lax.broadcasted_iota(jnp.int32, (2 * n_sel, LANES), 0)
    lane2 = lax.broadcasted_iota(jnp.int32, (2 * n_sel, LANES), 1)
    even = (row2 % 2) == 0
    act2 = jnp.zeros((2 * n_sel, LANES), f32)
    for t in range(n_tok):
        rows = pltpu.bitcast(ug_ref[t * n_sel:(t + 1) * n_sel, :], bf16)
        r = lax.dot_general(rows, xp, nt, preferred_element_type=f32)
        act2 = jnp.where(lane2 == t, jnp.where(even, r[:, :LANES], r[:, LANES:]), act2)

    pair = (lax.broadcasted_iota(jnp.int32, (n_sel, 2 * n_sel), 1) // 2
            == lax.broadcasted_iota(jnp.int32, (n_sel, 2 * n_sel), 0)).astype(bf16)
    a_hi, a_lo = _split_bf16(act2)
    act = jnp.dot(pair, a_hi, preferred_element_type=f32) + jnp.dot(pair, a_lo, preferred_element_type=f32)
    a = act.T[0:n_tok, :]
    w_ref[...] = gate_ref[...] * (0.5 * a * (1.0 + lax.erf(a * (2.0 ** -0.5))))


def _weights(ug, xn, gate):
    t, d = xn.shape
    n_sel = gate.shape[1]
    half = ug.shape[1]
    nt = PEER_TOKENS
    return pl.pallas_call(
        _weights_kernel,
        grid=(t // nt,),
        in_specs=[
            pl.BlockSpec((nt * n_sel, half), lambda i: (i, 0)),
            pl.BlockSpec((nt, d), lambda i: (i, 0)),
            pl.BlockSpec((nt, n_sel), lambda i: (i, 0)),
        ],
        out_specs=pl.BlockSpec((nt, n_sel), lambda i: (i, 0)),
        out_shape=jax.ShapeDtypeStruct((t, n_sel), f32),
        compiler_params=_cparams("parallel"),
        name="peer_weights",
    )(ug, xn, gate)


def _weighted_rows(table, idx, w):
    n_tok, n_sel = w.shape
    words = table.shape[1]
    half_rows = n_sel // 2
    info = plsc.get_sparse_core_info()
    lanes = info.num_lanes
    n_workers = info.num_cores * info.num_subcores
    tpw = n_tok // n_workers
    rows_per_iter = 2
    block = 8 * lanes
    assert n_tok % (2 * n_workers) == 0 and words % block == 0 and half_rows % lanes == 0
    idx2 = idx.reshape(2 * n_tok, half_rows)
    mesh = plsc.VectorSubcoreMesh(core_axis_name="c", subcore_axis_name="s")

    @functools.partial(
        pl.kernel, out_type=jax.ShapeDtypeStruct((n_tok, 2 * words), f32), mesh=mesh,
        compiler_params=pltpu.CompilerParams(needs_layout_passes=False),
        scratch_types=[
            pltpu.VMEM((2 * tpw, half_rows), jnp.int32),
            pltpu.VMEM((tpw, n_sel), f32),
            pltpu.VMEM((half_rows, words), jnp.int32),
            pltpu.VMEM((half_rows, words), jnp.int32),
            pltpu.VMEM((2 * words,), f32),
            pltpu.VMEM((2 * words,), f32),
            pltpu.SemaphoreType.DMA, pltpu.SemaphoreType.DMA, pltpu.SemaphoreType.DMA, pltpu.SemaphoreType.DMA,
        ])
    def mix(t_hbm, i_hbm, w_hbm, o_hbm, idx_v, w_v, rows_a, rows_b, out_a, out_b, sem_a, sem_b, sem_oa, sem_ob):
        wid = lax.axis_index("s") * info.num_cores + lax.axis_index("c")
        base = wid * tpw
        pltpu.sync_copy(i_hbm.at[pl.ds(base * 2, 2 * tpw)], idx_v)
        pltpu.sync_copy(w_hbm.at[pl.ds(base, tpw)], w_v)

        def accumulate(rows, out_v, t, half, first):
            @pl.loop(0, words // block)
            def _(cb):
                def body(i, acc):
                    acc = list(acc)
                    r0 = rows_per_iter * i
                    wv = w_v[t, pl.ds(half * half_rows + (r0 // lanes) * lanes, lanes)]
                    for r in range(rows_per_iter):
                        lane_of_row = jnp.full((lanes,), r0 % lanes + r, jnp.int32)
                        wj = wv.at[lane_of_row].get(mode="promise_in_bounds")
                        for c in range(8):
                            packed = rows[r0 + r, pl.ds(cb * block + c * lanes, lanes)]
                            lo, hi = plsc.unpack(plsc.bitcast(packed, bf16), format=plsc.PackFormat.INTERLEAVED)
                            acc[2 * c] = acc[2 * c] + wj * lo
                            acc[2 * c + 1] = acc[2 * c + 1] + wj * hi
                    return tuple(acc)

                if first:
                    init = tuple(jnp.zeros((lanes,), f32) for _ in range(16))
                else:
                    init = tuple(out_v[pl.ds(p * words + cb * block + c * lanes, lanes)]
                                 for c in range(8) for p in range(2))
                acc = lax.fori_loop(0, half_rows // rows_per_iter, body, init)
                for c in range(8):
                    out_v[pl.ds(cb * block + c * lanes, lanes)] = acc[2 * c]
                    out_v[pl.ds(words + cb * block + c * lanes, lanes)] = acc[2 * c + 1]

        def gather(half_index, rows, sem):
            return pltpu.make_async_copy(t_hbm.at[idx_v.at[half_index]], rows, sem)

        def token(t, out_v, sem_o, out_in_flight):
            gather(2 * t + 1, rows_b, sem_b).start()
            gather(2 * t, rows_a, sem_a).wait()

            @pl.when(out_in_flight)
            def _():
                pltpu.make_async_copy(out_v, o_hbm.at[base + t], sem_o).wait()

            accumulate(rows_a, out_v, t, 0, True)

            @pl.when(t + 1 < tpw)
            def _():
                gather(2 * t + 2, rows_a, sem_a).start()

            gather(2 * t + 1, rows_b, sem_b).wait()
            accumulate(rows_b, out_v, t, 1, False)
            pltpu.make_async_copy(out_v, o_hbm.at[base + t], sem_o).start()

        gather(0, rows_a, sem_a).start()

        @pl.loop(0, tpw // 2)
        def _(i):
            token(2 * i, out_a, sem_oa, i > 0)
            token(2 * i + 1, out_b, sem_ob, i > 0)

        pltpu.make_async_copy(out_a, o_hbm.at[base], sem_oa).wait()
        pltpu.make_async_copy(out_b, o_hbm.at[base], sem_ob).wait()

    return mix(table, idx2, w)


def _final_kernel(hp_ref, peer_ref, gf_ref, yin_ref, y_ref):
    del yin_ref
    h = hp_ref[...] + peer_ref[...]
    y_ref[...] = h * lax.rsqrt(jnp.mean(h * h, axis=-1, keepdims=True) + EPS) * gf_ref[...]


def _final(hp, peer, gf, y_all, y_block0, tm):
    t, d = hp.shape
    return pl.pallas_call(
        _final_kernel,
        grid=(t // tm,),
        in_specs=[
            pl.BlockSpec((tm, d), lambda i: (i, 0)),
            pl.BlockSpec((tm, d), lambda i: (i, 0)),
            pl.BlockSpec((1, d), lambda i: (0, 0)),
            pl.BlockSpec(memory_space=pl.ANY),
        ],
        out_specs=pl.BlockSpec((tm, d), lambda i: (y_block0 + i, 0)),
        out_shape=jax.ShapeDtypeStruct(y_all.shape, f32),
        input_output_aliases={3: 0},
        compiler_params=_cparams("parallel"),
        name="final_norm",
    )(hp, peer, gf, y_all)


def _channel_mixer(oa, ob, x, x_block0, wo_bf, g2, wqt_bf, sk_bf, u_pk, v_pk, gf, tm, y_all, y_block0):
    hp, xn, eidx, gate = _retrieve(oa, ob, x, wo_bf, g2, wqt_bf, sk_bf, tm, x_block0)
    ug = _gather_rows(u_pk, eidx.reshape(-1))
    w = _weights(ug, xn, gate)
    peer = _weighted_rows(v_pk, eidx, w)
    return _final(hp, peer, gf, y_all, y_block0, tm)


def kernel(x_prompt, x_sample, cache_k, cache_v, state_hgrn, norm1, w_in, da_lq1, da_lk1, da_lq2, da_lk2,
           da_out_norm, hg_lb_logits, hg_out_norm, w_out, rel_bias, norm2, peer_w_q, peer_sub_keys,
           peer_u, peer_v, final_norm):
    batch, seq, d = x_prompt.shape
    dbatch, dseq, _ = x_sample.shape
    past = cache_k.shape[2]
    da_heads = cache_k.shape[3]
    hg_heads = state_hgrn.shape[2]
    depth = w_in.shape[0]
    assert depth == 1 and seq % ATT_BLOCK == 0 and seq % CHUNK == 0 and dseq % HG_SUB == 0 and dseq <= CHUNK
    assert past % CHUNK == 0 and (past + dseq - 1) // CHUNK == past // CHUNK

    l = 0
    lam_init = 0.8 - 0.6 * math.exp(-0.3 * l)
    lam = (jnp.exp(jnp.sum(da_lq1[l].astype(f32) * da_lk1[l].astype(f32)))
           - jnp.exp(jnp.sum(da_lq2[l].astype(f32) * da_lk2[l].astype(f32))) + lam_init).reshape(1)
    lb = jnp.cumsum(jax.nn.softmax(hg_lb_logits.astype(f32), axis=0), axis=0)[l].reshape(1, -1)
    da_gain = (jnp.tile(da_out_norm[l].astype(f32), 2) * (1.0 - lam_init)).reshape(1, LANES)
    hg_gain = hg_out_norm[l].astype(f32).reshape(1, HG_DIM)
    g1 = norm1[l].astype(f32).reshape(1, d)
    g2 = norm2[l].astype(f32).reshape(1, d)
    gf = final_norm.astype(f32).reshape(1, d)
    w_in_bf = w_in[l].astype(bf16)
    wkv_t_bf = w_in[l][:, cache_k.shape[3] * cache_k.shape[4]:3 * cache_k.shape[3] * cache_k.shape[4]].T.astype(bf16)
    wo_bf = w_out[l].astype(bf16)
    wqt_bf = peer_w_q[l].T.astype(bf16)
    sk_bf = peer_sub_keys[l].reshape(-1, PEER_KEYS, PEER_SUB_DIM).astype(bf16)
    u_pk = _pack_table(peer_u[l])
    v_pk = _pack_table(peer_v[l])

    bias_tiles = _prompt_bias_tiles(rel_bias)
    q_pos = past + jnp.arange(dseq)
    bias_s = _bias_of(jnp.arange(past + dseq)[None, :] - q_pos[:, None], rel_bias) * LOG2E
    bias_past, bias_new = bias_s[:, :, :past], bias_s[:, :, past:]
    zero_state = jnp.zeros((1, hg_heads, HG_DIM, HG_DIM), f32)

    dh2 = 2 * DA_HEAD_DIM
    wq = da_heads * dh2
    tm = 256
    x_all = x_prompt.reshape(batch * seq, d)
    kt_all = jnp.zeros((batch * wq, seq), f32)
    vt_all = jnp.zeros((batch * wq, seq), f32)
    y_all = jnp.zeros((batch * seq, d), f32)
    s_p = []
    for b in range(batch):
        q, kt, vt, hz, kt_all, vt_all = _inproj_prompt(x_all, g1, w_in_bf, wkv_t_bf, kt_all, vt_all, b, seq, tm)
        oa = _attn_prompt(q, kt, vt, bias_tiles, da_gain, lam, 1, seq)
        ob, s_new = _hgrn(hz, lb, hg_gain, zero_state, 1, seq, CHUNK, 4)
        y_all = _channel_mixer(oa, ob, x_all, b * (seq // tm), wo_bf, g2, wqt_bf, sk_bf, u_pk, v_pk, gf, tm,
                               y_all, b * (seq // tm))
        s_p.append(s_new)

    xs = x_sample.reshape(dbatch * dseq, d)
    q, k_s, v_s, hz = _inproj_sample(xs, g1, w_in_bf, wq)
    kp_t = jnp.transpose(cache_k[l], (0, 2, 3, 1)).reshape(dbatch * wq, past)
    vp_t = jnp.transpose(cache_v[l], (0, 2, 3, 1)).reshape(dbatch * wq, past)
    oa = _attn_sample(q, k_s, v_s, kp_t, vp_t, bias_past, bias_new, da_gain, lam, dbatch, dseq, past)
    ob, s_s = _hgrn(hz, lb, hg_gain, state_hgrn[l].astype(f32), dbatch, dseq, dseq, 1)
    y_s = _channel_mixer(oa, ob, xs, 0, wo_bf, g2, wqt_bf, sk_bf, u_pk, v_pk, gf, dbatch * dseq,
                         jnp.zeros((dbatch * dseq, d), f32), 0)

    y_prompt = y_all.reshape(batch, seq, d)
    y_sample = y_s.reshape(dbatch, dseq, d)
    k_prompt = jnp.transpose(kt_all.reshape(batch, da_heads, dh2, seq), (0, 3, 1, 2))[None]
    v_prompt = jnp.transpose(vt_all.reshape(batch, da_heads, DA_V_DIM, seq), (0, 3, 1, 2))[None]
    state_prompt = jnp.concatenate(s_p, axis=0)[None].astype(state_hgrn.dtype)
    k_sample = k_s.reshape(1, dbatch, dseq, da_heads, dh2)
    v_sample = v_s.reshape(1, dbatch, dseq, da_heads, DA_V_DIM)
    state_sample = s_s[None].astype(state_hgrn.dtype)
    return (y_prompt, y_sample, k_prompt, v_prompt, state_prompt, k_sample, v_sample, state_sample)
```

```python
import functools
import math

import jax
import jax.numpy as jnp
from jax import lax
from jax.experimental import pallas as pl
from jax.experimental.pallas import tpu as pltpu
from jax.experimental.pallas import tpu_sc as plsc

CHUNK = 64
DA_HEAD_DIM = 32
DA_V_DIM = 2 * DA_HEAD_DIM
HG_DIM = 128
REL_BUCKETS = 32
REL_MAX_DIST = 128
PEER_HEADS = 8
PEER_KEYS = 128
PEER_TOPK = 16
PEER_SUB_DIM = 64
EPS = 1e-6
NEG = -1e30
LOG2E = 1.4426950408889634

LANES = 128
ATT_BLOCK = 128
ATT_STEP = 512
HG_SUB = 16
VMEM_LIMIT = 56 * 1024 * 1024

f32 = jnp.float32
bf16 = jnp.bfloat16


def _cparams(*sem):
    return pltpu.CompilerParams(dimension_semantics=sem, vmem_limit_bytes=VMEM_LIMIT)


def _inproj_prompt_kernel(x_ref, g_ref, w_ref, wkv_t_ref, kin_ref, vin_ref,
                          q_ref, kt_ref, vt_ref, hz_ref, ktf_ref, vtf_ref):
    del kin_ref, vin_ref
    x = x_ref[...]
    xn = x * lax.rsqrt(jnp.mean(x * x, axis=-1, keepdims=True) + EPS) * g_ref[...]
    xb = xn.astype(bf16)
    wq = q_ref.shape[1]
    nt = (((1,), (1,)), ((), ()))
    q_ref[...] = jnp.dot(xb, w_ref[:, 0:wq], preferred_element_type=f32).astype(bf16)
    kt = lax.dot_general(wkv_t_ref[0:wq, :], xb, nt, preferred_element_type=f32)
    ktf_ref[...] = kt
    kt_ref[...] = kt.astype(bf16)
    vt = lax.dot_general(wkv_t_ref[wq:2 * wq, :], xb, nt, preferred_element_type=f32)
    vtf_ref[...] = vt
    vt_ref[...] = vt.astype(bf16)
    hz_ref[...] = jnp.dot(xb, w_ref[:, 3 * wq:], preferred_element_type=f32)


def _inproj_prompt(x, g, w_bf, wkv_t_bf, kt_all, vt_all, b, seq, tm):
    d = x.shape[1]
    e = w_bf.shape[1]
    wq = wkv_t_bf.shape[0] // 2
    steps = seq // tm
    return pl.pallas_call(
        _inproj_prompt_kernel,
        grid=(steps,),
        in_specs=[
            pl.BlockSpec((tm, d), lambda i: (b * steps + i, 0)),
            pl.BlockSpec((1, d), lambda i: (0, 0)),
            pl.BlockSpec((d, e), lambda i: (0, 0)),
            pl.BlockSpec((2 * wq, d), lambda i: (0, 0)),
            pl.BlockSpec(memory_space=pl.ANY),
            pl.BlockSpec(memory_space=pl.ANY),
        ],
        out_specs=[
            pl.BlockSpec((tm, wq), lambda i: (i, 0)),
            pl.BlockSpec((wq, tm), lambda i: (0, i)),
            pl.BlockSpec((wq, tm), lambda i: (0, i)),
            pl.BlockSpec((tm, e - 3 * wq), lambda i: (i, 0)),
            pl.BlockSpec((wq, tm), lambda i: (b, i)),
            pl.BlockSpec((wq, tm), lambda i: (b, i)),
        ],
        out_shape=[
            jax.ShapeDtypeStruct((seq, wq), bf16),
            jax.ShapeDtypeStruct((wq, seq), bf16),
            jax.ShapeDtypeStruct((wq, seq), bf16),
            jax.ShapeDtypeStruct((seq, e - 3 * wq), f32),
            jax.ShapeDtypeStruct(kt_all.shape, f32),
            jax.ShapeDtypeStruct(vt_all.shape, f32),
        ],
        input_output_aliases={4: 4, 5: 5},
        compiler_params=_cparams("parallel"),
        name="inproj",
    )(x, g, w_bf, wkv_t_bf, kt_all, vt_all)


def _inproj_sample_kernel(x_ref, g_ref, w_ref, q_ref, k_ref, v_ref, hz_ref):
    x = x_ref[...]
    xn = x * lax.rsqrt(jnp.mean(x * x, axis=-1, keepdims=True) + EPS) * g_ref[...]
    xb = xn.astype(bf16)
    wq = q_ref.shape[1]
    q_ref[...] = jnp.dot(xb, w_ref[:, 0:wq], preferred_element_type=f32).astype(bf16)
    k_ref[...] = jnp.dot(xb, w_ref[:, wq:2 * wq], preferred_element_type=f32)
    v_ref[...] = jnp.dot(xb, w_ref[:, 2 * wq:3 * wq], preferred_element_type=f32)
    hz_ref[...] = jnp.dot(xb, w_ref[:, 3 * wq:], preferred_element_type=f32)


def _inproj_sample(x, g, w_bf, wq):
    t, d = x.shape
    e = w_bf.shape[1]
    return pl.pallas_call(
        _inproj_sample_kernel,
        grid=(1,),
        in_specs=[
            pl.BlockSpec((t, d), lambda i: (0, 0)),
            pl.BlockSpec((1, d), lambda i: (0, 0)),
            pl.BlockSpec((d, e), lambda i: (0, 0)),
        ],
        out_specs=[
            pl.BlockSpec((t, wq), lambda i: (0, 0)),
            pl.BlockSpec((t, wq), lambda i: (0, 0)),
            pl.BlockSpec((t, wq), lambda i: (0, 0)),
            pl.BlockSpec((t, e - 3 * wq), lambda i: (0, 0)),
        ],
        out_shape=[
            jax.ShapeDtypeStruct((t, wq), bf16),
            jax.ShapeDtypeStruct((t, wq), f32),
            jax.ShapeDtypeStruct((t, wq), f32),
            jax.ShapeDtypeStruct((t, e - 3 * wq), f32),
        ],
        compiler_params=_cparams("arbitrary"),
        name="inproj_sample",
    )(x, g, w_bf)


def _rel_bucket(rel):
    nb = REL_BUCKETS // 2
    max_exact = nb // 2
    ret = jnp.where(rel > 0, nb, 0)
    n = jnp.abs(rel)
    nf = jnp.maximum(n, 1).astype(f32)
    large = max_exact + (jnp.log(nf / max_exact) / math.log(REL_MAX_DIST / max_exact)
                         * (nb - max_exact)).astype(jnp.int32)
    large = jnp.minimum(large, nb - 1)
    return ret + jnp.where(n < max_exact, n, large)


def _bias_of(rel, rel_bias):
    bucket = _rel_bucket(rel)
    out = jnp.zeros((rel_bias.shape[1],) + rel.shape, f32)
    for b in range(REL_BUCKETS):
        out = out + jnp.where(bucket[None] == b, rel_bias[b].astype(f32)[:, None, None], 0.0)
    return out


def _attn_prompt_kernel(lam_ref, q_ref, kt_ref, vt_ref, bias_ref, gain_ref, o_ref, m_sc, acc_sc, sa_sc, sb_sc):
    qb = pl.program_id(2)
    tq = q_ref.shape[0]
    seq = kt_ref.shape[1]
    per_step = ATT_STEP // ATT_BLOCK
    lane = lax.broadcasted_iota(jnp.int32, (tq, LANES), 1)
    feat_k = lax.broadcasted_iota(jnp.int32, (LANES, ATT_STEP), 0)
    qs = q_ref[...].astype(f32) * (DA_HEAD_DIM ** -0.5 * LOG2E)
    q4 = jnp.concatenate(
        [jnp.where((lane // DA_HEAD_DIM) == i, qs, 0.0).astype(bf16) for i in range(4)], axis=0)

    m_sc[...] = jnp.full(m_sc.shape, -jnp.inf, f32)
    acc_sc[...] = jnp.zeros(acc_sc.shape, f32)

    n_steps = qb // per_step + 1
    last = n_steps - 1
    r = qb % per_step

    def scores(k):
        start = pl.multiple_of(jnp.minimum(k * ATT_STEP, seq - ATT_STEP), ATT_STEP)
        return jnp.dot(q4, kt_ref[:, pl.ds(start, ATT_STEP)], preferred_element_type=f32)

    def half(k, cur_ref, nxt_ref):
        nxt_ref[...] = scores(k + 1)
        tile = jnp.where(k == last, 1 + r, jnp.where((k == last - 1) & (r == 0), per_step + 1, 0))
        vv = vt_ref[:, pl.ds(pl.multiple_of(k * ATT_STEP, ATT_STEP), ATT_STEP)]
        for hh in range(2):
            bias = bias_ref[0, tile, hh * tq:(hh + 1) * tq, :]
            ps, alphas = [], []
            for c in range(2):
                rows = slice((2 * hh + c) * tq, (2 * hh + c + 1) * tq)
                s = cur_ref[rows, :] + bias
                m_old = m_sc[rows]
                m_new = jnp.maximum(m_old, jnp.max(s, axis=-1, keepdims=True))
                ps.append(jnp.exp2(s - m_new[:, 0:1]).astype(bf16))
                alphas.append(jnp.exp2(m_old - m_new))
                m_sc[rows] = m_new
            rows2 = slice(2 * hh * tq, (2 * hh + 2) * tq)
            vaug = jnp.where((feat_k // DA_V_DIM) == hh, vv, jnp.ones_like(vv))
            acc_sc[rows2] = (acc_sc[rows2] * jnp.concatenate(alphas, axis=0)
                             + lax.dot_general(jnp.concatenate(ps, axis=0), vaug, (((1,), (1,)), ((), ())),
                                               preferred_element_type=f32))

    sa_sc[...] = scores(0)

    def two_steps(kk, carry):
        half(2 * kk, sa_sc, sb_sc)

        @pl.when(2 * kk + 1 < n_steps)
        def _():
            half(2 * kk + 1, sb_sc, sa_sc)

        return carry

    lax.fori_loop(0, (n_steps + 1) // 2, two_steps, 0)

    lam = lam_ref[0]
    outs = []
    for hh in range(2):
        own = (lane // DA_V_DIM) == hh
        a1 = acc_sc[2 * hh * tq:(2 * hh + 1) * tq]
        a2 = acc_sc[(2 * hh + 1) * tq:(2 * hh + 2) * tq]
        l1 = jnp.max(jnp.where(own, 0.0, a1), axis=-1, keepdims=True)
        l2 = jnp.max(jnp.where(own, 0.0, a2), axis=-1, keepdims=True)
        o = a1 / l1 - lam * (a2 / l2)
        ssq = jnp.sum(jnp.where(own, o * o, 0.0), axis=-1, keepdims=True)
        outs.append(o * lax.rsqrt(ssq * (1.0 / DA_V_DIM) + EPS))
    o = jnp.where((lane // DA_V_DIM) == 0, outs[0], outs[1])
    o_ref[...] = (o * gain_ref[...]).astype(o_ref.dtype)


def _attn_prompt(q, kt, vt, bias_tiles, gain, lam, batch, seq):
    t, w = q.shape
    pairs = w // LANES
    nq = seq // ATT_BLOCK
    assert seq % ATT_STEP == 0
    n_tiles = ATT_STEP // ATT_BLOCK + 2
    return pl.pallas_call(
        _attn_prompt_kernel,
        grid=(batch, pairs, nq),
        in_specs=[
            pl.BlockSpec(memory_space=pltpu.SMEM),
            pl.BlockSpec((ATT_BLOCK, LANES), lambda b, p, i: (b * nq + i, p)),
            pl.BlockSpec((LANES, seq), lambda b, p, i: (p, b)),
            pl.BlockSpec((LANES, seq), lambda b, p, i: (p, b)),
            pl.BlockSpec((1, n_tiles, 2 * ATT_BLOCK, ATT_STEP), lambda b, p, i: (p, 0, 0, 0)),
            pl.BlockSpec((1, LANES), lambda b, p, i: (0, 0)),
        ],
        out_specs=pl.BlockSpec((ATT_BLOCK, LANES), lambda b, p, i: (b * nq + i, p)),
        out_shape=jax.ShapeDtypeStruct((t, w), bf16),
        scratch_shapes=[
            pltpu.VMEM((4 * ATT_BLOCK, LANES), f32),
            pltpu.VMEM((4 * ATT_BLOCK, LANES), f32),
            pltpu.VMEM((4 * ATT_BLOCK, ATT_STEP), f32),
            pltpu.VMEM((4 * ATT_BLOCK, ATT_STEP), f32),
        ],
        compiler_params=_cparams("parallel", "parallel", "arbitrary"),
        name="attn_prompt",
    )(lam, q, kt, vt, bias_tiles, gain)


def _prompt_bias_tiles(rel_bias):
    h = rel_bias.shape[1]
    per_step = ATT_STEP // ATT_BLOCK
    i = jnp.arange(ATT_BLOCK)
    rel_diag = i[None, :] - i[:, None]
    rel_prev = rel_diag - ATT_BLOCK
    far = rel_bias[REL_BUCKETS // 2 - 1].astype(f32)
    b_diag = (_bias_of(rel_diag, rel_bias) - far[:, None, None]) * LOG2E
    b_prev = (_bias_of(rel_prev, rel_bias) - far[:, None, None]) * LOG2E
    mask = (i[None, :] // CHUNK) <= (i[:, None] // CHUNK)
    b_diag = jnp.where(mask[None], b_diag, NEG)
    zero = jnp.zeros_like(b_diag)
    dead = jnp.full_like(b_diag, NEG)
    tiles = [jnp.concatenate([zero] * per_step, axis=-1)]
    for r in range(per_step):
        blocks = [zero if j < r - 1 else b_prev if j == r - 1 else b_diag if j == r else dead
                  for j in range(per_step)]
        tiles.append(jnp.concatenate(blocks, axis=-1))
    tiles.append(jnp.concatenate([zero] * (per_step - 1) + [b_prev], axis=-1))
    tiles = jnp.stack(tiles, axis=1)
    tiles = tiles.reshape(h // 2, 2, per_step + 2, ATT_BLOCK, ATT_STEP)
    return jnp.transpose(tiles, (0, 2, 1, 3, 4)).reshape(h // 2, per_step + 2, 2 * ATT_BLOCK, ATT_STEP)


def _attn_sample_kernel(lam_ref, q_ref, kp_ref, vp_ref, kn_ref, vn_ref, bp_ref, bn_ref, gain_ref, o_ref):
    nq = q_ref.shape[0]
    pairs = q_ref.shape[1] // LANES
    lane = lax.broadcasted_iota(jnp.int32, (nq, LANES), 1)
    lam = lam_ref[0]
    nt = (((1,), (1,)), ((), ()))
    for p in range(pairs):
        sl = slice(p * LANES, (p + 1) * LANES)
        qs = q_ref[:, sl].astype(f32) * (DA_HEAD_DIM ** -0.5 * LOG2E)
        kp = kp_ref[sl, :].astype(bf16)
        vp = vp_ref[sl, :].astype(bf16)
        kn = kn_ref[:, sl].astype(bf16)
        vn = vn_ref[:, sl].astype(bf16)
        outs = []
        for hh in range(2):
            h = 2 * p + hh
            own = (lane // DA_V_DIM) == hh
            res = []
            for c in range(2):
                qm = jnp.where((lane // DA_HEAD_DIM) == 2 * hh + c, qs, 0.0).astype(bf16)
                sp = jnp.dot(qm, kp, preferred_element_type=f32) + bp_ref[h]
                sn = lax.dot_general(qm, kn, nt, preferred_element_type=f32) + bn_ref[h]
                m = jnp.maximum(jnp.max(sp, axis=-1, keepdims=True), jnp.max(sn, axis=-1, keepdims=True))
                pp = jnp.exp2(sp - m)
                pn = jnp.exp2(sn - m)
                l = jnp.sum(pp, axis=-1, keepdims=True) + jnp.sum(pn, axis=-1, keepdims=True)
                pv = (lax.dot_general(pp.astype(bf16), vp, nt, preferred_element_type=f32)
                      + jnp.dot(pn.astype(bf16), vn, preferred_element_type=f32))
                res.append(pv / l)
            o = res[0] - lam * res[1]
            ssq = jnp.sum(jnp.where(own, o * o, 0.0), axis=-1, keepdims=True)
            outs.append(o * lax.rsqrt(ssq * (1.0 / DA_V_DIM) + EPS))
        o = jnp.where((lane // DA_V_DIM) == 0, outs[0], outs[1])
        o_ref[:, sl] = (o * gain_ref[...]).astype(o_ref.dtype)


def _attn_sample(q, k_new, v_new, k_past, v_past, bias_past, bias_new, gain, lam, batch, nq, past):
    w = q.shape[1]
    h = bias_past.shape[0]
    return pl.pallas_call(
        _attn_sample_kernel,
        grid=(batch,),
        in_specs=[
            pl.BlockSpec(memory_space=pltpu.SMEM),
            pl.BlockSpec((nq, w), lambda b: (b, 0)),
            pl.BlockSpec((w, past), lambda b: (b, 0)),
            pl.BlockSpec((w, past), lambda b: (b, 0)),
            pl.BlockSpec((nq, w), lambda b: (b, 0)),
            pl.BlockSpec((nq, w), lambda b: (b, 0)),
            pl.BlockSpec((h, nq, past), lambda b: (0, 0, 0)),
            pl.BlockSpec((h, nq, nq), lambda b: (0, 0, 0)),
            pl.BlockSpec((1, LANES), lambda b: (0, 0)),
        ],
        out_specs=pl.BlockSpec((nq, w), lambda b: (b, 0)),
        out_shape=jax.ShapeDtypeStruct((batch * nq, w), bf16),
        compiler_params=_cparams("parallel"),
        name="attn_sample",
    )(lam, q, k_past, v_past, k_new, v_new, bias_past, bias_new, gain)


def _hgrn_chunk(q, f_logit, vv, g_logit, lb, gain, s0, lc):
    n_sub = lc // HG_SUB
    f = lb + (1.0 - lb) * jax.nn.sigmoid(f_logit)
    logf = jnp.log(f)
    k = 1.0 - f
    row = lax.broadcasted_iota(jnp.int32, (lc, lc), 0)
    col = lax.broadcasted_iota(jnp.int32, (lc, lc), 1)
    tril = (col <= row).astype(f32)
    b = jnp.dot(tril, logf, precision=lax.Precision.HIGHEST, preferred_element_type=f32)
    b_last = b[lc - 1:lc, :]

    o = jnp.dot((q * jnp.exp(b)).astype(bf16), s0.astype(bf16), preferred_element_type=f32)
    k_hat = k * jnp.exp(b_last - b)
    bt = b.T
    decay_col = jnp.exp(bt[:, lc - 1:lc])
    s_new = decay_col * s0 + jnp.dot(k_hat.T.astype(bf16), vv.astype(bf16), preferred_element_type=f32)

    if n_sub > 1:
        rows = []
        nt = (((1,), (1,)), ((), ()))
        for i in range(n_sub):
            lo = i * HG_SUB
            if i == 0:
                rows.append(jnp.zeros((HG_SUB, lc), f32))
                continue
            beta = b[lo - 1:lo, :]
            q_t = q[lo:lo + HG_SUB, :] * jnp.exp(b[lo:lo + HG_SUB, :] - beta)
            k_t = k * jnp.exp(jnp.minimum(beta - b, 0.0))
            rows.append(lax.dot_general(q_t.astype(bf16), k_t.astype(bf16), nt, preferred_element_type=f32))
        a_off = jnp.concatenate(rows, axis=0)
        a_off = jnp.where(col < (row // HG_SUB) * HG_SUB, a_off, 0.0)
        o = o + jnp.dot(a_off.astype(bf16), vv.astype(bf16), preferred_element_type=f32)

    ones = jnp.ones((HG_DIM, LANES), bf16)
    srow = lax.broadcasted_iota(jnp.int32, (HG_SUB, HG_DIM), 0)
    orow = lax.broadcasted_iota(jnp.int32, (HG_SUB, HG_DIM), 0)
    blocks = []
    for i in range(n_sub):
        lo = i * HG_SUB
        b_i = b[lo:lo + HG_SUB, :]
        k_i = k[lo:lo + HG_SUB, :]
        v_i = vv[lo:lo + HG_SUB, :]
        q_i = q[lo:lo + HG_SUB, :]
        d_rows = []
        for t in range(HG_SUB):
            arg = jnp.where(srow <= t, b_i[t:t + 1, :] - b_i, -jnp.inf)
            d_rows.append(q_i[t:t + 1, :] * k_i * jnp.exp(arg))
        d3 = jnp.concatenate(d_rows, axis=0)
        a_rep = jnp.dot(d3.astype(bf16), ones, preferred_element_type=f32)
        o_i = jnp.zeros((HG_SUB, HG_DIM), f32)
        for t in range(HG_SUB):
            o_t = jnp.sum(a_rep[t * HG_SUB:(t + 1) * HG_SUB, :] * v_i, axis=0, keepdims=True)
            o_i = jnp.where(orow == t, o_t, o_i)
        blocks.append(o_i)
    o = o + (jnp.concatenate(blocks, axis=0) if n_sub > 1 else blocks[0])

    on = o * lax.rsqrt(jnp.mean(o * o, axis=-1, keepdims=True) + EPS) * gain
    gate = g_logit * jax.nn.sigmoid(g_logit)
    return on * gate, s_new


def _hgrn_kernel(hz_ref, lb_ref, gain_ref, s0_ref, o_ref, s_ref, st_sc, *, lc, heads):
    j = pl.program_id(1)
    w = heads * HG_DIM

    @pl.when(j == 0)
    def _():
        st_sc[...] = s0_ref[0]

    n_chunks = hz_ref.shape[0] // lc
    for c in range(n_chunks):
        rs = slice(c * lc, (c + 1) * lc)
        for h in range(heads):
            cs = slice(h * HG_DIM, (h + 1) * HG_DIM)
            o, s_new = _hgrn_chunk(
                hz_ref[rs, h * HG_DIM:(h + 1) * HG_DIM],
                hz_ref[rs, w + h * HG_DIM:w + (h + 1) * HG_DIM],
                hz_ref[rs, 2 * w + h * HG_DIM:2 * w + (h + 1) * HG_DIM],
                hz_ref[rs, 3 * w + h * HG_DIM:3 * w + (h + 1) * HG_DIM],
                lb_ref[:, cs], gain_ref[...], st_sc[h], lc)
            st_sc[h] = s_new
            o_ref[rs, cs] = o.astype(o_ref.dtype)

    @pl.when(j == pl.num_programs(1) - 1)
    def _():
        s_ref[0] = st_sc[...]


def _hgrn(hz, lb, gain, s0, batch, seq, lc, chunks_per_step):
    t = hz.shape[0]
    w = hz.shape[1] // 4
    heads = w // HG_DIM
    rows = lc * chunks_per_step
    steps = seq // rows
    return pl.pallas_call(
        functools.partial(_hgrn_kernel, lc=lc, heads=heads),
        grid=(batch, steps),
        in_specs=[
            pl.BlockSpec((rows, 4 * w), lambda b, j: (b * steps + j, 0)),
            pl.BlockSpec((1, w), lambda b, j: (0, 0)),
            pl.BlockSpec((1, HG_DIM), lambda b, j: (0, 0)),
            pl.BlockSpec((1, heads, HG_DIM, HG_DIM), lambda b, j: (b, 0, 0, 0)),
        ],
        out_specs=[
            pl.BlockSpec((rows, w), lambda b, j: (b * steps + j, 0)),
            pl.BlockSpec((1, heads, HG_DIM, HG_DIM), lambda b, j: (b, 0, 0, 0)),
        ],
        out_shape=[
            jax.ShapeDtypeStruct((t, w), bf16),
            jax.ShapeDtypeStruct((batch, heads, HG_DIM, HG_DIM), f32),
        ],
        scratch_shapes=[pltpu.VMEM((heads, HG_DIM, HG_DIM), f32)],
        compiler_params=_cparams("parallel", "arbitrary"),
        name="hgrn2",
    )(hz, lb, gain, s0)


def _top_rows(vals, k, payload=None):
    n_rows = vals.shape[0]
    row = lax.broadcasted_iota(jnp.int32, vals.shape, 0)
    out_v, out_i = [], []
    for _ in range(k):
        m = jnp.max(vals, axis=0, keepdims=True)
        first = jnp.min(jnp.where(vals == m, row, n_rows), axis=0, keepdims=True)
        hit = row == first
        out_v.append(m)
        if payload is None:
            out_i.append(first)
        else:
            out_i.append(jnp.max(jnp.where(hit, payload, -1), axis=0, keepdims=True))
        vals = jnp.where(hit, -jnp.inf, vals)
    return jnp.concatenate(out_v, axis=0), jnp.concatenate(out_i, axis=0)


def _retrieve_kernel(oa_ref, ob_ref, x_ref, wo_ref, g_ref, wqt_ref, sk_ref,
                     hp_ref, xn_ref, idx_ref, gate_ref, qt_sc, s1v_sc, s1i_sc, idx_sc, gate_sc):
    half = oa_ref.shape[1]
    y = (jnp.dot(oa_ref[...], wo_ref[0:half, :], preferred_element_type=f32)
         + jnp.dot(ob_ref[...], wo_ref[half:, :], preferred_element_type=f32))
    hp = x_ref[...] + y
    hp_ref[...] = hp
    xn = hp * lax.rsqrt(jnp.mean(hp * hp, axis=-1, keepdims=True) + EPS) * g_ref[...]
    xn_ref[...] = xn
    nt = (((1,), (1,)), ((), ()))
    qt_sc[...] = lax.dot_general(wqt_ref[...], xn.astype(bf16), nt, preferred_element_type=f32).astype(bf16)

    n_half = sk_ref.shape[0]

    def stage1(hc, carry):
        start = pl.multiple_of(hc * PEER_SUB_DIM, PEER_SUB_DIM)
        s = jnp.dot(sk_ref[hc], qt_sc[pl.ds(start, PEER_SUB_DIM), :], preferred_element_type=f32)
        v, i = _top_rows(s, PEER_TOPK)
        s1v_sc[hc] = v
        s1i_sc[hc] = i
        return carry

    lax.fori_loop(0, n_half, stage1, 0)

    tm = x_ref.shape[0]
    sub = lax.broadcasted_iota(jnp.int32, (8, tm), 0)

    def stage2(h, carry):
        v1, i1 = s1v_sc[2 * h], s1i_sc[2 * h]
        v2, i2 = s1v_sc[2 * h + 1], s1i_sc[2 * h + 1]
        cv, ci = [], []
        for i in range(PEER_TOPK):
            nj = PEER_TOPK // (i + 1)
            rows = PEER_TOPK if nj > 8 else 8
            a = v1[i:i + 1, :] + v2[0:rows, :]
            e = i1[i:i + 1, :] * PEER_KEYS + i2[0:rows, :]
            if nj < rows:
                a = jnp.where(sub < nj, a, -jnp.inf)
            cv.append(a)
            ci.append(e)
        top_s, eidx = _top_rows(jnp.concatenate(cv, axis=0), PEER_TOPK, jnp.concatenate(ci, axis=0))
        p = jnp.exp(top_s - top_s[0:1, :])
        g = p / jnp.sum(p, axis=0, keepdims=True)
        r0 = pl.multiple_of(h * PEER_TOPK, PEER_TOPK)
        gate_sc[pl.ds(r0, PEER_TOPK), :] = g
        idx_sc[pl.ds(r0, PEER_TOPK), :] = eidx
        return carry

    lax.fori_loop(0, n_half // 2, stage2, 0)
    idx_ref[...] = pltpu.bitcast(pltpu.bitcast(idx_sc[...], f32).T, jnp.int32)
    gate_ref[...] = gate_sc[...].T


def _retrieve(oa, ob, x, wo_bf, g2, wqt_bf, sk_bf, tm, x_block0=0):
    t, half = oa.shape
    d = x.shape[1]
    e = wqt_bf.shape[0]
    n_half = sk_bf.shape[0]
    n_sel = (n_half // 2) * PEER_TOPK
    return pl.pallas_call(
        _retrieve_kernel,
        grid=(t // tm,),
        in_specs=[
            pl.BlockSpec((tm, half), lambda i: (i, 0)),
            pl.BlockSpec((tm, half), lambda i: (i, 0)),
            pl.BlockSpec((tm, d), lambda i: (x_block0 + i, 0)),
            pl.BlockSpec((2 * half, d), lambda i: (0, 0)),
            pl.BlockSpec((1, d), lambda i: (0, 0)),
            pl.BlockSpec((e, d), lambda i: (0, 0)),
            pl.BlockSpec((n_half, PEER_KEYS, PEER_SUB_DIM), lambda i: (0, 0, 0)),
        ],
        out_specs=[
            pl.BlockSpec((tm, d), lambda i: (i, 0)),
            pl.BlockSpec((tm, d), lambda i: (i, 0)),
            pl.BlockSpec((tm, n_sel), lambda i: (i, 0)),
            pl.BlockSpec((tm, n_sel), lambda i: (i, 0)),
        ],
        out_shape=[
            jax.ShapeDtypeStruct((t, d), f32),
            jax.ShapeDtypeStruct((t, d), f32),
            jax.ShapeDtypeStruct((t, n_sel), jnp.int32),
            jax.ShapeDtypeStruct((t, n_sel), f32),
        ],
        scratch_shapes=[
            pltpu.VMEM((e, tm), bf16),
            pltpu.VMEM((n_half, PEER_TOPK, tm), f32),
            pltpu.VMEM((n_half, PEER_TOPK, tm), jnp.int32),
            pltpu.VMEM((n_sel, tm), jnp.int32),
            pltpu.VMEM((n_sel, tm), f32),
        ],
        compiler_params=_cparams("parallel"),
        name="retrieve",
    )(oa, ob, x, wo_bf, g2, wqt_bf, sk_bf)


def _pack_table(tab):
    half = tab.shape[1] // 2
    lo = lax.bitcast_convert_type(tab[:, :half].astype(bf16), jnp.uint16).astype(jnp.uint32)
    hi = lax.bitcast_convert_type(tab[:, half:].astype(bf16), jnp.uint16).astype(jnp.uint32)
    return lax.bitcast_convert_type(lo | (hi << 16), jnp.int32)


def _sc_layout(n_tok, n_sel, words):
    info = plsc.get_sparse_core_info()
    n_workers = info.num_cores * info.num_subcores
    lanes = info.num_lanes
    block = 8 * lanes
    assert n_tok % (2 * n_workers) == 0 and words % block == 0 and (n_sel // 2) % lanes == 0
    return info, lanes, n_tok // n_workers, n_sel // 2, block


def _selected_dots(table, idx, x):
    n_tok, n_sel = idx.shape
    words = table.shape[1]
    info, lanes, tpw, half_rows, block = _sc_layout(n_tok, n_sel, words)
    rows_per_iter = 4
    idx2 = idx.reshape(2 * n_tok, half_rows)
    mesh = plsc.VectorSubcoreMesh(core_axis_name="c", subcore_axis_name="s")

    @functools.partial(
        pl.kernel, out_type=jax.ShapeDtypeStruct((n_tok, n_sel), f32), mesh=mesh,
        compiler_params=pltpu.CompilerParams(needs_layout_passes=False),
        scratch_types=[
            pltpu.VMEM((2 * tpw, half_rows), jnp.int32),
            pltpu.VMEM((half_rows, words), jnp.int32),
            pltpu.VMEM((half_rows, words), jnp.int32),
            pltpu.VMEM((2 * words,), f32),
            pltpu.VMEM((2 * words,), f32),
            pltpu.VMEM((n_sel // 8, 8 * lanes), f32),
            pltpu.VMEM((tpw, n_sel), f32),
            pltpu.SemaphoreType.DMA, pltpu.SemaphoreType.DMA, pltpu.SemaphoreType.DMA, pltpu.SemaphoreType.DMA,
        ])
    def dots(t_hbm, i_hbm, x_hbm, o_hbm, idx_v, rows_a, rows_b, x_a, x_b, part_v, act_v, sem_a, sem_b, sem_xa, sem_xb):
        wid = lax.axis_index("s") * info.num_cores + lax.axis_index("c")
        base = wid * tpw
        pltpu.sync_copy(i_hbm.at[pl.ds(base * 2, 2 * tpw)], idx_v)

        def accumulate(rows, x_v, half):
            @pl.loop(0, words // block)
            def _(cb):
                xs = [x_v[pl.ds(p * words + cb * block + c * lanes, lanes)] for c in range(8) for p in range(2)]

                @pl.loop(0, half_rows // rows_per_iter)
                def _(i):
                    rws = [rows_per_iter * i + r for r in range(rows_per_iter)]
                    packed = [[rows[row, pl.ds(cb * block + c * lanes, lanes)] for c in range(8)] for row in rws]
                    unp = [[plsc.unpack(plsc.bitcast(p, bf16), format=plsc.PackFormat.INTERLEAVED) for p in pr]
                           for pr in packed]
                    prods = [[u[c][p] * xs[2 * c + p] for c in range(8) for p in range(2)] for u in unp]
                    while len(prods[0]) > 1:
                        prods = [[pr[k] + pr[k + 1] for k in range(0, len(pr), 2)] for pr in prods]
                    for r, row in enumerate(rws):
                        prow = half * half_rows + row
                        plsc.addupdate(part_v.at[prow // 8, pl.ds((prow % 8) * lanes, lanes)], prods[r][0])

        def gather(half_index, rows, sem):
            return pltpu.make_async_copy(t_hbm.at[idx_v.at[half_index]], rows, sem)

        def xcopy(t, x_v, sem):
            return pltpu.make_async_copy(x_hbm.at[base + t], x_v, sem)

        def token(t, x_v, sem_x, x_next, sem_xn):
            gather(2 * t + 1, rows_b, sem_b).start()

            @pl.when(t + 1 < tpw)
            def _():
                xcopy(t + 1, x_next, sem_xn).start()

            zero = jnp.zeros((lanes,), f32)
            for r in range(n_sel):
                part_v[r // 8, pl.ds((r % 8) * lanes, lanes)] = zero
            xcopy(t, x_v, sem_x).wait()
            gather(2 * t, rows_a, sem_a).wait()
            accumulate(rows_a, x_v, 0)

            @pl.when(t + 1 < tpw)
            def _():
                gather(2 * t + 2, rows_a, sem_a).start()

            gather(2 * t + 1, rows_b, sem_b).wait()
            accumulate(rows_b, x_v, 1)
            it = lax.iota(jnp.int32, lanes)
            for g in range(n_sel // lanes):
                prow = g * (lanes // 8) + it // 8
                pcol = (it % 8) * lanes
                tot = plsc.load_gather(part_v, [prow, pcol])
                for l in range(1, lanes):
                    tot = tot + plsc.load_gather(part_v, [prow, pcol + l])
                act_v[t, pl.ds(g * lanes, lanes)] = tot

        gather(0, rows_a, sem_a).start()
        xcopy(0, x_a, sem_xa).start()

        @pl.loop(0, tpw // 2)
        def _(i):
            token(2 * i, x_a, sem_xa, x_b, sem_xb)
            token(2 * i + 1, x_b, sem_xb, x_a, sem_xa)

        pltpu.sync_copy(act_v, o_hbm.at[pl.ds(base, tpw)])

    return dots(table, idx2, x)


def _gate_weights_kernel(act_ref, gate_ref, w_ref):
    a = act_ref[...]
    w_ref[...] = gate_ref[...] * (0.5 * a * (1.0 + lax.erf(a * (2.0 ** -0.5))))


def _gate_weights(act, gate, tm):
    t, n_sel = act.shape
    spec = pl.BlockSpec((tm, n_sel), lambda i: (i, 0))
    return pl.pallas_call(
        _gate_weights_kernel,
        grid=(t // tm,),
        in_specs=[spec, spec],
        out_specs=spec,
        out_shape=jax.ShapeDtypeStruct((t, n_sel), f32),
        compiler_params=_cparams("parallel"),
        name="peer_weights",
    )(act, gate)


def _weighted_rows(table, idx, w):
    n_tok, n_sel = w.shape
    words = table.shape[1]
    info, lanes, tpw, half_rows, block = _sc_layout(n_tok, n_sel, words)
    rows_per_iter = 2
    idx2 = idx.reshape(2 * n_tok, half_rows)
    mesh = plsc.VectorSubcoreMesh(core_axis_name="c", subcore_axis_name="s")

    @functools.partial(
        pl.kernel, out_type=jax.ShapeDtypeStruct((n_tok, 2 * words), f32), mesh=mesh,
        compiler_params=pltpu.CompilerParams(needs_layout_passes=False),
        scratch_types=[
            pltpu.VMEM((2 * tpw, half_rows), jnp.int32),
            pltpu.VMEM((tpw, n_sel), f32),
            pltpu.VMEM((half_rows, words), jnp.int32),
            pltpu.VMEM((half_rows, words), jnp.int32),
            pltpu.VMEM((2 * words,), f32),
            pltpu.VMEM((2 * words,), f32),
            pltpu.SemaphoreType.DMA, pltpu.SemaphoreType.DMA, pltpu.SemaphoreType.DMA, pltpu.SemaphoreType.DMA,
        ])
    def mix(t_hbm, i_hbm, w_hbm, o_hbm, idx_v, w_v, rows_a, rows_b, out_a, out_b, sem_a, sem_b, sem_oa, sem_ob):
        wid = lax.axis_index("s") * info.num_cores + lax.axis_index("c")
        base = wid * tpw
        pltpu.sync_copy(i_hbm.at[pl.ds(base * 2, 2 * tpw)], idx_v)
        pltpu.sync_copy(w_hbm.at[pl.ds(base, tpw)], w_v)

        def accumulate(rows, out_v, t, half, first):
            @pl.loop(0, words // block)
            def _(cb):
                def body(i, acc):
                    acc = list(acc)
                    r0 = rows_per_iter * i
                    wv = w_v[t, pl.ds(half * half_rows + (r0 // lanes) * lanes, lanes)]
                    for r in range(rows_per_iter):
                        lane_of_row = jnp.full((lanes,), r0 % lanes + r, jnp.int32)
                        wj = wv.at[lane_of_row].get(mode="promise_in_bounds")
                        for c in range(8):
                            packed = rows[r0 + r, pl.ds(cb * block + c * lanes, lanes)]
                            lo, hi = plsc.unpack(plsc.bitcast(packed, bf16), format=plsc.PackFormat.INTERLEAVED)
                            acc[2 * c] = acc[2 * c] + wj * lo
                            acc[2 * c + 1] = acc[2 * c + 1] + wj * hi
                    return tuple(acc)

                if first:
                    init = tuple(jnp.zeros((lanes,), f32) for _ in range(16))
                else:
                    init = tuple(out_v[pl.ds(p * words + cb * block + c * lanes, lanes)]
                                 for c in range(8) for p in range(2))
                acc = lax.fori_loop(0, half_rows // rows_per_iter, body, init)
                for c in range(8):
                    out_v[pl.ds(cb * block + c * lanes, lanes)] = acc[2 * c]
                    out_v[pl.ds(words + cb * block + c * lanes, lanes)] = acc[2 * c + 1]

        def gather(half_index, rows, sem):
            return pltpu.make_async_copy(t_hbm.at[idx_v.at[half_index]], rows, sem)

        def token(t, out_v, sem_o, out_in_flight):
            gather(2 * t + 1, rows_b, sem_b).start()
            gather(2 * t, rows_a, sem_a).wait()

            @pl.when(out_in_flight)
            def _():
                pltpu.make_async_copy(out_v, o_hbm.at[base + t], sem_o).wait()

            accumulate(rows_a, out_v, t, 0, True)

            @pl.when(t + 1 < tpw)
            def _():
                gather(2 * t + 2, rows_a, sem_a).start()

            gather(2 * t + 1, rows_b, sem_b).wait()
            accumulate(rows_b, out_v, t, 1, False)
            pltpu.make_async_copy(out_v, o_hbm.at[base + t], sem_o).start()

        gather(0, rows_a, sem_a).start()

        @pl.loop(0, tpw // 2)
        def _(i):
            token(2 * i, out_a, sem_oa, i > 0)
            token(2 * i + 1, out_b, sem_ob, i > 0)

        pltpu.make_async_copy(out_a, o_hbm.at[base], sem_oa).wait()
        pltpu.make_async_copy(out_b, o_hbm.at[base], sem_ob).wait()

    return mix(table, idx2, w)


def _final_kernel(hp_ref, peer_ref, gf_ref, yin_ref, y_ref):
    del yin_ref
    h = hp_ref[...] + peer_ref[...]
    y_ref[...] = h * lax.rsqrt(jnp.mean(h * h, axis=-1, keepdims=True) + EPS) * gf_ref[...]


def _final(hp, peer, gf, y_all, y_block0, tm):
    t, d = hp.shape
    return pl.pallas_call(
        _final_kernel,
        grid=(t // tm,),
        in_specs=[
            pl.BlockSpec((tm, d), lambda i: (i, 0)),
            pl.BlockSpec((tm, d), lambda i: (i, 0)),
            pl.BlockSpec((1, d), lambda i: (0, 0)),
            pl.BlockSpec(memory_space=pl.ANY),
        ],
        out_specs=pl.BlockSpec((tm, d), lambda i: (y_block0 + i, 0)),
        out_shape=jax.ShapeDtypeStruct(y_all.shape, f32),
        input_output_aliases={3: 0},
        compiler_params=_cparams("parallel"),
        name="final_norm",
    )(hp, peer, gf, y_all)


def _channel_mixer(oa, ob, x, x_block0, wo_bf, g2, wqt_bf, sk_bf, u_pk, v_pk, gf, tm, y_all, y_block0):
    hp, xn, eidx, gate = _retrieve(oa, ob, x, wo_bf, g2, wqt_bf, sk_bf, tm, x_block0)
    act = _selected_dots(u_pk, eidx, xn)
    w = _gate_weights(act, gate, tm)
    peer = _weighted_rows(v_pk, eidx, w)
    return _final(hp, peer, gf, y_all, y_block0, tm)


def kernel(x_prompt, x_sample, cache_k, cache_v, state_hgrn, norm1, w_in, da_lq1, da_lk1, da_lq2, da_lk2,
           da_out_norm, hg_lb_logits, hg_out_norm, w_out, rel_bias, norm2, peer_w_q, peer_sub_keys,
           peer_u, peer_v, final_norm):
    batch, seq, d = x_prompt.shape
    dbatch, dseq, _ = x_sample.shape
    past = cache_k.shape[2]
    da_heads = cache_k.shape[3]
    hg_heads = state_hgrn.shape[2]
    depth = w_in.shape[0]
    assert depth == 1 and seq % ATT_BLOCK == 0 and seq % CHUNK == 0 and dseq % HG_SUB == 0 and dseq <= CHUNK
    assert past % CHUNK == 0 and (past + dseq - 1) // CHUNK == past // CHUNK

    l = 0
    lam_init = 0.8 - 0.6 * math.exp(-0.3 * l)
    lam = (jnp.exp(jnp.sum(da_lq1[l].astype(f32) * da_lk1[l].astype(f32)))
           - jnp.exp(jnp.sum(da_lq2[l].astype(f32) * da_lk2[l].astype(f32))) + lam_init).reshape(1)
    lb = jnp.cumsum(jax.nn.softmax(hg_lb_logits.astype(f32), axis=0), axis=0)[l].reshape(1, -1)
    da_gain = (jnp.tile(da_out_norm[l].astype(f32), 2) * (1.0 - lam_init)).reshape(1, LANES)
    hg_gain = hg_out_norm[l].astype(f32).reshape(1, HG_DIM)
    g1 = norm1[l].astype(f32).reshape(1, d)
    g2 = norm2[l].astype(f32).reshape(1, d)
    gf = final_norm.astype(f32).reshape(1, d)
    w_in_bf = w_in[l].astype(bf16)
    wkv_t_bf = w_in[l][:, cache_k.shape[3] * cache_k.shape[4]:3 * cache_k.shape[3] * cache_k.shape[4]].T.astype(bf16)
    wo_bf = w_out[l].astype(bf16)
    wqt_bf = peer_w_q[l].T.astype(bf16)
    sk_bf = peer_sub_keys[l].reshape(-1, PEER_KEYS, PEER_SUB_DIM).astype(bf16)
    u_pk = _pack_table(peer_u[l])
    v_pk = _pack_table(peer_v[l])

    bias_tiles = _prompt_bias_tiles(rel_bias)
    q_pos = past + jnp.arange(dseq)
    bias_s = _bias_of(jnp.arange(past + dseq)[None, :] - q_pos[:, None], rel_bias) * LOG2E
    bias_past, bias_new = bias_s[:, :, :past], bias_s[:, :, past:]
    zero_state = jnp.zeros((1, hg_heads, HG_DIM, HG_DIM), f32)

    dh2 = 2 * DA_HEAD_DIM
    wq = da_heads * dh2
    tm = 256
    x_all = x_prompt.reshape(batch * seq, d)
    kt_all = jnp.zeros((batch * wq, seq), f32)
    vt_all = jnp.zeros((batch * wq, seq), f32)
    y_all = jnp.zeros((batch * seq, d), f32)
    s_p = []
    for b in range(batch):
        q, kt, vt, hz, kt_all, vt_all = _inproj_prompt(x_all, g1, w_in_bf, wkv_t_bf, kt_all, vt_all, b, seq, tm)
        oa = _attn_prompt(q, kt, vt, bias_tiles, da_gain, lam, 1, seq)
        ob, s_new = _hgrn(hz, lb, hg_gain, zero_state, 1, seq, CHUNK, 4)
        y_all = _channel_mixer(oa, ob, x_all, b * (seq // tm), wo_bf, g2, wqt_bf, sk_bf, u_pk, v_pk, gf, tm,
                               y_all, b * (seq // tm))
        s_p.append(s_new)

    xs = x_sample.reshape(dbatch * dseq, d)
    q, k_s, v_s, hz = _inproj_sample(xs, g1, w_in_bf, wq)
    kp_t = jnp.transpose(cache_k[l], (0, 2, 3, 1)).reshape(dbatch * wq, past)
    vp_t = jnp.transpose(cache_v[l], (0, 2, 3, 1)).reshape(dbatch * wq, past)
    oa = _attn_sample(q, k_s, v_s, kp_t, vp_t, bias_past, bias_new, da_gain, lam, dbatch, dseq, past)
    ob, s_s = _hgrn(hz, lb, hg_gain, state_hgrn[l].astype(f32), dbatch, dseq, dseq, 1)
    y_s = _channel_mixer(oa, ob, xs, 0, wo_bf, g2, wqt_bf, sk_bf, u_pk, v_pk, gf, dbatch * dseq,
                         jnp.zeros((dbatch * dseq, d), f32), 0)

    y_prompt = y_all.reshape(batch, seq, d)
    y_sample = y_s.reshape(dbatch, dseq, d)
    k_prompt = jnp.transpose(kt_all.reshape(batch, da_heads, dh2, seq), (0, 3, 1, 2))[None]
    v_prompt = jnp.transpose(vt_all.reshape(batch, da_heads, DA_V_DIM, seq), (0, 3, 1, 2))[None]
    state_prompt = jnp.concatenate(s_p, axis=0)[None].astype(state_hgrn.dtype)
    k_sample = k_s.reshape(1, dbatch, dseq, da_heads, dh2)
    v_sample = v_s.reshape(1, dbatch, dseq, da_heads, DA_V_DIM)
    state_sample = s_s[None].astype(state_hgrn.dtype)
    return (y_prompt, y_sample, k_prompt, v_prompt, state_prompt, k_sample, v_sample, state_sample)
```

```python
import functools
import math

import jax
import jax.numpy as jnp
from jax import lax
from jax.experimental import pallas as pl
from jax.experimental.pallas import tpu as pltpu
from jax.experimental.pallas import tpu_sc as plsc

CHUNK = 64
DA_HEAD_DIM = 32
DA_V_DIM = 2 * DA_HEAD_DIM
HG_DIM = 128
REL_BUCKETS = 32
REL_MAX_DIST = 128
PEER_HEADS = 8
PEER_KEYS = 128
PEER_TOPK = 16
PEER_SUB_DIM = 64
EPS = 1e-6
NEG = -1e30
LOG2E = 1.4426950408889634

LANES = 128
ATT_BLOCK = 128
ATT_STEP = 512
HG_SUB = 16
VMEM_LIMIT = 56 * 1024 * 1024

f32 = jnp.float32
bf16 = jnp.bfloat16


def _cparams(*sem):
    return pltpu.CompilerParams(dimension_semantics=sem, vmem_limit_bytes=VMEM_LIMIT)


def _inproj_prompt_kernel(x_ref, g_ref, w_ref, wkv_t_ref, kin_ref, vin_ref, after_ref,
                          q_ref, kt_ref, vt_ref, hz_ref, ktf_ref, vtf_ref):
    del kin_ref, vin_ref
    del after_ref
    x = x_ref[...]
    xn = x * lax.rsqrt(jnp.mean(x * x, axis=-1, keepdims=True) + EPS) * g_ref[...]
    xb = xn.astype(bf16)
    wq = q_ref.shape[1]
    nt = (((1,), (1,)), ((), ()))
    q_ref[...] = jnp.dot(xb, w_ref[:, 0:wq], preferred_element_type=f32).astype(bf16)
    kt = lax.dot_general(wkv_t_ref[0:wq, :], xb, nt, preferred_element_type=f32)
    ktf_ref[...] = kt
    kt_ref[...] = kt.astype(bf16)
    vt = lax.dot_general(wkv_t_ref[wq:2 * wq, :], xb, nt, preferred_element_type=f32)
    vtf_ref[...] = vt
    vt_ref[...] = vt.astype(bf16)
    hz_ref[...] = jnp.dot(xb, w_ref[:, 3 * wq:], preferred_element_type=f32)


def _inproj_prompt(x, g, w_bf, wkv_t_bf, kt_all, vt_all, after, b, seq, tm):
    d = x.shape[1]
    e = w_bf.shape[1]
    wq = wkv_t_bf.shape[0] // 2
    steps = seq // tm
    return pl.pallas_call(
        _inproj_prompt_kernel,
        grid=(steps,),
        in_specs=[
            pl.BlockSpec((tm, d), lambda i: (b * steps + i, 0)),
            pl.BlockSpec((1, d), lambda i: (0, 0)),
            pl.BlockSpec((d, e), lambda i: (0, 0)),
            pl.BlockSpec((2 * wq, d), lambda i: (0, 0)),
            pl.BlockSpec(memory_space=pl.ANY),
            pl.BlockSpec(memory_space=pl.ANY),
            pl.BlockSpec(memory_space=pl.ANY),
        ],
        out_specs=[
            pl.BlockSpec((tm, wq), lambda i: (i, 0)),
            pl.BlockSpec((wq, tm), lambda i: (0, i)),
            pl.BlockSpec((wq, tm), lambda i: (0, i)),
            pl.BlockSpec((tm, e - 3 * wq), lambda i: (i, 0)),
            pl.BlockSpec((wq, tm), lambda i: (b, i)),
            pl.BlockSpec((wq, tm), lambda i: (b, i)),
        ],
        out_shape=[
            jax.ShapeDtypeStruct((seq, wq), bf16),
            jax.ShapeDtypeStruct((wq, seq), bf16),
            jax.ShapeDtypeStruct((wq, seq), bf16),
            jax.ShapeDtypeStruct((seq, e - 3 * wq), f32),
            jax.ShapeDtypeStruct(kt_all.shape, f32),
            jax.ShapeDtypeStruct(vt_all.shape, f32),
        ],
        input_output_aliases={4: 4, 5: 5},
        compiler_params=_cparams("parallel"),
        name="inproj",
    )(x, g, w_bf, wkv_t_bf, kt_all, vt_all, after)


def _inproj_sample_kernel(x_ref, g_ref, w_ref, q_ref, k_ref, v_ref, hz_ref):
    x = x_ref[...]
    xn = x * lax.rsqrt(jnp.mean(x * x, axis=-1, keepdims=True) + EPS) * g_ref[...]
    xb = xn.astype(bf16)
    wq = q_ref.shape[1]
    q_ref[...] = jnp.dot(xb, w_ref[:, 0:wq], preferred_element_type=f32).astype(bf16)
    k_ref[...] = jnp.dot(xb, w_ref[:, wq:2 * wq], preferred_element_type=f32)
    v_ref[...] = jnp.dot(xb, w_ref[:, 2 * wq:3 * wq], preferred_element_type=f32)
    hz_ref[...] = jnp.dot(xb, w_ref[:, 3 * wq:], preferred_element_type=f32)


def _inproj_sample(x, g, w_bf, wq):
    t, d = x.shape
    e = w_bf.shape[1]
    return pl.pallas_call(
        _inproj_sample_kernel,
        grid=(1,),
        in_specs=[
            pl.BlockSpec((t, d), lambda i: (0, 0)),
            pl.BlockSpec((1, d), lambda i: (0, 0)),
            pl.BlockSpec((d, e), lambda i: (0, 0)),
        ],
        out_specs=[
            pl.BlockSpec((t, wq), lambda i: (0, 0)),
            pl.BlockSpec((t, wq), lambda i: (0, 0)),
            pl.BlockSpec((t, wq), lambda i: (0, 0)),
            pl.BlockSpec((t, e - 3 * wq), lambda i: (0, 0)),
        ],
        out_shape=[
            jax.ShapeDtypeStruct((t, wq), bf16),
            jax.ShapeDtypeStruct((t, wq), f32),
            jax.ShapeDtypeStruct((t, wq), f32),
            jax.ShapeDtypeStruct((t, e - 3 * wq), f32),
        ],
        compiler_params=_cparams("arbitrary"),
        name="inproj_sample",
    )(x, g, w_bf)


def _rel_bucket(rel):
    nb = REL_BUCKETS // 2
    max_exact = nb // 2
    ret = jnp.where(rel > 0, nb, 0)
    n = jnp.abs(rel)
    nf = jnp.maximum(n, 1).astype(f32)
    large = max_exact + (jnp.log(nf / max_exact) / math.log(REL_MAX_DIST / max_exact)
                         * (nb - max_exact)).astype(jnp.int32)
    large = jnp.minimum(large, nb - 1)
    return ret + jnp.where(n < max_exact, n, large)


def _bias_of(rel, rel_bias):
    bucket = _rel_bucket(rel)
    out = jnp.zeros((rel_bias.shape[1],) + rel.shape, f32)
    for b in range(REL_BUCKETS):
        out = out + jnp.where(bucket[None] == b, rel_bias[b].astype(f32)[:, None, None], 0.0)
    return out


def _attn_prompt_kernel(lam_ref, q_ref, kt_ref, vt_ref, bias_ref, gain_ref, o_ref, m_sc, acc_sc, sa_sc, sb_sc):
    qb = pl.program_id(2)
    tq = q_ref.shape[0]
    seq = kt_ref.shape[1]
    per_step = ATT_STEP // ATT_BLOCK
    lane = lax.broadcasted_iota(jnp.int32, (tq, LANES), 1)
    feat_k = lax.broadcasted_iota(jnp.int32, (LANES, ATT_STEP), 0)
    qs = q_ref[...].astype(f32) * (DA_HEAD_DIM ** -0.5 * LOG2E)
    q4 = jnp.concatenate(
        [jnp.where((lane // DA_HEAD_DIM) == i, qs, 0.0).astype(bf16) for i in range(4)], axis=0)

    m_sc[...] = jnp.full(m_sc.shape, -jnp.inf, f32)
    acc_sc[...] = jnp.zeros(acc_sc.shape, f32)

    n_steps = qb // per_step + 1
    last = n_steps - 1
    r = qb % per_step

    def scores(k):
        start = pl.multiple_of(jnp.minimum(k * ATT_STEP, seq - ATT_STEP), ATT_STEP)
        return jnp.dot(q4, kt_ref[:, pl.ds(start, ATT_STEP)], preferred_element_type=f32)

    def half(k, cur_ref, nxt_ref):
        nxt_ref[...] = scores(k + 1)
        tile = jnp.where(k == last, 1 + r, jnp.where((k == last - 1) & (r == 0), per_step + 1, 0))
        vv = vt_ref[:, pl.ds(pl.multiple_of(k * ATT_STEP, ATT_STEP), ATT_STEP)]
        for hh in range(2):
            bias = bias_ref[0, tile, hh * tq:(hh + 1) * tq, :]
            ps, alphas = [], []
            for c in range(2):
                rows = slice((2 * hh + c) * tq, (2 * hh + c + 1) * tq)
                s = cur_ref[rows, :] + bias
                m_old = m_sc[rows]
                m_new = jnp.maximum(m_old, jnp.max(s, axis=-1, keepdims=True))
                ps.append(jnp.exp2(s - m_new[:, 0:1]).astype(bf16))
                alphas.append(jnp.exp2(m_old - m_new))
                m_sc[rows] = m_new
            rows2 = slice(2 * hh * tq, (2 * hh + 2) * tq)
            vaug = jnp.where((feat_k // DA_V_DIM) == hh, vv, jnp.ones_like(vv))
            acc_sc[rows2] = (acc_sc[rows2] * jnp.concatenate(alphas, axis=0)
                             + lax.dot_general(jnp.concatenate(ps, axis=0), vaug, (((1,), (1,)), ((), ())),
                                               preferred_element_type=f32))

    sa_sc[...] = scores(0)

    def two_steps(kk, carry):
        half(2 * kk, sa_sc, sb_sc)

        @pl.when(2 * kk + 1 < n_steps)
        def _():
            half(2 * kk + 1, sb_sc, sa_sc)

        return carry

    lax.fori_loop(0, (n_steps + 1) // 2, two_steps, 0)

    lam = lam_ref[0]
    outs = []
    for hh in range(2):
        own = (lane // DA_V_DIM) == hh
        a1 = acc_sc[2 * hh * tq:(2 * hh + 1) * tq]
        a2 = acc_sc[(2 * hh + 1) * tq:(2 * hh + 2) * tq]
        l1 = jnp.max(jnp.where(own, 0.0, a1), axis=-1, keepdims=True)
        l2 = jnp.max(jnp.where(own, 0.0, a2), axis=-1, keepdims=True)
        o = a1 / l1 - lam * (a2 / l2)
        ssq = jnp.sum(jnp.where(own, o * o, 0.0), axis=-1, keepdims=True)
        outs.append(o * lax.rsqrt(ssq * (1.0 / DA_V_DIM) + EPS))
    o = jnp.where((lane // DA_V_DIM) == 0, outs[0], outs[1])
    o_ref[...] = (o * gain_ref[...]).astype(o_ref.dtype)


def _attn_prompt(q, kt, vt, bias_tiles, gain, lam, batch, seq):
    t, w = q.shape
    pairs = w // LANES
    nq = seq // ATT_BLOCK
    assert seq % ATT_STEP == 0
    n_tiles = ATT_STEP // ATT_BLOCK + 2
    return pl.pallas_call(
        _attn_prompt_kernel,
        grid=(batch, pairs, nq),
        in_specs=[
            pl.BlockSpec(memory_space=pltpu.SMEM),
            pl.BlockSpec((ATT_BLOCK, LANES), lambda b, p, i: (b * nq + i, p)),
            pl.BlockSpec((LANES, seq), lambda b, p, i: (p, b)),
            pl.BlockSpec((LANES, seq), lambda b, p, i: (p, b)),
            pl.BlockSpec((1, n_tiles, 2 * ATT_BLOCK, ATT_STEP), lambda b, p, i: (p, 0, 0, 0)),
            pl.BlockSpec((1, LANES), lambda b, p, i: (0, 0)),
        ],
        out_specs=pl.BlockSpec((ATT_BLOCK, LANES), lambda b, p, i: (b * nq + i, p)),
        out_shape=jax.ShapeDtypeStruct((t, w), bf16),
        scratch_shapes=[
            pltpu.VMEM((4 * ATT_BLOCK, LANES), f32),
            pltpu.VMEM((4 * ATT_BLOCK, LANES), f32),
            pltpu.VMEM((4 * ATT_BLOCK, ATT_STEP), f32),
            pltpu.VMEM((4 * ATT_BLOCK, ATT_STEP), f32),
        ],
        compiler_params=_cparams("parallel", "parallel", "arbitrary"),
        name="attn_prompt",
    )(lam, q, kt, vt, bias_tiles, gain)


def _prompt_bias_tiles(rel_bias):
    h = rel_bias.shape[1]
    per_step = ATT_STEP // ATT_BLOCK
    i = jnp.arange(ATT_BLOCK)
    rel_diag = i[None, :] - i[:, None]
    rel_prev = rel_diag - ATT_BLOCK
    far = rel_bias[REL_BUCKETS // 2 - 1].astype(f32)
    b_diag = (_bias_of(rel_diag, rel_bias) - far[:, None, None]) * LOG2E
    b_prev = (_bias_of(rel_prev, rel_bias) - far[:, None, None]) * LOG2E
    mask = (i[None, :] // CHUNK) <= (i[:, None] // CHUNK)
    b_diag = jnp.where(mask[None], b_diag, NEG)
    zero = jnp.zeros_like(b_diag)
    dead = jnp.full_like(b_diag, NEG)
    tiles = [jnp.concatenate([zero] * per_step, axis=-1)]
    for r in range(per_step):
        blocks = [zero if j < r - 1 else b_prev if j == r - 1 else b_diag if j == r else dead
                  for j in range(per_step)]
        tiles.append(jnp.concatenate(blocks, axis=-1))
    tiles.append(jnp.concatenate([zero] * (per_step - 1) + [b_prev], axis=-1))
    tiles = jnp.stack(tiles, axis=1)
    tiles = tiles.reshape(h // 2, 2, per_step + 2, ATT_BLOCK, ATT_STEP)
    return jnp.transpose(tiles, (0, 2, 1, 3, 4)).reshape(h // 2, per_step + 2, 2 * ATT_BLOCK, ATT_STEP)


def _attn_sample_kernel(lam_ref, q_ref, kp_ref, vp_ref, kn_ref, vn_ref, bp_ref, bn_ref, gain_ref, o_ref):
    nq = q_ref.shape[0]
    pairs = q_ref.shape[1] // LANES
    lane = lax.broadcasted_iota(jnp.int32, (nq, LANES), 1)
    lam = lam_ref[0]
    nt = (((1,), (1,)), ((), ()))
    for p in range(pairs):
        sl = slice(p * LANES, (p + 1) * LANES)
        qs = q_ref[:, sl].astype(f32) * (DA_HEAD_DIM ** -0.5 * LOG2E)
        kp = kp_ref[sl, :].astype(bf16)
        vp = vp_ref[sl, :].astype(bf16)
        kn = kn_ref[:, sl].astype(bf16)
        vn = vn_ref[:, sl].astype(bf16)
        outs = []
        for hh in range(2):
            h = 2 * p + hh
            own = (lane // DA_V_DIM) == hh
            res = []
            for c in range(2):
                qm = jnp.where((lane // DA_HEAD_DIM) == 2 * hh + c, qs, 0.0).astype(bf16)
                sp = jnp.dot(qm, kp, preferred_element_type=f32) + bp_ref[h]
                sn = lax.dot_general(qm, kn, nt, preferred_element_type=f32) + bn_ref[h]
                m = jnp.maximum(jnp.max(sp, axis=-1, keepdims=True), jnp.max(sn, axis=-1, keepdims=True))
                pp = jnp.exp2(sp - m)
                pn = jnp.exp2(sn - m)
                l = jnp.sum(pp, axis=-1, keepdims=True) + jnp.sum(pn, axis=-1, keepdims=True)
                pv = (lax.dot_general(pp.astype(bf16), vp, nt, preferred_element_type=f32)
                      + jnp.dot(pn.astype(bf16), vn, preferred_element_type=f32))
                res.append(pv / l)
            o = res[0] - lam * res[1]
            ssq = jnp.sum(jnp.where(own, o * o, 0.0), axis=-1, keepdims=True)
            outs.append(o * lax.rsqrt(ssq * (1.0 / DA_V_DIM) + EPS))
        o = jnp.where((lane // DA_V_DIM) == 0, outs[0], outs[1])
        o_ref[:, sl] = (o * gain_ref[...]).astype(o_ref.dtype)


def _attn_sample(q, k_new, v_new, k_past, v_past, bias_past, bias_new, gain, lam, batch, nq, past):
    w = q.shape[1]
    h = bias_past.shape[0]
    return pl.pallas_call(
        _attn_sample_kernel,
        grid=(batch,),
        in_specs=[
            pl.BlockSpec(memory_space=pltpu.SMEM),
            pl.BlockSpec((nq, w), lambda b: (b, 0)),
            pl.BlockSpec((w, past), lambda b: (b, 0)),
            pl.BlockSpec((w, past), lambda b: (b, 0)),
            pl.BlockSpec((nq, w), lambda b: (b, 0)),
            pl.BlockSpec((nq, w), lambda b: (b, 0)),
            pl.BlockSpec((h, nq, past), lambda b: (0, 0, 0)),
            pl.BlockSpec((h, nq, nq), lambda b: (0, 0, 0)),
            pl.BlockSpec((1, LANES), lambda b: (0, 0)),
        ],
        out_specs=pl.BlockSpec((nq, w), lambda b: (b, 0)),
        out_shape=jax.ShapeDtypeStruct((batch * nq, w), bf16),
        compiler_params=_cparams("parallel"),
        name="attn_sample",
    )(lam, q, k_past, v_past, k_new, v_new, bias_past, bias_new, gain)


def _hgrn_chunk(q, f_logit, vv, g_logit, lb, gain, s0, lc):
    n_sub = lc // HG_SUB
    f = lb + (1.0 - lb) * jax.nn.sigmoid(f_logit)
    logf = jnp.log(f)
    k = 1.0 - f
    row = lax.broadcasted_iota(jnp.int32, (lc, lc), 0)
    col = lax.broadcasted_iota(jnp.int32, (lc, lc), 1)
    tril = (col <= row).astype(f32)
    b = jnp.dot(tril, logf, precision=lax.Precision.HIGHEST, preferred_element_type=f32)
    b_last = b[lc - 1:lc, :]

    o = jnp.dot((q * jnp.exp(b)).astype(bf16), s0.astype(bf16), preferred_element_type=f32)
    k_hat = k * jnp.exp(b_last - b)
    bt = b.T
    decay_col = jnp.exp(bt[:, lc - 1:lc])
    s_new = decay_col * s0 + jnp.dot(k_hat.T.astype(bf16), vv.astype(bf16), preferred_element_type=f32)

    if n_sub > 1:
        rows = []
        nt = (((1,), (1,)), ((), ()))
        for i in range(n_sub):
            lo = i * HG_SUB
            if i == 0:
                rows.append(jnp.zeros((HG_SUB, lc), f32))
                continue
            beta = b[lo - 1:lo, :]
            q_t = q[lo:lo + HG_SUB, :] * jnp.exp(b[lo:lo + HG_SUB, :] - beta)
            k_t = k * jnp.exp(jnp.minimum(beta - b, 0.0))
            rows.append(lax.dot_general(q_t.astype(bf16), k_t.astype(bf16), nt, preferred_element_type=f32))
        a_off = jnp.concatenate(rows, axis=0)
        a_off = jnp.where(col < (row // HG_SUB) * HG_SUB, a_off, 0.0)
        o = o + jnp.dot(a_off.astype(bf16), vv.astype(bf16), preferred_element_type=f32)

    ones = jnp.ones((HG_DIM, LANES), bf16)
    srow = lax.broadcasted_iota(jnp.int32, (HG_SUB, HG_DIM), 0)
    orow = lax.broadcasted_iota(jnp.int32, (HG_SUB, HG_DIM), 0)
    blocks = []
    for i in range(n_sub):
        lo = i * HG_SUB
        b_i = b[lo:lo + HG_SUB, :]
        k_i = k[lo:lo + HG_SUB, :]
        v_i = vv[lo:lo + HG_SUB, :]
        q_i = q[lo:lo + HG_SUB, :]
        d_rows = []
        for t in range(HG_SUB):
            arg = jnp.where(srow <= t, b_i[t:t + 1, :] - b_i, -jnp.inf)
            d_rows.append(q_i[t:t + 1, :] * k_i * jnp.exp(arg))
        d3 = jnp.concatenate(d_rows, axis=0)
        a_rep = jnp.dot(d3.astype(bf16), ones, preferred_element_type=f32)
        o_i = jnp.zeros((HG_SUB, HG_DIM), f32)
        for t in range(HG_SUB):
            o_t = jnp.sum(a_rep[t * HG_SUB:(t + 1) * HG_SUB, :] * v_i, axis=0, keepdims=True)
            o_i = jnp.where(orow == t, o_t, o_i)
        blocks.append(o_i)
    o = o + (jnp.concatenate(blocks, axis=0) if n_sub > 1 else blocks[0])

    on = o * lax.rsqrt(jnp.mean(o * o, axis=-1, keepdims=True) + EPS) * gain
    gate = g_logit * jax.nn.sigmoid(g_logit)
    return on * gate, s_new


def _hgrn_kernel(hz_ref, lb_ref, gain_ref, s0_ref, o_ref, s_ref, st_sc, *, lc, heads):
    j = pl.program_id(1)
    w = heads * HG_DIM

    @pl.when(j == 0)
    def _():
        st_sc[...] = s0_ref[0]

    n_chunks = hz_ref.shape[0] // lc
    for c in range(n_chunks):
        rs = slice(c * lc, (c + 1) * lc)
        for h in range(heads):
            cs = slice(h * HG_DIM, (h + 1) * HG_DIM)
            o, s_new = _hgrn_chunk(
                hz_ref[rs, h * HG_DIM:(h + 1) * HG_DIM],
                hz_ref[rs, w + h * HG_DIM:w + (h + 1) * HG_DIM],
                hz_ref[rs, 2 * w + h * HG_DIM:2 * w + (h + 1) * HG_DIM],
                hz_ref[rs, 3 * w + h * HG_DIM:3 * w + (h + 1) * HG_DIM],
                lb_ref[:, cs], gain_ref[...], st_sc[h], lc)
            st_sc[h] = s_new
            o_ref[rs, cs] = o.astype(o_ref.dtype)

    @pl.when(j == pl.num_programs(1) - 1)
    def _():
        s_ref[0] = st_sc[...]


def _hgrn(hz, lb, gain, s0, batch, seq, lc, chunks_per_step):
    t = hz.shape[0]
    w = hz.shape[1] // 4
    heads = w // HG_DIM
    rows = lc * chunks_per_step
    steps = seq // rows
    return pl.pallas_call(
        functools.partial(_hgrn_kernel, lc=lc, heads=heads),
        grid=(batch, steps),
        in_specs=[
            pl.BlockSpec((rows, 4 * w), lambda b, j: (b * steps + j, 0)),
            pl.BlockSpec((1, w), lambda b, j: (0, 0)),
            pl.BlockSpec((1, HG_DIM), lambda b, j: (0, 0)),
            pl.BlockSpec((1, heads, HG_DIM, HG_DIM), lambda b, j: (b, 0, 0, 0)),
        ],
        out_specs=[
            pl.BlockSpec((rows, w), lambda b, j: (b * steps + j, 0)),
            pl.BlockSpec((1, heads, HG_DIM, HG_DIM), lambda b, j: (b, 0, 0, 0)),
        ],
        out_shape=[
            jax.ShapeDtypeStruct((t, w), bf16),
            jax.ShapeDtypeStruct((batch, heads, HG_DIM, HG_DIM), f32),
        ],
        scratch_shapes=[pltpu.VMEM((heads, HG_DIM, HG_DIM), f32)],
        compiler_params=_cparams("parallel", "arbitrary"),
        name="hgrn2",
    )(hz, lb, gain, s0)


def _top_rows(vals, k, payload=None):
    n_rows = vals.shape[0]
    row = lax.broadcasted_iota(jnp.int32, vals.shape, 0)
    out_v, out_i = [], []
    for _ in range(k):
        m = jnp.max(vals, axis=0, keepdims=True)
        first = jnp.min(jnp.where(vals == m, row, n_rows), axis=0, keepdims=True)
        hit = row == first
        out_v.append(m)
        if payload is None:
            out_i.append(first)
        else:
            out_i.append(jnp.max(jnp.where(hit, payload, -1), axis=0, keepdims=True))
        vals = jnp.where(hit, -jnp.inf, vals)
    return jnp.concatenate(out_v, axis=0), jnp.concatenate(out_i, axis=0)


def _retrieve_kernel(oa_ref, ob_ref, x_ref, wo_ref, g_ref, wqt_ref, sk_ref,
                     hp_ref, xn_ref, idx_ref, gate_ref, qt_sc, s1v_sc, s1i_sc, idx_sc, gate_sc):
    half = oa_ref.shape[1]
    y = (jnp.dot(oa_ref[...], wo_ref[0:half, :], preferred_element_type=f32)
         + jnp.dot(ob_ref[...], wo_ref[half:, :], preferred_element_type=f32))
    hp = x_ref[...] + y
    hp_ref[...] = hp
    xn = hp * lax.rsqrt(jnp.mean(hp * hp, axis=-1, keepdims=True) + EPS) * g_ref[...]
    xn_ref[...] = xn
    nt = (((1,), (1,)), ((), ()))
    qt_sc[...] = lax.dot_general(wqt_ref[...], xn.astype(bf16), nt, preferred_element_type=f32).astype(bf16)

    n_half = sk_ref.shape[0]

    def stage1(hc, carry):
        start = pl.multiple_of(hc * PEER_SUB_DIM, PEER_SUB_DIM)
        s = jnp.dot(sk_ref[hc], qt_sc[pl.ds(start, PEER_SUB_DIM), :], preferred_element_type=f32)
        v, i = _top_rows(s, PEER_TOPK)
        s1v_sc[hc] = v
        s1i_sc[hc] = i
        return carry

    lax.fori_loop(0, n_half, stage1, 0)

    tm = x_ref.shape[0]
    sub = lax.broadcasted_iota(jnp.int32, (8, tm), 0)

    def stage2(h, carry):
        v1, i1 = s1v_sc[2 * h], s1i_sc[2 * h]
        v2, i2 = s1v_sc[2 * h + 1], s1i_sc[2 * h + 1]
        cv, ci = [], []
        for i in range(PEER_TOPK):
            nj = PEER_TOPK // (i + 1)
            rows = PEER_TOPK if nj > 8 else 8
            a = v1[i:i + 1, :] + v2[0:rows, :]
            e = i1[i:i + 1, :] * PEER_KEYS + i2[0:rows, :]
            if nj < rows:
                a = jnp.where(sub < nj, a, -jnp.inf)
            cv.append(a)
            ci.append(e)
        top_s, eidx = _top_rows(jnp.concatenate(cv, axis=0), PEER_TOPK, jnp.concatenate(ci, axis=0))
        p = jnp.exp(top_s - top_s[0:1, :])
        g = p / jnp.sum(p, axis=0, keepdims=True)
        r0 = pl.multiple_of(h * PEER_TOPK, PEER_TOPK)
        gate_sc[pl.ds(r0, PEER_TOPK), :] = g
        idx_sc[pl.ds(r0, PEER_TOPK), :] = eidx
        return carry

    lax.fori_loop(0, n_half // 2, stage2, 0)
    idx_ref[...] = pltpu.bitcast(pltpu.bitcast(idx_sc[...], f32).T, jnp.int32)
    gate_ref[...] = gate_sc[...].T


def _retrieve(oa, ob, x, wo_bf, g2, wqt_bf, sk_bf, tm, x_block0=0):
    t, half = oa.shape
    d = x.shape[1]
    e = wqt_bf.shape[0]
    n_half = sk_bf.shape[0]
    n_sel = (n_half // 2) * PEER_TOPK
    return pl.pallas_call(
        _retrieve_kernel,
        grid=(t // tm,),
        in_specs=[
            pl.BlockSpec((tm, half), lambda i: (i, 0)),
            pl.BlockSpec((tm, half), lambda i: (i, 0)),
            pl.BlockSpec((tm, d), lambda i: (x_block0 + i, 0)),
            pl.BlockSpec((2 * half, d), lambda i: (0, 0)),
            pl.BlockSpec((1, d), lambda i: (0, 0)),
            pl.BlockSpec((e, d), lambda i: (0, 0)),
            pl.BlockSpec((n_half, PEER_KEYS, PEER_SUB_DIM), lambda i: (0, 0, 0)),
        ],
        out_specs=[
            pl.BlockSpec((tm, d), lambda i: (i, 0)),
            pl.BlockSpec((tm, d), lambda i: (i, 0)),
            pl.BlockSpec((tm, n_sel), lambda i: (i, 0)),
            pl.BlockSpec((tm, n_sel), lambda i: (i, 0)),
        ],
        out_shape=[
            jax.ShapeDtypeStruct((t, d), f32),
            jax.ShapeDtypeStruct((t, d), f32),
            jax.ShapeDtypeStruct((t, n_sel), jnp.int32),
            jax.ShapeDtypeStruct((t, n_sel), f32),
        ],
        scratch_shapes=[
            pltpu.VMEM((e, tm), bf16),
            pltpu.VMEM((n_half, PEER_TOPK, tm), f32),
            pltpu.VMEM((n_half, PEER_TOPK, tm), jnp.int32),
            pltpu.VMEM((n_sel, tm), jnp.int32),
            pltpu.VMEM((n_sel, tm), f32),
        ],
        compiler_params=_cparams("parallel"),
        name="retrieve",
    )(oa, ob, x, wo_bf, g2, wqt_bf, sk_bf)


def _pack_table(tab):
    half = tab.shape[1] // 2
    lo = lax.bitcast_convert_type(tab[:, :half].astype(bf16), jnp.uint16).astype(jnp.uint32)
    hi = lax.bitcast_convert_type(tab[:, half:].astype(bf16), jnp.uint16).astype(jnp.uint32)
    return lax.bitcast_convert_type(lo | (hi << 16), jnp.int32)


def _sc_layout(n_tok, n_sel, words):
    info = plsc.get_sparse_core_info()
    n_workers = info.num_cores * info.num_subcores
    lanes = info.num_lanes
    block = 8 * lanes
    assert n_tok % (2 * n_workers) == 0 and words % block == 0 and (n_sel // 2) % lanes == 0
    return info, lanes, n_tok // n_workers, n_sel // 2, block


def _selected_dots(table, idx, x):
    n_tok, n_sel = idx.shape
    words = table.shape[1]
    info, lanes, tpw, half_rows, block = _sc_layout(n_tok, n_sel, words)
    rows_per_iter = 4
    idx2 = idx.reshape(2 * n_tok, half_rows)
    mesh = plsc.VectorSubcoreMesh(core_axis_name="c", subcore_axis_name="s")

    @functools.partial(
        pl.kernel, out_type=jax.ShapeDtypeStruct((n_tok, n_sel), f32), mesh=mesh,
        compiler_params=pltpu.CompilerParams(needs_layout_passes=False),
        scratch_types=[
            pltpu.VMEM((2 * tpw, half_rows), jnp.int32),
            pltpu.VMEM((half_rows, words), jnp.int32),
            pltpu.VMEM((half_rows, words), jnp.int32),
            pltpu.VMEM((2 * words,), f32),
            pltpu.VMEM((2 * words,), f32),
            pltpu.VMEM((n_sel // 8, 8 * lanes), f32),
            pltpu.VMEM((tpw, n_sel), f32),
            pltpu.SemaphoreType.DMA, pltpu.SemaphoreType.DMA, pltpu.SemaphoreType.DMA, pltpu.SemaphoreType.DMA,
        ])
    def dots(t_hbm, i_hbm, x_hbm, o_hbm, idx_v, rows_a, rows_b, x_a, x_b, part_v, act_v, sem_a, sem_b, sem_xa, sem_xb):
        wid = lax.axis_index("s") * info.num_cores + lax.axis_index("c")
        base = wid * tpw
        pltpu.sync_copy(i_hbm.at[pl.ds(base * 2, 2 * tpw)], idx_v)

        def accumulate(rows, x_v, half):
            @pl.loop(0, words // block)
            def _(cb):
                xs = [x_v[pl.ds(p * words + cb * block + c * lanes, lanes)] for c in range(8) for p in range(2)]

                @pl.loop(0, half_rows // rows_per_iter)
                def _(i):
                    rws = [rows_per_iter * i + r for r in range(rows_per_iter)]
                    packed = [[rows[row, pl.ds(cb * block + c * lanes, lanes)] for c in range(8)] for row in rws]
                    unp = [[plsc.unpack(plsc.bitcast(p, bf16), format=plsc.PackFormat.INTERLEAVED) for p in pr]
                           for pr in packed]
                    prods = [[u[c][p] * xs[2 * c + p] for c in range(8) for p in range(2)] for u in unp]
                    while len(prods[0]) > 1:
                        prods = [[pr[k] + pr[k + 1] for k in range(0, len(pr), 2)] for pr in prods]
                    for r, row in enumerate(rws):
                        prow = half * half_rows + row
                        plsc.addupdate(part_v.at[prow // 8, pl.ds((prow % 8) * lanes, lanes)], prods[r][0])

        def gather(half_index, rows, sem):
            return pltpu.make_async_copy(t_hbm.at[idx_v.at[half_index]], rows, sem)

        def xcopy(t, x_v, sem):
            return pltpu.make_async_copy(x_hbm.at[base + t], x_v, sem)

        def token(t, x_v, sem_x, x_next, sem_xn):
            gather(2 * t + 1, rows_b, sem_b).start()

            @pl.when(t + 1 < tpw)
            def _():
                xcopy(t + 1, x_next, sem_xn).start()

            zero = jnp.zeros((lanes,), f32)
            for r in range(n_sel):
                part_v[r // 8, pl.ds((r % 8) * lanes, lanes)] = zero
            xcopy(t, x_v, sem_x).wait()
            gather(2 * t, rows_a, sem_a).wait()
            accumulate(rows_a, x_v, 0)

            @pl.when(t + 1 < tpw)
            def _():
                gather(2 * t + 2, rows_a, sem_a).start()

            gather(2 * t + 1, rows_b, sem_b).wait()
            accumulate(rows_b, x_v, 1)
            it = lax.iota(jnp.int32, lanes)
            for g in range(n_sel // lanes):
                prow = g * (lanes // 8) + it // 8
                pcol = (it % 8) * lanes
                tot = plsc.load_gather(part_v, [prow, pcol])
                for l in range(1, lanes):
                    tot = tot + plsc.load_gather(part_v, [prow, pcol + l])
                act_v[t, pl.ds(g * lanes, lanes)] = tot

        gather(0, rows_a, sem_a).start()
        xcopy(0, x_a, sem_xa).start()

        @pl.loop(0, tpw // 2)
        def _(i):
            token(2 * i, x_a, sem_xa, x_b, sem_xb)
            token(2 * i + 1, x_b, sem_xb, x_a, sem_xa)

        pltpu.sync_copy(act_v, o_hbm.at[pl.ds(base, tpw)])

    return dots(table, idx2, x)


def _gate_weights_kernel(act_ref, gate_ref, w_ref):
    a = act_ref[...]
    w_ref[...] = gate_ref[...] * (0.5 * a * (1.0 + lax.erf(a * (2.0 ** -0.5))))


def _gate_weights(act, gate, tm):
    t, n_sel = act.shape
    spec = pl.BlockSpec((tm, n_sel), lambda i: (i, 0))
    return pl.pallas_call(
        _gate_weights_kernel,
        grid=(t // tm,),
        in_specs=[spec, spec],
        out_specs=spec,
        out_shape=jax.ShapeDtypeStruct((t, n_sel), f32),
        compiler_params=_cparams("parallel"),
        name="peer_weights",
    )(act, gate)


def _weighted_rows(table, idx, w):
    n_tok, n_sel = w.shape
    words = table.shape[1]
    info, lanes, tpw, half_rows, block = _sc_layout(n_tok, n_sel, words)
    rows_per_iter = 2
    idx2 = idx.reshape(2 * n_tok, half_rows)
    mesh = plsc.VectorSubcoreMesh(core_axis_name="c", subcore_axis_name="s")

    @functools.partial(
        pl.kernel, out_type=jax.ShapeDtypeStruct((n_tok, 2 * words), f32), mesh=mesh,
        compiler_params=pltpu.CompilerParams(needs_layout_passes=False),
        scratch_types=[
            pltpu.VMEM((2 * tpw, half_rows), jnp.int32),
            pltpu.VMEM((tpw, n_sel), f32),
            pltpu.VMEM((half_rows, words), jnp.int32),
            pltpu.VMEM((half_rows, words), jnp.int32),
            pltpu.VMEM((2 * words,), f32),
            pltpu.VMEM((2 * words,), f32),
            pltpu.SemaphoreType.DMA, pltpu.SemaphoreType.DMA, pltpu.SemaphoreType.DMA, pltpu.SemaphoreType.DMA,
        ])
    def mix(t_hbm, i_hbm, w_hbm, o_hbm, idx_v, w_v, rows_a, rows_b, out_a, out_b, sem_a, sem_b, sem_oa, sem_ob):
        wid = lax.axis_index("s") * info.num_cores + lax.axis_index("c")
        base = wid * tpw
        pltpu.sync_copy(i_hbm.at[pl.ds(base * 2, 2 * tpw)], idx_v)
        pltpu.sync_copy(w_hbm.at[pl.ds(base, tpw)], w_v)

        def accumulate(rows, out_v, t, half, first):
            @pl.loop(0, words // block)
            def _(cb):
                def body(i, acc):
                    acc = list(acc)
                    r0 = rows_per_iter * i
                    wv = w_v[t, pl.ds(half * half_rows + (r0 // lanes) * lanes, lanes)]
                    for r in range(rows_per_iter):
                        lane_of_row = jnp.full((lanes,), r0 % lanes + r, jnp.int32)
                        wj = wv.at[lane_of_row].get(mode="promise_in_bounds")
                        for c in range(8):
                            packed = rows[r0 + r, pl.ds(cb * block + c * lanes, lanes)]
                            lo, hi = plsc.unpack(plsc.bitcast(packed, bf16), format=plsc.PackFormat.INTERLEAVED)
                            acc[2 * c] = acc[2 * c] + wj * lo
                            acc[2 * c + 1] = acc[2 * c + 1] + wj * hi
                    return tuple(acc)

                if first:
                    init = tuple(jnp.zeros((lanes,), f32) for _ in range(16))
                else:
                    init = tuple(out_v[pl.ds(p * words + cb * block + c * lanes, lanes)]
                                 for c in range(8) for p in range(2))
                acc = lax.fori_loop(0, half_rows // rows_per_iter, body, init)
                for c in range(8):
                    out_v[pl.ds(cb * block + c * lanes, lanes)] = acc[2 * c]
                    out_v[pl.ds(words + cb * block + c * lanes, lanes)] = acc[2 * c + 1]

        def gather(half_index, rows, sem):
            return pltpu.make_async_copy(t_hbm.at[idx_v.at[half_index]], rows, sem)

        def token(t, out_v, sem_o, out_in_flight):
            gather(2 * t + 1, rows_b, sem_b).start()
            gather(2 * t, rows_a, sem_a).wait()

            @pl.when(out_in_flight)
            def _():
                pltpu.make_async_copy(out_v, o_hbm.at[base + t], sem_o).wait()

            accumulate(rows_a, out_v, t, 0, True)

            @pl.when(t + 1 < tpw)
            def _():
                gather(2 * t + 2, rows_a, sem_a).start()

            gather(2 * t + 1, rows_b, sem_b).wait()
            accumulate(rows_b, out_v, t, 1, False)
            pltpu.make_async_copy(out_v, o_hbm.at[base + t], sem_o).start()

        gather(0, rows_a, sem_a).start()

        @pl.loop(0, tpw // 2)
        def _(i):
            token(2 * i, out_a, sem_oa, i > 0)
            token(2 * i + 1, out_b, sem_ob, i > 0)

        pltpu.make_async_copy(out_a, o_hbm.at[base], sem_oa).wait()
        pltpu.make_async_copy(out_b, o_hbm.at[base], sem_ob).wait()

    return mix(table, idx2, w)


def _final_kernel(hp_ref, peer_ref, gf_ref, yin_ref, y_ref):
    del yin_ref
    h = hp_ref[...] + peer_ref[...]
    y_ref[...] = h * lax.rsqrt(jnp.mean(h * h, axis=-1, keepdims=True) + EPS) * gf_ref[...]


def _final(hp, peer, gf, y_all, y_block0, tm):
    t, d = hp.shape
    return pl.pallas_call(
        _final_kernel,
        grid=(t // tm,),
        in_specs=[
            pl.BlockSpec((tm, d), lambda i: (i, 0)),
            pl.BlockSpec((tm, d), lambda i: (i, 0)),
            pl.BlockSpec((1, d), lambda i: (0, 0)),
            pl.BlockSpec(memory_space=pl.ANY),
        ],
        out_specs=pl.BlockSpec((tm, d), lambda i: (y_block0 + i, 0)),
        out_shape=jax.ShapeDtypeStruct(y_all.shape, f32),
        input_output_aliases={3: 0},
        compiler_params=_cparams("parallel"),
        name="final_norm",
    )(hp, peer, gf, y_all)


def _channel_mixer(oa, ob, x, x_block0, wo_bf, g2, wqt_bf, sk_bf, u_pk, v_pk, gf, tm, y_all, y_block0):
    hp, xn, eidx, gate = _retrieve(oa, ob, x, wo_bf, g2, wqt_bf, sk_bf, tm, x_block0)
    act = _selected_dots(u_pk, eidx, xn)
    w = _gate_weights(act, gate, tm)
    peer = _weighted_rows(v_pk, eidx, w)
    return _final(hp, peer, gf, y_all, y_block0, tm), w


def kernel(x_prompt, x_sample, cache_k, cache_v, state_hgrn, norm1, w_in, da_lq1, da_lk1, da_lq2, da_lk2,
           da_out_norm, hg_lb_logits, hg_out_norm, w_out, rel_bias, norm2, peer_w_q, peer_sub_keys,
           peer_u, peer_v, final_norm):
    batch, seq, d = x_prompt.shape
    dbatch, dseq, _ = x_sample.shape
    past = cache_k.shape[2]
    da_heads = cache_k.shape[3]
    hg_heads = state_hgrn.shape[2]
    depth = w_in.shape[0]
    assert depth == 1 and seq % ATT_BLOCK == 0 and seq % CHUNK == 0 and dseq % HG_SUB == 0 and dseq <= CHUNK
    assert past % CHUNK == 0 and (past + dseq - 1) // CHUNK == past // CHUNK

    l = 0
    lam_init = 0.8 - 0.6 * math.exp(-0.3 * l)
    lam = (jnp.exp(jnp.sum(da_lq1[l].astype(f32) * da_lk1[l].astype(f32)))
           - jnp.exp(jnp.sum(da_lq2[l].astype(f32) * da_lk2[l].astype(f32))) + lam_init).reshape(1)
    lb = jnp.cumsum(jax.nn.softmax(hg_lb_logits.astype(f32), axis=0), axis=0)[l].reshape(1, -1)
    da_gain = (jnp.tile(da_out_norm[l].astype(f32), 2) * (1.0 - lam_init)).reshape(1, LANES)
    hg_gain = hg_out_norm[l].astype(f32).reshape(1, HG_DIM)
    g1 = norm1[l].astype(f32).reshape(1, d)
    g2 = norm2[l].astype(f32).reshape(1, d)
    gf = final_norm.astype(f32).reshape(1, d)
    w_in_bf = w_in[l].astype(bf16)
    wkv_t_bf = w_in[l][:, cache_k.shape[3] * cache_k.shape[4]:3 * cache_k.shape[3] * cache_k.shape[4]].T.astype(bf16)
    wo_bf = w_out[l].astype(bf16)
    wqt_bf = peer_w_q[l].T.astype(bf16)
    sk_bf = peer_sub_keys[l].reshape(-1, PEER_KEYS, PEER_SUB_DIM).astype(bf16)
    u_pk = _pack_table(peer_u[l])
    v_pk = _pack_table(peer_v[l])

    bias_tiles = _prompt_bias_tiles(rel_bias)
    q_pos = past + jnp.arange(dseq)
    bias_s = _bias_of(jnp.arange(past + dseq)[None, :] - q_pos[:, None], rel_bias) * LOG2E
    bias_past, bias_new = bias_s[:, :, :past], bias_s[:, :, past:]
    zero_state = jnp.zeros((1, hg_heads, HG_DIM, HG_DIM), f32)

    dh2 = 2 * DA_HEAD_DIM
    wq = da_heads * dh2
    tm = 256
    x_all = x_prompt.reshape(batch * seq, d)
    kt_all = jnp.zeros((batch * wq, seq), f32)
    vt_all = jnp.zeros((batch * wq, seq), f32)
    y_all = jnp.zeros((batch * seq, d), f32)
    s_p = []
    weights = [jnp.zeros((8, LANES), f32)] * 2
    for b in range(batch):
        q, kt, vt, hz, kt_all, vt_all = _inproj_prompt(x_all, g1, w_in_bf, wkv_t_bf, kt_all, vt_all, weights[b],
                                                       b, seq, tm)
        oa = _attn_prompt(q, kt, vt, bias_tiles, da_gain, lam, 1, seq)
        ob, s_new = _hgrn(hz, lb, hg_gain, zero_state, 1, seq, CHUNK, 4)
        y_all, w = _channel_mixer(oa, ob, x_all, b * (seq // tm), wo_bf, g2, wqt_bf, sk_bf, u_pk, v_pk, gf, tm,
                                  y_all, b * (seq // tm))
        weights.append(w)
        s_p.append(s_new)

    xs = x_sample.reshape(dbatch * dseq, d)
    q, k_s, v_s, hz = _inproj_sample(xs, g1, w_in_bf, wq)
    kp_t = jnp.transpose(cache_k[l], (0, 2, 3, 1)).reshape(dbatch * wq, past)
    vp_t = jnp.transpose(cache_v[l], (0, 2, 3, 1)).reshape(dbatch * wq, past)
    oa = _attn_sample(q, k_s, v_s, kp_t, vp_t, bias_past, bias_new, da_gain, lam, dbatch, dseq, past)
    ob, s_s = _hgrn(hz, lb, hg_gain, state_hgrn[l].astype(f32), dbatch, dseq, dseq, 1)
    y_s, _ = _channel_mixer(oa, ob, xs, 0, wo_bf, g2, wqt_bf, sk_bf, u_pk, v_pk, gf, dbatch * dseq,
                            jnp.zeros((dbatch * dseq, d), f32), 0)

    y_prompt = y_all.reshape(batch, seq, d)
    y_sample = y_s.reshape(dbatch, dseq, d)
    k_prompt = jnp.transpose(kt_all.reshape(batch, da_heads, dh2, seq), (0, 3, 1, 2))[None]
    v_prompt = jnp.transpose(vt_all.reshape(batch, da_heads, DA_V_DIM, seq), (0, 3, 1, 2))[None]
    state_prompt = jnp.concatenate(s_p, axis=0)[None].astype(state_hgrn.dtype)
    k_sample = k_s.reshape(1, dbatch, dseq, da_heads, dh2)
    v_sample = v_s.reshape(1, dbatch, dseq, da_heads, DA_V_DIM)
    state_sample = s_s[None].astype(state_hgrn.dtype)
    return (y_prompt, y_sample, k_prompt, v_prompt, state_prompt, k_sample, v_sample, state_sample)
```

```python
import functools
import math

import jax
import jax.numpy as jnp
from jax import lax
from jax.experimental import pallas as pl
from jax.experimental.pallas import tpu as pltpu
from jax.experimental.pallas import tpu_sc as plsc

CHUNK = 64
DA_HEAD_DIM = 32
DA_V_DIM = 2 * DA_HEAD_DIM
HG_DIM = 128
REL_BUCKETS = 32
REL_MAX_DIST = 128
PEER_HEADS = 8
PEER_KEYS = 128
PEER_TOPK = 16
PEER_SUB_DIM = 64
EPS = 1e-6
NEG = -1e30
LOG2E = 1.4426950408889634

LANES = 128
ATT_BLOCK = 128
ATT_STEP = 512
HG_SUB = 16
VMEM_LIMIT = 56 * 1024 * 1024

f32 = jnp.float32
bf16 = jnp.bfloat16


def _cparams(*sem):
    return pltpu.CompilerParams(dimension_semantics=sem, vmem_limit_bytes=VMEM_LIMIT)


def _inproj_prompt_kernel(x_ref, g_ref, w_ref, wkv_t_ref, kin_ref, vin_ref, after_ref,
                          q_ref, kt_ref, vt_ref, hz_ref, ktf_ref, vtf_ref):
    del kin_ref, vin_ref
    del after_ref
    x = x_ref[...]
    xn = x * lax.rsqrt(jnp.mean(x * x, axis=-1, keepdims=True) + EPS) * g_ref[...]
    xb = xn.astype(bf16)
    wq = q_ref.shape[1]
    nt = (((1,), (1,)), ((), ()))
    q_ref[...] = jnp.dot(xb, w_ref[:, 0:wq], preferred_element_type=f32).astype(bf16)
    kt = lax.dot_general(wkv_t_ref[0:wq, :], xb, nt, preferred_element_type=f32)
    ktf_ref[...] = kt
    kt_ref[...] = kt.astype(bf16)
    vt = lax.dot_general(wkv_t_ref[wq:2 * wq, :], xb, nt, preferred_element_type=f32)
    vtf_ref[...] = vt
    vt_ref[...] = vt.astype(bf16)
    hz_ref[...] = jnp.dot(xb, w_ref[:, 3 * wq:], preferred_element_type=f32)


def _inproj_prompt(x, g, w_bf, wkv_t_bf, kt_all, vt_all, after, b, seq, tm):
    d = x.shape[1]
    e = w_bf.shape[1]
    wq = wkv_t_bf.shape[0] // 2
    steps = seq // tm
    return pl.pallas_call(
        _inproj_prompt_kernel,
        grid=(steps,),
        in_specs=[
            pl.BlockSpec((tm, d), lambda i: (b * steps + i, 0)),
            pl.BlockSpec((1, d), lambda i: (0, 0)),
            pl.BlockSpec((d, e), lambda i: (0, 0)),
            pl.BlockSpec((2 * wq, d), lambda i: (0, 0)),
            pl.BlockSpec(memory_space=pl.ANY),
            pl.BlockSpec(memory_space=pl.ANY),
            pl.BlockSpec(memory_space=pl.ANY),
        ],
        out_specs=[
            pl.BlockSpec((tm, wq), lambda i: (i, 0)),
            pl.BlockSpec((wq, tm), lambda i: (0, i)),
            pl.BlockSpec((wq, tm), lambda i: (0, i)),
            pl.BlockSpec((tm, e - 3 * wq), lambda i: (i, 0)),
            pl.BlockSpec((wq, tm), lambda i: (b, i)),
            pl.BlockSpec((wq, tm), lambda i: (b, i)),
        ],
        out_shape=[
            jax.ShapeDtypeStruct((seq, wq), bf16),
            jax.ShapeDtypeStruct((wq, seq), bf16),
            jax.ShapeDtypeStruct((wq, seq), bf16),
            jax.ShapeDtypeStruct((seq, e - 3 * wq), f32),
            jax.ShapeDtypeStruct(kt_all.shape, f32),
            jax.ShapeDtypeStruct(vt_all.shape, f32),
        ],
        input_output_aliases={4: 4, 5: 5},
        compiler_params=_cparams("parallel"),
        name="inproj",
    )(x, g, w_bf, wkv_t_bf, kt_all, vt_all, after)


def _inproj_sample_kernel(x_ref, g_ref, w_ref, q_ref, k_ref, v_ref, hz_ref):
    x = x_ref[...]
    xn = x * lax.rsqrt(jnp.mean(x * x, axis=-1, keepdims=True) + EPS) * g_ref[...]
    xb = xn.astype(bf16)
    wq = q_ref.shape[1]
    q_ref[...] = jnp.dot(xb, w_ref[:, 0:wq], preferred_element_type=f32).astype(bf16)
    k_ref[...] = jnp.dot(xb, w_ref[:, wq:2 * wq], preferred_element_type=f32)
    v_ref[...] = jnp.dot(xb, w_ref[:, 2 * wq:3 * wq], preferred_element_type=f32)
    hz_ref[...] = jnp.dot(xb, w_ref[:, 3 * wq:], preferred_element_type=f32)


def _inproj_sample(x, g, w_bf, wq):
    t, d = x.shape
    e = w_bf.shape[1]
    return pl.pallas_call(
        _inproj_sample_kernel,
        grid=(1,),
        in_specs=[
            pl.BlockSpec((t, d), lambda i: (0, 0)),
            pl.BlockSpec((1, d), lambda i: (0, 0)),
            pl.BlockSpec((d, e), lambda i: (0, 0)),
        ],
        out_specs=[
            pl.BlockSpec((t, wq), lambda i: (0, 0)),
            pl.BlockSpec((t, wq), lambda i: (0, 0)),
            pl.BlockSpec((t, wq), lambda i: (0, 0)),
            pl.BlockSpec((t, e - 3 * wq), lambda i: (0, 0)),
        ],
        out_shape=[
            jax.ShapeDtypeStruct((t, wq), bf16),
            jax.ShapeDtypeStruct((t, wq), f32),
            jax.ShapeDtypeStruct((t, wq), f32),
            jax.ShapeDtypeStruct((t, e - 3 * wq), f32),
        ],
        compiler_params=_cparams("arbitrary"),
        name="inproj_sample",
    )(x, g, w_bf)


def _rel_bucket(rel):
    nb = REL_BUCKETS // 2
    max_exact = nb // 2
    ret = jnp.where(rel > 0, nb, 0)
    n = jnp.abs(rel)
    nf = jnp.maximum(n, 1).astype(f32)
    large = max_exact + (jnp.log(nf / max_exact) / math.log(REL_MAX_DIST / max_exact)
                         * (nb - max_exact)).astype(jnp.int32)
    large = jnp.minimum(large, nb - 1)
    return ret + jnp.where(n < max_exact, n, large)


def _bias_of(rel, rel_bias):
    bucket = _rel_bucket(rel)
    out = jnp.zeros((rel_bias.shape[1],) + rel.shape, f32)
    for b in range(REL_BUCKETS):
        out = out + jnp.where(bucket[None] == b, rel_bias[b].astype(f32)[:, None, None], 0.0)
    return out


def _attn_prompt_kernel(lam_ref, q_ref, kt_ref, vt_ref, bias_ref, gain_ref, o_ref, m_sc, acc_sc, sa_sc, sb_sc):
    qb = pl.program_id(2)
    tq = q_ref.shape[0]
    seq = kt_ref.shape[1]
    per_step = ATT_STEP // ATT_BLOCK
    lane = lax.broadcasted_iota(jnp.int32, (tq, LANES), 1)
    feat_k = lax.broadcasted_iota(jnp.int32, (LANES, ATT_STEP), 0)
    qs = q_ref[...].astype(f32) * (DA_HEAD_DIM ** -0.5 * LOG2E)
    q4 = jnp.concatenate(
        [jnp.where((lane // DA_HEAD_DIM) == i, qs, 0.0).astype(bf16) for i in range(4)], axis=0)

    m_sc[...] = jnp.full(m_sc.shape, -jnp.inf, f32)
    acc_sc[...] = jnp.zeros(acc_sc.shape, f32)

    n_steps = qb // per_step + 1
    last = n_steps - 1
    r = qb % per_step

    def scores(k):
        start = pl.multiple_of(jnp.minimum(k * ATT_STEP, seq - ATT_STEP), ATT_STEP)
        return jnp.dot(q4, kt_ref[:, pl.ds(start, ATT_STEP)], preferred_element_type=f32)

    def half(k, cur_ref, nxt_ref):
        nxt_ref[...] = scores(k + 1)
        tile = jnp.where(k == last, 1 + r, jnp.where((k == last - 1) & (r == 0), per_step + 1, 0))
        vv = vt_ref[:, pl.ds(pl.multiple_of(k * ATT_STEP, ATT_STEP), ATT_STEP)]
        for hh in range(2):
            bias = bias_ref[0, tile, hh * tq:(hh + 1) * tq, :]
            ps, alphas = [], []
            for c in range(2):
                rows = slice((2 * hh + c) * tq, (2 * hh + c + 1) * tq)
                s = cur_ref[rows, :] + bias
                m_old = m_sc[rows]
                m_new = jnp.maximum(m_old, jnp.max(s, axis=-1, keepdims=True))
                ps.append(jnp.exp2(s - m_new[:, 0:1]).astype(bf16))
                alphas.append(jnp.exp2(m_old - m_new))
                m_sc[rows] = m_new
            rows2 = slice(2 * hh * tq, (2 * hh + 2) * tq)
            vaug = jnp.where((feat_k // DA_V_DIM) == hh, vv, jnp.ones_like(vv))
            acc_sc[rows2] = (acc_sc[rows2] * jnp.concatenate(alphas, axis=0)
                             + lax.dot_general(jnp.concatenate(ps, axis=0), vaug, (((1,), (1,)), ((), ())),
                                               preferred_element_type=f32))

    sa_sc[...] = scores(0)

    def two_steps(kk, carry):
        half(2 * kk, sa_sc, sb_sc)

        @pl.when(2 * kk + 1 < n_steps)
        def _():
            half(2 * kk + 1, sb_sc, sa_sc)

        return carry

    lax.fori_loop(0, (n_steps + 1) // 2, two_steps, 0)

    lam = lam_ref[0]
    outs = []
    for hh in range(2):
        own = (lane // DA_V_DIM) == hh
        a1 = acc_sc[2 * hh * tq:(2 * hh + 1) * tq]
        a2 = acc_sc[(2 * hh + 1) * tq:(2 * hh + 2) * tq]
        l1 = jnp.max(jnp.where(own, 0.0, a1), axis=-1, keepdims=True)
        l2 = jnp.max(jnp.where(own, 0.0, a2), axis=-1, keepdims=True)
        o = a1 / l1 - lam * (a2 / l2)
        ssq = jnp.sum(jnp.where(own, o * o, 0.0), axis=-1, keepdims=True)
        outs.append(o * lax.rsqrt(ssq * (1.0 / DA_V_DIM) + EPS))
    o = jnp.where((lane // DA_V_DIM) == 0, outs[0], outs[1])
    o_ref[...] = (o * gain_ref[...]).astype(o_ref.dtype)


def _attn_prompt(q, kt, vt, bias_tiles, gain, lam, batch, seq):
    t, w = q.shape
    pairs = w // LANES
    nq = seq // ATT_BLOCK
    assert seq % ATT_STEP == 0
    n_tiles = ATT_STEP // ATT_BLOCK + 2
    return pl.pallas_call(
        _attn_prompt_kernel,
        grid=(batch, pairs, nq),
        in_specs=[
            pl.BlockSpec(memory_space=pltpu.SMEM),
            pl.BlockSpec((ATT_BLOCK, LANES), lambda b, p, i: (b * nq + i, p)),
            pl.BlockSpec((LANES, seq), lambda b, p, i: (p, b)),
            pl.BlockSpec((LANES, seq), lambda b, p, i: (p, b)),
            pl.BlockSpec((1, n_tiles, 2 * ATT_BLOCK, ATT_STEP), lambda b, p, i: (p, 0, 0, 0)),
            pl.BlockSpec((1, LANES), lambda b, p, i: (0, 0)),
        ],
        out_specs=pl.BlockSpec((ATT_BLOCK, LANES), lambda b, p, i: (b * nq + i, p)),
        out_shape=jax.ShapeDtypeStruct((t, w), bf16),
        scratch_shapes=[
            pltpu.VMEM((4 * ATT_BLOCK, LANES), f32),
            pltpu.VMEM((4 * ATT_BLOCK, LANES), f32),
            pltpu.VMEM((4 * ATT_BLOCK, ATT_STEP), f32),
            pltpu.VMEM((4 * ATT_BLOCK, ATT_STEP), f32),
        ],
        compiler_params=_cparams("parallel", "parallel", "arbitrary"),
        name="attn_prompt",
    )(lam, q, kt, vt, bias_tiles, gain)


def _prompt_bias_tiles(rel_bias):
    h = rel_bias.shape[1]
    per_step = ATT_STEP // ATT_BLOCK
    i = jnp.arange(ATT_BLOCK)
    rel_diag = i[None, :] - i[:, None]
    rel_prev = rel_diag - ATT_BLOCK
    far = rel_bias[REL_BUCKETS // 2 - 1].astype(f32)
    b_diag = (_bias_of(rel_diag, rel_bias) - far[:, None, None]) * LOG2E
    b_prev = (_bias_of(rel_prev, rel_bias) - far[:, None, None]) * LOG2E
    mask = (i[None, :] // CHUNK) <= (i[:, None] // CHUNK)
    b_diag = jnp.where(mask[None], b_diag, NEG)
    zero = jnp.zeros_like(b_diag)
    dead = jnp.full_like(b_diag, NEG)
    tiles = [jnp.concatenate([zero] * per_step, axis=-1)]
    for r in range(per_step):
        blocks = [zero if j < r - 1 else b_prev if j == r - 1 else b_diag if j == r else dead
                  for j in range(per_step)]
        tiles.append(jnp.concatenate(blocks, axis=-1))
    tiles.append(jnp.concatenate([zero] * (per_step - 1) + [b_prev], axis=-1))
    tiles = jnp.stack(tiles, axis=1)
    tiles = tiles.reshape(h // 2, 2, per_step + 2, ATT_BLOCK, ATT_STEP)
    return jnp.transpose(tiles, (0, 2, 1, 3, 4)).reshape(h // 2, per_step + 2, 2 * ATT_BLOCK, ATT_STEP)


def _attn_sample_kernel(lam_ref, q_ref, kp_ref, vp_ref, kn_ref, vn_ref, bp_ref, bn_ref, gain_ref, o_ref):
    nq = q_ref.shape[0]
    pairs = q_ref.shape[1] // LANES
    lane = lax.broadcasted_iota(jnp.int32, (nq, LANES), 1)
    lam = lam_ref[0]
    nt = (((1,), (1,)), ((), ()))
    for p in range(pairs):
        sl = slice(p * LANES, (p + 1) * LANES)
        qs = q_ref[:, sl].astype(f32) * (DA_HEAD_DIM ** -0.5 * LOG2E)
        kp = kp_ref[sl, :].astype(bf16)
        vp = vp_ref[sl, :].astype(bf16)
        kn = kn_ref[:, sl].astype(bf16)
        vn = vn_ref[:, sl].astype(bf16)
        outs = []
        for hh in range(2):
            h = 2 * p + hh
            own = (lane // DA_V_DIM) == hh
            res = []
            for c in range(2):
                qm = jnp.where((lane // DA_HEAD_DIM) == 2 * hh + c, qs, 0.0).astype(bf16)
                sp = jnp.dot(qm, kp, preferred_element_type=f32) + bp_ref[h]
                sn = lax.dot_general(qm, kn, nt, preferred_element_type=f32) + bn_ref[h]
                m = jnp.maximum(jnp.max(sp, axis=-1, keepdims=True), jnp.max(sn, axis=-1, keepdims=True))
                pp = jnp.exp2(sp - m)
                pn = jnp.exp2(sn - m)
                l = jnp.sum(pp, axis=-1, keepdims=True) + jnp.sum(pn, axis=-1, keepdims=True)
                pv = (lax.dot_general(pp.astype(bf16), vp, nt, preferred_element_type=f32)
                      + jnp.dot(pn.astype(bf16), vn, preferred_element_type=f32))
                res.append(pv / l)
            o = res[0] - lam * res[1]
            ssq = jnp.sum(jnp.where(own, o * o, 0.0), axis=-1, keepdims=True)
            outs.append(o * lax.rsqrt(ssq * (1.0 / DA_V_DIM) + EPS))
        o = jnp.where((lane // DA_V_DIM) == 0, outs[0], outs[1])
        o_ref[:, sl] = (o * gain_ref[...]).astype(o_ref.dtype)


def _attn_sample(q, k_new, v_new, k_past, v_past, bias_past, bias_new, gain, lam, batch, nq, past):
    w = q.shape[1]
    h = bias_past.shape[0]
    return pl.pallas_call(
        _attn_sample_kernel,
        grid=(batch,),
        in_specs=[
            pl.BlockSpec(memory_space=pltpu.SMEM),
            pl.BlockSpec((nq, w), lambda b: (b, 0)),
            pl.BlockSpec((w, past), lambda b: (b, 0)),
            pl.BlockSpec((w, past), lambda b: (b, 0)),
            pl.BlockSpec((nq, w), lambda b: (b, 0)),
            pl.BlockSpec((nq, w), lambda b: (b, 0)),
            pl.BlockSpec((h, nq, past), lambda b: (0, 0, 0)),
            pl.BlockSpec((h, nq, nq), lambda b: (0, 0, 0)),
            pl.BlockSpec((1, LANES), lambda b: (0, 0)),
        ],
        out_specs=pl.BlockSpec((nq, w), lambda b: (b, 0)),
        out_shape=jax.ShapeDtypeStruct((batch * nq, w), bf16),
        compiler_params=_cparams("parallel"),
        name="attn_sample",
    )(lam, q, k_past, v_past, k_new, v_new, bias_past, bias_new, gain)


def _hgrn_chunk(q, f_logit, vv, g_logit, lb, gain, s0, lc):
    n_sub = lc // HG_SUB
    f = lb + (1.0 - lb) * jax.nn.sigmoid(f_logit)
    logf = jnp.log(f)
    k = 1.0 - f
    row = lax.broadcasted_iota(jnp.int32, (lc, lc), 0)
    col = lax.broadcasted_iota(jnp.int32, (lc, lc), 1)
    tril = (col <= row).astype(f32)
    b = jnp.dot(tril, logf, precision=lax.Precision.HIGHEST, preferred_element_type=f32)
    b_last = b[lc - 1:lc, :]

    o = jnp.dot((q * jnp.exp(b)).astype(bf16), s0.astype(bf16), preferred_element_type=f32)
    k_hat = k * jnp.exp(b_last - b)
    bt = b.T
    decay_col = jnp.exp(bt[:, lc - 1:lc])
    s_new = decay_col * s0 + jnp.dot(k_hat.T.astype(bf16), vv.astype(bf16), preferred_element_type=f32)

    if n_sub > 1:
        rows = []
        nt = (((1,), (1,)), ((), ()))
        for i in range(n_sub):
            lo = i * HG_SUB
            if i == 0:
                rows.append(jnp.zeros((HG_SUB, lc), f32))
                continue
            beta = b[lo - 1:lo, :]
            q_t = q[lo:lo + HG_SUB, :] * jnp.exp(b[lo:lo + HG_SUB, :] - beta)
            k_t = k * jnp.exp(jnp.minimum(beta - b, 0.0))
            rows.append(lax.dot_general(q_t.astype(bf16), k_t.astype(bf16), nt, preferred_element_type=f32))
        a_off = jnp.concatenate(rows, axis=0)
        a_off = jnp.where(col < (row // HG_SUB) * HG_SUB, a_off, 0.0)
        o = o + jnp.dot(a_off.astype(bf16), vv.astype(bf16), preferred_element_type=f32)

    ones = jnp.ones((HG_DIM, LANES), bf16)
    srow = lax.broadcasted_iota(jnp.int32, (HG_SUB, HG_DIM), 0)
    orow = lax.broadcasted_iota(jnp.int32, (HG_SUB, HG_DIM), 0)
    blocks = []
    for i in range(n_sub):
        lo = i * HG_SUB
        b_i = b[lo:lo + HG_SUB, :]
        k_i = k[lo:lo + HG_SUB, :]
        v_i = vv[lo:lo + HG_SUB, :]
        q_i = q[lo:lo + HG_SUB, :]
        d_rows = []
        for t in range(HG_SUB):
            arg = jnp.where(srow <= t, b_i[t:t + 1, :] - b_i, -jnp.inf)
            d_rows.append(q_i[t:t + 1, :] * k_i * jnp.exp(arg))
        d3 = jnp.concatenate(d_rows, axis=0)
        a_rep = jnp.dot(d3.astype(bf16), ones, preferred_element_type=f32)
        o_i = jnp.zeros((HG_SUB, HG_DIM), f32)
        for t in range(HG_SUB):
            o_t = jnp.sum(a_rep[t * HG_SUB:(t + 1) * HG_SUB, :] * v_i, axis=0, keepdims=True)
            o_i = jnp.where(orow == t, o_t, o_i)
        blocks.append(o_i)
    o = o + (jnp.concatenate(blocks, axis=0) if n_sub > 1 else blocks[0])

    on = o * lax.rsqrt(jnp.mean(o * o, axis=-1, keepdims=True) + EPS) * gain
    gate = g_logit * jax.nn.sigmoid(g_logit)
    return on * gate, s_new


def _hgrn_kernel(hz_ref, lb_ref, gain_ref, s0_ref, o_ref, s_ref, st_sc, *, lc, heads):
    j = pl.program_id(1)
    w = heads * HG_DIM

    @pl.when(j == 0)
    def _():
        st_sc[...] = s0_ref[0]

    n_chunks = hz_ref.shape[0] // lc
    for c in range(n_chunks):
        rs = slice(c * lc, (c + 1) * lc)
        for h in range(heads):
            cs = slice(h * HG_DIM, (h + 1) * HG_DIM)
            o, s_new = _hgrn_chunk(
                hz_ref[rs, h * HG_DIM:(h + 1) * HG_DIM],
                hz_ref[rs, w + h * HG_DIM:w + (h + 1) * HG_DIM],
                hz_ref[rs, 2 * w + h * HG_DIM:2 * w + (h + 1) * HG_DIM],
                hz_ref[rs, 3 * w + h * HG_DIM:3 * w + (h + 1) * HG_DIM],
                lb_ref[:, cs], gain_ref[...], st_sc[h], lc)
            st_sc[h] = s_new
            o_ref[rs, cs] = o.astype(o_ref.dtype)

    @pl.when(j == pl.num_programs(1) - 1)
    def _():
        s_ref[0] = st_sc[...]


def _hgrn(hz, lb, gain, s0, batch, seq, lc, chunks_per_step):
    t = hz.shape[0]
    w = hz.shape[1] // 4
    heads = w // HG_DIM
    rows = lc * chunks_per_step
    steps = seq // rows
    return pl.pallas_call(
        functools.partial(_hgrn_kernel, lc=lc, heads=heads),
        grid=(batch, steps),
        in_specs=[
            pl.BlockSpec((rows, 4 * w), lambda b, j: (b * steps + j, 0)),
            pl.BlockSpec((1, w), lambda b, j: (0, 0)),
            pl.BlockSpec((1, HG_DIM), lambda b, j: (0, 0)),
            pl.BlockSpec((1, heads, HG_DIM, HG_DIM), lambda b, j: (b, 0, 0, 0)),
        ],
        out_specs=[
            pl.BlockSpec((rows, w), lambda b, j: (b * steps + j, 0)),
            pl.BlockSpec((1, heads, HG_DIM, HG_DIM), lambda b, j: (b, 0, 0, 0)),
        ],
        out_shape=[
            jax.ShapeDtypeStruct((t, w), bf16),
            jax.ShapeDtypeStruct((batch, heads, HG_DIM, HG_DIM), f32),
        ],
        scratch_shapes=[pltpu.VMEM((heads, HG_DIM, HG_DIM), f32)],
        compiler_params=_cparams("parallel", "arbitrary"),
        name="hgrn2",
    )(hz, lb, gain, s0)


def _top_rows(vals, k, payload=None):
    n_rows = vals.shape[0]
    row = lax.broadcasted_iota(jnp.int32, vals.shape, 0)
    out_v, out_i = [], []
    for _ in range(k):
        m = jnp.max(vals, axis=0, keepdims=True)
        first = jnp.min(jnp.where(vals == m, row, n_rows), axis=0, keepdims=True)
        hit = row == first
        out_v.append(m)
        if payload is None:
            out_i.append(first)
        else:
            out_i.append(jnp.max(jnp.where(hit, payload, -1), axis=0, keepdims=True))
        vals = jnp.where(hit, -jnp.inf, vals)
    return jnp.concatenate(out_v, axis=0), jnp.concatenate(out_i, axis=0)


def _retrieve_kernel(oa_ref, ob_ref, x_ref, wo_ref, g_ref, wqt_ref, sk_ref,
                     hp_ref, xn_ref, idx_ref, gate_ref, qt_sc, s1v_sc, s1i_sc, idx_sc, gate_sc):
    half = oa_ref.shape[1]
    y = (jnp.dot(oa_ref[...], wo_ref[0:half, :], preferred_element_type=f32)
         + jnp.dot(ob_ref[...], wo_ref[half:, :], preferred_element_type=f32))
    hp = x_ref[...] + y
    hp_ref[...] = hp
    xn = hp * lax.rsqrt(jnp.mean(hp * hp, axis=-1, keepdims=True) + EPS) * g_ref[...]
    xn_ref[...] = xn
    nt = (((1,), (1,)), ((), ()))
    qt_sc[...] = lax.dot_general(wqt_ref[...], xn.astype(bf16), nt, preferred_element_type=f32).astype(bf16)

    n_half = sk_ref.shape[0]

    def stage1(hc, carry):
        start = pl.multiple_of(hc * PEER_SUB_DIM, PEER_SUB_DIM)
        s = jnp.dot(sk_ref[hc], qt_sc[pl.ds(start, PEER_SUB_DIM), :], preferred_element_type=f32)
        v, i = _top_rows(s, PEER_TOPK)
        s1v_sc[hc] = v
        s1i_sc[hc] = i
        return carry

    lax.fori_loop(0, n_half, stage1, 0)

    tm = x_ref.shape[0]
    sub = lax.broadcasted_iota(jnp.int32, (8, tm), 0)

    def stage2(h, carry):
        v1, i1 = s1v_sc[2 * h], s1i_sc[2 * h]
        v2, i2 = s1v_sc[2 * h + 1], s1i_sc[2 * h + 1]
        cv, ci = [], []
        for i in range(PEER_TOPK):
            nj = PEER_TOPK // (i + 1)
            rows = PEER_TOPK if nj > 8 else 8
            a = v1[i:i + 1, :] + v2[0:rows, :]
            e = i1[i:i + 1, :] * PEER_KEYS + i2[0:rows, :]
            if nj < rows:
                a = jnp.where(sub < nj, a, -jnp.inf)
            cv.append(a)
            ci.append(e)
        top_s, eidx = _top_rows(jnp.concatenate(cv, axis=0), PEER_TOPK, jnp.concatenate(ci, axis=0))
        p = jnp.exp(top_s - top_s[0:1, :])
        g = p / jnp.sum(p, axis=0, keepdims=True)
        r0 = pl.multiple_of(h * PEER_TOPK, PEER_TOPK)
        gate_sc[pl.ds(r0, PEER_TOPK), :] = g
        idx_sc[pl.ds(r0, PEER_TOPK), :] = eidx
        return carry

    lax.fori_loop(0, n_half // 2, stage2, 0)
    idx_ref[...] = pltpu.bitcast(pltpu.bitcast(idx_sc[...], f32).T, jnp.int32)
    gate_ref[...] = gate_sc[...].T


def _retrieve(oa, ob, x, wo_bf, g2, wqt_bf, sk_bf, tm, x_block0=0):
    t, half = oa.shape
    d = x.shape[1]
    e = wqt_bf.shape[0]
    n_half = sk_bf.shape[0]
    n_sel = (n_half // 2) * PEER_TOPK
    return pl.pallas_call(
        _retrieve_kernel,
        grid=(t // tm,),
        in_specs=[
            pl.BlockSpec((tm, half), lambda i: (i, 0)),
            pl.BlockSpec((tm, half), lambda i: (i, 0)),
            pl.BlockSpec((tm, d), lambda i: (x_block0 + i, 0)),
            pl.BlockSpec((2 * half, d), lambda i: (0, 0)),
            pl.BlockSpec((1, d), lambda i: (0, 0)),
            pl.BlockSpec((e, d), lambda i: (0, 0)),
            pl.BlockSpec((n_half, PEER_KEYS, PEER_SUB_DIM), lambda i: (0, 0, 0)),
        ],
        out_specs=[
            pl.BlockSpec((tm, d), lambda i: (i, 0)),
            pl.BlockSpec((tm, d), lambda i: (i, 0)),
            pl.BlockSpec((tm, n_sel), lambda i: (i, 0)),
            pl.BlockSpec((tm, n_sel), lambda i: (i, 0)),
        ],
        out_shape=[
            jax.ShapeDtypeStruct((t, d), f32),
            jax.ShapeDtypeStruct((t, d), f32),
            jax.ShapeDtypeStruct((t, n_sel), jnp.int32),
            jax.ShapeDtypeStruct((t, n_sel), f32),
        ],
        scratch_shapes=[
            pltpu.VMEM((e, tm), bf16),
            pltpu.VMEM((n_half, PEER_TOPK, tm), f32),
            pltpu.VMEM((n_half, PEER_TOPK, tm), jnp.int32),
            pltpu.VMEM((n_sel, tm), jnp.int32),
            pltpu.VMEM((n_sel, tm), f32),
        ],
        compiler_params=_cparams("parallel"),
        name="retrieve",
    )(oa, ob, x, wo_bf, g2, wqt_bf, sk_bf)


def _pack_table(tab):
    half = tab.shape[1] // 2
    lo = lax.bitcast_convert_type(tab[:, :half].astype(bf16), jnp.uint16).astype(jnp.uint32)
    hi = lax.bitcast_convert_type(tab[:, half:].astype(bf16), jnp.uint16).astype(jnp.uint32)
    return lax.bitcast_convert_type(lo | (hi << 16), jnp.int32)


def _sc_layout(n_tok, n_sel, words):
    info = plsc.get_sparse_core_info()
    n_workers = info.num_cores * info.num_subcores
    lanes = info.num_lanes
    block = 8 * lanes
    assert n_tok % (2 * n_workers) == 0 and words % block == 0 and (n_sel // 2) % lanes == 0
    return info, lanes, n_tok // n_workers, n_sel // 2, block


def _selected_dots(table, idx, x):
    n_tok, n_sel = idx.shape
    words = table.shape[1]
    info, lanes, tpw, half_rows, block = _sc_layout(n_tok, n_sel, words)
    rows_per_iter = 4
    idx2 = idx.reshape(2 * n_tok, half_rows)
    mesh = plsc.VectorSubcoreMesh(core_axis_name="c", subcore_axis_name="s")

    @functools.partial(
        pl.kernel, out_type=jax.ShapeDtypeStruct((n_tok, n_sel), f32), mesh=mesh,
        compiler_params=pltpu.CompilerParams(needs_layout_passes=False),
        scratch_types=[
            pltpu.VMEM((2 * tpw, half_rows), jnp.int32),
            pltpu.VMEM((half_rows, words), jnp.int32),
            pltpu.VMEM((half_rows, words), jnp.int32),
            pltpu.VMEM((2 * words,), f32),
            pltpu.VMEM((2 * words,), f32),
            pltpu.VMEM((n_sel // 8, 8 * lanes), f32),
            pltpu.VMEM((tpw, n_sel), f32),
            pltpu.SemaphoreType.DMA, pltpu.SemaphoreType.DMA, pltpu.SemaphoreType.DMA, pltpu.SemaphoreType.DMA,
        ])
    def dots(t_hbm, i_hbm, x_hbm, o_hbm, idx_v, rows_a, rows_b, x_a, x_b, part_v, act_v, sem_a, sem_b, sem_xa, sem_xb):
        wid = lax.axis_index("s") * info.num_cores + lax.axis_index("c")
        base = wid * tpw
        pltpu.sync_copy(i_hbm.at[pl.ds(base * 2, 2 * tpw)], idx_v)

        def accumulate(rows, x_v, half):
            @pl.loop(0, words // block)
            def _(cb):
                xs = [x_v[pl.ds(p * words + cb * block + c * lanes, lanes)] for c in range(8) for p in range(2)]

                @pl.loop(0, half_rows // rows_per_iter)
                def _(i):
                    rws = [rows_per_iter * i + r for r in range(rows_per_iter)]
                    packed = [[rows[row, pl.ds(cb * block + c * lanes, lanes)] for c in range(8)] for row in rws]
                    unp = [[plsc.unpack(plsc.bitcast(p, bf16), format=plsc.PackFormat.INTERLEAVED) for p in pr]
                           for pr in packed]
                    prods = [[u[c][p] * xs[2 * c + p] for c in range(8) for p in range(2)] for u in unp]
                    while len(prods[0]) > 1:
                        prods = [[pr[k] + pr[k + 1] for k in range(0, len(pr), 2)] for pr in prods]
                    for r, row in enumerate(rws):
                        prow = half * half_rows + row
                        plsc.addupdate(part_v.at[prow // 8, pl.ds((prow % 8) * lanes, lanes)], prods[r][0])

        def gather(half_index, rows, sem):
            return pltpu.make_async_copy(t_hbm.at[idx_v.at[half_index]], rows, sem)

        def xcopy(t, x_v, sem):
            return pltpu.make_async_copy(x_hbm.at[base + t], x_v, sem)

        def token(t, x_v, sem_x, x_next, sem_xn):
            gather(2 * t + 1, rows_b, sem_b).start()

            @pl.when(t + 1 < tpw)
            def _():
                xcopy(t + 1, x_next, sem_xn).start()

            zero = jnp.zeros((lanes,), f32)
            for r in range(n_sel):
                part_v[r // 8, pl.ds((r % 8) * lanes, lanes)] = zero
            xcopy(t, x_v, sem_x).wait()
            gather(2 * t, rows_a, sem_a).wait()
            accumulate(rows_a, x_v, 0)

            @pl.when(t + 1 < tpw)
            def _():
                gather(2 * t + 2, rows_a, sem_a).start()

            gather(2 * t + 1, rows_b, sem_b).wait()
            accumulate(rows_b, x_v, 1)
            it = lax.iota(jnp.int32, lanes)
            for g in range(n_sel // lanes):
                prow = g * (lanes // 8) + it // 8
                pcol = (it % 8) * lanes
                tot = plsc.load_gather(part_v, [prow, pcol])
                for l in range(1, lanes):
                    tot = tot + plsc.load_gather(part_v, [prow, pcol + l])
                act_v[t, pl.ds(g * lanes, lanes)] = tot

        gather(0, rows_a, sem_a).start()
        xcopy(0, x_a, sem_xa).start()

        @pl.loop(0, tpw // 2)
        def _(i):
            token(2 * i, x_a, sem_xa, x_b, sem_xb)
            token(2 * i + 1, x_b, sem_xb, x_a, sem_xa)

        pltpu.sync_copy(act_v, o_hbm.at[pl.ds(base, tpw)])

    return dots(table, idx2, x)


def _gate_weights_kernel(act_ref, gate_ref, w_ref):
    a = act_ref[...]
    w_ref[...] = gate_ref[...] * (0.5 * a * (1.0 + lax.erf(a * (2.0 ** -0.5))))


def _gate_weights(act, gate, tm):
    t, n_sel = act.shape
    spec = pl.BlockSpec((tm, n_sel), lambda i: (i, 0))
    return pl.pallas_call(
        _gate_weights_kernel,
        grid=(t // tm,),
        in_specs=[spec, spec],
        out_specs=spec,
        out_shape=jax.ShapeDtypeStruct((t, n_sel), f32),
        compiler_params=_cparams("parallel"),
        name="peer_weights",
    )(act, gate)


def _weighted_rows(table, idx, w):
    n_tok, n_sel = w.shape
    words = table.shape[1]
    info, lanes, tpw, half_rows, block = _sc_layout(n_tok, n_sel, words)
    rows_per_iter = 2
    idx2 = idx.reshape(2 * n_tok, half_rows)
    mesh = plsc.VectorSubcoreMesh(core_axis_name="c", subcore_axis_name="s")

    @functools.partial(
        pl.kernel, out_type=jax.ShapeDtypeStruct((n_tok, 2 * words), f32), mesh=mesh,
        compiler_params=pltpu.CompilerParams(needs_layout_passes=False),
        scratch_types=[
            pltpu.VMEM((2 * tpw, half_rows), jnp.int32),
            pltpu.VMEM((tpw, n_sel), f32),
            pltpu.VMEM((half_rows, words), jnp.int32),
            pltpu.VMEM((half_rows, words), jnp.int32),
            pltpu.VMEM((2 * words,), f32),
            pltpu.VMEM((2 * words,), f32),
            pltpu.SemaphoreType.DMA, pltpu.SemaphoreType.DMA, pltpu.SemaphoreType.DMA, pltpu.SemaphoreType.DMA,
        ])
    def mix(t_hbm, i_hbm, w_hbm, o_hbm, idx_v, w_v, rows_a, rows_b, out_a, out_b, sem_a, sem_b, sem_oa, sem_ob):
        wid = lax.axis_index("s") * info.num_cores + lax.axis_index("c")
        base = wid * tpw
        pltpu.sync_copy(i_hbm.at[pl.ds(base * 2, 2 * tpw)], idx_v)
        pltpu.sync_copy(w_hbm.at[pl.ds(base, tpw)], w_v)

        def accumulate(rows, out_v, t, half, first):
            @pl.loop(0, words // block)
            def _(cb):
                def body(i, acc):
                    acc = list(acc)
                    r0 = rows_per_iter * i
                    wv = w_v[t, pl.ds(half * half_rows + (r0 // lanes) * lanes, lanes)]
                    for r in range(rows_per_iter):
                        lane_of_row = jnp.full((lanes,), r0 % lanes + r, jnp.int32)
                        wj = wv.at[lane_of_row].get(mode="promise_in_bounds")
                        for c in range(8):
                            packed = rows[r0 + r, pl.ds(cb * block + c * lanes, lanes)]
                            lo, hi = plsc.unpack(plsc.bitcast(packed, bf16), format=plsc.PackFormat.INTERLEAVED)
                            acc[2 * c] = acc[2 * c] + wj * lo
                            acc[2 * c + 1] = acc[2 * c + 1] + wj * hi
                    return tuple(acc)

                if first:
                    init = tuple(jnp.zeros((lanes,), f32) for _ in range(16))
                else:
                    init = tuple(out_v[pl.ds(p * words + cb * block + c * lanes, lanes)]
                                 for c in range(8) for p in range(2))
                acc = lax.fori_loop(0, half_rows // rows_per_iter, body, init)
                for c in range(8):
                    out_v[pl.ds(cb * block + c * lanes, lanes)] = acc[2 * c]
                    out_v[pl.ds(words + cb * block + c * lanes, lanes)] = acc[2 * c + 1]

        def gather(half_index, rows, sem):
            return pltpu.make_async_copy(t_hbm.at[idx_v.at[half_index]], rows, sem)

        def token(t, out_v, sem_o, out_in_flight):
            gather(2 * t + 1, rows_b, sem_b).start()
            gather(2 * t, rows_a, sem_a).wait()

            @pl.when(out_in_flight)
            def _():
                pltpu.make_async_copy(out_v, o_hbm.at[base + t], sem_o).wait()

            accumulate(rows_a, out_v, t, 0, True)

            @pl.when(t + 1 < tpw)
            def _():
                gather(2 * t + 2, rows_a, sem_a).start()

            gather(2 * t + 1, rows_b, sem_b).wait()
            accumulate(rows_b, out_v, t, 1, False)
            pltpu.make_async_copy(out_v, o_hbm.at[base + t], sem_o).start()

        gather(0, rows_a, sem_a).start()

        @pl.loop(0, tpw // 2)
        def _(i):
            token(2 * i, out_a, sem_oa, i > 0)
            token(2 * i + 1, out_b, sem_ob, i > 0)

        pltpu.make_async_copy(out_a, o_hbm.at[base], sem_oa).wait()
        pltpu.make_async_copy(out_b, o_hbm.at[base], sem_ob).wait()

    return mix(table, idx2, w)


def _final_kernel(hp_ref, peer_ref, gf_ref, yin_ref, y_ref, done_ref):
    del yin_ref
    done_ref[...] = jnp.zeros(done_ref.shape, f32)
    h = hp_ref[...] + peer_ref[...]
    y_ref[...] = h * lax.rsqrt(jnp.mean(h * h, axis=-1, keepdims=True) + EPS) * gf_ref[...]


def _final(hp, peer, gf, y_all, y_block0, tm):
    t, d = hp.shape
    return pl.pallas_call(
        _final_kernel,
        grid=(t // tm,),
        in_specs=[
            pl.BlockSpec((tm, d), lambda i: (i, 0)),
            pl.BlockSpec((tm, d), lambda i: (i, 0)),
            pl.BlockSpec((1, d), lambda i: (0, 0)),
            pl.BlockSpec(memory_space=pl.ANY),
        ],
        out_specs=[pl.BlockSpec((tm, d), lambda i: (y_block0 + i, 0)),
                   pl.BlockSpec((8, LANES), lambda i: (0, 0))],
        out_shape=[jax.ShapeDtypeStruct(y_all.shape, f32), jax.ShapeDtypeStruct((8, LANES), f32)],
        input_output_aliases={3: 0},
        compiler_params=_cparams("arbitrary"),
        name="final_norm",
    )(hp, peer, gf, y_all)


def _channel_mixer(oa, ob, x, x_block0, wo_bf, g2, wqt_bf, sk_bf, u_pk, v_pk, gf, tm, y_all, y_block0):
    hp, xn, eidx, gate = _retrieve(oa, ob, x, wo_bf, g2, wqt_bf, sk_bf, tm, x_block0)
    act = _selected_dots(u_pk, eidx, xn)
    w = _gate_weights(act, gate, tm)
    peer = _weighted_rows(v_pk, eidx, w)
    return _final(hp, peer, gf, y_all, y_block0, tm)


def kernel(x_prompt, x_sample, cache_k, cache_v, state_hgrn, norm1, w_in, da_lq1, da_lk1, da_lq2, da_lk2,
           da_out_norm, hg_lb_logits, hg_out_norm, w_out, rel_bias, norm2, peer_w_q, peer_sub_keys,
           peer_u, peer_v, final_norm):
    batch, seq, d = x_prompt.shape
    dbatch, dseq, _ = x_sample.shape
    past = cache_k.shape[2]
    da_heads = cache_k.shape[3]
    hg_heads = state_hgrn.shape[2]
    depth = w_in.shape[0]
    assert depth == 1 and seq % ATT_BLOCK == 0 and seq % CHUNK == 0 and dseq % HG_SUB == 0 and dseq <= CHUNK
    assert past % CHUNK == 0 and (past + dseq - 1) // CHUNK == past // CHUNK

    l = 0
    lam_init = 0.8 - 0.6 * math.exp(-0.3 * l)
    lam = (jnp.exp(jnp.sum(da_lq1[l].astype(f32) * da_lk1[l].astype(f32)))
           - jnp.exp(jnp.sum(da_lq2[l].astype(f32) * da_lk2[l].astype(f32))) + lam_init).reshape(1)
    lb = jnp.cumsum(jax.nn.softmax(hg_lb_logits.astype(f32), axis=0), axis=0)[l].reshape(1, -1)
    da_gain = (jnp.tile(da_out_norm[l].astype(f32), 2) * (1.0 - lam_init)).reshape(1, LANES)
    hg_gain = hg_out_norm[l].astype(f32).reshape(1, HG_DIM)
    g1 = norm1[l].astype(f32).reshape(1, d)
    g2 = norm2[l].astype(f32).reshape(1, d)
    gf = final_norm.astype(f32).reshape(1, d)
    w_in_bf = w_in[l].astype(bf16)
    wkv_t_bf = w_in[l][:, cache_k.shape[3] * cache_k.shape[4]:3 * cache_k.shape[3] * cache_k.shape[4]].T.astype(bf16)
    wo_bf = w_out[l].astype(bf16)
    wqt_bf = peer_w_q[l].T.astype(bf16)
    sk_bf = peer_sub_keys[l].reshape(-1, PEER_KEYS, PEER_SUB_DIM).astype(bf16)
    u_pk = _pack_table(peer_u[l])
    v_pk = _pack_table(peer_v[l])

    bias_tiles = _prompt_bias_tiles(rel_bias)
    q_pos = past + jnp.arange(dseq)
    bias_s = _bias_of(jnp.arange(past + dseq)[None, :] - q_pos[:, None], rel_bias) * LOG2E
    bias_past, bias_new = bias_s[:, :, :past], bias_s[:, :, past:]
    zero_state = jnp.zeros((1, hg_heads, HG_DIM, HG_DIM), f32)

    dh2 = 2 * DA_HEAD_DIM
    wq = da_heads * dh2
    tm = 256
    x_all = x_prompt.reshape(batch * seq, d)
    kt_all = jnp.zeros((batch * wq, seq), f32)
    vt_all = jnp.zeros((batch * wq, seq), f32)
    y_all = jnp.zeros((batch * seq, d), f32)
    s_p = []
    done = [jnp.zeros((8, LANES), f32)] * 3
    for b in range(batch):
        q, kt, vt, hz, kt_all, vt_all = _inproj_prompt(x_all, g1, w_in_bf, wkv_t_bf, kt_all, vt_all, done[b],
                                                       b, seq, tm)
        oa = _attn_prompt(q, kt, vt, bias_tiles, da_gain, lam, 1, seq)
        ob, s_new = _hgrn(hz, lb, hg_gain, zero_state, 1, seq, CHUNK, 4)
        y_all, token = _channel_mixer(oa, ob, x_all, b * (seq // tm), wo_bf, g2, wqt_bf, sk_bf, u_pk, v_pk, gf,
                                      tm, y_all, b * (seq // tm))
        done.append(token)
        s_p.append(s_new)

    xs = x_sample.reshape(dbatch * dseq, d)
    q, k_s, v_s, hz = _inproj_sample(xs, g1, w_in_bf, wq)
    kp_t = jnp.transpose(cache_k[l], (0, 2, 3, 1)).reshape(dbatch * wq, past)
    vp_t = jnp.transpose(cache_v[l], (0, 2, 3, 1)).reshape(dbatch * wq, past)
    oa = _attn_sample(q, k_s, v_s, kp_t, vp_t, bias_past, bias_new, da_gain, lam, dbatch, dseq, past)
    ob, s_s = _hgrn(hz, lb, hg_gain, state_hgrn[l].astype(f32), dbatch, dseq, dseq, 1)
    y_s, _ = _channel_mixer(oa, ob, xs, 0, wo_bf, g2, wqt_bf, sk_bf, u_pk, v_pk, gf, dbatch * dseq,
                            jnp.zeros((dbatch * dseq, d), f32), 0)

    y_prompt = y_all.reshape(batch, seq, d)
    y_sample = y_s.reshape(dbatch, dseq, d)
    k_prompt = jnp.transpose(kt_all.reshape(batch, da_heads, dh2, seq), (0, 3, 1, 2))[None]
    v_prompt = jnp.transpose(vt_all.reshape(batch, da_heads, DA_V_DIM, seq), (0, 3, 1, 2))[None]
    state_prompt = jnp.concatenate(s_p, axis=0)[None].astype(state_hgrn.dtype)
    k_sample = k_s.reshape(1, dbatch, dseq, da_heads, dh2)
    v_sample = v_s.reshape(1, dbatch, dseq, da_heads, DA_V_DIM)
    state_sample = s_s[None].astype(state_hgrn.dtype)
    return (y_prompt, y_sample, k_prompt, v_prompt, state_prompt, k_sample, v_sample, state_sample)
```

```python
import functools
import math

import jax
import jax.numpy as jnp
from jax import lax
from jax.experimental import pallas as pl
from jax.experimental.pallas import tpu as pltpu
from jax.experimental.pallas import tpu_sc as plsc

CHUNK = 64
DA_HEAD_DIM = 32
DA_V_DIM = 2 * DA_HEAD_DIM
HG_DIM = 128
REL_BUCKETS = 32
REL_MAX_DIST = 128
PEER_HEADS = 8
PEER_KEYS = 128
PEER_TOPK = 16
PEER_SUB_DIM = 64
EPS = 1e-6
NEG = -1e30
LOG2E = 1.4426950408889634

LANES = 128
ATT_BLOCK = 128
ATT_STEP = 512
HG_SUB = 16
VMEM_LIMIT = 56 * 1024 * 1024

f32 = jnp.float32
bf16 = jnp.bfloat16


def _cparams(*sem):
    return pltpu.CompilerParams(dimension_semantics=sem, vmem_limit_bytes=VMEM_LIMIT)


def _inproj_prompt_kernel(x_ref, g_ref, w_ref, wkv_t_ref, kin_ref, vin_ref, after_ref,
                          q_ref, kt_ref, vt_ref, hz_ref, ktf_ref, vtf_ref):
    del kin_ref, vin_ref
    del after_ref
    x = x_ref[...]
    xn = x * lax.rsqrt(jnp.mean(x * x, axis=-1, keepdims=True) + EPS) * g_ref[...]
    xb = xn.astype(bf16)
    wq = q_ref.shape[1]
    nt = (((1,), (1,)), ((), ()))
    q_ref[...] = jnp.dot(xb, w_ref[:, 0:wq], preferred_element_type=f32).astype(bf16)
    kt = lax.dot_general(wkv_t_ref[0:wq, :], xb, nt, preferred_element_type=f32)
    ktf_ref[...] = kt
    kt_ref[...] = kt.astype(bf16)
    vt = lax.dot_general(wkv_t_ref[wq:2 * wq, :], xb, nt, preferred_element_type=f32)
    vtf_ref[...] = vt
    vt_ref[...] = vt.astype(bf16)
    hz_ref[...] = jnp.dot(xb, w_ref[:, 3 * wq:], preferred_element_type=f32)


def _inproj_prompt(x, g, w_bf, wkv_t_bf, kt_all, vt_all, after, b, seq, tm):
    d = x.shape[1]
    e = w_bf.shape[1]
    wq = wkv_t_bf.shape[0] // 2
    steps = seq // tm
    return pl.pallas_call(
        _inproj_prompt_kernel,
        grid=(steps,),
        in_specs=[
            pl.BlockSpec((tm, d), lambda i: (b * steps + i, 0)),
            pl.BlockSpec((1, d), lambda i: (0, 0)),
            pl.BlockSpec((d, e), lambda i: (0, 0)),
            pl.BlockSpec((2 * wq, d), lambda i: (0, 0)),
            pl.BlockSpec(memory_space=pl.ANY),
            pl.BlockSpec(memory_space=pl.ANY),
            pl.BlockSpec(memory_space=pl.ANY),
        ],
        out_specs=[
            pl.BlockSpec((tm, wq), lambda i: (i, 0)),
            pl.BlockSpec((wq, tm), lambda i: (0, i)),
            pl.BlockSpec((wq, tm), lambda i: (0, i)),
            pl.BlockSpec((tm, e - 3 * wq), lambda i: (i, 0)),
            pl.BlockSpec((wq, tm), lambda i: (b, i)),
            pl.BlockSpec((wq, tm), lambda i: (b, i)),
        ],
        out_shape=[
            jax.ShapeDtypeStruct((seq, wq), bf16),
            jax.ShapeDtypeStruct((wq, seq), bf16),
            jax.ShapeDtypeStruct((wq, seq), bf16),
            jax.ShapeDtypeStruct((seq, e - 3 * wq), f32),
            jax.ShapeDtypeStruct(kt_all.shape, f32),
            jax.ShapeDtypeStruct(vt_all.shape, f32),
        ],
        input_output_aliases={4: 4, 5: 5},
        compiler_params=_cparams("parallel"),
        name="inproj",
    )(x, g, w_bf, wkv_t_bf, kt_all, vt_all, after)


def _inproj_sample_kernel(x_ref, g_ref, w_ref, q_ref, k_ref, v_ref, hz_ref):
    x = x_ref[...]
    xn = x * lax.rsqrt(jnp.mean(x * x, axis=-1, keepdims=True) + EPS) * g_ref[...]
    xb = xn.astype(bf16)
    wq = q_ref.shape[1]
    q_ref[...] = jnp.dot(xb, w_ref[:, 0:wq], preferred_element_type=f32).astype(bf16)
    k_ref[...] = jnp.dot(xb, w_ref[:, wq:2 * wq], preferred_element_type=f32)
    v_ref[...] = jnp.dot(xb, w_ref[:, 2 * wq:3 * wq], preferred_element_type=f32)
    hz_ref[...] = jnp.dot(xb, w_ref[:, 3 * wq:], preferred_element_type=f32)


def _inproj_sample(x, g, w_bf, wq):
    t, d = x.shape
    e = w_bf.shape[1]
    return pl.pallas_call(
        _inproj_sample_kernel,
        grid=(1,),
        in_specs=[
            pl.BlockSpec((t, d), lambda i: (0, 0)),
            pl.BlockSpec((1, d), lambda i: (0, 0)),
            pl.BlockSpec((d, e), lambda i: (0, 0)),
        ],
        out_specs=[
            pl.BlockSpec((t, wq), lambda i: (0, 0)),
            pl.BlockSpec((t, wq), lambda i: (0, 0)),
            pl.BlockSpec((t, wq), lambda i: (0, 0)),
            pl.BlockSpec((t, e - 3 * wq), lambda i: (0, 0)),
        ],
        out_shape=[
            jax.ShapeDtypeStruct((t, wq), bf16),
            jax.ShapeDtypeStruct((t, wq), f32),
            jax.ShapeDtypeStruct((t, wq), f32),
            jax.ShapeDtypeStruct((t, e - 3 * wq), f32),
        ],
        compiler_params=_cparams("arbitrary"),
        name="inproj_sample",
    )(x, g, w_bf)


def _rel_bucket(rel):
    nb = REL_BUCKETS // 2
    max_exact = nb // 2
    ret = jnp.where(rel > 0, nb, 0)
    n = jnp.abs(rel)
    nf = jnp.maximum(n, 1).astype(f32)
    large = max_exact + (jnp.log(nf / max_exact) / math.log(REL_MAX_DIST / max_exact)
                         * (nb - max_exact)).astype(jnp.int32)
    large = jnp.minimum(large, nb - 1)
    return ret + jnp.where(n < max_exact, n, large)


def _bias_of(rel, rel_bias):
    bucket = _rel_bucket(rel)
    out = jnp.zeros((rel_bias.shape[1],) + rel.shape, f32)
    for b in range(REL_BUCKETS):
        out = out + jnp.where(bucket[None] == b, rel_bias[b].astype(f32)[:, None, None], 0.0)
    return out


def _attn_prompt_kernel(lam_ref, q_ref, kt_ref, vt_ref, bias_ref, gain_ref, o_ref, m_sc, acc_sc, sa_sc, sb_sc):
    qb = pl.program_id(2)
    tq = q_ref.shape[0]
    seq = kt_ref.shape[1]
    per_step = ATT_STEP // ATT_BLOCK
    lane = lax.broadcasted_iota(jnp.int32, (tq, LANES), 1)
    feat_k = lax.broadcasted_iota(jnp.int32, (LANES, ATT_STEP), 0)
    qs = q_ref[...].astype(f32) * (DA_HEAD_DIM ** -0.5 * LOG2E)
    q4 = jnp.concatenate(
        [jnp.where((lane // DA_HEAD_DIM) == i, qs, 0.0).astype(bf16) for i in range(4)], axis=0)

    m_sc[...] = jnp.full(m_sc.shape, -jnp.inf, f32)
    acc_sc[...] = jnp.zeros(acc_sc.shape, f32)

    n_steps = qb // per_step + 1
    last = n_steps - 1
    r = qb % per_step

    def scores(k):
        start = pl.multiple_of(jnp.minimum(k * ATT_STEP, seq - ATT_STEP), ATT_STEP)
        return jnp.dot(q4, kt_ref[:, pl.ds(start, ATT_STEP)], preferred_element_type=f32)

    def half(k, cur_ref, nxt_ref):
        nxt_ref[...] = scores(k + 1)
        tile = jnp.where(k == last, 1 + r, jnp.where((k == last - 1) & (r == 0), per_step + 1, 0))
        vv = vt_ref[:, pl.ds(pl.multiple_of(k * ATT_STEP, ATT_STEP), ATT_STEP)]
        for hh in range(2):
            bias = bias_ref[0, tile, hh * tq:(hh + 1) * tq, :]
            ps, alphas = [], []
            for c in range(2):
                rows = slice((2 * hh + c) * tq, (2 * hh + c + 1) * tq)
                s = cur_ref[rows, :] + bias
                m_old = m_sc[rows]
                m_new = jnp.maximum(m_old, jnp.max(s, axis=-1, keepdims=True))
                ps.append(jnp.exp2(s - m_new[:, 0:1]).astype(bf16))
                alphas.append(jnp.exp2(m_old - m_new))
                m_sc[rows] = m_new
            rows2 = slice(2 * hh * tq, (2 * hh + 2) * tq)
            vaug = jnp.where((feat_k // DA_V_DIM) == hh, vv, jnp.ones_like(vv))
            acc_sc[rows2] = (acc_sc[rows2] * jnp.concatenate(alphas, axis=0)
                             + lax.dot_general(jnp.concatenate(ps, axis=0), vaug, (((1,), (1,)), ((), ())),
                                               preferred_element_type=f32))

    sa_sc[...] = scores(0)

    def two_steps(kk, carry):
        half(2 * kk, sa_sc, sb_sc)

        @pl.when(2 * kk + 1 < n_steps)
        def _():
            half(2 * kk + 1, sb_sc, sa_sc)

        return carry

    lax.fori_loop(0, (n_steps + 1) // 2, two_steps, 0)

    lam = lam_ref[0]
    outs = []
    for hh in range(2):
        own = (lane // DA_V_DIM) == hh
        a1 = acc_sc[2 * hh * tq:(2 * hh + 1) * tq]
        a2 = acc_sc[(2 * hh + 1) * tq:(2 * hh + 2) * tq]
        l1 = jnp.max(jnp.where(own, 0.0, a1), axis=-1, keepdims=True)
        l2 = jnp.max(jnp.where(own, 0.0, a2), axis=-1, keepdims=True)
        o = a1 / l1 - lam * (a2 / l2)
        ssq = jnp.sum(jnp.where(own, o * o, 0.0), axis=-1, keepdims=True)
        outs.append(o * lax.rsqrt(ssq * (1.0 / DA_V_DIM) + EPS))
    o = jnp.where((lane // DA_V_DIM) == 0, outs[0], outs[1])
    o_ref[...] = (o * gain_ref[...]).astype(o_ref.dtype)


def _attn_prompt(q, kt, vt, bias_tiles, gain, lam, batch, seq):
    t, w = q.shape
    pairs = w // LANES
    nq = seq // ATT_BLOCK
    assert seq % ATT_STEP == 0
    n_tiles = ATT_STEP // ATT_BLOCK + 2
    return pl.pallas_call(
        _attn_prompt_kernel,
        grid=(batch, pairs, nq),
        in_specs=[
            pl.BlockSpec(memory_space=pltpu.SMEM),
            pl.BlockSpec((ATT_BLOCK, LANES), lambda b, p, i: (b * nq + i, p)),
            pl.BlockSpec((LANES, seq), lambda b, p, i: (p, b)),
            pl.BlockSpec((LANES, seq), lambda b, p, i: (p, b)),
            pl.BlockSpec((1, n_tiles, 2 * ATT_BLOCK, ATT_STEP), lambda b, p, i: (p, 0, 0, 0)),
            pl.BlockSpec((1, LANES), lambda b, p, i: (0, 0)),
        ],
        out_specs=pl.BlockSpec((ATT_BLOCK, LANES), lambda b, p, i: (b * nq + i, p)),
        out_shape=jax.ShapeDtypeStruct((t, w), bf16),
        scratch_shapes=[
            pltpu.VMEM((4 * ATT_BLOCK, LANES), f32),
            pltpu.VMEM((4 * ATT_BLOCK, LANES), f32),
            pltpu.VMEM((4 * ATT_BLOCK, ATT_STEP), f32),
            pltpu.VMEM((4 * ATT_BLOCK, ATT_STEP), f32),
        ],
        compiler_params=_cparams("parallel", "parallel", "arbitrary"),
        name="attn_prompt",
    )(lam, q, kt, vt, bias_tiles, gain)


def _prompt_bias_tiles(rel_bias):
    h = rel_bias.shape[1]
    per_step = ATT_STEP // ATT_BLOCK
    i = jnp.arange(ATT_BLOCK)
    rel_diag = i[None, :] - i[:, None]
    rel_prev = rel_diag - ATT_BLOCK
    far = rel_bias[REL_BUCKETS // 2 - 1].astype(f32)
    b_diag = (_bias_of(rel_diag, rel_bias) - far[:, None, None]) * LOG2E
    b_prev = (_bias_of(rel_prev, rel_bias) - far[:, None, None]) * LOG2E
    mask = (i[None, :] // CHUNK) <= (i[:, None] // CHUNK)
    b_diag = jnp.where(mask[None], b_diag, NEG)
    zero = jnp.zeros_like(b_diag)
    dead = jnp.full_like(b_diag, NEG)
    tiles = [jnp.concatenate([zero] * per_step, axis=-1)]
    for r in range(per_step):
        blocks = [zero if j < r - 1 else b_prev if j == r - 1 else b_diag if j == r else dead
                  for j in range(per_step)]
        tiles.append(jnp.concatenate(blocks, axis=-1))
    tiles.append(jnp.concatenate([zero] * (per_step - 1) + [b_prev], axis=-1))
    tiles = jnp.stack(tiles, axis=1)
    tiles = tiles.reshape(h // 2, 2, per_step + 2, ATT_BLOCK, ATT_STEP)
    return jnp.transpose(tiles, (0, 2, 1, 3, 4)).reshape(h // 2, per_step + 2, 2 * ATT_BLOCK, ATT_STEP)


def _attn_sample_kernel(lam_ref, q_ref, kp_ref, vp_ref, kn_ref, vn_ref, bp_ref, bn_ref, gain_ref, o_ref):
    nq = q_ref.shape[0]
    pairs = q_ref.shape[1] // LANES
    lane = lax.broadcasted_iota(jnp.int32, (nq, LANES), 1)
    lam = lam_ref[0]
    nt = (((1,), (1,)), ((), ()))
    for p in range(pairs):
        sl = slice(p * LANES, (p + 1) * LANES)
        qs = q_ref[:, sl].astype(f32) * (DA_HEAD_DIM ** -0.5 * LOG2E)
        kp = kp_ref[sl, :].astype(bf16)
        vp = vp_ref[sl, :].astype(bf16)
        kn = kn_ref[:, sl].astype(bf16)
        vn = vn_ref[:, sl].astype(bf16)
        outs = []
        for hh in range(2):
            h = 2 * p + hh
            own = (lane // DA_V_DIM) == hh
            res = []
            for c in range(2):
                qm = jnp.where((lane // DA_HEAD_DIM) == 2 * hh + c, qs, 0.0).astype(bf16)
                sp = jnp.dot(qm, kp, preferred_element_type=f32) + bp_ref[h]
                sn = lax.dot_general(qm, kn, nt, preferred_element_type=f32) + bn_ref[h]
                m = jnp.maximum(jnp.max(sp, axis=-1, keepdims=True), jnp.max(sn, axis=-1, keepdims=True))
                pp = jnp.exp2(sp - m)
                pn = jnp.exp2(sn - m)
                l = jnp.sum(pp, axis=-1, keepdims=True) + jnp.sum(pn, axis=-1, keepdims=True)
                pv = (lax.dot_general(pp.astype(bf16), vp, nt, preferred_element_type=f32)
                      + jnp.dot(pn.astype(bf16), vn, preferred_element_type=f32))
                res.append(pv / l)
            o = res[0] - lam * res[1]
            ssq = jnp.sum(jnp.where(own, o * o, 0.0), axis=-1, keepdims=True)
            outs.append(o * lax.rsqrt(ssq * (1.0 / DA_V_DIM) + EPS))
        o = jnp.where((lane // DA_V_DIM) == 0, outs[0], outs[1])
        o_ref[:, sl] = (o * gain_ref[...]).astype(o_ref.dtype)


def _attn_sample(q, k_new, v_new, k_past, v_past, bias_past, bias_new, gain, lam, batch, nq, past):
    w = q.shape[1]
    h = bias_past.shape[0]
    return pl.pallas_call(
        _attn_sample_kernel,
        grid=(batch,),
        in_specs=[
            pl.BlockSpec(memory_space=pltpu.SMEM),
            pl.BlockSpec((nq, w), lambda b: (b, 0)),
            pl.BlockSpec((w, past), lambda b: (b, 0)),
            pl.BlockSpec((w, past), lambda b: (b, 0)),
            pl.BlockSpec((nq, w), lambda b: (b, 0)),
            pl.BlockSpec((nq, w), lambda b: (b, 0)),
            pl.BlockSpec((h, nq, past), lambda b: (0, 0, 0)),
            pl.BlockSpec((h, nq, nq), lambda b: (0, 0, 0)),
            pl.BlockSpec((1, LANES), lambda b: (0, 0)),
        ],
        out_specs=pl.BlockSpec((nq, w), lambda b: (b, 0)),
        out_shape=jax.ShapeDtypeStruct((batch * nq, w), bf16),
        compiler_params=_cparams("parallel"),
        name="attn_sample",
    )(lam, q, k_past, v_past, k_new, v_new, bias_past, bias_new, gain)


def _hgrn_chunk(q, f_logit, vv, g_logit, lb, gain, s0, lc):
    n_sub = lc // HG_SUB
    f = lb + (1.0 - lb) * jax.nn.sigmoid(f_logit)
    logf = jnp.log(f)
    k = 1.0 - f
    row = lax.broadcasted_iota(jnp.int32, (lc, lc), 0)
    col = lax.broadcasted_iota(jnp.int32, (lc, lc), 1)
    tril = (col <= row).astype(f32)
    b = jnp.dot(tril, logf, precision=lax.Precision.HIGHEST, preferred_element_type=f32)
    b_last = b[lc - 1:lc, :]

    o = jnp.dot((q * jnp.exp(b)).astype(bf16), s0.astype(bf16), preferred_element_type=f32)
    k_hat = k * jnp.exp(b_last - b)
    bt = b.T
    decay_col = jnp.exp(bt[:, lc - 1:lc])
    s_new = decay_col * s0 + jnp.dot(k_hat.T.astype(bf16), vv.astype(bf16), preferred_element_type=f32)

    if n_sub > 1:
        rows = []
        nt = (((1,), (1,)), ((), ()))
        for i in range(n_sub):
            lo = i * HG_SUB
            if i == 0:
                rows.append(jnp.zeros((HG_SUB, lc), f32))
                continue
            beta = b[lo - 1:lo, :]
            q_t = q[lo:lo + HG_SUB, :] * jnp.exp(b[lo:lo + HG_SUB, :] - beta)
            k_t = k * jnp.exp(jnp.minimum(beta - b, 0.0))
            rows.append(lax.dot_general(q_t.astype(bf16), k_t.astype(bf16), nt, preferred_element_type=f32))
        a_off = jnp.concatenate(rows, axis=0)
        a_off = jnp.where(col < (row // HG_SUB) * HG_SUB, a_off, 0.0)
        o = o + jnp.dot(a_off.astype(bf16), vv.astype(bf16), preferred_element_type=f32)

    ones = jnp.ones((HG_DIM, LANES), bf16)
    srow = lax.broadcasted_iota(jnp.int32, (HG_SUB, HG_DIM), 0)
    orow = lax.broadcasted_iota(jnp.int32, (HG_SUB, HG_DIM), 0)
    blocks = []
    for i in range(n_sub):
        lo = i * HG_SUB
        b_i = b[lo:lo + HG_SUB, :]
        k_i = k[lo:lo + HG_SUB, :]
        v_i = vv[lo:lo + HG_SUB, :]
        q_i = q[lo:lo + HG_SUB, :]
        d_rows = []
        for t in range(HG_SUB):
            arg = jnp.where(srow <= t, b_i[t:t + 1, :] - b_i, -jnp.inf)
            d_rows.append(q_i[t:t + 1, :] * k_i * jnp.exp(arg))
        d3 = jnp.concatenate(d_rows, axis=0)
        a_rep = jnp.dot(d3.astype(bf16), ones, preferred_element_type=f32)
        o_i = jnp.zeros((HG_SUB, HG_DIM), f32)
        for t in range(HG_SUB):
            o_t = jnp.sum(a_rep[t * HG_SUB:(t + 1) * HG_SUB, :] * v_i, axis=0, keepdims=True)
            o_i = jnp.where(orow == t, o_t, o_i)
        blocks.append(o_i)
    o = o + (jnp.concatenate(blocks, axis=0) if n_sub > 1 else blocks[0])

    on = o * lax.rsqrt(jnp.mean(o * o, axis=-1, keepdims=True) + EPS) * gain
    gate = g_logit * jax.nn.sigmoid(g_logit)
    return on * gate, s_new


def _hgrn_kernel(hz_ref, lb_ref, gain_ref, s0_ref, o_ref, s_ref, st_sc, *, lc, heads):
    j = pl.program_id(1)
    w = heads * HG_DIM

    @pl.when(j == 0)
    def _():
        st_sc[...] = s0_ref[0]

    n_chunks = hz_ref.shape[0] // lc
    for c in range(n_chunks):
        rs = slice(c * lc, (c + 1) * lc)
        for h in range(heads):
            cs = slice(h * HG_DIM, (h + 1) * HG_DIM)
            o, s_new = _hgrn_chunk(
                hz_ref[rs, h * HG_DIM:(h + 1) * HG_DIM],
                hz_ref[rs, w + h * HG_DIM:w + (h + 1) * HG_DIM],
                hz_ref[rs, 2 * w + h * HG_DIM:2 * w + (h + 1) * HG_DIM],
                hz_ref[rs, 3 * w + h * HG_DIM:3 * w + (h + 1) * HG_DIM],
                lb_ref[:, cs], gain_ref[...], st_sc[h], lc)
            st_sc[h] = s_new
            o_ref[rs, cs] = o.astype(o_ref.dtype)

    @pl.when(j == pl.num_programs(1) - 1)
    def _():
        s_ref[0] = st_sc[...]


def _hgrn(hz, lb, gain, s0, batch, seq, lc, chunks_per_step):
    t = hz.shape[0]
    w = hz.shape[1] // 4
    heads = w // HG_DIM
    rows = lc * chunks_per_step
    steps = seq // rows
    return pl.pallas_call(
        functools.partial(_hgrn_kernel, lc=lc, heads=heads),
        grid=(batch, steps),
        in_specs=[
            pl.BlockSpec((rows, 4 * w), lambda b, j: (b * steps + j, 0)),
            pl.BlockSpec((1, w), lambda b, j: (0, 0)),
            pl.BlockSpec((1, HG_DIM), lambda b, j: (0, 0)),
            pl.BlockSpec((1, heads, HG_DIM, HG_DIM), lambda b, j: (b, 0, 0, 0)),
        ],
        out_specs=[
            pl.BlockSpec((rows, w), lambda b, j: (b * steps + j, 0)),
            pl.BlockSpec((1, heads, HG_DIM, HG_DIM), lambda b, j: (b, 0, 0, 0)),
        ],
        out_shape=[
            jax.ShapeDtypeStruct((t, w), bf16),
            jax.ShapeDtypeStruct((batch, heads, HG_DIM, HG_DIM), f32),
        ],
        scratch_shapes=[pltpu.VMEM((heads, HG_DIM, HG_DIM), f32)],
        compiler_params=_cparams("parallel", "arbitrary"),
        name="hgrn2",
    )(hz, lb, gain, s0)


def _top_rows(vals, k, payload=None):
    n_rows = vals.shape[0]
    row = lax.broadcasted_iota(jnp.int32, vals.shape, 0)
    out_v, out_i = [], []
    for _ in range(k):
        m = jnp.max(vals, axis=0, keepdims=True)
        first = jnp.min(jnp.where(vals == m, row, n_rows), axis=0, keepdims=True)
        hit = row == first
        out_v.append(m)
        if payload is None:
            out_i.append(first)
        else:
            out_i.append(jnp.max(jnp.where(hit, payload, -1), axis=0, keepdims=True))
        vals = jnp.where(hit, -jnp.inf, vals)
    return jnp.concatenate(out_v, axis=0), jnp.concatenate(out_i, axis=0)


def _retrieve_kernel(oa_ref, ob_ref, x_ref, wo_ref, g_ref, wqt_ref, sk_ref,
                     hp_ref, xn_ref, idx_ref, gate_ref, qt_sc, s1v_sc, s1i_sc, idx_sc, gate_sc):
    half = oa_ref.shape[1]
    y = (jnp.dot(oa_ref[...], wo_ref[0:half, :], preferred_element_type=f32)
         + jnp.dot(ob_ref[...], wo_ref[half:, :], preferred_element_type=f32))
    hp = x_ref[...] + y
    hp_ref[...] = hp
    xn = hp * lax.rsqrt(jnp.mean(hp * hp, axis=-1, keepdims=True) + EPS) * g_ref[...]
    xn_ref[...] = xn
    nt = (((1,), (1,)), ((), ()))
    qt_sc[...] = lax.dot_general(wqt_ref[...], xn.astype(bf16), nt, preferred_element_type=f32).astype(bf16)

    n_half = sk_ref.shape[0]

    def stage1(hc, carry):
        start = pl.multiple_of(hc * PEER_SUB_DIM, PEER_SUB_DIM)
        s = jnp.dot(sk_ref[hc], qt_sc[pl.ds(start, PEER_SUB_DIM), :], preferred_element_type=f32)
        v, i = _top_rows(s, PEER_TOPK)
        s1v_sc[hc] = v
        s1i_sc[hc] = i
        return carry

    lax.fori_loop(0, n_half, stage1, 0)

    tm = x_ref.shape[0]
    sub = lax.broadcasted_iota(jnp.int32, (8, tm), 0)

    def stage2(h, carry):
        v1, i1 = s1v_sc[2 * h], s1i_sc[2 * h]
        v2, i2 = s1v_sc[2 * h + 1], s1i_sc[2 * h + 1]
        cv, ci = [], []
        for i in range(PEER_TOPK):
            nj = PEER_TOPK // (i + 1)
            rows = PEER_TOPK if nj > 8 else 8
            a = v1[i:i + 1, :] + v2[0:rows, :]
            e = i1[i:i + 1, :] * PEER_KEYS + i2[0:rows, :]
            if nj < rows:
                a = jnp.where(sub < nj, a, -jnp.inf)
            cv.append(a)
            ci.append(e)
        top_s, eidx = _top_rows(jnp.concatenate(cv, axis=0), PEER_TOPK, jnp.concatenate(ci, axis=0))
        p = jnp.exp(top_s - top_s[0:1, :])
        g = p / jnp.sum(p, axis=0, keepdims=True)
        r0 = pl.multiple_of(h * PEER_TOPK, PEER_TOPK)
        gate_sc[pl.ds(r0, PEER_TOPK), :] = g
        idx_sc[pl.ds(r0, PEER_TOPK), :] = eidx
        return carry

    lax.fori_loop(0, n_half // 2, stage2, 0)
    idx_ref[...] = pltpu.bitcast(pltpu.bitcast(idx_sc[...], f32).T, jnp.int32)
    gate_ref[...] = gate_sc[...].T


def _retrieve(oa, ob, x, wo_bf, g2, wqt_bf, sk_bf, tm, x_block0=0):
    t, half = oa.shape
    d = x.shape[1]
    e = wqt_bf.shape[0]
    n_half = sk_bf.shape[0]
    n_sel = (n_half // 2) * PEER_TOPK
    return pl.pallas_call(
        _retrieve_kernel,
        grid=(t // tm,),
        in_specs=[
            pl.BlockSpec((tm, half), lambda i: (i, 0)),
            pl.BlockSpec((tm, half), lambda i: (i, 0)),
            pl.BlockSpec((tm, d), lambda i: (x_block0 + i, 0)),
            pl.BlockSpec((2 * half, d), lambda i: (0, 0)),
            pl.BlockSpec((1, d), lambda i: (0, 0)),
            pl.BlockSpec((e, d), lambda i: (0, 0)),
            pl.BlockSpec((n_half, PEER_KEYS, PEER_SUB_DIM), lambda i: (0, 0, 0)),
        ],
        out_specs=[
            pl.BlockSpec((tm, d), lambda i: (i, 0)),
            pl.BlockSpec((tm, d), lambda i: (i, 0)),
            pl.BlockSpec((tm, n_sel), lambda i: (i, 0)),
            pl.BlockSpec((tm, n_sel), lambda i: (i, 0)),
        ],
        out_shape=[
            jax.ShapeDtypeStruct((t, d), f32),
            jax.ShapeDtypeStruct((t, d), f32),
            jax.ShapeDtypeStruct((t, n_sel), jnp.int32),
            jax.ShapeDtypeStruct((t, n_sel), f32),
        ],
        scratch_shapes=[
            pltpu.VMEM((e, tm), bf16),
            pltpu.VMEM((n_half, PEER_TOPK, tm), f32),
            pltpu.VMEM((n_half, PEER_TOPK, tm), jnp.int32),
            pltpu.VMEM((n_sel, tm), jnp.int32),
            pltpu.VMEM((n_sel, tm), f32),
        ],
        compiler_params=_cparams("parallel"),
        name="retrieve",
    )(oa, ob, x, wo_bf, g2, wqt_bf, sk_bf)


def _pack_table(tab):
    half = tab.shape[1] // 2
    lo = lax.bitcast_convert_type(tab[:, :half].astype(bf16), jnp.uint16).astype(jnp.uint32)
    hi = lax.bitcast_convert_type(tab[:, half:].astype(bf16), jnp.uint16).astype(jnp.uint32)
    return lax.bitcast_convert_type(lo | (hi << 16), jnp.int32)


def _sc_layout(n_tok, n_sel, words):
    info = plsc.get_sparse_core_info()
    n_workers = info.num_cores * info.num_subcores
    lanes = info.num_lanes
    block = 8 * lanes
    assert n_tok % (2 * n_workers) == 0 and words % block == 0 and (n_sel // 2) % lanes == 0
    return info, lanes, n_tok // n_workers, n_sel // 2, block


def _selected_dots(table, idx, x):
    n_tok, n_sel = idx.shape
    words = table.shape[1]
    info, lanes, tpw, half_rows, block = _sc_layout(n_tok, n_sel, words)
    idx2 = idx.reshape(2 * n_tok, half_rows)
    mesh = plsc.VectorSubcoreMesh(core_axis_name="c", subcore_axis_name="s")

    @functools.partial(
        pl.kernel, out_type=jax.ShapeDtypeStruct((n_tok, n_sel), f32), mesh=mesh,
        compiler_params=pltpu.CompilerParams(needs_layout_passes=False),
        scratch_types=[
            pltpu.VMEM((2 * tpw, half_rows), jnp.int32),
            pltpu.VMEM((half_rows, words), jnp.int32),
            pltpu.VMEM((half_rows, words), jnp.int32),
            pltpu.VMEM((2 * words,), f32),
            pltpu.VMEM((2 * words,), f32),
            pltpu.VMEM((n_sel // 8, 8 * lanes), f32),
            pltpu.VMEM((tpw, n_sel), f32),
            pltpu.SemaphoreType.DMA, pltpu.SemaphoreType.DMA, pltpu.SemaphoreType.DMA, pltpu.SemaphoreType.DMA,
        ])
    def dots(t_hbm, i_hbm, x_hbm, o_hbm, idx_v, rows_a, rows_b, x_a, x_b, part_v, act_v, sem_a, sem_b, sem_xa, sem_xb):
        wid = lax.axis_index("s") * info.num_cores + lax.axis_index("c")
        base = wid * tpw
        pltpu.sync_copy(i_hbm.at[pl.ds(base * 2, 2 * tpw)], idx_v)

        def accumulate(rows, x_v, half):
            @pl.loop(0, words // block)
            def _(cb):
                xs = [x_v[pl.ds(p * words + cb * block + c * lanes, lanes)] for c in range(8) for p in range(2)]

                @plsc.parallel_loop(0, half_rows)
                def _(row):
                    packed = [rows[row, pl.ds(cb * block + c * lanes, lanes)] for c in range(8)]
                    unp = [plsc.unpack(plsc.bitcast(p, bf16), format=plsc.PackFormat.INTERLEAVED) for p in packed]
                    prods = [unp[c][p] * xs[2 * c + p] for c in range(8) for p in range(2)]
                    while len(prods) > 1:
                        prods = [prods[k] + prods[k + 1] for k in range(0, len(prods), 2)]
                    prow = half * half_rows + row
                    plsc.addupdate(part_v.at[prow // 8, pl.ds((prow % 8) * lanes, lanes)], prods[0])

        def gather(half_index, rows, sem):
            return pltpu.make_async_copy(t_hbm.at[idx_v.at[half_index]], rows, sem)

        def xcopy(t, x_v, sem):
            return pltpu.make_async_copy(x_hbm.at[base + t], x_v, sem)

        def token(t, x_v, sem_x, x_next, sem_xn):
            gather(2 * t + 1, rows_b, sem_b).start()

            @pl.when(t + 1 < tpw)
            def _():
                xcopy(t + 1, x_next, sem_xn).start()

            zero = jnp.zeros((lanes,), f32)
            for r in range(n_sel):
                part_v[r // 8, pl.ds((r % 8) * lanes, lanes)] = zero
            xcopy(t, x_v, sem_x).wait()
            gather(2 * t, rows_a, sem_a).wait()
            accumulate(rows_a, x_v, 0)

            @pl.when(t + 1 < tpw)
            def _():
                gather(2 * t + 2, rows_a, sem_a).start()

            gather(2 * t + 1, rows_b, sem_b).wait()
            accumulate(rows_b, x_v, 1)
            it = lax.iota(jnp.int32, lanes)
            for g in range(n_sel // lanes):
                prow = g * (lanes // 8) + it // 8
                pcol = (it % 8) * lanes
                tot = plsc.load_gather(part_v, [prow, pcol])
                for l in range(1, lanes):
                    tot = tot + plsc.load_gather(part_v, [prow, pcol + l])
                act_v[t, pl.ds(g * lanes, lanes)] = tot

        gather(0, rows_a, sem_a).start()
        xcopy(0, x_a, sem_xa).start()

        @pl.loop(0, tpw // 2)
        def _(i):
            token(2 * i, x_a, sem_xa, x_b, sem_xb)
            token(2 * i + 1, x_b, sem_xb, x_a, sem_xa)

        pltpu.sync_copy(act_v, o_hbm.at[pl.ds(base, tpw)])

    return dots(table, idx2, x)


def _gate_weights_kernel(act_ref, gate_ref, w_ref):
    a = act_ref[...]
    w_ref[...] = gate_ref[...] * (0.5 * a * (1.0 + lax.erf(a * (2.0 ** -0.5))))


def _gate_weights(act, gate, tm):
    t, n_sel = act.shape
    spec = pl.BlockSpec((tm, n_sel), lambda i: (i, 0))
    return pl.pallas_call(
        _gate_weights_kernel,
        grid=(t // tm,),
        in_specs=[spec, spec],
        out_specs=spec,
        out_shape=jax.ShapeDtypeStruct((t, n_sel), f32),
        compiler_params=_cparams("parallel"),
        name="peer_weights",
    )(act, gate)


def _weighted_rows(table, idx, w):
    n_tok, n_sel = w.shape
    words = table.shape[1]
    info, lanes, tpw, half_rows, _ = _sc_layout(n_tok, n_sel, words)
    idx2 = idx.reshape(2 * n_tok, half_rows)
    mesh = plsc.VectorSubcoreMesh(core_axis_name="c", subcore_axis_name="s")

    @functools.partial(
        pl.kernel, out_type=jax.ShapeDtypeStruct((n_tok, 2 * words), f32), mesh=mesh,
        compiler_params=pltpu.CompilerParams(needs_layout_passes=False),
        scratch_types=[
            pltpu.VMEM((2 * tpw, half_rows), jnp.int32),
            pltpu.VMEM((tpw, n_sel), f32),
            pltpu.VMEM((half_rows, words), jnp.int32),
            pltpu.VMEM((half_rows, words), jnp.int32),
            pltpu.VMEM((2 * words,), f32),
            pltpu.VMEM((2 * words,), f32),
            pltpu.SemaphoreType.DMA, pltpu.SemaphoreType.DMA, pltpu.SemaphoreType.DMA, pltpu.SemaphoreType.DMA,
        ])
    def mix(t_hbm, i_hbm, w_hbm, o_hbm, idx_v, w_v, rows_a, rows_b, out_a, out_b, sem_a, sem_b, sem_oa, sem_ob):
        wid = lax.axis_index("s") * info.num_cores + lax.axis_index("c")
        base = wid * tpw
        pltpu.sync_copy(i_hbm.at[pl.ds(base * 2, 2 * tpw)], idx_v)
        pltpu.sync_copy(w_hbm.at[pl.ds(base, tpw)], w_v)

        def accumulate(rows, out_v, t, half, first):
            if first:
                zero = jnp.zeros((lanes,), f32)
                for c in range(2 * words // lanes):
                    out_v[pl.ds(c * lanes, lanes)] = zero

            @pl.loop(0, half_rows // lanes)
            def _(g):
                wv = w_v[t, pl.ds(half * half_rows + g * lanes, lanes)]
                ws = [wv.at[jnp.full((lanes,), r, jnp.int32)].get(mode="promise_in_bounds") for r in range(lanes)]

                @plsc.parallel_loop(0, words // lanes)
                def _(c):
                    packed = [rows[g * lanes + r, pl.ds(c * lanes, lanes)] for r in range(lanes)]
                    unp = [plsc.unpack(plsc.bitcast(p, bf16), format=plsc.PackFormat.INTERLEAVED) for p in packed]
                    lo = [unp[r][0] * ws[r] for r in range(lanes)]
                    hi = [unp[r][1] * ws[r] for r in range(lanes)]
                    while len(lo) > 1:
                        lo = [lo[k] + lo[k + 1] for k in range(0, len(lo), 2)]
                        hi = [hi[k] + hi[k + 1] for k in range(0, len(hi), 2)]
                    plsc.addupdate(out_v.at[pl.ds(c * lanes, lanes)], lo[0])
                    plsc.addupdate(out_v.at[pl.ds(words + c * lanes, lanes)], hi[0])

        def gather(half_index, rows, sem):
            return pltpu.make_async_copy(t_hbm.at[idx_v.at[half_index]], rows, sem)

        def token(t, out_v, sem_o, out_in_flight):
            gather(2 * t + 1, rows_b, sem_b).start()
            gather(2 * t, rows_a, sem_a).wait()

            @pl.when(out_in_flight)
            def _():
                pltpu.make_async_copy(out_v, o_hbm.at[base + t], sem_o).wait()

            accumulate(rows_a, out_v, t, 0, True)

            @pl.when(t + 1 < tpw)
            def _():
                gather(2 * t + 2, rows_a, sem_a).start()

            gather(2 * t + 1, rows_b, sem_b).wait()
            accumulate(rows_b, out_v, t, 1, False)
            pltpu.make_async_copy(out_v, o_hbm.at[base + t], sem_o).start()

        gather(0, rows_a, sem_a).start()

        @pl.loop(0, tpw // 2)
        def _(i):
            token(2 * i, out_a, sem_oa, i > 0)
            token(2 * i + 1, out_b, sem_ob, i > 0)

        pltpu.make_async_copy(out_a, o_hbm.at[base], sem_oa).wait()
        pltpu.make_async_copy(out_b, o_hbm.at[base], sem_ob).wait()

    return mix(table, idx2, w)


def _final_kernel(hp_ref, peer_ref, gf_ref, yin_ref, y_ref, done_ref):
    del yin_ref
    done_ref[...] = jnp.zeros(done_ref.shape, f32)
    h = hp_ref[...] + peer_ref[...]
    y_ref[...] = h * lax.rsqrt(jnp.mean(h * h, axis=-1, keepdims=True) + EPS) * gf_ref[...]


def _final(hp, peer, gf, y_all, y_block0, tm):
    t, d = hp.shape
    return pl.pallas_call(
        _final_kernel,
        grid=(t // tm,),
        in_specs=[
            pl.BlockSpec((tm, d), lambda i: (i, 0)),
            pl.BlockSpec((tm, d), lambda i: (i, 0)),
            pl.BlockSpec((1, d), lambda i: (0, 0)),
            pl.BlockSpec(memory_space=pl.ANY),
        ],
        out_specs=[pl.BlockSpec((tm, d), lambda i: (y_block0 + i, 0)),
                   pl.BlockSpec((8, LANES), lambda i: (0, 0))],
        out_shape=[jax.ShapeDtypeStruct(y_all.shape, f32), jax.ShapeDtypeStruct((8, LANES), f32)],
        input_output_aliases={3: 0},
        compiler_params=_cparams("arbitrary"),
        name="final_norm",
    )(hp, peer, gf, y_all)


def _channel_mixer(oa, ob, x, x_block0, wo_bf, g2, wqt_bf, sk_bf, u_pk, v_pk, gf, tm, y_all, y_block0):
    hp, xn, eidx, gate = _retrieve(oa, ob, x, wo_bf, g2, wqt_bf, sk_bf, tm, x_block0)
    act = _selected_dots(u_pk, eidx, xn)
    w = _gate_weights(act, gate, tm)
    peer = _weighted_rows(v_pk, eidx, w)
    return _final(hp, peer, gf, y_all, y_block0, tm)


def kernel(x_prompt, x_sample, cache_k, cache_v, state_hgrn, norm1, w_in, da_lq1, da_lk1, da_lq2, da_lk2,
           da_out_norm, hg_lb_logits, hg_out_norm, w_out, rel_bias, norm2, peer_w_q, peer_sub_keys,
           peer_u, peer_v, final_norm):
    batch, seq, d = x_prompt.shape
    dbatch, dseq, _ = x_sample.shape
    past = cache_k.shape[2]
    da_heads = cache_k.shape[3]
    hg_heads = state_hgrn.shape[2]
    depth = w_in.shape[0]
    assert depth == 1 and seq % ATT_BLOCK == 0 and seq % CHUNK == 0 and dseq % HG_SUB == 0 and dseq <= CHUNK
    assert past % CHUNK == 0 and (past + dseq - 1) // CHUNK == past // CHUNK

    l = 0
    lam_init = 0.8 - 0.6 * math.exp(-0.3 * l)
    lam = (jnp.exp(jnp.sum(da_lq1[l].astype(f32) * da_lk1[l].astype(f32)))
           - jnp.exp(jnp.sum(da_lq2[l].astype(f32) * da_lk2[l].astype(f32))) + lam_init).reshape(1)
    lb = jnp.cumsum(jax.nn.softmax(hg_lb_logits.astype(f32), axis=0), axis=0)[l].reshape(1, -1)
    da_gain = (jnp.tile(da_out_norm[l].astype(f32), 2) * (1.0 - lam_init)).reshape(1, LANES)
    hg_gain = hg_out_norm[l].astype(f32).reshape(1, HG_DIM)
    g1 = norm1[l].astype(f32).reshape(1, d)
    g2 = norm2[l].astype(f32).reshape(1, d)
    gf = final_norm.astype(f32).reshape(1, d)
    w_in_bf = w_in[l].astype(bf16)
    wkv_t_bf = w_in[l][:, cache_k.shape[3] * cache_k.shape[4]:3 * cache_k.shape[3] * cache_k.shape[4]].T.astype(bf16)
    wo_bf = w_out[l].astype(bf16)
    wqt_bf = peer_w_q[l].T.astype(bf16)
    sk_bf = peer_sub_keys[l].reshape(-1, PEER_KEYS, PEER_SUB_DIM).astype(bf16)
    u_pk = _pack_table(peer_u[l])
    v_pk = _pack_table(peer_v[l])

    bias_tiles = _prompt_bias_tiles(rel_bias)
    q_pos = past + jnp.arange(dseq)
    bias_s = _bias_of(jnp.arange(past + dseq)[None, :] - q_pos[:, None], rel_bias) * LOG2E
    bias_past, bias_new = bias_s[:, :, :past], bias_s[:, :, past:]
    zero_state = jnp.zeros((1, hg_heads, HG_DIM, HG_DIM), f32)

    dh2 = 2 * DA_HEAD_DIM
    wq = da_heads * dh2
    tm = 256
    x_all = x_prompt.reshape(batch * seq, d)
    kt_all = jnp.zeros((batch * wq, seq), f32)
    vt_all = jnp.zeros((batch * wq, seq), f32)
    y_all = jnp.zeros((batch * seq, d), f32)
    s_p = []
    done = [jnp.zeros((8, LANES), f32)] * 3
    for b in range(batch):
        q, kt, vt, hz, kt_all, vt_all = _inproj_prompt(x_all, g1, w_in_bf, wkv_t_bf, kt_all, vt_all, done[b],
                                                       b, seq, tm)
        oa = _attn_prompt(q, kt, vt, bias_tiles, da_gain, lam, 1, seq)
        ob, s_new = _hgrn(hz, lb, hg_gain, zero_state, 1, seq, CHUNK, 4)
        y_all, token = _channel_mixer(oa, ob, x_all, b * (seq // tm), wo_bf, g2, wqt_bf, sk_bf, u_pk, v_pk, gf,
                                      tm, y_all, b * (seq // tm))
        done.append(token)
        s_p.append(s_new)

    xs = x_sample.reshape(dbatch * dseq, d)
    q, k_s, v_s, hz = _inproj_sample(xs, g1, w_in_bf, wq)
    kp_t = jnp.transpose(cache_k[l], (0, 2, 3, 1)).reshape(dbatch * wq, past)
    vp_t = jnp.transpose(cache_v[l], (0, 2, 3, 1)).reshape(dbatch * wq, past)
    oa = _attn_sample(q, k_s, v_s, kp_t, vp_t, bias_past, bias_new, da_gain, lam, dbatch, dseq, past)
    ob, s_s = _hgrn(hz, lb, hg_gain, state_hgrn[l].astype(f32), dbatch, dseq, dseq, 1)
    y_s, _ = _channel_mixer(oa, ob, xs, 0, wo_bf, g2, wqt_bf, sk_bf, u_pk, v_pk, gf, dbatch * dseq,
                            jnp.zeros((dbatch * dseq, d), f32), 0)

    y_prompt = y_all.reshape(batch, seq, d)
    y_sample = y_s.reshape(dbatch, dseq, d)
    k_prompt = jnp.transpose(kt_all.reshape(batch, da_heads, dh2, seq), (0, 3, 1, 2))[None]
    v_prompt = jnp.transpose(vt_all.reshape(batch, da_heads, DA_V_DIM, seq), (0, 3, 1, 2))[None]
    state_prompt = jnp.concatenate(s_p, axis=0)[None].astype(state_hgrn.dtype)
    k_sample = k_s.reshape(1, dbatch, dseq, da_heads, dh2)
    v_sample = v_s.reshape(1, dbatch, dseq, da_heads, DA_V_DIM)
    state_sample = s_s[None].astype(state_hgrn.dtype)
    return (y_prompt, y_sample, k_prompt, v_prompt, state_prompt, k_sample, v_sample, state_sample)
```

```python
import functools
import math

import jax
import jax.numpy as jnp
from jax import lax
from jax.experimental import pallas as pl
from jax.experimental.pallas import tpu as pltpu
from jax.experimental.pallas import tpu_sc as plsc

CHUNK = 64
DA_HEAD_DIM = 32
DA_V_DIM = 2 * DA_HEAD_DIM
HG_DIM = 128
REL_BUCKETS = 32
REL_MAX_DIST = 128
PEER_HEADS = 8
PEER_KEYS = 128
PEER_TOPK = 16
PEER_SUB_DIM = 64
EPS = 1e-6
NEG = -1e30
LOG2E = 1.4426950408889634

LANES = 128
ATT_BLOCK = 128
ATT_STEP = 512
HG_SUB = 16
VMEM_LIMIT = 56 * 1024 * 1024

f32 = jnp.float32
bf16 = jnp.bfloat16


def _cparams(*sem):
    return pltpu.CompilerParams(dimension_semantics=sem, vmem_limit_bytes=VMEM_LIMIT)


def _inproj_prompt_kernel(x_ref, g_ref, w_ref, wkv_t_ref, kin_ref, vin_ref, after_ref,
                          q_ref, kt_ref, vt_ref, hz_ref, ktf_ref, vtf_ref):
    del kin_ref, vin_ref
    del after_ref
    x = x_ref[...]
    xn = x * lax.rsqrt(jnp.mean(x * x, axis=-1, keepdims=True) + EPS) * g_ref[...]
    xb = xn.astype(bf16)
    wq = q_ref.shape[1]
    nt = (((1,), (1,)), ((), ()))
    q_ref[...] = jnp.dot(xb, w_ref[:, 0:wq], preferred_element_type=f32).astype(bf16)
    kt = lax.dot_general(wkv_t_ref[0:wq, :], xb, nt, preferred_element_type=f32)
    ktf_ref[...] = kt
    kt_ref[...] = kt.astype(bf16)
    vt = lax.dot_general(wkv_t_ref[wq:2 * wq, :], xb, nt, preferred_element_type=f32)
    vtf_ref[...] = vt
    vt_ref[...] = vt.astype(bf16)
    hz_ref[...] = jnp.dot(xb, w_ref[:, 3 * wq:], preferred_element_type=f32)


def _inproj_prompt(x, g, w_bf, wkv_t_bf, kt_all, vt_all, after, b, seq, tm):
    d = x.shape[1]
    e = w_bf.shape[1]
    wq = wkv_t_bf.shape[0] // 2
    steps = seq // tm
    return pl.pallas_call(
        _inproj_prompt_kernel,
        grid=(steps,),
        in_specs=[
            pl.BlockSpec((tm, d), lambda i: (b * steps + i, 0)),
            pl.BlockSpec((1, d), lambda i: (0, 0)),
            pl.BlockSpec((d, e), lambda i: (0, 0)),
            pl.BlockSpec((2 * wq, d), lambda i: (0, 0)),
            pl.BlockSpec(memory_space=pl.ANY),
            pl.BlockSpec(memory_space=pl.ANY),
            pl.BlockSpec(memory_space=pl.ANY),
        ],
        out_specs=[
            pl.BlockSpec((tm, wq), lambda i: (i, 0)),
            pl.BlockSpec((wq, tm), lambda i: (0, i)),
            pl.BlockSpec((wq, tm), lambda i: (0, i)),
            pl.BlockSpec((tm, e - 3 * wq), lambda i: (i, 0)),
            pl.BlockSpec((wq, tm), lambda i: (b, i)),
            pl.BlockSpec((wq, tm), lambda i: (b, i)),
        ],
        out_shape=[
            jax.ShapeDtypeStruct((seq, wq), bf16),
            jax.ShapeDtypeStruct((wq, seq), bf16),
            jax.ShapeDtypeStruct((wq, seq), bf16),
            jax.ShapeDtypeStruct((seq, e - 3 * wq), f32),
            jax.ShapeDtypeStruct(kt_all.shape, f32),
            jax.ShapeDtypeStruct(vt_all.shape, f32),
        ],
        input_output_aliases={4: 4, 5: 5},
        compiler_params=_cparams("parallel"),
        name="inproj",
    )(x, g, w_bf, wkv_t_bf, kt_all, vt_all, after)


def _inproj_sample_kernel(x_ref, g_ref, w_ref, q_ref, k_ref, v_ref, hz_ref):
    x = x_ref[...]
    xn = x * lax.rsqrt(jnp.mean(x * x, axis=-1, keepdims=True) + EPS) * g_ref[...]
    xb = xn.astype(bf16)
    wq = q_ref.shape[1]
    q_ref[...] = jnp.dot(xb, w_ref[:, 0:wq], preferred_element_type=f32).astype(bf16)
    k_ref[...] = jnp.dot(xb, w_ref[:, wq:2 * wq], preferred_element_type=f32)
    v_ref[...] = jnp.dot(xb, w_ref[:, 2 * wq:3 * wq], preferred_element_type=f32)
    hz_ref[...] = jnp.dot(xb, w_ref[:, 3 * wq:], preferred_element_type=f32)


def _inproj_sample(x, g, w_bf, wq):
    t, d = x.shape
    e = w_bf.shape[1]
    return pl.pallas_call(
        _inproj_sample_kernel,
        grid=(1,),
        in_specs=[
            pl.BlockSpec((t, d), lambda i: (0, 0)),
            pl.BlockSpec((1, d), lambda i: (0, 0)),
            pl.BlockSpec((d, e), lambda i: (0, 0)),
        ],
        out_specs=[
            pl.BlockSpec((t, wq), lambda i: (0, 0)),
            pl.BlockSpec((t, wq), lambda i: (0, 0)),
            pl.BlockSpec((t, wq), lambda i: (0, 0)),
            pl.BlockSpec((t, e - 3 * wq), lambda i: (0, 0)),
        ],
        out_shape=[
            jax.ShapeDtypeStruct((t, wq), bf16),
            jax.ShapeDtypeStruct((t, wq), f32),
            jax.ShapeDtypeStruct((t, wq), f32),
            jax.ShapeDtypeStruct((t, e - 3 * wq), f32),
        ],
        compiler_params=_cparams("arbitrary"),
        name="inproj_sample",
    )(x, g, w_bf)


def _rel_bucket(rel):
    nb = REL_BUCKETS // 2
    max_exact = nb // 2
    ret = jnp.where(rel > 0, nb, 0)
    n = jnp.abs(rel)
    nf = jnp.maximum(n, 1).astype(f32)
    large = max_exact + (jnp.log(nf / max_exact) / math.log(REL_MAX_DIST / max_exact)
                         * (nb - max_exact)).astype(jnp.int32)
    large = jnp.minimum(large, nb - 1)
    return ret + jnp.where(n < max_exact, n, large)


def _bias_of(rel, rel_bias):
    bucket = _rel_bucket(rel)
    out = jnp.zeros((rel_bias.shape[1],) + rel.shape, f32)
    for b in range(REL_BUCKETS):
        out = out + jnp.where(bucket[None] == b, rel_bias[b].astype(f32)[:, None, None], 0.0)
    return out


def _attn_prompt_kernel(lam_ref, q_ref, kt_ref, vt_ref, bias_ref, gain_ref, o_ref, m_sc, acc_sc, sa_sc, sb_sc,
                        *, q_block0):
    qb = pl.program_id(1) + q_block0
    tq = q_ref.shape[0]
    seq = kt_ref.shape[1]
    per_step = ATT_STEP // ATT_BLOCK
    lane = lax.broadcasted_iota(jnp.int32, (tq, LANES), 1)
    feat_k = lax.broadcasted_iota(jnp.int32, (LANES, ATT_STEP), 0)
    qs = q_ref[...].astype(f32) * (DA_HEAD_DIM ** -0.5 * LOG2E)
    q4 = jnp.concatenate(
        [jnp.where((lane // DA_HEAD_DIM) == i, qs, 0.0).astype(bf16) for i in range(4)], axis=0)

    m_sc[...] = jnp.full(m_sc.shape, -jnp.inf, f32)
    acc_sc[...] = jnp.zeros(acc_sc.shape, f32)

    n_steps = qb // per_step + 1
    last = n_steps - 1
    r = qb % per_step

    def scores(k):
        start = pl.multiple_of(jnp.minimum(k * ATT_STEP, seq - ATT_STEP), ATT_STEP)
        return jnp.dot(q4, kt_ref[:, pl.ds(start, ATT_STEP)], preferred_element_type=f32)

    def half(k, cur_ref, nxt_ref):
        nxt_ref[...] = scores(k + 1)
        tile = jnp.where(k == last, 1 + r, jnp.where((k == last - 1) & (r == 0), per_step + 1, 0))
        vv = vt_ref[:, pl.ds(pl.multiple_of(k * ATT_STEP, ATT_STEP), ATT_STEP)]
        for hh in range(2):
            bias = bias_ref[0, tile, hh * tq:(hh + 1) * tq, :]
            ps, alphas = [], []
            for c in range(2):
                rows = slice((2 * hh + c) * tq, (2 * hh + c + 1) * tq)
                s = cur_ref[rows, :] + bias
                m_old = m_sc[rows]
                m_new = jnp.maximum(m_old, jnp.max(s, axis=-1, keepdims=True))
                ps.append(jnp.exp2(s - m_new[:, 0:1]).astype(bf16))
                alphas.append(jnp.exp2(m_old - m_new))
                m_sc[rows] = m_new
            rows2 = slice(2 * hh * tq, (2 * hh + 2) * tq)
            vaug = jnp.where((feat_k // DA_V_DIM) == hh, vv, jnp.ones_like(vv))
            acc_sc[rows2] = (acc_sc[rows2] * jnp.concatenate(alphas, axis=0)
                             + lax.dot_general(jnp.concatenate(ps, axis=0), vaug, (((1,), (1,)), ((), ())),
                                               preferred_element_type=f32))

    sa_sc[...] = scores(0)

    def two_steps(kk, carry):
        half(2 * kk, sa_sc, sb_sc)

        @pl.when(2 * kk + 1 < n_steps)
        def _():
            half(2 * kk + 1, sb_sc, sa_sc)

        return carry

    lax.fori_loop(0, (n_steps + 1) // 2, two_steps, 0)

    lam = lam_ref[0]
    outs = []
    for hh in range(2):
        own = (lane // DA_V_DIM) == hh
        a1 = acc_sc[2 * hh * tq:(2 * hh + 1) * tq]
        a2 = acc_sc[(2 * hh + 1) * tq:(2 * hh + 2) * tq]
        l1 = jnp.max(jnp.where(own, 0.0, a1), axis=-1, keepdims=True)
        l2 = jnp.max(jnp.where(own, 0.0, a2), axis=-1, keepdims=True)
        o = a1 / l1 - lam * (a2 / l2)
        ssq = jnp.sum(jnp.where(own, o * o, 0.0), axis=-1, keepdims=True)
        outs.append(o * lax.rsqrt(ssq * (1.0 / DA_V_DIM) + EPS))
    o = jnp.where((lane // DA_V_DIM) == 0, outs[0], outs[1])
    o_ref[...] = (o * gain_ref[...]).astype(o_ref.dtype)


def _attn_prompt(q, kt, vt, bias_tiles, gain, lam, q_block0, n_q):
    seq, w = q.shape
    pairs = w // LANES
    assert seq % ATT_STEP == 0
    n_tiles = ATT_STEP // ATT_BLOCK + 2
    return pl.pallas_call(
        functools.partial(_attn_prompt_kernel, q_block0=q_block0),
        grid=(pairs, n_q),
        in_specs=[
            pl.BlockSpec(memory_space=pltpu.SMEM),
            pl.BlockSpec((ATT_BLOCK, LANES), lambda p, i: (q_block0 + i, p)),
            pl.BlockSpec((LANES, seq), lambda p, i: (p, 0)),
            pl.BlockSpec((LANES, seq), lambda p, i: (p, 0)),
            pl.BlockSpec((1, n_tiles, 2 * ATT_BLOCK, ATT_STEP), lambda p, i: (p, 0, 0, 0)),
            pl.BlockSpec((1, LANES), lambda p, i: (0, 0)),
        ],
        out_specs=pl.BlockSpec((ATT_BLOCK, LANES), lambda p, i: (i, p)),
        out_shape=jax.ShapeDtypeStruct((n_q * ATT_BLOCK, w), bf16),
        scratch_shapes=[
            pltpu.VMEM((4 * ATT_BLOCK, LANES), f32),
            pltpu.VMEM((4 * ATT_BLOCK, LANES), f32),
            pltpu.VMEM((4 * ATT_BLOCK, ATT_STEP), f32),
            pltpu.VMEM((4 * ATT_BLOCK, ATT_STEP), f32),
        ],
        compiler_params=_cparams("parallel", "arbitrary"),
        name="attn_prompt",
    )(lam, q, kt, vt, bias_tiles, gain)


def _prompt_bias_tiles(rel_bias):
    h = rel_bias.shape[1]
    per_step = ATT_STEP // ATT_BLOCK
    i = jnp.arange(ATT_BLOCK)
    rel_diag = i[None, :] - i[:, None]
    rel_prev = rel_diag - ATT_BLOCK
    far = rel_bias[REL_BUCKETS // 2 - 1].astype(f32)
    b_diag = (_bias_of(rel_diag, rel_bias) - far[:, None, None]) * LOG2E
    b_prev = (_bias_of(rel_prev, rel_bias) - far[:, None, None]) * LOG2E
    mask = (i[None, :] // CHUNK) <= (i[:, None] // CHUNK)
    b_diag = jnp.where(mask[None], b_diag, NEG)
    zero = jnp.zeros_like(b_diag)
    dead = jnp.full_like(b_diag, NEG)
    tiles = [jnp.concatenate([zero] * per_step, axis=-1)]
    for r in range(per_step):
        blocks = [zero if j < r - 1 else b_prev if j == r - 1 else b_diag if j == r else dead
                  for j in range(per_step)]
        tiles.append(jnp.concatenate(blocks, axis=-1))
    tiles.append(jnp.concatenate([zero] * (per_step - 1) + [b_prev], axis=-1))
    tiles = jnp.stack(tiles, axis=1)
    tiles = tiles.reshape(h // 2, 2, per_step + 2, ATT_BLOCK, ATT_STEP)
    return jnp.transpose(tiles, (0, 2, 1, 3, 4)).reshape(h // 2, per_step + 2, 2 * ATT_BLOCK, ATT_STEP)


def _attn_sample_kernel(lam_ref, q_ref, kp_ref, vp_ref, kn_ref, vn_ref, bp_ref, bn_ref, gain_ref, o_ref):
    nq = q_ref.shape[0]
    pairs = q_ref.shape[1] // LANES
    lane = lax.broadcasted_iota(jnp.int32, (nq, LANES), 1)
    lam = lam_ref[0]
    nt = (((1,), (1,)), ((), ()))
    for p in range(pairs):
        sl = slice(p * LANES, (p + 1) * LANES)
        qs = q_ref[:, sl].astype(f32) * (DA_HEAD_DIM ** -0.5 * LOG2E)
        kp = kp_ref[sl, :].astype(bf16)
        vp = vp_ref[sl, :].astype(bf16)
        kn = kn_ref[:, sl].astype(bf16)
        vn = vn_ref[:, sl].astype(bf16)
        outs = []
        for hh in range(2):
            h = 2 * p + hh
            own = (lane // DA_V_DIM) == hh
            res = []
            for c in range(2):
                qm = jnp.where((lane // DA_HEAD_DIM) == 2 * hh + c, qs, 0.0).astype(bf16)
                sp = jnp.dot(qm, kp, preferred_element_type=f32) + bp_ref[h]
                sn = lax.dot_general(qm, kn, nt, preferred_element_type=f32) + bn_ref[h]
                m = jnp.maximum(jnp.max(sp, axis=-1, keepdims=True), jnp.max(sn, axis=-1, keepdims=True))
                pp = jnp.exp2(sp - m)
                pn = jnp.exp2(sn - m)
                l = jnp.sum(pp, axis=-1, keepdims=True) + jnp.sum(pn, axis=-1, keepdims=True)
                pv = (lax.dot_general(pp.astype(bf16), vp, nt, preferred_element_type=f32)
                      + jnp.dot(pn.astype(bf16), vn, preferred_element_type=f32))
                res.append(pv / l)
            o = res[0] - lam * res[1]
            ssq = jnp.sum(jnp.where(own, o * o, 0.0), axis=-1, keepdims=True)
            outs.append(o * lax.rsqrt(ssq * (1.0 / DA_V_DIM) + EPS))
        o = jnp.where((lane // DA_V_DIM) == 0, outs[0], outs[1])
        o_ref[:, sl] = (o * gain_ref[...]).astype(o_ref.dtype)


def _attn_sample(q, k_new, v_new, k_past, v_past, bias_past, bias_new, gain, lam, batch, nq, past):
    w = q.shape[1]
    h = bias_past.shape[0]
    return pl.pallas_call(
        _attn_sample_kernel,
        grid=(batch,),
        in_specs=[
            pl.BlockSpec(memory_space=pltpu.SMEM),
            pl.BlockSpec((nq, w), lambda b: (b, 0)),
            pl.BlockSpec((w, past), lambda b: (b, 0)),
            pl.BlockSpec((w, past), lambda b: (b, 0)),
            pl.BlockSpec((nq, w), lambda b: (b, 0)),
            pl.BlockSpec((nq, w), lambda b: (b, 0)),
            pl.BlockSpec((h, nq, past), lambda b: (0, 0, 0)),
            pl.BlockSpec((h, nq, nq), lambda b: (0, 0, 0)),
            pl.BlockSpec((1, LANES), lambda b: (0, 0)),
        ],
        out_specs=pl.BlockSpec((nq, w), lambda b: (b, 0)),
        out_shape=jax.ShapeDtypeStruct((batch * nq, w), bf16),
        compiler_params=_cparams("parallel"),
        name="attn_sample",
    )(lam, q, k_past, v_past, k_new, v_new, bias_past, bias_new, gain)


def _hgrn_chunk(q, f_logit, vv, g_logit, lb, gain, s0, lc):
    n_sub = lc // HG_SUB
    f = lb + (1.0 - lb) * jax.nn.sigmoid(f_logit)
    logf = jnp.log(f)
    k = 1.0 - f
    row = lax.broadcasted_iota(jnp.int32, (lc, lc), 0)
    col = lax.broadcasted_iota(jnp.int32, (lc, lc), 1)
    tril = (col <= row).astype(f32)
    b = jnp.dot(tril, logf, precision=lax.Precision.HIGHEST, preferred_element_type=f32)
    b_last = b[lc - 1:lc, :]

    o = jnp.dot((q * jnp.exp(b)).astype(bf16), s0.astype(bf16), preferred_element_type=f32)
    k_hat = k * jnp.exp(b_last - b)
    bt = b.T
    decay_col = jnp.exp(bt[:, lc - 1:lc])
    s_new = decay_col * s0 + jnp.dot(k_hat.T.astype(bf16), vv.astype(bf16), preferred_element_type=f32)

    if n_sub > 1:
        rows = []
        nt = (((1,), (1,)), ((), ()))
        for i in range(n_sub):
            lo = i * HG_SUB
            if i == 0:
                rows.append(jnp.zeros((HG_SUB, lc), f32))
                continue
            beta = b[lo - 1:lo, :]
            q_t = q[lo:lo + HG_SUB, :] * jnp.exp(b[lo:lo + HG_SUB, :] - beta)
            k_t = k * jnp.exp(jnp.minimum(beta - b, 0.0))
            rows.append(lax.dot_general(q_t.astype(bf16), k_t.astype(bf16), nt, preferred_element_type=f32))
        a_off = jnp.concatenate(rows, axis=0)
        a_off = jnp.where(col < (row // HG_SUB) * HG_SUB, a_off, 0.0)
        o = o + jnp.dot(a_off.astype(bf16), vv.astype(bf16), preferred_element_type=f32)

    ones = jnp.ones((HG_DIM, LANES), bf16)
    srow = lax.broadcasted_iota(jnp.int32, (HG_SUB, HG_DIM), 0)
    orow = lax.broadcasted_iota(jnp.int32, (HG_SUB, HG_DIM), 0)
    blocks = []
    for i in range(n_sub):
        lo = i * HG_SUB
        b_i = b[lo:lo + HG_SUB, :]
        k_i = k[lo:lo + HG_SUB, :]
        v_i = vv[lo:lo + HG_SUB, :]
        q_i = q[lo:lo + HG_SUB, :]
        d_rows = []
        for t in range(HG_SUB):
            arg = jnp.where(srow <= t, b_i[t:t + 1, :] - b_i, -jnp.inf)
            d_rows.append(q_i[t:t + 1, :] * k_i * jnp.exp(arg))
        d3 = jnp.concatenate(d_rows, axis=0)
        a_rep = jnp.dot(d3.astype(bf16), ones, preferred_element_type=f32)
        o_i = jnp.zeros((HG_SUB, HG_DIM), f32)
        for t in range(HG_SUB):
            o_t = jnp.sum(a_rep[t * HG_SUB:(t + 1) * HG_SUB, :] * v_i, axis=0, keepdims=True)
            o_i = jnp.where(orow == t, o_t, o_i)
        blocks.append(o_i)
    o = o + (jnp.concatenate(blocks, axis=0) if n_sub > 1 else blocks[0])

    on = o * lax.rsqrt(jnp.mean(o * o, axis=-1, keepdims=True) + EPS) * gain
    gate = g_logit * jax.nn.sigmoid(g_logit)
    return on * gate, s_new


def _hgrn_kernel(hz_ref, lb_ref, gain_ref, s0_ref, o_ref, s_ref, st_sc, *, lc, heads):
    j = pl.program_id(1)
    w = heads * HG_DIM

    @pl.when(j == 0)
    def _():
        st_sc[...] = s0_ref[0]

    n_chunks = hz_ref.shape[0] // lc
    for c in range(n_chunks):
        rs = slice(c * lc, (c + 1) * lc)
        for h in range(heads):
            cs = slice(h * HG_DIM, (h + 1) * HG_DIM)
            o, s_new = _hgrn_chunk(
                hz_ref[rs, h * HG_DIM:(h + 1) * HG_DIM],
                hz_ref[rs, w + h * HG_DIM:w + (h + 1) * HG_DIM],
                hz_ref[rs, 2 * w + h * HG_DIM:2 * w + (h + 1) * HG_DIM],
                hz_ref[rs, 3 * w + h * HG_DIM:3 * w + (h + 1) * HG_DIM],
                lb_ref[:, cs], gain_ref[...], st_sc[h], lc)
            st_sc[h] = s_new
            o_ref[rs, cs] = o.astype(o_ref.dtype)

    @pl.when(j == pl.num_programs(1) - 1)
    def _():
        s_ref[0] = st_sc[...]


def _hgrn(hz, lb, gain, s0, batch, seq, lc, chunks_per_step):
    t = hz.shape[0]
    w = hz.shape[1] // 4
    heads = w // HG_DIM
    rows = lc * chunks_per_step
    steps = seq // rows
    return pl.pallas_call(
        functools.partial(_hgrn_kernel, lc=lc, heads=heads),
        grid=(batch, steps),
        in_specs=[
            pl.BlockSpec((rows, 4 * w), lambda b, j: (b * steps + j, 0)),
            pl.BlockSpec((1, w), lambda b, j: (0, 0)),
            pl.BlockSpec((1, HG_DIM), lambda b, j: (0, 0)),
            pl.BlockSpec((1, heads, HG_DIM, HG_DIM), lambda b, j: (b, 0, 0, 0)),
        ],
        out_specs=[
            pl.BlockSpec((rows, w), lambda b, j: (b * steps + j, 0)),
            pl.BlockSpec((1, heads, HG_DIM, HG_DIM), lambda b, j: (b, 0, 0, 0)),
        ],
        out_shape=[
            jax.ShapeDtypeStruct((t, w), bf16),
            jax.ShapeDtypeStruct((batch, heads, HG_DIM, HG_DIM), f32),
        ],
        scratch_shapes=[pltpu.VMEM((heads, HG_DIM, HG_DIM), f32)],
        compiler_params=_cparams("parallel", "arbitrary"),
        name="hgrn2",
    )(hz, lb, gain, s0)


def _top_rows(vals, k, payload=None):
    n_rows = vals.shape[0]
    row = lax.broadcasted_iota(jnp.int32, vals.shape, 0)
    out_v, out_i = [], []
    for _ in range(k):
        m = jnp.max(vals, axis=0, keepdims=True)
        first = jnp.min(jnp.where(vals == m, row, n_rows), axis=0, keepdims=True)
        hit = row == first
        out_v.append(m)
        if payload is None:
            out_i.append(first)
        else:
            out_i.append(jnp.max(jnp.where(hit, payload, -1), axis=0, keepdims=True))
        vals = jnp.where(hit, -jnp.inf, vals)
    return jnp.concatenate(out_v, axis=0), jnp.concatenate(out_i, axis=0)


def _retrieve_kernel(oa_ref, ob_ref, x_ref, wo_ref, g_ref, wqt_ref, sk_ref,
                     hp_ref, xn_ref, idx_ref, gate_ref, qt_sc, s1v_sc, s1i_sc, idx_sc, gate_sc):
    half = oa_ref.shape[1]
    y = (jnp.dot(oa_ref[...], wo_ref[0:half, :], preferred_element_type=f32)
         + jnp.dot(ob_ref[...], wo_ref[half:, :], preferred_element_type=f32))
    hp = x_ref[...] + y
    hp_ref[...] = hp
    xn = hp * lax.rsqrt(jnp.mean(hp * hp, axis=-1, keepdims=True) + EPS) * g_ref[...]
    xn_ref[...] = xn
    nt = (((1,), (1,)), ((), ()))
    qt_sc[...] = lax.dot_general(wqt_ref[...], xn.astype(bf16), nt, preferred_element_type=f32).astype(bf16)

    n_half = sk_ref.shape[0]

    def stage1(hc, carry):
        start = pl.multiple_of(hc * PEER_SUB_DIM, PEER_SUB_DIM)
        s = jnp.dot(sk_ref[hc], qt_sc[pl.ds(start, PEER_SUB_DIM), :], preferred_element_type=f32)
        v, i = _top_rows(s, PEER_TOPK)
        s1v_sc[hc] = v
        s1i_sc[hc] = i
        return carry

    lax.fori_loop(0, n_half, stage1, 0)

    tm = x_ref.shape[0]
    sub = lax.broadcasted_iota(jnp.int32, (8, tm), 0)

    def stage2(h, carry):
        v1, i1 = s1v_sc[2 * h], s1i_sc[2 * h]
        v2, i2 = s1v_sc[2 * h + 1], s1i_sc[2 * h + 1]
        cv, ci = [], []
        for i in range(PEER_TOPK):
            nj = PEER_TOPK // (i + 1)
            rows = PEER_TOPK if nj > 8 else 8
            a = v1[i:i + 1, :] + v2[0:rows, :]
            e = i1[i:i + 1, :] * PEER_KEYS + i2[0:rows, :]
            if nj < rows:
                a = jnp.where(sub < nj, a, -jnp.inf)
            cv.append(a)
            ci.append(e)
        top_s, eidx = _top_rows(jnp.concatenate(cv, axis=0), PEER_TOPK, jnp.concatenate(ci, axis=0))
        p = jnp.exp(top_s - top_s[0:1, :])
        g = p / jnp.sum(p, axis=0, keepdims=True)
        r0 = pl.multiple_of(h * PEER_TOPK, PEER_TOPK)
        gate_sc[pl.ds(r0, PEER_TOPK), :] = g
        idx_sc[pl.ds(r0, PEER_TOPK), :] = eidx
        return carry

    lax.fori_loop(0, n_half // 2, stage2, 0)
    idx_ref[...] = pltpu.bitcast(pltpu.bitcast(idx_sc[...], f32).T, jnp.int32)
    gate_ref[...] = gate_sc[...].T


def _retrieve(oa, ob, x, wo_bf, g2, wqt_bf, sk_bf, tm, x_block0=0, ob_block0=0):
    t, half = oa.shape
    d = x.shape[1]
    e = wqt_bf.shape[0]
    n_half = sk_bf.shape[0]
    n_sel = (n_half // 2) * PEER_TOPK
    return pl.pallas_call(
        _retrieve_kernel,
        grid=(t // tm,),
        in_specs=[
            pl.BlockSpec((tm, half), lambda i: (i, 0)),
            pl.BlockSpec((tm, half), lambda i: (ob_block0 + i, 0)),
            pl.BlockSpec((tm, d), lambda i: (x_block0 + i, 0)),
            pl.BlockSpec((2 * half, d), lambda i: (0, 0)),
            pl.BlockSpec((1, d), lambda i: (0, 0)),
            pl.BlockSpec((e, d), lambda i: (0, 0)),
            pl.BlockSpec((n_half, PEER_KEYS, PEER_SUB_DIM), lambda i: (0, 0, 0)),
        ],
        out_specs=[
            pl.BlockSpec((tm, d), lambda i: (i, 0)),
            pl.BlockSpec((tm, d), lambda i: (i, 0)),
            pl.BlockSpec((tm, n_sel), lambda i: (i, 0)),
            pl.BlockSpec((tm, n_sel), lambda i: (i, 0)),
        ],
        out_shape=[
            jax.ShapeDtypeStruct((t, d), f32),
            jax.ShapeDtypeStruct((t, d), f32),
            jax.ShapeDtypeStruct((t, n_sel), jnp.int32),
            jax.ShapeDtypeStruct((t, n_sel), f32),
        ],
        scratch_shapes=[
            pltpu.VMEM((e, tm), bf16),
            pltpu.VMEM((n_half, PEER_TOPK, tm), f32),
            pltpu.VMEM((n_half, PEER_TOPK, tm), jnp.int32),
            pltpu.VMEM((n_sel, tm), jnp.int32),
            pltpu.VMEM((n_sel, tm), f32),
        ],
        compiler_params=_cparams("parallel"),
        name="retrieve",
    )(oa, ob, x, wo_bf, g2, wqt_bf, sk_bf)


def _pack_table(tab):
    half = tab.shape[1] // 2
    lo = lax.bitcast_convert_type(tab[:, :half].astype(bf16), jnp.uint16).astype(jnp.uint32)
    hi = lax.bitcast_convert_type(tab[:, half:].astype(bf16), jnp.uint16).astype(jnp.uint32)
    return lax.bitcast_convert_type(lo | (hi << 16), jnp.int32)


def _sc_layout(n_tok, n_sel, words):
    info = plsc.get_sparse_core_info()
    n_workers = info.num_cores * info.num_subcores
    lanes = info.num_lanes
    block = 8 * lanes
    assert n_tok % (2 * n_workers) == 0 and words % block == 0 and (n_sel // 2) % lanes == 0
    return info, lanes, n_tok // n_workers, n_sel // 2, block


def _selected_dots(table, idx, x):
    n_tok, n_sel = idx.shape
    words = table.shape[1]
    info, lanes, tpw, half_rows, block = _sc_layout(n_tok, n_sel, words)
    idx2 = idx.reshape(2 * n_tok, half_rows)
    mesh = plsc.VectorSubcoreMesh(core_axis_name="c", subcore_axis_name="s")

    @functools.partial(
        pl.kernel, out_type=jax.ShapeDtypeStruct((n_tok, n_sel), f32), mesh=mesh,
        compiler_params=pltpu.CompilerParams(needs_layout_passes=False),
        scratch_types=[
            pltpu.VMEM((2 * tpw, half_rows), jnp.int32),
            pltpu.VMEM((half_rows, words), jnp.int32),
            pltpu.VMEM((half_rows, words), jnp.int32),
            pltpu.VMEM((2 * words,), f32),
            pltpu.VMEM((2 * words,), f32),
            pltpu.VMEM((n_sel // 8, 8 * lanes), f32),
            pltpu.VMEM((tpw, n_sel), f32),
            pltpu.SemaphoreType.DMA, pltpu.SemaphoreType.DMA, pltpu.SemaphoreType.DMA, pltpu.SemaphoreType.DMA,
        ])
    def dots(t_hbm, i_hbm, x_hbm, o_hbm, idx_v, rows_a, rows_b, x_a, x_b, part_v, act_v, sem_a, sem_b, sem_xa, sem_xb):
        wid = lax.axis_index("s") * info.num_cores + lax.axis_index("c")
        base = wid * tpw
        pltpu.sync_copy(i_hbm.at[pl.ds(base * 2, 2 * tpw)], idx_v)

        def accumulate(rows, x_v, half):
            @pl.loop(0, words // block)
            def _(cb):
                xs = [x_v[pl.ds(p * words + cb * block + c * lanes, lanes)] for c in range(8) for p in range(2)]

                @plsc.parallel_loop(0, half_rows)
                def _(row):
                    packed = [rows[row, pl.ds(cb * block + c * lanes, lanes)] for c in range(8)]
                    unp = [plsc.unpack(plsc.bitcast(p, bf16), format=plsc.PackFormat.INTERLEAVED) for p in packed]
                    prods = [unp[c][p] * xs[2 * c + p] for c in range(8) for p in range(2)]
                    while len(prods) > 1:
                        prods = [prods[k] + prods[k + 1] for k in range(0, len(prods), 2)]
                    prow = half * half_rows + row
                    plsc.addupdate(part_v.at[prow // 8, pl.ds((prow % 8) * lanes, lanes)], prods[0])

        def gather(half_index, rows, sem):
            return pltpu.make_async_copy(t_hbm.at[idx_v.at[half_index]], rows, sem)

        def xcopy(t, x_v, sem):
            return pltpu.make_async_copy(x_hbm.at[base + t], x_v, sem)

        def token(t, x_v, sem_x, x_next, sem_xn):
            gather(2 * t + 1, rows_b, sem_b).start()

            @pl.when(t + 1 < tpw)
            def _():
                xcopy(t + 1, x_next, sem_xn).start()

            zero = jnp.zeros((lanes,), f32)
            for r in range(n_sel):
                part_v[r // 8, pl.ds((r % 8) * lanes, lanes)] = zero
            xcopy(t, x_v, sem_x).wait()
            gather(2 * t, rows_a, sem_a).wait()
            accumulate(rows_a, x_v, 0)

            @pl.when(t + 1 < tpw)
            def _():
                gather(2 * t + 2, rows_a, sem_a).start()

            gather(2 * t + 1, rows_b, sem_b).wait()
            accumulate(rows_b, x_v, 1)
            it = lax.iota(jnp.int32, lanes)
            for g in range(n_sel // lanes):
                prow = g * (lanes // 8) + it // 8
                pcol = (it % 8) * lanes
                tot = plsc.load_gather(part_v, [prow, pcol])
                for l in range(1, lanes):
                    tot = tot + plsc.load_gather(part_v, [prow, pcol + l])
                act_v[t, pl.ds(g * lanes, lanes)] = tot

        gather(0, rows_a, sem_a).start()
        xcopy(0, x_a, sem_xa).start()

        @pl.loop(0, tpw // 2)
        def _(i):
            token(2 * i, x_a, sem_xa, x_b, sem_xb)
            token(2 * i + 1, x_b, sem_xb, x_a, sem_xa)

        pltpu.sync_copy(act_v, o_hbm.at[pl.ds(base, tpw)])

    return dots(table, idx2, x)


def _gate_weights_kernel(act_ref, gate_ref, after_ref, w_ref):
    del after_ref
    a = act_ref[...]
    w_ref[...] = gate_ref[...] * (0.5 * a * (1.0 + lax.erf(a * (2.0 ** -0.5))))


def _gate_weights(act, gate, after, tm):
    t, n_sel = act.shape
    spec = pl.BlockSpec((tm, n_sel), lambda i: (i, 0))
    return pl.pallas_call(
        _gate_weights_kernel,
        grid=(t // tm,),
        in_specs=[spec, spec, pl.BlockSpec(memory_space=pl.ANY)],
        out_specs=spec,
        out_shape=jax.ShapeDtypeStruct((t, n_sel), f32),
        compiler_params=_cparams("parallel"),
        name="peer_weights",
    )(act, gate, after)


def _weighted_rows(table, idx, w):
    n_tok, n_sel = w.shape
    words = table.shape[1]
    info, lanes, tpw, half_rows, _ = _sc_layout(n_tok, n_sel, words)
    idx2 = idx.reshape(2 * n_tok, half_rows)
    mesh = plsc.VectorSubcoreMesh(core_axis_name="c", subcore_axis_name="s")

    @functools.partial(
        pl.kernel, out_type=jax.ShapeDtypeStruct((n_tok, 2 * words), f32), mesh=mesh,
        compiler_params=pltpu.CompilerParams(needs_layout_passes=False),
        scratch_types=[
            pltpu.VMEM((2 * tpw, half_rows), jnp.int32),
            pltpu.VMEM((tpw, n_sel), f32),
            pltpu.VMEM((half_rows, words), jnp.int32),
            pltpu.VMEM((half_rows, words), jnp.int32),
            pltpu.VMEM((2 * words,), f32),
            pltpu.VMEM((2 * words,), f32),
            pltpu.SemaphoreType.DMA, pltpu.SemaphoreType.DMA, pltpu.SemaphoreType.DMA, pltpu.SemaphoreType.DMA,
        ])
    def mix(t_hbm, i_hbm, w_hbm, o_hbm, idx_v, w_v, rows_a, rows_b, out_a, out_b, sem_a, sem_b, sem_oa, sem_ob):
        wid = lax.axis_index("s") * info.num_cores + lax.axis_index("c")
        base = wid * tpw
        pltpu.sync_copy(i_hbm.at[pl.ds(base * 2, 2 * tpw)], idx_v)
        pltpu.sync_copy(w_hbm.at[pl.ds(base, tpw)], w_v)

        def accumulate(rows, out_v, t, half, first):
            if first:
                zero = jnp.zeros((lanes,), f32)
                for c in range(2 * words // lanes):
                    out_v[pl.ds(c * lanes, lanes)] = zero

            @pl.loop(0, half_rows // lanes)
            def _(g):
                wv = w_v[t, pl.ds(half * half_rows + g * lanes, lanes)]
                ws = [wv.at[jnp.full((lanes,), r, jnp.int32)].get(mode="promise_in_bounds") for r in range(lanes)]

                @plsc.parallel_loop(0, words // lanes)
                def _(c):
                    packed = [rows[g * lanes + r, pl.ds(c * lanes, lanes)] for r in range(lanes)]
                    unp = [plsc.unpack(plsc.bitcast(p, bf16), format=plsc.PackFormat.INTERLEAVED) for p in packed]
                    lo = [unp[r][0] * ws[r] for r in range(lanes)]
                    hi = [unp[r][1] * ws[r] for r in range(lanes)]
                    while len(lo) > 1:
                        lo = [lo[k] + lo[k + 1] for k in range(0, len(lo), 2)]
                        hi = [hi[k] + hi[k + 1] for k in range(0, len(hi), 2)]
                    plsc.addupdate(out_v.at[pl.ds(c * lanes, lanes)], lo[0])
                    plsc.addupdate(out_v.at[pl.ds(words + c * lanes, lanes)], hi[0])

        def gather(half_index, rows, sem):
            return pltpu.make_async_copy(t_hbm.at[idx_v.at[half_index]], rows, sem)

        def token(t, out_v, sem_o, out_in_flight):
            gather(2 * t + 1, rows_b, sem_b).start()
            gather(2 * t, rows_a, sem_a).wait()

            @pl.when(out_in_flight)
            def _():
                pltpu.make_async_copy(out_v, o_hbm.at[base + t], sem_o).wait()

            accumulate(rows_a, out_v, t, 0, True)

            @pl.when(t + 1 < tpw)
            def _():
                gather(2 * t + 2, rows_a, sem_a).start()

            gather(2 * t + 1, rows_b, sem_b).wait()
            accumulate(rows_b, out_v, t, 1, False)
            pltpu.make_async_copy(out_v, o_hbm.at[base + t], sem_o).start()

        gather(0, rows_a, sem_a).start()

        @pl.loop(0, tpw // 2)
        def _(i):
            token(2 * i, out_a, sem_oa, i > 0)
            token(2 * i + 1, out_b, sem_ob, i > 0)

        pltpu.make_async_copy(out_a, o_hbm.at[base], sem_oa).wait()
        pltpu.make_async_copy(out_b, o_hbm.at[base], sem_ob).wait()

    return mix(table, idx2, w)


def _final_kernel(hp_ref, peer_ref, gf_ref, yin_ref, y_ref, done_ref):
    del yin_ref
    done_ref[...] = jnp.zeros(done_ref.shape, f32)
    h = hp_ref[...] + peer_ref[...]
    y_ref[...] = h * lax.rsqrt(jnp.mean(h * h, axis=-1, keepdims=True) + EPS) * gf_ref[...]


def _final(hp, peer, gf, y_all, y_block0, tm):
    t, d = hp.shape
    return pl.pallas_call(
        _final_kernel,
        grid=(t // tm,),
        in_specs=[
            pl.BlockSpec((tm, d), lambda i: (i, 0)),
            pl.BlockSpec((tm, d), lambda i: (i, 0)),
            pl.BlockSpec((1, d), lambda i: (0, 0)),
            pl.BlockSpec(memory_space=pl.ANY),
        ],
        out_specs=[pl.BlockSpec((tm, d), lambda i: (y_block0 + i, 0)),
                   pl.BlockSpec((8, LANES), lambda i: (0, 0))],
        out_shape=[jax.ShapeDtypeStruct(y_all.shape, f32), jax.ShapeDtypeStruct((8, LANES), f32)],
        input_output_aliases={3: 0},
        compiler_params=_cparams("arbitrary"),
        name="final_norm",
    )(hp, peer, gf, y_all)


def _channel_mixer_front(oa, ob, x, x_block0, ob_block0, wo_bf, g2, wqt_bf, sk_bf, u_pk, tm):
    hp, xn, eidx, gate = _retrieve(oa, ob, x, wo_bf, g2, wqt_bf, sk_bf, tm, x_block0, ob_block0)
    return hp, eidx, gate, _selected_dots(u_pk, eidx, xn)


def _channel_mixer_back(front, after, v_pk, gf, tm, y_all, y_block0):
    hp, eidx, gate, act = front
    w = _gate_weights(act, gate, after, tm)
    peer = _weighted_rows(v_pk, eidx, w)
    return _final(hp, peer, gf, y_all, y_block0, tm)


def kernel(x_prompt, x_sample, cache_k, cache_v, state_hgrn, norm1, w_in, da_lq1, da_lk1, da_lq2, da_lk2,
           da_out_norm, hg_lb_logits, hg_out_norm, w_out, rel_bias, norm2, peer_w_q, peer_sub_keys,
           peer_u, peer_v, final_norm):
    batch, seq, d = x_prompt.shape
    dbatch, dseq, _ = x_sample.shape
    past = cache_k.shape[2]
    da_heads = cache_k.shape[3]
    hg_heads = state_hgrn.shape[2]
    depth = w_in.shape[0]
    assert depth == 1 and seq % ATT_BLOCK == 0 and seq % CHUNK == 0 and dseq % HG_SUB == 0 and dseq <= CHUNK
    assert past % CHUNK == 0 and (past + dseq - 1) // CHUNK == past // CHUNK

    l = 0
    lam_init = 0.8 - 0.6 * math.exp(-0.3 * l)
    lam = (jnp.exp(jnp.sum(da_lq1[l].astype(f32) * da_lk1[l].astype(f32)))
           - jnp.exp(jnp.sum(da_lq2[l].astype(f32) * da_lk2[l].astype(f32))) + lam_init).reshape(1)
    lb = jnp.cumsum(jax.nn.softmax(hg_lb_logits.astype(f32), axis=0), axis=0)[l].reshape(1, -1)
    da_gain = (jnp.tile(da_out_norm[l].astype(f32), 2) * (1.0 - lam_init)).reshape(1, LANES)
    hg_gain = hg_out_norm[l].astype(f32).reshape(1, HG_DIM)
    g1 = norm1[l].astype(f32).reshape(1, d)
    g2 = norm2[l].astype(f32).reshape(1, d)
    gf = final_norm.astype(f32).reshape(1, d)
    w_in_bf = w_in[l].astype(bf16)
    wkv_t_bf = w_in[l][:, cache_k.shape[3] * cache_k.shape[4]:3 * cache_k.shape[3] * cache_k.shape[4]].T.astype(bf16)
    wo_bf = w_out[l].astype(bf16)
    wqt_bf = peer_w_q[l].T.astype(bf16)
    sk_bf = peer_sub_keys[l].reshape(-1, PEER_KEYS, PEER_SUB_DIM).astype(bf16)
    u_pk = _pack_table(peer_u[l])
    v_pk = _pack_table(peer_v[l])

    bias_tiles = _prompt_bias_tiles(rel_bias)
    q_pos = past + jnp.arange(dseq)
    bias_s = _bias_of(jnp.arange(past + dseq)[None, :] - q_pos[:, None], rel_bias) * LOG2E
    bias_past, bias_new = bias_s[:, :, :past], bias_s[:, :, past:]
    zero_state = jnp.zeros((1, hg_heads, HG_DIM, HG_DIM), f32)

    dh2 = 2 * DA_HEAD_DIM
    wq = da_heads * dh2
    tm = 256
    x_all = x_prompt.reshape(batch * seq, d)
    kt_all = jnp.zeros((batch * wq, seq), f32)
    vt_all = jnp.zeros((batch * wq, seq), f32)
    y_all = jnp.zeros((batch * seq, d), f32)
    nq = seq // ATT_BLOCK
    blocks = seq // tm
    segments = [(0, 0, nq // 2), (0, nq // 2, nq - nq // 2)] + [(b, 0, nq) for b in range(1, batch)]
    s_p = []
    done = [jnp.zeros((8, LANES), f32)] * 3
    fronts = []
    for s, (b, q0, n_q) in enumerate(segments):
        if q0 == 0:
            q, kt, vt, hz, kt_all, vt_all = _inproj_prompt(x_all, g1, w_in_bf, wkv_t_bf, kt_all, vt_all, done[s],
                                                           b, seq, tm)
            ob, s_new = _hgrn(hz, lb, hg_gain, zero_state, 1, seq, CHUNK, 4)
            s_p.append(s_new)
        oa = _attn_prompt(q, kt, vt, bias_tiles, da_gain, lam, q0, n_q)
        row0 = q0 * ATT_BLOCK // tm
        fronts.append((_channel_mixer_front(oa, ob, x_all, b * blocks + row0, row0, wo_bf, g2, wqt_bf, sk_bf, u_pk,
                                            tm), b * blocks + row0))
        if s >= 1:
            y_all, token = _channel_mixer_back(fronts[s - 1][0], oa, v_pk, gf, tm, y_all, fronts[s - 1][1])
            done.append(token)
    y_all, _ = _channel_mixer_back(fronts[-1][0], fronts[-1][0][0], v_pk, gf, tm, y_all, fronts[-1][1])

    xs = x_sample.reshape(dbatch * dseq, d)
    q, k_s, v_s, hz = _inproj_sample(xs, g1, w_in_bf, wq)
    kp_t = jnp.transpose(cache_k[l], (0, 2, 3, 1)).reshape(dbatch * wq, past)
    vp_t = jnp.transpose(cache_v[l], (0, 2, 3, 1)).reshape(dbatch * wq, past)
    oa = _attn_sample(q, k_s, v_s, kp_t, vp_t, bias_past, bias_new, da_gain, lam, dbatch, dseq, past)
    ob, s_s = _hgrn(hz, lb, hg_gain, state_hgrn[l].astype(f32), dbatch, dseq, dseq, 1)
    front = _channel_mixer_front(oa, ob, xs, 0, 0, wo_bf, g2, wqt_bf, sk_bf, u_pk, dbatch * dseq)
    y_s, _ = _channel_mixer_back(front, front[0], v_pk, gf, dbatch * dseq, jnp.zeros((dbatch * dseq, d), f32), 0)

    y_prompt = y_all.reshape(batch, seq, d)
    y_sample = y_s.reshape(dbatch, dseq, d)
    k_prompt = jnp.transpose(kt_all.reshape(batch, da_heads, dh2, seq), (0, 3, 1, 2))[None]
    v_prompt = jnp.transpose(vt_all.reshape(batch, da_heads, DA_V_DIM, seq), (0, 3, 1, 2))[None]
    state_prompt = jnp.concatenate(s_p, axis=0)[None].astype(state_hgrn.dtype)
    k_sample = k_s.reshape(1, dbatch, dseq, da_heads, dh2)
    v_sample = v_s.reshape(1, dbatch, dseq, da_heads, DA_V_DIM)
    state_sample = s_s[None].astype(state_hgrn.dtype)
    return (y_prompt, y_sample, k_prompt, v_prompt, state_prompt, k_sample, v_sample, state_sample)
```

```python
import functools
import math

import jax
import jax.numpy as jnp
from jax import lax
from jax.experimental import pallas as pl
from jax.experimental.pallas import tpu as pltpu
from jax.experimental.pallas import tpu_sc as plsc

CHUNK = 64
DA_HEAD_DIM = 32
DA_V_DIM = 2 * DA_HEAD_DIM
HG_DIM = 128
REL_BUCKETS = 32
REL_MAX_DIST = 128
PEER_KEYS = 128
PEER_TOPK = 16
PEER_SUB_DIM = 64
EPS = 1e-6
NEG = -1e30
LOG2E = 1.4426950408889634

LANES = 128
ATT_BLOCK = 128
ATT_STEP = 512
HG_SUB = 16
V7X_VMEM_BYTES = 64 * 1024 * 1024
VMEM_LIMIT = V7X_VMEM_BYTES * 7 // 8

f32 = jnp.float32
bf16 = jnp.bfloat16


def _cparams(*sem):
    return pltpu.CompilerParams(dimension_semantics=sem, vmem_limit_bytes=VMEM_LIMIT)


def _inproj_prompt_kernel(x_ref, g_ref, w_ref, wkv_t_ref, kin_ref, vin_ref, after_ref,
                          q_ref, kt_ref, vt_ref, hz_ref, ktf_ref, vtf_ref):
    del kin_ref, vin_ref
    del after_ref
    x = x_ref[...]
    xn = x * lax.rsqrt(jnp.mean(x * x, axis=-1, keepdims=True) + EPS) * g_ref[...]
    xb = xn.astype(bf16)
    wq = q_ref.shape[1]
    nt = (((1,), (1,)), ((), ()))
    q_ref[...] = jnp.dot(xb, w_ref[:, 0:wq], preferred_element_type=f32).astype(bf16)
    kt = lax.dot_general(wkv_t_ref[0:wq, :], xb, nt, preferred_element_type=f32)
    ktf_ref[...] = kt
    kt_ref[...] = kt.astype(bf16)
    vt = lax.dot_general(wkv_t_ref[wq:2 * wq, :], xb, nt, preferred_element_type=f32)
    vtf_ref[...] = vt
    vt_ref[...] = vt.astype(bf16)
    hz_ref[...] = jnp.dot(xb, w_ref[:, 3 * wq:], preferred_element_type=f32)


def _inproj_prompt(x, g, w_bf, wkv_t_bf, kt_all, vt_all, after, b, seq, tm):
    d = x.shape[1]
    e = w_bf.shape[1]
    wq = wkv_t_bf.shape[0] // 2
    steps = seq // tm
    return pl.pallas_call(
        _inproj_prompt_kernel,
        grid=(steps,),
        in_specs=[
            pl.BlockSpec((tm, d), lambda i: (b * steps + i, 0)),
            pl.BlockSpec((1, d), lambda i: (0, 0)),
            pl.BlockSpec((d, e), lambda i: (0, 0)),
            pl.BlockSpec((2 * wq, d), lambda i: (0, 0)),
            pl.BlockSpec(memory_space=pl.ANY),
            pl.BlockSpec(memory_space=pl.ANY),
            pl.BlockSpec(memory_space=pl.ANY),
        ],
        out_specs=[
            pl.BlockSpec((tm, wq), lambda i: (i, 0)),
            pl.BlockSpec((wq, tm), lambda i: (0, i)),
            pl.BlockSpec((wq, tm), lambda i: (0, i)),
            pl.BlockSpec((tm, e - 3 * wq), lambda i: (i, 0)),
            pl.BlockSpec((wq, tm), lambda i: (b, i)),
            pl.BlockSpec((wq, tm), lambda i: (b, i)),
        ],
        out_shape=[
            jax.ShapeDtypeStruct((seq, wq), bf16),
            jax.ShapeDtypeStruct((wq, seq), bf16),
            jax.ShapeDtypeStruct((wq, seq), bf16),
            jax.ShapeDtypeStruct((seq, e - 3 * wq), f32),
            jax.ShapeDtypeStruct(kt_all.shape, f32),
            jax.ShapeDtypeStruct(vt_all.shape, f32),
        ],
        input_output_aliases={4: 4, 5: 5},
        compiler_params=_cparams("parallel"),
        name="inproj",
    )(x, g, w_bf, wkv_t_bf, kt_all, vt_all, after)


def _inproj_sample_kernel(x_ref, g_ref, w_ref, q_ref, k_ref, v_ref, hz_ref):
    x = x_ref[...]
    xn = x * lax.rsqrt(jnp.mean(x * x, axis=-1, keepdims=True) + EPS) * g_ref[...]
    xb = xn.astype(bf16)
    wq = q_ref.shape[1]
    q_ref[...] = jnp.dot(xb, w_ref[:, 0:wq], preferred_element_type=f32).astype(bf16)
    k_ref[...] = jnp.dot(xb, w_ref[:, wq:2 * wq], preferred_element_type=f32)
    v_ref[...] = jnp.dot(xb, w_ref[:, 2 * wq:3 * wq], preferred_element_type=f32)
    hz_ref[...] = jnp.dot(xb, w_ref[:, 3 * wq:], preferred_element_type=f32)


def _inproj_sample(x, g, w_bf, wq):
    t, d = x.shape
    e = w_bf.shape[1]
    return pl.pallas_call(
        _inproj_sample_kernel,
        grid=(1,),
        in_specs=[
            pl.BlockSpec((t, d), lambda i: (0, 0)),
            pl.BlockSpec((1, d), lambda i: (0, 0)),
            pl.BlockSpec((d, e), lambda i: (0, 0)),
        ],
        out_specs=[
            pl.BlockSpec((t, wq), lambda i: (0, 0)),
            pl.BlockSpec((t, wq), lambda i: (0, 0)),
            pl.BlockSpec((t, wq), lambda i: (0, 0)),
            pl.BlockSpec((t, e - 3 * wq), lambda i: (0, 0)),
        ],
        out_shape=[
            jax.ShapeDtypeStruct((t, wq), bf16),
            jax.ShapeDtypeStruct((t, wq), f32),
            jax.ShapeDtypeStruct((t, wq), f32),
            jax.ShapeDtypeStruct((t, e - 3 * wq), f32),
        ],
        compiler_params=_cparams("arbitrary"),
        name="inproj_sample",
    )(x, g, w_bf)


def _rel_bucket(rel):
    nb = REL_BUCKETS // 2
    max_exact = nb // 2
    ret = jnp.where(rel > 0, nb, 0)
    n = jnp.abs(rel)
    nf = jnp.maximum(n, 1).astype(f32)
    large = max_exact + (jnp.log(nf / max_exact) / math.log(REL_MAX_DIST / max_exact)
                         * (nb - max_exact)).astype(jnp.int32)
    large = jnp.minimum(large, nb - 1)
    return ret + jnp.where(n < max_exact, n, large)


def _bias_of(rel, rel_bias):
    bucket = _rel_bucket(rel)
    out = jnp.zeros((rel_bias.shape[1],) + rel.shape, f32)
    for b in range(REL_BUCKETS):
        out = out + jnp.where(bucket[None] == b, rel_bias[b].astype(f32)[:, None, None], 0.0)
    return out


def _attn_prompt_kernel(lam_ref, q_ref, kt_ref, vt_ref, bias_ref, gain_ref, o_ref, m_sc, acc_sc, sa_sc, sb_sc,
                        *, q_block0):
    qb = pl.program_id(1) + q_block0
    tq = q_ref.shape[0]
    seq = kt_ref.shape[1]
    per_step = ATT_STEP // ATT_BLOCK
    lane = lax.broadcasted_iota(jnp.int32, (tq, LANES), 1)
    feat_k = lax.broadcasted_iota(jnp.int32, (LANES, ATT_STEP), 0)
    qs = q_ref[...].astype(f32) * (DA_HEAD_DIM ** -0.5 * LOG2E)
    q4 = jnp.concatenate(
        [jnp.where((lane // DA_HEAD_DIM) == i, qs, 0.0).astype(bf16) for i in range(4)], axis=0)

    m_sc[...] = jnp.full(m_sc.shape, -jnp.inf, f32)
    acc_sc[...] = jnp.zeros(acc_sc.shape, f32)

    n_steps = qb // per_step + 1
    last = n_steps - 1
    r = qb % per_step

    def scores(k):
        start = pl.multiple_of(jnp.minimum(k * ATT_STEP, seq - ATT_STEP), ATT_STEP)
        return jnp.dot(q4, kt_ref[:, pl.ds(start, ATT_STEP)], preferred_element_type=f32)

    def half(k, cur_ref, nxt_ref):
        nxt_ref[...] = scores(k + 1)
        tile = jnp.where(k == last, 1 + r, jnp.where((k == last - 1) & (r == 0), per_step + 1, 0))
        vv = vt_ref[:, pl.ds(pl.multiple_of(k * ATT_STEP, ATT_STEP), ATT_STEP)]
        for hh in range(2):
            bias = bias_ref[0, tile, hh * tq:(hh + 1) * tq, :]
            ps, alphas = [], []
            for c in range(2):
                rows = slice((2 * hh + c) * tq, (2 * hh + c + 1) * tq)
                s = cur_ref[rows, :] + bias
                m_old = m_sc[rows]
                m_new = jnp.maximum(m_old, jnp.max(s, axis=-1, keepdims=True))
                ps.append(jnp.exp2(s - m_new[:, 0:1]).astype(bf16))
                alphas.append(jnp.exp2(m_old - m_new))
                m_sc[rows] = m_new
            rows2 = slice(2 * hh * tq, (2 * hh + 2) * tq)
            vaug = jnp.where((feat_k // DA_V_DIM) == hh, vv, jnp.ones_like(vv))
            acc_sc[rows2] = (acc_sc[rows2] * jnp.concatenate(alphas, axis=0)
                             + lax.dot_general(jnp.concatenate(ps, axis=0), vaug, (((1,), (1,)), ((), ())),
                                               preferred_element_type=f32))

    sa_sc[...] = scores(0)

    def two_steps(kk, carry):
        half(2 * kk, sa_sc, sb_sc)

        @pl.when(2 * kk + 1 < n_steps)
        def _():
            half(2 * kk + 1, sb_sc, sa_sc)

        return carry

    lax.fori_loop(0, (n_steps + 1) // 2, two_steps, 0)

    lam = lam_ref[0]
    outs = []
    for hh in range(2):
        own = (lane // DA_V_DIM) == hh
        a1 = acc_sc[2 * hh * tq:(2 * hh + 1) * tq]
        a2 = acc_sc[(2 * hh + 1) * tq:(2 * hh + 2) * tq]
        l1 = jnp.max(jnp.where(own, 0.0, a1), axis=-1, keepdims=True)
        l2 = jnp.max(jnp.where(own, 0.0, a2), axis=-1, keepdims=True)
        o = a1 / l1 - lam * (a2 / l2)
        ssq = jnp.sum(jnp.where(own, o * o, 0.0), axis=-1, keepdims=True)
        outs.append(o * lax.rsqrt(ssq * (1.0 / DA_V_DIM) + EPS))
    o = jnp.where((lane // DA_V_DIM) == 0, outs[0], outs[1])
    o_ref[...] = (o * gain_ref[...]).astype(o_ref.dtype)


def _attn_prompt(q, kt, vt, bias_tiles, gain, lam, q_block0, n_q):
    seq, w = q.shape
    pairs = w // LANES
    assert seq % ATT_STEP == 0
    n_tiles = ATT_STEP // ATT_BLOCK + 2
    return pl.pallas_call(
        functools.partial(_attn_prompt_kernel, q_block0=q_block0),
        grid=(pairs, n_q),
        in_specs=[
            pl.BlockSpec(memory_space=pltpu.SMEM),
            pl.BlockSpec((ATT_BLOCK, LANES), lambda p, i: (q_block0 + i, p)),
            pl.BlockSpec((LANES, seq), lambda p, i: (p, 0)),
            pl.BlockSpec((LANES, seq), lambda p, i: (p, 0)),
            pl.BlockSpec((1, n_tiles, 2 * ATT_BLOCK, ATT_STEP), lambda p, i: (p, 0, 0, 0)),
            pl.BlockSpec((1, LANES), lambda p, i: (0, 0)),
        ],
        out_specs=pl.BlockSpec((ATT_BLOCK, LANES), lambda p, i: (i, p)),
        out_shape=jax.ShapeDtypeStruct((n_q * ATT_BLOCK, w), bf16),
        scratch_shapes=[
            pltpu.VMEM((4 * ATT_BLOCK, LANES), f32),
            pltpu.VMEM((4 * ATT_BLOCK, LANES), f32),
            pltpu.VMEM((4 * ATT_BLOCK, ATT_STEP), f32),
            pltpu.VMEM((4 * ATT_BLOCK, ATT_STEP), f32),
        ],
        compiler_params=_cparams("parallel", "arbitrary"),
        name="attn_prompt",
    )(lam, q, kt, vt, bias_tiles, gain)


def _prompt_bias_tiles(rel_bias):
    h = rel_bias.shape[1]
    per_step = ATT_STEP // ATT_BLOCK
    i = jnp.arange(ATT_BLOCK)
    rel_diag = i[None, :] - i[:, None]
    rel_prev = rel_diag - ATT_BLOCK
    far = rel_bias[REL_BUCKETS // 2 - 1].astype(f32)
    b_diag = (_bias_of(rel_diag, rel_bias) - far[:, None, None]) * LOG2E
    b_prev = (_bias_of(rel_prev, rel_bias) - far[:, None, None]) * LOG2E
    mask = (i[None, :] // CHUNK) <= (i[:, None] // CHUNK)
    b_diag = jnp.where(mask[None], b_diag, NEG)
    zero = jnp.zeros_like(b_diag)
    dead = jnp.full_like(b_diag, NEG)
    tiles = [jnp.concatenate([zero] * per_step, axis=-1)]
    for r in range(per_step):
        blocks = [zero if j < r - 1 else b_prev if j == r - 1 else b_diag if j == r else dead
                  for j in range(per_step)]
        tiles.append(jnp.concatenate(blocks, axis=-1))
    tiles.append(jnp.concatenate([zero] * (per_step - 1) + [b_prev], axis=-1))
    tiles = jnp.stack(tiles, axis=1)
    tiles = tiles.reshape(h // 2, 2, per_step + 2, ATT_BLOCK, ATT_STEP)
    return jnp.transpose(tiles, (0, 2, 1, 3, 4)).reshape(h // 2, per_step + 2, 2 * ATT_BLOCK, ATT_STEP)


def _attn_sample_kernel(lam_ref, q_ref, kp_ref, vp_ref, kn_ref, vn_ref, bp_ref, bn_ref, gain_ref, o_ref):
    nq = q_ref.shape[0]
    pairs = q_ref.shape[1] // LANES
    lane = lax.broadcasted_iota(jnp.int32, (nq, LANES), 1)
    lam = lam_ref[0]
    nt = (((1,), (1,)), ((), ()))
    for p in range(pairs):
        sl = slice(p * LANES, (p + 1) * LANES)
        qs = q_ref[:, sl].astype(f32) * (DA_HEAD_DIM ** -0.5 * LOG2E)
        kp = kp_ref[sl, :].astype(bf16)
        vp = vp_ref[sl, :].astype(bf16)
        kn = kn_ref[:, sl].astype(bf16)
        vn = vn_ref[:, sl].astype(bf16)
        outs = []
        for hh in range(2):
            h = 2 * p + hh
            own = (lane // DA_V_DIM) == hh
            res = []
            for c in range(2):
                qm = jnp.where((lane // DA_HEAD_DIM) == 2 * hh + c, qs, 0.0).astype(bf16)
                sp = jnp.dot(qm, kp, preferred_element_type=f32) + bp_ref[h]
                sn = lax.dot_general(qm, kn, nt, preferred_element_type=f32) + bn_ref[h]
                m = jnp.maximum(jnp.max(sp, axis=-1, keepdims=True), jnp.max(sn, axis=-1, keepdims=True))
                pp = jnp.exp2(sp - m)
                pn = jnp.exp2(sn - m)
                l = jnp.sum(pp, axis=-1, keepdims=True) + jnp.sum(pn, axis=-1, keepdims=True)
                pv = (lax.dot_general(pp.astype(bf16), vp, nt, preferred_element_type=f32)
                      + jnp.dot(pn.astype(bf16), vn, preferred_element_type=f32))
                res.append(pv / l)
            o = res[0] - lam * res[1]
            ssq = jnp.sum(jnp.where(own, o * o, 0.0), axis=-1, keepdims=True)
            outs.append(o * lax.rsqrt(ssq * (1.0 / DA_V_DIM) + EPS))
        o = jnp.where((lane // DA_V_DIM) == 0, outs[0], outs[1])
        o_ref[:, sl] = (o * gain_ref[...]).astype(o_ref.dtype)


def _attn_sample(q, k_new, v_new, k_past, v_past, bias_past, bias_new, gain, lam, batch, nq, past):
    w = q.shape[1]
    h = bias_past.shape[0]
    return pl.pallas_call(
        _attn_sample_kernel,
        grid=(batch,),
        in_specs=[
            pl.BlockSpec(memory_space=pltpu.SMEM),
            pl.BlockSpec((nq, w), lambda b: (b, 0)),
            pl.BlockSpec((w, past), lambda b: (b, 0)),
            pl.BlockSpec((w, past), lambda b: (b, 0)),
            pl.BlockSpec((nq, w), lambda b: (b, 0)),
            pl.BlockSpec((nq, w), lambda b: (b, 0)),
            pl.BlockSpec((h, nq, past), lambda b: (0, 0, 0)),
            pl.BlockSpec((h, nq, nq), lambda b: (0, 0, 0)),
            pl.BlockSpec((1, LANES), lambda b: (0, 0)),
        ],
        out_specs=pl.BlockSpec((nq, w), lambda b: (b, 0)),
        out_shape=jax.ShapeDtypeStruct((batch * nq, w), bf16),
        compiler_params=_cparams("parallel"),
        name="attn_sample",
    )(lam, q, k_past, v_past, k_new, v_new, bias_past, bias_new, gain)


def _hgrn_chunk(q, f_logit, vv, g_logit, lb, gain, s0, lc):
    n_sub = lc // HG_SUB
    f = lb + (1.0 - lb) * jax.nn.sigmoid(f_logit)
    logf = jnp.log(f)
    k = 1.0 - f
    row = lax.broadcasted_iota(jnp.int32, (lc, lc), 0)
    col = lax.broadcasted_iota(jnp.int32, (lc, lc), 1)
    tril = (col <= row).astype(f32)
    b = jnp.dot(tril, logf, precision=lax.Precision.HIGHEST, preferred_element_type=f32)
    b_last = b[lc - 1:lc, :]

    o = jnp.dot((q * jnp.exp(b)).astype(bf16), s0.astype(bf16), preferred_element_type=f32)
    k_hat = k * jnp.exp(b_last - b)
    bt = b.T
    decay_col = jnp.exp(bt[:, lc - 1:lc])
    s_new = decay_col * s0 + jnp.dot(k_hat.T.astype(bf16), vv.astype(bf16), preferred_element_type=f32)

    if n_sub > 1:
        rows = []
        nt = (((1,), (1,)), ((), ()))
        for i in range(n_sub):
            lo = i * HG_SUB
            if i == 0:
                rows.append(jnp.zeros((HG_SUB, lc), f32))
                continue
            beta = b[lo - 1:lo, :]
            q_t = q[lo:lo + HG_SUB, :] * jnp.exp(b[lo:lo + HG_SUB, :] - beta)
            k_t = k * jnp.exp(jnp.minimum(beta - b, 0.0))
            rows.append(lax.dot_general(q_t.astype(bf16), k_t.astype(bf16), nt, preferred_element_type=f32))
        a_off = jnp.concatenate(rows, axis=0)
        a_off = jnp.where(col < (row // HG_SUB) * HG_SUB, a_off, 0.0)
        o = o + jnp.dot(a_off.astype(bf16), vv.astype(bf16), preferred_element_type=f32)

    ones = jnp.ones((HG_DIM, LANES), bf16)
    srow = lax.broadcasted_iota(jnp.int32, (HG_SUB, HG_DIM), 0)
    orow = lax.broadcasted_iota(jnp.int32, (HG_SUB, HG_DIM), 0)
    blocks = []
    for i in range(n_sub):
        lo = i * HG_SUB
        b_i = b[lo:lo + HG_SUB, :]
        k_i = k[lo:lo + HG_SUB, :]
        v_i = vv[lo:lo + HG_SUB, :]
        q_i = q[lo:lo + HG_SUB, :]
        d_rows = []
        for t in range(HG_SUB):
            arg = jnp.where(srow <= t, b_i[t:t + 1, :] - b_i, -jnp.inf)
            d_rows.append(q_i[t:t + 1, :] * k_i * jnp.exp(arg))
        d3 = jnp.concatenate(d_rows, axis=0)
        a_rep = jnp.dot(d3.astype(bf16), ones, preferred_element_type=f32)
        o_i = jnp.zeros((HG_SUB, HG_DIM), f32)
        for t in range(HG_SUB):
            o_t = jnp.sum(a_rep[t * HG_SUB:(t + 1) * HG_SUB, :] * v_i, axis=0, keepdims=True)
            o_i = jnp.where(orow == t, o_t, o_i)
        blocks.append(o_i)
    o = o + (jnp.concatenate(blocks, axis=0) if n_sub > 1 else blocks[0])

    on = o * lax.rsqrt(jnp.mean(o * o, axis=-1, keepdims=True) + EPS) * gain
    gate = g_logit * jax.nn.sigmoid(g_logit)
    return on * gate, s_new


def _hgrn_kernel(hz_ref, lb_ref, gain_ref, s0_ref, o_ref, s_ref, st_sc, *, lc, heads):
    j = pl.program_id(1)
    w = heads * HG_DIM

    @pl.when(j == 0)
    def _():
        st_sc[...] = s0_ref[0]

    n_chunks = hz_ref.shape[0] // lc
    for c in range(n_chunks):
        rs = slice(c * lc, (c + 1) * lc)
        for h in range(heads):
            cs = slice(h * HG_DIM, (h + 1) * HG_DIM)
            o, s_new = _hgrn_chunk(
                hz_ref[rs, h * HG_DIM:(h + 1) * HG_DIM],
                hz_ref[rs, w + h * HG_DIM:w + (h + 1) * HG_DIM],
                hz_ref[rs, 2 * w + h * HG_DIM:2 * w + (h + 1) * HG_DIM],
                hz_ref[rs, 3 * w + h * HG_DIM:3 * w + (h + 1) * HG_DIM],
                lb_ref[:, cs], gain_ref[...], st_sc[h], lc)
            st_sc[h] = s_new
            o_ref[rs, cs] = o.astype(o_ref.dtype)

    @pl.when(j == pl.num_programs(1) - 1)
    def _():
        s_ref[0] = st_sc[...]


def _hgrn(hz, lb, gain, s0, batch, seq, lc, chunks_per_step):
    t = hz.shape[0]
    w = hz.shape[1] // 4
    heads = w // HG_DIM
    rows = lc * chunks_per_step
    steps = seq // rows
    return pl.pallas_call(
        functools.partial(_hgrn_kernel, lc=lc, heads=heads),
        grid=(batch, steps),
        in_specs=[
            pl.BlockSpec((rows, 4 * w), lambda b, j: (b * steps + j, 0)),
            pl.BlockSpec((1, w), lambda b, j: (0, 0)),
            pl.BlockSpec((1, HG_DIM), lambda b, j: (0, 0)),
            pl.BlockSpec((1, heads, HG_DIM, HG_DIM), lambda b, j: (b, 0, 0, 0)),
        ],
        out_specs=[
            pl.BlockSpec((rows, w), lambda b, j: (b * steps + j, 0)),
            pl.BlockSpec((1, heads, HG_DIM, HG_DIM), lambda b, j: (b, 0, 0, 0)),
        ],
        out_shape=[
            jax.ShapeDtypeStruct((t, w), bf16),
            jax.ShapeDtypeStruct((batch, heads, HG_DIM, HG_DIM), f32),
        ],
        scratch_shapes=[pltpu.VMEM((heads, HG_DIM, HG_DIM), f32)],
        compiler_params=_cparams("parallel", "arbitrary"),
        name="hgrn2",
    )(hz, lb, gain, s0)


def _top_rows(vals, k, payload=None):
    n_rows = vals.shape[0]
    row = lax.broadcasted_iota(jnp.int32, vals.shape, 0)
    out_v, out_i = [], []
    for _ in range(k):
        m = jnp.max(vals, axis=0, keepdims=True)
        first = jnp.min(jnp.where(vals == m, row, n_rows), axis=0, keepdims=True)
        hit = row == first
        out_v.append(m)
        if payload is None:
            out_i.append(first)
        else:
            out_i.append(jnp.max(jnp.where(hit, payload, -1), axis=0, keepdims=True))
        vals = jnp.where(hit, -jnp.inf, vals)
    return jnp.concatenate(out_v, axis=0), jnp.concatenate(out_i, axis=0)


def _retrieve_kernel(oa_ref, ob_ref, x_ref, wo_ref, g_ref, wqt_ref, sk_ref,
                     hp_ref, xn_ref, idx_ref, gate_ref, qt_sc, s1v_sc, s1i_sc, idx_sc, gate_sc):
    half = oa_ref.shape[1]
    y = (jnp.dot(oa_ref[...], wo_ref[0:half, :], preferred_element_type=f32)
         + jnp.dot(ob_ref[...], wo_ref[half:, :], preferred_element_type=f32))
    hp = x_ref[...] + y
    hp_ref[...] = hp
    xn = hp * lax.rsqrt(jnp.mean(hp * hp, axis=-1, keepdims=True) + EPS) * g_ref[...]
    xn_ref[...] = xn
    nt = (((1,), (1,)), ((), ()))
    qt_sc[...] = lax.dot_general(wqt_ref[...], xn.astype(bf16), nt, preferred_element_type=f32).astype(bf16)

    n_half = sk_ref.shape[0]

    def stage1(hc, carry):
        start = pl.multiple_of(hc * PEER_SUB_DIM, PEER_SUB_DIM)
        s = jnp.dot(sk_ref[hc], qt_sc[pl.ds(start, PEER_SUB_DIM), :], preferred_element_type=f32)
        v, i = _top_rows(s, PEER_TOPK)
        s1v_sc[hc] = v
        s1i_sc[hc] = i
        return carry

    lax.fori_loop(0, n_half, stage1, 0)

    tm = x_ref.shape[0]
    sub = lax.broadcasted_iota(jnp.int32, (8, tm), 0)

    def stage2(h, carry):
        v1, i1 = s1v_sc[2 * h], s1i_sc[2 * h]
        v2, i2 = s1v_sc[2 * h + 1], s1i_sc[2 * h + 1]
        cv, ci = [], []
        for i in range(PEER_TOPK):
            nj = PEER_TOPK // (i + 1)
            rows = PEER_TOPK if nj > 8 else 8
            a = v1[i:i + 1, :] + v2[0:rows, :]
            e = i1[i:i + 1, :] * PEER_KEYS + i2[0:rows, :]
            if nj < rows:
                a = jnp.where(sub < nj, a, -jnp.inf)
            cv.append(a)
            ci.append(e)
        top_s, eidx = _top_rows(jnp.concatenate(cv, axis=0), PEER_TOPK, jnp.concatenate(ci, axis=0))
        p = jnp.exp(top_s - top_s[0:1, :])
        g = p / jnp.sum(p, axis=0, keepdims=True)
        r0 = pl.multiple_of(h * PEER_TOPK, PEER_TOPK)
        gate_sc[pl.ds(r0, PEER_TOPK), :] = g
        idx_sc[pl.ds(r0, PEER_TOPK), :] = eidx
        return carry

    lax.fori_loop(0, n_half // 2, stage2, 0)
    idx_ref[...] = pltpu.bitcast(pltpu.bitcast(idx_sc[...], f32).T, jnp.int32)
    gate_ref[...] = gate_sc[...].T


def _retrieve(oa, ob, x, wo_bf, g2, wqt_bf, sk_bf, tm, x_block0=0, ob_block0=0):
    t, half = oa.shape
    d = x.shape[1]
    e = wqt_bf.shape[0]
    n_half = sk_bf.shape[0]
    n_sel = (n_half // 2) * PEER_TOPK
    return pl.pallas_call(
        _retrieve_kernel,
        grid=(t // tm,),
        in_specs=[
            pl.BlockSpec((tm, half), lambda i: (i, 0)),
            pl.BlockSpec((tm, half), lambda i: (ob_block0 + i, 0)),
            pl.BlockSpec((tm, d), lambda i: (x_block0 + i, 0)),
            pl.BlockSpec((2 * half, d), lambda i: (0, 0)),
            pl.BlockSpec((1, d), lambda i: (0, 0)),
            pl.BlockSpec((e, d), lambda i: (0, 0)),
            pl.BlockSpec((n_half, PEER_KEYS, PEER_SUB_DIM), lambda i: (0, 0, 0)),
        ],
        out_specs=[
            pl.BlockSpec((tm, d), lambda i: (i, 0)),
            pl.BlockSpec((tm, d), lambda i: (i, 0)),
            pl.BlockSpec((tm, n_sel), lambda i: (i, 0)),
            pl.BlockSpec((tm, n_sel), lambda i: (i, 0)),
        ],
        out_shape=[
            jax.ShapeDtypeStruct((t, d), f32),
            jax.ShapeDtypeStruct((t, d), f32),
            jax.ShapeDtypeStruct((t, n_sel), jnp.int32),
            jax.ShapeDtypeStruct((t, n_sel), f32),
        ],
        scratch_shapes=[
            pltpu.VMEM((e, tm), bf16),
            pltpu.VMEM((n_half, PEER_TOPK, tm), f32),
            pltpu.VMEM((n_half, PEER_TOPK, tm), jnp.int32),
            pltpu.VMEM((n_sel, tm), jnp.int32),
            pltpu.VMEM((n_sel, tm), f32),
        ],
        compiler_params=_cparams("parallel"),
        name="retrieve",
    )(oa, ob, x, wo_bf, g2, wqt_bf, sk_bf)


def _pack_table(tab):
    half = tab.shape[1] // 2
    lo = lax.bitcast_convert_type(tab[:, :half].astype(bf16), jnp.uint16).astype(jnp.uint32)
    hi = lax.bitcast_convert_type(tab[:, half:].astype(bf16), jnp.uint16).astype(jnp.uint32)
    return lax.bitcast_convert_type(lo | (hi << 16), jnp.int32)


def _sc_layout(n_tok, n_sel, words):
    info = plsc.get_sparse_core_info()
    n_workers = info.num_cores * info.num_subcores
    lanes = info.num_lanes
    block = 8 * lanes
    assert n_tok % (2 * n_workers) == 0 and words % block == 0 and (n_sel // 2) % lanes == 0
    return info, lanes, n_tok // n_workers, n_sel // 2, block


def _selected_dots(table, idx, x):
    n_tok, n_sel = idx.shape
    words = table.shape[1]
    info, lanes, tpw, half_rows, block = _sc_layout(n_tok, n_sel, words)
    idx2 = idx.reshape(2 * n_tok, half_rows)
    mesh = plsc.VectorSubcoreMesh(core_axis_name="c", subcore_axis_name="s")

    @functools.partial(
        pl.kernel, out_type=jax.ShapeDtypeStruct((n_tok, n_sel), f32), mesh=mesh,
        compiler_params=pltpu.CompilerParams(needs_layout_passes=False),
        scratch_types=[
            pltpu.VMEM((2 * tpw, half_rows), jnp.int32),
            pltpu.VMEM((half_rows, words), jnp.int32),
            pltpu.VMEM((half_rows, words), jnp.int32),
            pltpu.VMEM((2 * words,), f32),
            pltpu.VMEM((2 * words,), f32),
            pltpu.VMEM((n_sel // 8, 8 * lanes), f32),
            pltpu.VMEM((tpw, n_sel), f32),
            pltpu.SemaphoreType.DMA, pltpu.SemaphoreType.DMA, pltpu.SemaphoreType.DMA, pltpu.SemaphoreType.DMA,
        ])
    def dots(t_hbm, i_hbm, x_hbm, o_hbm, idx_v, rows_a, rows_b, x_a, x_b, part_v, act_v, sem_a, sem_b, sem_xa, sem_xb):
        wid = lax.axis_index("s") * info.num_cores + lax.axis_index("c")
        base = wid * tpw
        pltpu.sync_copy(i_hbm.at[pl.ds(base * 2, 2 * tpw)], idx_v)

        def accumulate(rows, x_v, half):
            @pl.loop(0, words // block)
            def _(cb):
                xs = [x_v[pl.ds(p * words + cb * block + c * lanes, lanes)] for c in range(8) for p in range(2)]

                @plsc.parallel_loop(0, half_rows)
                def _(row):
                    packed = [rows[row, pl.ds(cb * block + c * lanes, lanes)] for c in range(8)]
                    unp = [plsc.unpack(plsc.bitcast(p, bf16), format=plsc.PackFormat.INTERLEAVED) for p in packed]
                    prods = [unp[c][p] * xs[2 * c + p] for c in range(8) for p in range(2)]
                    while len(prods) > 1:
                        prods = [prods[k] + prods[k + 1] for k in range(0, len(prods), 2)]
                    prow = half * half_rows + row
                    plsc.addupdate(part_v.at[prow // 8, pl.ds((prow % 8) * lanes, lanes)], prods[0])

        def gather(half_index, rows, sem):
            return pltpu.make_async_copy(t_hbm.at[idx_v.at[half_index]], rows, sem)

        def xcopy(t, x_v, sem):
            return pltpu.make_async_copy(x_hbm.at[base + t], x_v, sem)

        def token(t, x_v, sem_x, x_next, sem_xn):
            gather(2 * t + 1, rows_b, sem_b).start()

            @pl.when(t + 1 < tpw)
            def _():
                xcopy(t + 1, x_next, sem_xn).start()

            zero = jnp.zeros((lanes,), f32)
            for r in range(n_sel):
                part_v[r // 8, pl.ds((r % 8) * lanes, lanes)] = zero
            xcopy(t, x_v, sem_x).wait()
            gather(2 * t, rows_a, sem_a).wait()
            accumulate(rows_a, x_v, 0)

            @pl.when(t + 1 < tpw)
            def _():
                gather(2 * t + 2, rows_a, sem_a).start()

            gather(2 * t + 1, rows_b, sem_b).wait()
            accumulate(rows_b, x_v, 1)
            it = lax.iota(jnp.int32, lanes)
            for g in range(n_sel // lanes):
                prow = g * (lanes // 8) + it // 8
                pcol = (it % 8) * lanes
                tot = plsc.load_gather(part_v, [prow, pcol])
                for l in range(1, lanes):
                    tot = tot + plsc.load_gather(part_v, [prow, pcol + l])
                act_v[t, pl.ds(g * lanes, lanes)] = tot

        gather(0, rows_a, sem_a).start()
        xcopy(0, x_a, sem_xa).start()

        @pl.loop(0, tpw // 2)
        def _(i):
            token(2 * i, x_a, sem_xa, x_b, sem_xb)
            token(2 * i + 1, x_b, sem_xb, x_a, sem_xa)

        pltpu.sync_copy(act_v, o_hbm.at[pl.ds(base, tpw)])

    return dots(table, idx2, x)


def _gate_weights_kernel(act_ref, gate_ref, after_ref, w_ref):
    del after_ref
    a = act_ref[...]
    w_ref[...] = gate_ref[...] * (0.5 * a * (1.0 + lax.erf(a * (2.0 ** -0.5))))


def _gate_weights(act, gate, after, tm):
    t, n_sel = act.shape
    spec = pl.BlockSpec((tm, n_sel), lambda i: (i, 0))
    return pl.pallas_call(
        _gate_weights_kernel,
        grid=(t // tm,),
        in_specs=[spec, spec, pl.BlockSpec(memory_space=pl.ANY)],
        out_specs=spec,
        out_shape=jax.ShapeDtypeStruct((t, n_sel), f32),
        compiler_params=_cparams("parallel"),
        name="peer_weights",
    )(act, gate, after)


def _weighted_rows(table, idx, w):
    n_tok, n_sel = w.shape
    words = table.shape[1]
    info, lanes, tpw, half_rows, _ = _sc_layout(n_tok, n_sel, words)
    idx2 = idx.reshape(2 * n_tok, half_rows)
    mesh = plsc.VectorSubcoreMesh(core_axis_name="c", subcore_axis_name="s")

    @functools.partial(
        pl.kernel, out_type=jax.ShapeDtypeStruct((n_tok, 2 * words), f32), mesh=mesh,
        compiler_params=pltpu.CompilerParams(needs_layout_passes=False),
        scratch_types=[
            pltpu.VMEM((2 * tpw, half_rows), jnp.int32),
            pltpu.VMEM((tpw, n_sel), f32),
            pltpu.VMEM((half_rows, words), jnp.int32),
            pltpu.VMEM((half_rows, words), jnp.int32),
            pltpu.VMEM((2 * words,), f32),
            pltpu.VMEM((2 * words,), f32),
            pltpu.SemaphoreType.DMA, pltpu.SemaphoreType.DMA, pltpu.SemaphoreType.DMA, pltpu.SemaphoreType.DMA,
        ])
    def mix(t_hbm, i_hbm, w_hbm, o_hbm, idx_v, w_v, rows_a, rows_b, out_a, out_b, sem_a, sem_b, sem_oa, sem_ob):
        wid = lax.axis_index("s") * info.num_cores + lax.axis_index("c")
        base = wid * tpw
        pltpu.sync_copy(i_hbm.at[pl.ds(base * 2, 2 * tpw)], idx_v)
        pltpu.sync_copy(w_hbm.at[pl.ds(base, tpw)], w_v)

        def accumulate(rows, out_v, t, half, first):
            if first:
                zero = jnp.zeros((lanes,), f32)
                for c in range(2 * words // lanes):
                    out_v[pl.ds(c * lanes, lanes)] = zero

            @pl.loop(0, half_rows // lanes)
            def _(g):
                wv = w_v[t, pl.ds(half * half_rows + g * lanes, lanes)]
                ws = [wv.at[jnp.full((lanes,), r, jnp.int32)].get(mode="promise_in_bounds") for r in range(lanes)]

                @plsc.parallel_loop(0, words // lanes)
                def _(c):
                    packed = [rows[g * lanes + r, pl.ds(c * lanes, lanes)] for r in range(lanes)]
                    unp = [plsc.unpack(plsc.bitcast(p, bf16), format=plsc.PackFormat.INTERLEAVED) for p in packed]
                    lo = [unp[r][0] * ws[r] for r in range(lanes)]
                    hi = [unp[r][1] * ws[r] for r in range(lanes)]
                    while len(lo) > 1:
                        lo = [lo[k] + lo[k + 1] for k in range(0, len(lo), 2)]
                        hi = [hi[k] + hi[k + 1] for k in range(0, len(hi), 2)]
                    plsc.addupdate(out_v.at[pl.ds(c * lanes, lanes)], lo[0])
                    plsc.addupdate(out_v.at[pl.ds(words + c * lanes, lanes)], hi[0])

        def gather(half_index, rows, sem):
            return pltpu.make_async_copy(t_hbm.at[idx_v.at[half_index]], rows, sem)

        def token(t, out_v, sem_o, out_in_flight):
            gather(2 * t + 1, rows_b, sem_b).start()
            gather(2 * t, rows_a, sem_a).wait()

            @pl.when(out_in_flight)
            def _():
                pltpu.make_async_copy(out_v, o_hbm.at[base + t], sem_o).wait()

            accumulate(rows_a, out_v, t, 0, True)

            @pl.when(t + 1 < tpw)
            def _():
                gather(2 * t + 2, rows_a, sem_a).start()

            gather(2 * t + 1, rows_b, sem_b).wait()
            accumulate(rows_b, out_v, t, 1, False)
            pltpu.make_async_copy(out_v, o_hbm.at[base + t], sem_o).start()

        gather(0, rows_a, sem_a).start()

        @pl.loop(0, tpw // 2)
        def _(i):
            token(2 * i, out_a, sem_oa, i > 0)
            token(2 * i + 1, out_b, sem_ob, i > 0)

        pltpu.make_async_copy(out_a, o_hbm.at[base], sem_oa).wait()
        pltpu.make_async_copy(out_b, o_hbm.at[base], sem_ob).wait()

    return mix(table, idx2, w)


def _final_kernel(hp_ref, peer_ref, gf_ref, yin_ref, y_ref, done_ref):
    del yin_ref
    done_ref[...] = jnp.zeros(done_ref.shape, f32)
    h = hp_ref[...] + peer_ref[...]
    y_ref[...] = h * lax.rsqrt(jnp.mean(h * h, axis=-1, keepdims=True) + EPS) * gf_ref[...]


def _final(hp, peer, gf, y_all, y_block0, tm):
    t, d = hp.shape
    return pl.pallas_call(
        _final_kernel,
        grid=(t // tm,),
        in_specs=[
            pl.BlockSpec((tm, d), lambda i: (i, 0)),
            pl.BlockSpec((tm, d), lambda i: (i, 0)),
            pl.BlockSpec((1, d), lambda i: (0, 0)),
            pl.BlockSpec(memory_space=pl.ANY),
        ],
        out_specs=[pl.BlockSpec((tm, d), lambda i: (y_block0 + i, 0)),
                   pl.BlockSpec((8, LANES), lambda i: (0, 0))],
        out_shape=[jax.ShapeDtypeStruct(y_all.shape, f32), jax.ShapeDtypeStruct((8, LANES), f32)],
        input_output_aliases={3: 0},
        compiler_params=_cparams("arbitrary"),
        name="final_norm",
    )(hp, peer, gf, y_all)


def _channel_mixer_front(oa, ob, x, x_block0, ob_block0, wo_bf, g2, wqt_bf, sk_bf, u_pk, tm):
    hp, xn, eidx, gate = _retrieve(oa, ob, x, wo_bf, g2, wqt_bf, sk_bf, tm, x_block0, ob_block0)
    return hp, eidx, gate, _selected_dots(u_pk, eidx, xn)


def _channel_mixer_mix(front, after, v_pk, tm):
    _, eidx, gate, act = front
    return _weighted_rows(v_pk, eidx, _gate_weights(act, gate, after, tm))


def kernel(x_prompt, x_sample, cache_k, cache_v, state_hgrn, norm1, w_in, da_lq1, da_lk1, da_lq2, da_lk2,
           da_out_norm, hg_lb_logits, hg_out_norm, w_out, rel_bias, norm2, peer_w_q, peer_sub_keys,
           peer_u, peer_v, final_norm):
    batch, seq, d = x_prompt.shape
    dbatch, dseq, _ = x_sample.shape
    past = cache_k.shape[2]
    da_heads = cache_k.shape[3]
    hg_heads = state_hgrn.shape[2]
    depth = w_in.shape[0]
    assert depth == 1 and seq % ATT_BLOCK == 0 and seq % CHUNK == 0 and dseq % HG_SUB == 0 and dseq <= CHUNK
    assert past % CHUNK == 0 and (past + dseq - 1) // CHUNK == past // CHUNK

    l = 0
    lam_init = 0.8 - 0.6 * math.exp(-0.3 * l)
    lam = (jnp.exp(jnp.sum(da_lq1[l].astype(f32) * da_lk1[l].astype(f32)))
           - jnp.exp(jnp.sum(da_lq2[l].astype(f32) * da_lk2[l].astype(f32))) + lam_init).reshape(1)
    lb = jnp.cumsum(jax.nn.softmax(hg_lb_logits.astype(f32), axis=0), axis=0)[l].reshape(1, -1)
    da_gain = (jnp.tile(da_out_norm[l].astype(f32), 2) * (1.0 - lam_init)).reshape(1, LANES)
    hg_gain = hg_out_norm[l].astype(f32).reshape(1, HG_DIM)
    g1 = norm1[l].astype(f32).reshape(1, d)
    g2 = norm2[l].astype(f32).reshape(1, d)
    gf = final_norm.astype(f32).reshape(1, d)
    w_in_bf = w_in[l].astype(bf16)
    wkv_t_bf = w_in[l][:, cache_k.shape[3] * cache_k.shape[4]:3 * cache_k.shape[3] * cache_k.shape[4]].T.astype(bf16)
    wo_bf = w_out[l].astype(bf16)
    wqt_bf = peer_w_q[l].T.astype(bf16)
    sk_bf = peer_sub_keys[l].reshape(-1, PEER_KEYS, PEER_SUB_DIM).astype(bf16)
    u_pk = _pack_table(peer_u[l])
    v_pk = _pack_table(peer_v[l])

    bias_tiles = _prompt_bias_tiles(rel_bias)
    q_pos = past + jnp.arange(dseq)
    bias_s = _bias_of(jnp.arange(past + dseq)[None, :] - q_pos[:, None], rel_bias) * LOG2E
    bias_past, bias_new = bias_s[:, :, :past], bias_s[:, :, past:]
    zero_state = jnp.zeros((1, hg_heads, HG_DIM, HG_DIM), f32)

    dh2 = 2 * DA_HEAD_DIM
    wq = da_heads * dh2
    tm = 256
    x_all = x_prompt.reshape(batch * seq, d)
    kt_all = jnp.zeros((batch * wq, seq), f32)
    vt_all = jnp.zeros((batch * wq, seq), f32)
    y_all = jnp.zeros((batch * seq, d), f32)
    nq = seq // ATT_BLOCK
    blocks = seq // tm
    segments = [(0, 0, nq // 2), (0, nq // 2, nq - nq // 2)] + [(b, 0, nq) for b in range(1, batch)]
    s_p = []
    done = [jnp.zeros((8, LANES), f32)] * 3
    fronts, peers = [], []

    def finish(k, y_all):
        return _final(fronts[k][0][0], peers[k], gf, y_all, fronts[k][1], tm)

    for s, (b, q0, n_q) in enumerate(segments):
        if q0 == 0:
            q, kt, vt, hz, kt_all, vt_all = _inproj_prompt(x_all, g1, w_in_bf, wkv_t_bf, kt_all, vt_all, done[s],
                                                           b, seq, tm)
            ob, s_new = _hgrn(hz, lb, hg_gain, zero_state, 1, seq, CHUNK, 4)
            s_p.append(s_new)
        oa = _attn_prompt(q, kt, vt, bias_tiles, da_gain, lam, q0, n_q)
        if s >= 1:
            peers.append(_channel_mixer_mix(fronts[s - 1][0], oa, v_pk, tm))
        row0 = q0 * ATT_BLOCK // tm
        fronts.append((_channel_mixer_front(oa, ob, x_all, b * blocks + row0, row0, wo_bf, g2, wqt_bf, sk_bf, u_pk,
                                            tm), b * blocks + row0))
        if s >= 2:
            y_all, token = finish(s - 2, y_all)
            done.append(token)
    peers.append(_channel_mixer_mix(fronts[-1][0], fronts[-1][0][0], v_pk, tm))
    for k in (len(segments) - 2, len(segments) - 1):
        y_all, _ = finish(k, y_all)

    xs = x_sample.reshape(dbatch * dseq, d)
    q, k_s, v_s, hz = _inproj_sample(xs, g1, w_in_bf, wq)
    kp_t = jnp.transpose(cache_k[l], (0, 2, 3, 1)).reshape(dbatch * wq, past)
    vp_t = jnp.transpose(cache_v[l], (0, 2, 3, 1)).reshape(dbatch * wq, past)
    oa = _attn_sample(q, k_s, v_s, kp_t, vp_t, bias_past, bias_new, da_gain, lam, dbatch, dseq, past)
    ob, s_s = _hgrn(hz, lb, hg_gain, state_hgrn[l].astype(f32), dbatch, dseq, dseq, 1)
    front = _channel_mixer_front(oa, ob, xs, 0, 0, wo_bf, g2, wqt_bf, sk_bf, u_pk, dbatch * dseq)
    peer = _channel_mixer_mix(front, front[0], v_pk, dbatch * dseq)
    y_s, _ = _final(front[0], peer, gf, jnp.zeros((dbatch * dseq, d), f32), 0, dbatch * dseq)

    y_prompt = y_all.reshape(batch, seq, d)
    y_sample = y_s.reshape(dbatch, dseq, d)
    k_prompt = jnp.transpose(kt_all.reshape(batch, da_heads, dh2, seq), (0, 3, 1, 2))[None]
    v_prompt = jnp.transpose(vt_all.reshape(batch, da_heads, DA_V_DIM, seq), (0, 3, 1, 2))[None]
    state_prompt = jnp.concatenate(s_p, axis=0)[None].astype(state_hgrn.dtype)
    k_sample = k_s.reshape(1, dbatch, dseq, da_heads, dh2)
    v_sample = v_s.reshape(1, dbatch, dseq, da_heads, DA_V_DIM)
    state_sample = s_s[None].astype(state_hgrn.dtype)
    return (y_prompt, y_sample, k_prompt, v_prompt, state_prompt, k_sample, v_sample, state_sample)
```

```python
import functools
import math

import jax
import jax.numpy as jnp
from jax import lax
from jax.experimental import pallas as pl
from jax.experimental.pallas import tpu as pltpu
from jax.experimental.pallas import tpu_sc as plsc

CHUNK = 64
DA_HEAD_DIM = 32
DA_V_DIM = 2 * DA_HEAD_DIM
HG_DIM = 128
REL_BUCKETS = 32
REL_MAX_DIST = 128
PEER_KEYS = 128
PEER_TOPK = 16
PEER_SUB_DIM = 64
EPS = 1e-6
NEG = -1e30
LOG2E = 1.4426950408889634

LANES = 128
ATT_BLOCK = 128
ATT_STEP = 512
HG_SUB = 16
V7X_VMEM_BYTES = 64 * 1024 * 1024
VMEM_LIMIT = V7X_VMEM_BYTES * 7 // 8

f32 = jnp.float32
bf16 = jnp.bfloat16


def _cparams(*sem):
    return pltpu.CompilerParams(dimension_semantics=sem, vmem_limit_bytes=VMEM_LIMIT)


def _inproj_prompt_kernel(x_ref, g_ref, w_ref, wkv_t_ref, kin_ref, vin_ref, after_ref,
                          q_ref, kt_ref, vt_ref, hz_ref, ktf_ref, vtf_ref):
    del kin_ref, vin_ref
    del after_ref
    x = x_ref[...]
    xn = x * lax.rsqrt(jnp.mean(x * x, axis=-1, keepdims=True) + EPS) * g_ref[...]
    xb = xn.astype(bf16)
    wq = q_ref.shape[1]
    nt = (((1,), (1,)), ((), ()))
    q_ref[...] = jnp.dot(xb, w_ref[:, 0:wq], preferred_element_type=f32).astype(bf16)
    kt = lax.dot_general(wkv_t_ref[0:wq, :], xb, nt, preferred_element_type=f32)
    ktf_ref[...] = kt
    kt_ref[...] = kt.astype(bf16)
    vt = lax.dot_general(wkv_t_ref[wq:2 * wq, :], xb, nt, preferred_element_type=f32)
    vtf_ref[...] = vt
    vt_ref[...] = vt.astype(bf16)
    hz_ref[...] = jnp.dot(xb, w_ref[:, 3 * wq:], preferred_element_type=f32)


def _inproj_prompt(x, g, w_bf, wkv_t_bf, kt_all, vt_all, after, b, seq, tm):
    d = x.shape[1]
    e = w_bf.shape[1]
    wq = wkv_t_bf.shape[0] // 2
    steps = seq // tm
    return pl.pallas_call(
        _inproj_prompt_kernel,
        grid=(steps,),
        in_specs=[
            pl.BlockSpec((tm, d), lambda i: (b * steps + i, 0)),
            pl.BlockSpec((1, d), lambda i: (0, 0)),
            pl.BlockSpec((d, e), lambda i: (0, 0)),
            pl.BlockSpec((2 * wq, d), lambda i: (0, 0)),
            pl.BlockSpec(memory_space=pl.ANY),
            pl.BlockSpec(memory_space=pl.ANY),
            pl.BlockSpec(memory_space=pl.ANY),
        ],
        out_specs=[
            pl.BlockSpec((tm, wq), lambda i: (i, 0)),
            pl.BlockSpec((wq, tm), lambda i: (0, i)),
            pl.BlockSpec((wq, tm), lambda i: (0, i)),
            pl.BlockSpec((tm, e - 3 * wq), lambda i: (i, 0)),
            pl.BlockSpec((wq, tm), lambda i: (b, i)),
            pl.BlockSpec((wq, tm), lambda i: (b, i)),
        ],
        out_shape=[
            jax.ShapeDtypeStruct((seq, wq), bf16),
            jax.ShapeDtypeStruct((wq, seq), bf16),
            jax.ShapeDtypeStruct((wq, seq), bf16),
            jax.ShapeDtypeStruct((seq, e - 3 * wq), f32),
            jax.ShapeDtypeStruct(kt_all.shape, f32),
            jax.ShapeDtypeStruct(vt_all.shape, f32),
        ],
        input_output_aliases={4: 4, 5: 5},
        compiler_params=_cparams("parallel"),
        name="inproj",
    )(x, g, w_bf, wkv_t_bf, kt_all, vt_all, after)


def _inproj_sample_kernel(x_ref, g_ref, w_ref, q_ref, k_ref, v_ref, hz_ref):
    x = x_ref[...]
    xn = x * lax.rsqrt(jnp.mean(x * x, axis=-1, keepdims=True) + EPS) * g_ref[...]
    xb = xn.astype(bf16)
    wq = q_ref.shape[1]
    q_ref[...] = jnp.dot(xb, w_ref[:, 0:wq], preferred_element_type=f32).astype(bf16)
    k_ref[...] = jnp.dot(xb, w_ref[:, wq:2 * wq], preferred_element_type=f32)
    v_ref[...] = jnp.dot(xb, w_ref[:, 2 * wq:3 * wq], preferred_element_type=f32)
    hz_ref[...] = jnp.dot(xb, w_ref[:, 3 * wq:], preferred_element_type=f32)


def _inproj_sample(x, g, w_bf, wq):
    t, d = x.shape
    e = w_bf.shape[1]
    return pl.pallas_call(
        _inproj_sample_kernel,
        grid=(1,),
        in_specs=[
            pl.BlockSpec((t, d), lambda i: (0, 0)),
            pl.BlockSpec((1, d), lambda i: (0, 0)),
            pl.BlockSpec((d, e), lambda i: (0, 0)),
        ],
        out_specs=[
            pl.BlockSpec((t, wq), lambda i: (0, 0)),
            pl.BlockSpec((t, wq), lambda i: (0, 0)),
            pl.BlockSpec((t, wq), lambda i: (0, 0)),
            pl.BlockSpec((t, e - 3 * wq), lambda i: (0, 0)),
        ],
        out_shape=[
            jax.ShapeDtypeStruct((t, wq), bf16),
            jax.ShapeDtypeStruct((t, wq), f32),
            jax.ShapeDtypeStruct((t, wq), f32),
            jax.ShapeDtypeStruct((t, e - 3 * wq), f32),
        ],
        compiler_params=_cparams("arbitrary"),
        name="inproj_sample",
    )(x, g, w_bf)


def _rel_bucket(rel):
    nb = REL_BUCKETS // 2
    max_exact = nb // 2
    ret = jnp.where(rel > 0, nb, 0)
    n = jnp.abs(rel)
    nf = jnp.maximum(n, 1).astype(f32)
    large = max_exact + (jnp.log(nf / max_exact) / math.log(REL_MAX_DIST / max_exact)
                         * (nb - max_exact)).astype(jnp.int32)
    large = jnp.minimum(large, nb - 1)
    return ret + jnp.where(n < max_exact, n, large)


def _bias_of(rel, rel_bias):
    bucket = _rel_bucket(rel)
    out = jnp.zeros((rel_bias.shape[1],) + rel.shape, f32)
    for b in range(REL_BUCKETS):
        out = out + jnp.where(bucket[None] == b, rel_bias[b].astype(f32)[:, None, None], 0.0)
    return out


def _attn_prompt_kernel(lam_ref, q_ref, kt_ref, vt_ref, bias_ref, gain_ref, o_ref, m_sc, acc_sc, sa_sc, sb_sc,
                        *, q_block0):
    qb = pl.program_id(1) + q_block0
    tq = q_ref.shape[0]
    seq = kt_ref.shape[1]
    per_step = ATT_STEP // ATT_BLOCK
    lane = lax.broadcasted_iota(jnp.int32, (tq, LANES), 1)
    feat_k = lax.broadcasted_iota(jnp.int32, (LANES, ATT_STEP), 0)
    qs = q_ref[...].astype(f32) * (DA_HEAD_DIM ** -0.5 * LOG2E)
    q4 = jnp.concatenate(
        [jnp.where((lane // DA_HEAD_DIM) == i, qs, 0.0).astype(bf16) for i in range(4)], axis=0)

    m_sc[...] = jnp.full(m_sc.shape, -jnp.inf, f32)
    acc_sc[...] = jnp.zeros(acc_sc.shape, f32)

    n_steps = qb // per_step + 1
    last = n_steps - 1
    r = qb % per_step

    def scores(k):
        start = pl.multiple_of(jnp.minimum(k * ATT_STEP, seq - ATT_STEP), ATT_STEP)
        return jnp.dot(q4, kt_ref[:, pl.ds(start, ATT_STEP)], preferred_element_type=f32)

    def half(k, cur_ref, nxt_ref):
        nxt_ref[...] = scores(k + 1)
        tile = jnp.where(k == last, 1 + r, jnp.where((k == last - 1) & (r == 0), per_step + 1, 0))
        vv = vt_ref[:, pl.ds(pl.multiple_of(k * ATT_STEP, ATT_STEP), ATT_STEP)]
        for hh in range(2):
            bias = bias_ref[0, tile, hh * tq:(hh + 1) * tq, :]
            ps, alphas = [], []
            for c in range(2):
                rows = slice((2 * hh + c) * tq, (2 * hh + c + 1) * tq)
                s = cur_ref[rows, :] + bias
                m_old = m_sc[rows]
                m_new = jnp.maximum(m_old, jnp.max(s, axis=-1, keepdims=True))
                ps.append(jnp.exp2(s - m_new[:, 0:1]).astype(bf16))
                alphas.append(jnp.exp2(m_old - m_new))
                m_sc[rows] = m_new
            rows2 = slice(2 * hh * tq, (2 * hh + 2) * tq)
            vaug = jnp.where((feat_k // DA_V_DIM) == hh, vv, jnp.ones_like(vv))
            acc_sc[rows2] = (acc_sc[rows2] * jnp.concatenate(alphas, axis=0)
                             + lax.dot_general(jnp.concatenate(ps, axis=0), vaug, (((1,), (1,)), ((), ())),
                                               preferred_element_type=f32))

    sa_sc[...] = scores(0)

    def two_steps(kk, carry):
        half(2 * kk, sa_sc, sb_sc)

        @pl.when(2 * kk + 1 < n_steps)
        def _():
            half(2 * kk + 1, sb_sc, sa_sc)

        return carry

    lax.fori_loop(0, (n_steps + 1) // 2, two_steps, 0)

    lam = lam_ref[0]
    outs = []
    for hh in range(2):
        own = (lane // DA_V_DIM) == hh
        a1 = acc_sc[2 * hh * tq:(2 * hh + 1) * tq]
        a2 = acc_sc[(2 * hh + 1) * tq:(2 * hh + 2) * tq]
        l1 = jnp.max(jnp.where(own, 0.0, a1), axis=-1, keepdims=True)
        l2 = jnp.max(jnp.where(own, 0.0, a2), axis=-1, keepdims=True)
        o = a1 / l1 - lam * (a2 / l2)
        ssq = jnp.sum(jnp.where(own, o * o, 0.0), axis=-1, keepdims=True)
        outs.append(o * lax.rsqrt(ssq * (1.0 / DA_V_DIM) + EPS))
    o = jnp.where((lane // DA_V_DIM) == 0, outs[0], outs[1])
    o_ref[...] = (o * gain_ref[...]).astype(o_ref.dtype)


def _attn_prompt(q, kt, vt, bias_tiles, gain, lam, q_block0, n_q):
    seq, w = q.shape
    pairs = w // LANES
    assert seq % ATT_STEP == 0
    n_tiles = ATT_STEP // ATT_BLOCK + 2
    return pl.pallas_call(
        functools.partial(_attn_prompt_kernel, q_block0=q_block0),
        grid=(pairs, n_q),
        in_specs=[
            pl.BlockSpec(memory_space=pltpu.SMEM),
            pl.BlockSpec((ATT_BLOCK, LANES), lambda p, i: (q_block0 + i, p)),
            pl.BlockSpec((LANES, seq), lambda p, i: (p, 0)),
            pl.BlockSpec((LANES, seq), lambda p, i: (p, 0)),
            pl.BlockSpec((1, n_tiles, 2 * ATT_BLOCK, ATT_STEP), lambda p, i: (p, 0, 0, 0)),
            pl.BlockSpec((1, LANES), lambda p, i: (0, 0)),
        ],
        out_specs=pl.BlockSpec((ATT_BLOCK, LANES), lambda p, i: (i, p)),
        out_shape=jax.ShapeDtypeStruct((n_q * ATT_BLOCK, w), bf16),
        scratch_shapes=[
            pltpu.VMEM((4 * ATT_BLOCK, LANES), f32),
            pltpu.VMEM((4 * ATT_BLOCK, LANES), f32),
            pltpu.VMEM((4 * ATT_BLOCK, ATT_STEP), f32),
            pltpu.VMEM((4 * ATT_BLOCK, ATT_STEP), f32),
        ],
        compiler_params=_cparams("parallel", "arbitrary"),
        name="attn_prompt",
    )(lam, q, kt, vt, bias_tiles, gain)


def _prompt_bias_tiles(rel_bias):
    h = rel_bias.shape[1]
    per_step = ATT_STEP // ATT_BLOCK
    i = jnp.arange(ATT_BLOCK)
    rel_diag = i[None, :] - i[:, None]
    rel_prev = rel_diag - ATT_BLOCK
    far = rel_bias[REL_BUCKETS // 2 - 1].astype(f32)
    b_diag = (_bias_of(rel_diag, rel_bias) - far[:, None, None]) * LOG2E
    b_prev = (_bias_of(rel_prev, rel_bias) - far[:, None, None]) * LOG2E
    mask = (i[None, :] // CHUNK) <= (i[:, None] // CHUNK)
    b_diag = jnp.where(mask[None], b_diag, NEG)
    zero = jnp.zeros_like(b_diag)
    dead = jnp.full_like(b_diag, NEG)
    tiles = [jnp.concatenate([zero] * per_step, axis=-1)]
    for r in range(per_step):
        blocks = [zero if j < r - 1 else b_prev if j == r - 1 else b_diag if j == r else dead
                  for j in range(per_step)]
        tiles.append(jnp.concatenate(blocks, axis=-1))
    tiles.append(jnp.concatenate([zero] * (per_step - 1) + [b_prev], axis=-1))
    tiles = jnp.stack(tiles, axis=1)
    tiles = tiles.reshape(h // 2, 2, per_step + 2, ATT_BLOCK, ATT_STEP)
    return jnp.transpose(tiles, (0, 2, 1, 3, 4)).reshape(h // 2, per_step + 2, 2 * ATT_BLOCK, ATT_STEP)


def _attn_sample_kernel(lam_ref, q_ref, kp_ref, vp_ref, kn_ref, vn_ref, bp_ref, bn_ref, gain_ref, o_ref):
    nq = q_ref.shape[0]
    pairs = q_ref.shape[1] // LANES
    lane = lax.broadcasted_iota(jnp.int32, (nq, LANES), 1)
    lam = lam_ref[0]
    nt = (((1,), (1,)), ((), ()))
    for p in range(pairs):
        sl = slice(p * LANES, (p + 1) * LANES)
        qs = q_ref[:, sl].astype(f32) * (DA_HEAD_DIM ** -0.5 * LOG2E)
        kp = kp_ref[sl, :].astype(bf16)
        vp = vp_ref[sl, :].astype(bf16)
        kn = kn_ref[:, sl].astype(bf16)
        vn = vn_ref[:, sl].astype(bf16)
        outs = []
        for hh in range(2):
            h = 2 * p + hh
            own = (lane // DA_V_DIM) == hh
            res = []
            for c in range(2):
                qm = jnp.where((lane // DA_HEAD_DIM) == 2 * hh + c, qs, 0.0).astype(bf16)
                sp = jnp.dot(qm, kp, preferred_element_type=f32) + bp_ref[h]
                sn = lax.dot_general(qm, kn, nt, preferred_element_type=f32) + bn_ref[h]
                m = jnp.maximum(jnp.max(sp, axis=-1, keepdims=True), jnp.max(sn, axis=-1, keepdims=True))
                pp = jnp.exp2(sp - m)
                pn = jnp.exp2(sn - m)
                l = jnp.sum(pp, axis=-1, keepdims=True) + jnp.sum(pn, axis=-1, keepdims=True)
                pv = (lax.dot_general(pp.astype(bf16), vp, nt, preferred_element_type=f32)
                      + jnp.dot(pn.astype(bf16), vn, preferred_element_type=f32))
                res.append(pv / l)
            o = res[0] - lam * res[1]
            ssq = jnp.sum(jnp.where(own, o * o, 0.0), axis=-1, keepdims=True)
            outs.append(o * lax.rsqrt(ssq * (1.0 / DA_V_DIM) + EPS))
        o = jnp.where((lane // DA_V_DIM) == 0, outs[0], outs[1])
        o_ref[:, sl] = (o * gain_ref[...]).astype(o_ref.dtype)


def _attn_sample(q, k_new, v_new, k_past, v_past, bias_past, bias_new, gain, lam, batch, nq, past):
    w = q.shape[1]
    h = bias_past.shape[0]
    return pl.pallas_call(
        _attn_sample_kernel,
        grid=(batch,),
        in_specs=[
            pl.BlockSpec(memory_space=pltpu.SMEM),
            pl.BlockSpec((nq, w), lambda b: (b, 0)),
            pl.BlockSpec((w, past), lambda b: (b, 0)),
            pl.BlockSpec((w, past), lambda b: (b, 0)),
            pl.BlockSpec((nq, w), lambda b: (b, 0)),
            pl.BlockSpec((nq, w), lambda b: (b, 0)),
            pl.BlockSpec((h, nq, past), lambda b: (0, 0, 0)),
            pl.BlockSpec((h, nq, nq), lambda b: (0, 0, 0)),
            pl.BlockSpec((1, LANES), lambda b: (0, 0)),
        ],
        out_specs=pl.BlockSpec((nq, w), lambda b: (b, 0)),
        out_shape=jax.ShapeDtypeStruct((batch * nq, w), bf16),
        compiler_params=_cparams("parallel"),
        name="attn_sample",
    )(lam, q, k_past, v_past, k_new, v_new, bias_past, bias_new, gain)


def _hgrn_chunk(q, f_logit, vv, g_logit, lb, gain, s0, lc):
    n_sub = lc // HG_SUB
    f = lb + (1.0 - lb) * jax.nn.sigmoid(f_logit)
    logf = jnp.log(f)
    k = 1.0 - f
    row = lax.broadcasted_iota(jnp.int32, (lc, lc), 0)
    col = lax.broadcasted_iota(jnp.int32, (lc, lc), 1)
    tril = (col <= row).astype(f32)
    b = jnp.dot(tril, logf, precision=lax.Precision.HIGHEST, preferred_element_type=f32)
    b_last = b[lc - 1:lc, :]

    o = jnp.dot((q * jnp.exp(b)).astype(bf16), s0.astype(bf16), preferred_element_type=f32)
    k_hat = k * jnp.exp(b_last - b)
    bt = b.T
    decay_col = jnp.exp(bt[:, lc - 1:lc])
    s_new = decay_col * s0 + jnp.dot(k_hat.T.astype(bf16), vv.astype(bf16), preferred_element_type=f32)

    if n_sub > 1:
        rows = []
        nt = (((1,), (1,)), ((), ()))
        for i in range(n_sub):
            lo = i * HG_SUB
            if i == 0:
                rows.append(jnp.zeros((HG_SUB, lc), f32))
                continue
            beta = b[lo - 1:lo, :]
            q_t = q[lo:lo + HG_SUB, :] * jnp.exp(b[lo:lo + HG_SUB, :] - beta)
            k_t = k * jnp.exp(jnp.minimum(beta - b, 0.0))
            rows.append(lax.dot_general(q_t.astype(bf16), k_t.astype(bf16), nt, preferred_element_type=f32))
        a_off = jnp.concatenate(rows, axis=0)
        a_off = jnp.where(col < (row // HG_SUB) * HG_SUB, a_off, 0.0)
        o = o + jnp.dot(a_off.astype(bf16), vv.astype(bf16), preferred_element_type=f32)

    ones = jnp.ones((HG_DIM, LANES), bf16)
    srow = lax.broadcasted_iota(jnp.int32, (HG_SUB, HG_DIM), 0)
    orow = lax.broadcasted_iota(jnp.int32, (HG_SUB, HG_DIM), 0)
    blocks = []
    for i in range(n_sub):
        lo = i * HG_SUB
        b_i = b[lo:lo + HG_SUB, :]
        k_i = k[lo:lo + HG_SUB, :]
        v_i = vv[lo:lo + HG_SUB, :]
        q_i = q[lo:lo + HG_SUB, :]
        d_rows = []
        for t in range(HG_SUB):
            arg = jnp.where(srow <= t, b_i[t:t + 1, :] - b_i, -jnp.inf)
            d_rows.append(q_i[t:t + 1, :] * k_i * jnp.exp(arg))
        d3 = jnp.concatenate(d_rows, axis=0)
        a_rep = jnp.dot(d3.astype(bf16), ones, preferred_element_type=f32)
        o_i = jnp.zeros((HG_SUB, HG_DIM), f32)
        for t in range(HG_SUB):
            o_t = jnp.sum(a_rep[t * HG_SUB:(t + 1) * HG_SUB, :] * v_i, axis=0, keepdims=True)
            o_i = jnp.where(orow == t, o_t, o_i)
        blocks.append(o_i)
    o = o + (jnp.concatenate(blocks, axis=0) if n_sub > 1 else blocks[0])

    on = o * lax.rsqrt(jnp.mean(o * o, axis=-1, keepdims=True) + EPS) * gain
    gate = g_logit * jax.nn.sigmoid(g_logit)
    return on * gate, s_new


def _hgrn_kernel(hz_ref, lb_ref, gain_ref, s0_ref, o_ref, s_ref, st_sc, *, lc, heads):
    j = pl.program_id(1)
    w = heads * HG_DIM

    @pl.when(j == 0)
    def _():
        st_sc[...] = s0_ref[0]

    n_chunks = hz_ref.shape[0] // lc
    for c in range(n_chunks):
        rs = slice(c * lc, (c + 1) * lc)
        for h in range(heads):
            cs = slice(h * HG_DIM, (h + 1) * HG_DIM)
            o, s_new = _hgrn_chunk(
                hz_ref[rs, h * HG_DIM:(h + 1) * HG_DIM],
                hz_ref[rs, w + h * HG_DIM:w + (h + 1) * HG_DIM],
                hz_ref[rs, 2 * w + h * HG_DIM:2 * w + (h + 1) * HG_DIM],
                hz_ref[rs, 3 * w + h * HG_DIM:3 * w + (h + 1) * HG_DIM],
                lb_ref[:, cs], gain_ref[...], st_sc[h], lc)
            st_sc[h] = s_new
            o_ref[rs, cs] = o.astype(o_ref.dtype)

    @pl.when(j == pl.num_programs(1) - 1)
    def _():
        s_ref[0] = st_sc[...]


def _hgrn(hz, lb, gain, s0, batch, seq, lc, chunks_per_step):
    t = hz.shape[0]
    w = hz.shape[1] // 4
    heads = w // HG_DIM
    rows = lc * chunks_per_step
    steps = seq // rows
    return pl.pallas_call(
        functools.partial(_hgrn_kernel, lc=lc, heads=heads),
        grid=(batch, steps),
        in_specs=[
            pl.BlockSpec((rows, 4 * w), lambda b, j: (b * steps + j, 0)),
            pl.BlockSpec((1, w), lambda b, j: (0, 0)),
            pl.BlockSpec((1, HG_DIM), lambda b, j: (0, 0)),
            pl.BlockSpec((1, heads, HG_DIM, HG_DIM), lambda b, j: (b, 0, 0, 0)),
        ],
        out_specs=[
            pl.BlockSpec((rows, w), lambda b, j: (b * steps + j, 0)),
            pl.BlockSpec((1, heads, HG_DIM, HG_DIM), lambda b, j: (b, 0, 0, 0)),
        ],
        out_shape=[
            jax.ShapeDtypeStruct((t, w), bf16),
            jax.ShapeDtypeStruct((batch, heads, HG_DIM, HG_DIM), f32),
        ],
        scratch_shapes=[pltpu.VMEM((heads, HG_DIM, HG_DIM), f32)],
        compiler_params=_cparams("parallel", "arbitrary"),
        name="hgrn2",
    )(hz, lb, gain, s0)


def _top_rows(vals, k, payload=None):
    n_rows = vals.shape[0]
    row = lax.broadcasted_iota(jnp.int32, vals.shape, 0)
    out_v, out_i = [], []
    for _ in range(k):
        m = jnp.max(vals, axis=0, keepdims=True)
        first = jnp.min(jnp.where(vals == m, row, n_rows), axis=0, keepdims=True)
        hit = row == first
        out_v.append(m)
        if payload is None:
            out_i.append(first)
        else:
            out_i.append(jnp.max(jnp.where(hit, payload, -1), axis=0, keepdims=True))
        vals = jnp.where(hit, -jnp.inf, vals)
    return jnp.concatenate(out_v, axis=0), jnp.concatenate(out_i, axis=0)


def _retrieve_kernel(oa_ref, ob_ref, x_ref, wo_ref, g_ref, wqt_ref, sk_ref,
                     hp_ref, xn_ref, idx_ref, gate_ref, qt_sc, s1v_sc, s1i_sc, idx_sc, gate_sc):
    half = oa_ref.shape[1]
    y = (jnp.dot(oa_ref[...], wo_ref[0:half, :], preferred_element_type=f32)
         + jnp.dot(ob_ref[...], wo_ref[half:, :], preferred_element_type=f32))
    hp = x_ref[...] + y
    hp_ref[...] = hp
    xn = hp * lax.rsqrt(jnp.mean(hp * hp, axis=-1, keepdims=True) + EPS) * g_ref[...]
    xn_ref[...] = xn
    nt = (((1,), (1,)), ((), ()))
    qt_sc[...] = lax.dot_general(wqt_ref[...], xn.astype(bf16), nt, preferred_element_type=f32).astype(bf16)

    n_half = sk_ref.shape[0]

    def stage1(hc, carry):
        start = pl.multiple_of(hc * PEER_SUB_DIM, PEER_SUB_DIM)
        s = jnp.dot(sk_ref[hc], qt_sc[pl.ds(start, PEER_SUB_DIM), :], preferred_element_type=f32)
        v, i = _top_rows(s, PEER_TOPK)
        s1v_sc[hc] = v
        s1i_sc[hc] = i
        return carry

    lax.fori_loop(0, n_half, stage1, 0)

    tm = x_ref.shape[0]
    sub = lax.broadcasted_iota(jnp.int32, (8, tm), 0)

    def stage2(h, carry):
        v1, i1 = s1v_sc[2 * h], s1i_sc[2 * h]
        v2, i2 = s1v_sc[2 * h + 1], s1i_sc[2 * h + 1]
        cv, ci = [], []
        for i in range(PEER_TOPK):
            nj = PEER_TOPK // (i + 1)
            rows = PEER_TOPK if nj > 8 else 8
            a = v1[i:i + 1, :] + v2[0:rows, :]
            e = i1[i:i + 1, :] * PEER_KEYS + i2[0:rows, :]
            if nj < rows:
                a = jnp.where(sub < nj, a, -jnp.inf)
            cv.append(a)
            ci.append(e)
        top_s, eidx = _top_rows(jnp.concatenate(cv, axis=0), PEER_TOPK, jnp.concatenate(ci, axis=0))
        p = jnp.exp(top_s - top_s[0:1, :])
        g = p / jnp.sum(p, axis=0, keepdims=True)
        r0 = pl.multiple_of(h * PEER_TOPK, PEER_TOPK)
        gate_sc[pl.ds(r0, PEER_TOPK), :] = g
        idx_sc[pl.ds(r0, PEER_TOPK), :] = eidx
        return carry

    lax.fori_loop(0, n_half // 2, stage2, 0)
    idx_ref[...] = pltpu.bitcast(pltpu.bitcast(idx_sc[...], f32).T, jnp.int32)
    gate_ref[...] = gate_sc[...].T


def _retrieve(oa, ob, x, wo_bf, g2, wqt_bf, sk_bf, tm, x_block0=0, ob_block0=0):
    t, half = oa.shape
    d = x.shape[1]
    e = wqt_bf.shape[0]
    n_half = sk_bf.shape[0]
    n_sel = (n_half // 2) * PEER_TOPK
    return pl.pallas_call(
        _retrieve_kernel,
        grid=(t // tm,),
        in_specs=[
            pl.BlockSpec((tm, half), lambda i: (i, 0)),
            pl.BlockSpec((tm, half), lambda i: (ob_block0 + i, 0)),
            pl.BlockSpec((tm, d), lambda i: (x_block0 + i, 0)),
            pl.BlockSpec((2 * half, d), lambda i: (0, 0)),
            pl.BlockSpec((1, d), lambda i: (0, 0)),
            pl.BlockSpec((e, d), lambda i: (0, 0)),
            pl.BlockSpec((n_half, PEER_KEYS, PEER_SUB_DIM), lambda i: (0, 0, 0)),
        ],
        out_specs=[
            pl.BlockSpec((tm, d), lambda i: (i, 0)),
            pl.BlockSpec((tm, d), lambda i: (i, 0)),
            pl.BlockSpec((tm, n_sel), lambda i: (i, 0)),
            pl.BlockSpec((tm, n_sel), lambda i: (i, 0)),
        ],
        out_shape=[
            jax.ShapeDtypeStruct((t, d), f32),
            jax.ShapeDtypeStruct((t, d), f32),
            jax.ShapeDtypeStruct((t, n_sel), jnp.int32),
            jax.ShapeDtypeStruct((t, n_sel), f32),
        ],
        scratch_shapes=[
            pltpu.VMEM((e, tm), bf16),
            pltpu.VMEM((n_half, PEER_TOPK, tm), f32),
            pltpu.VMEM((n_half, PEER_TOPK, tm), jnp.int32),
            pltpu.VMEM((n_sel, tm), jnp.int32),
            pltpu.VMEM((n_sel, tm), f32),
        ],
        compiler_params=_cparams("parallel"),
        name="retrieve",
    )(oa, ob, x, wo_bf, g2, wqt_bf, sk_bf)


def _pack_table(tab):
    half = tab.shape[1] // 2
    lo = lax.bitcast_convert_type(tab[:, :half].astype(bf16), jnp.uint16).astype(jnp.uint32)
    hi = lax.bitcast_convert_type(tab[:, half:].astype(bf16), jnp.uint16).astype(jnp.uint32)
    return lax.bitcast_convert_type(lo | (hi << 16), jnp.int32)


def _sc_layout(n_tok, n_sel, words):
    info = plsc.get_sparse_core_info()
    n_workers = info.num_cores * info.num_subcores
    lanes = info.num_lanes
    block = 8 * lanes
    assert n_tok % (2 * n_workers) == 0 and words % block == 0 and (n_sel // 2) % lanes == 0
    return info, lanes, n_tok // n_workers, n_sel // 2, block


def _selected_dots(table, idx, x):
    n_tok, n_sel = idx.shape
    words = table.shape[1]
    info, lanes, tpw, half_rows, block = _sc_layout(n_tok, n_sel, words)
    idx2 = idx.reshape(2 * n_tok, half_rows)
    mesh = plsc.VectorSubcoreMesh(core_axis_name="c", subcore_axis_name="s")

    @functools.partial(
        pl.kernel, out_type=jax.ShapeDtypeStruct((n_tok, n_sel), f32), mesh=mesh,
        compiler_params=pltpu.CompilerParams(needs_layout_passes=False),
        scratch_types=[
            pltpu.VMEM((2 * tpw, half_rows), jnp.int32),
            pltpu.VMEM((half_rows, words), jnp.int32),
            pltpu.VMEM((half_rows, words), jnp.int32),
            pltpu.VMEM((2 * words,), f32),
            pltpu.VMEM((2 * words,), f32),
            pltpu.VMEM((n_sel // 8, 8 * lanes), f32),
            pltpu.VMEM((tpw, n_sel), f32),
            pltpu.SemaphoreType.DMA, pltpu.SemaphoreType.DMA, pltpu.SemaphoreType.DMA, pltpu.SemaphoreType.DMA,
        ])
    def dots(t_hbm, i_hbm, x_hbm, o_hbm, idx_v, rows_a, rows_b, x_a, x_b, part_v, act_v, sem_a, sem_b, sem_xa, sem_xb):
        wid = lax.axis_index("s") * info.num_cores + lax.axis_index("c")
        base = wid * tpw
        pltpu.sync_copy(i_hbm.at[pl.ds(base * 2, 2 * tpw)], idx_v)

        def accumulate(rows, x_v, half):
            @pl.loop(0, words // block)
            def _(cb):
                xs = [x_v[pl.ds(p * words + cb * block + c * lanes, lanes)] for c in range(8) for p in range(2)]

                @plsc.parallel_loop(0, half_rows)
                def _(row):
                    packed = [rows[row, pl.ds(cb * block + c * lanes, lanes)] for c in range(8)]
                    unp = [plsc.unpack(plsc.bitcast(p, bf16), format=plsc.PackFormat.INTERLEAVED) for p in packed]
                    prods = [unp[c][p] * xs[2 * c + p] for c in range(8) for p in range(2)]
                    while len(prods) > 1:
                        prods = [prods[k] + prods[k + 1] for k in range(0, len(prods), 2)]
                    prow = half * half_rows + row
                    plsc.addupdate(part_v.at[prow // 8, pl.ds((prow % 8) * lanes, lanes)], prods[0])

        def gather(half_index, rows, sem):
            return pltpu.make_async_copy(t_hbm.at[idx_v.at[half_index]], rows, sem)

        def xcopy(t, x_v, sem):
            return pltpu.make_async_copy(x_hbm.at[base + t], x_v, sem)

        def token(t, x_v, sem_x, x_next, sem_xn):
            gather(2 * t + 1, rows_b, sem_b).start()

            @pl.when(t + 1 < tpw)
            def _():
                xcopy(t + 1, x_next, sem_xn).start()

            zero = jnp.zeros((lanes,), f32)
            for r in range(n_sel):
                part_v[r // 8, pl.ds((r % 8) * lanes, lanes)] = zero
            xcopy(t, x_v, sem_x).wait()
            gather(2 * t, rows_a, sem_a).wait()
            accumulate(rows_a, x_v, 0)

            @pl.when(t + 1 < tpw)
            def _():
                gather(2 * t + 2, rows_a, sem_a).start()

            gather(2 * t + 1, rows_b, sem_b).wait()
            accumulate(rows_b, x_v, 1)
            it = lax.iota(jnp.int32, lanes)
            for g in range(n_sel // lanes):
                prow = g * (lanes // 8) + it // 8
                pcol = (it % 8) * lanes
                tot = plsc.load_gather(part_v, [prow, pcol])
                for l in range(1, lanes):
                    tot = tot + plsc.load_gather(part_v, [prow, pcol + l])
                act_v[t, pl.ds(g * lanes, lanes)] = tot

        gather(0, rows_a, sem_a).start()
        xcopy(0, x_a, sem_xa).start()

        @pl.loop(0, tpw // 2)
        def _(i):
            token(2 * i, x_a, sem_xa, x_b, sem_xb)
            token(2 * i + 1, x_b, sem_xb, x_a, sem_xa)

        pltpu.sync_copy(act_v, o_hbm.at[pl.ds(base, tpw)])

    return dots(table, idx2, x)


def _gate_weights_kernel(act_ref, gate_ref, after_ref, w_ref):
    del after_ref
    a = act_ref[...]
    w_ref[...] = gate_ref[...] * (0.5 * a * (1.0 + lax.erf(a * (2.0 ** -0.5))))


def _gate_weights(act, gate, after, tm):
    t, n_sel = act.shape
    spec = pl.BlockSpec((tm, n_sel), lambda i: (i, 0))
    return pl.pallas_call(
        _gate_weights_kernel,
        grid=(t // tm,),
        in_specs=[spec, spec, pl.BlockSpec(memory_space=pl.ANY)],
        out_specs=spec,
        out_shape=jax.ShapeDtypeStruct((t, n_sel), f32),
        compiler_params=_cparams("parallel"),
        name="peer_weights",
    )(act, gate, after)


def _weighted_rows(table, idx, w):
    n_tok, n_sel = w.shape
    words = table.shape[1]
    info, lanes, tpw, half_rows, _ = _sc_layout(n_tok, n_sel, words)
    idx2 = idx.reshape(2 * n_tok, half_rows)
    mesh = plsc.VectorSubcoreMesh(core_axis_name="c", subcore_axis_name="s")

    @functools.partial(
        pl.kernel, out_type=jax.ShapeDtypeStruct((n_tok, 2 * words), f32), mesh=mesh,
        compiler_params=pltpu.CompilerParams(needs_layout_passes=False),
        scratch_types=[
            pltpu.VMEM((2 * tpw, half_rows), jnp.int32),
            pltpu.VMEM((tpw, n_sel), f32),
            pltpu.VMEM((half_rows, words), jnp.int32),
            pltpu.VMEM((half_rows, words), jnp.int32),
            pltpu.VMEM((2 * words,), f32),
            pltpu.VMEM((2 * words,), f32),
            pltpu.SemaphoreType.DMA, pltpu.SemaphoreType.DMA, pltpu.SemaphoreType.DMA, pltpu.SemaphoreType.DMA,
        ])
    def mix(t_hbm, i_hbm, w_hbm, o_hbm, idx_v, w_v, rows_a, rows_b, out_a, out_b, sem_a, sem_b, sem_oa, sem_ob):
        wid = lax.axis_index("s") * info.num_cores + lax.axis_index("c")
        base = wid * tpw
        pltpu.sync_copy(i_hbm.at[pl.ds(base * 2, 2 * tpw)], idx_v)
        pltpu.sync_copy(w_hbm.at[pl.ds(base, tpw)], w_v)

        def accumulate(rows, out_v, t, half, first):
            if first:
                zero = jnp.zeros((lanes,), f32)
                for c in range(2 * words // lanes):
                    out_v[pl.ds(c * lanes, lanes)] = zero

            @pl.loop(0, half_rows // lanes)
            def _(g):
                wv = w_v[t, pl.ds(half * half_rows + g * lanes, lanes)]
                ws = [wv.at[jnp.full((lanes,), r, jnp.int32)].get(mode="promise_in_bounds") for r in range(lanes)]

                @plsc.parallel_loop(0, words // lanes)
                def _(c):
                    packed = [rows[g * lanes + r, pl.ds(c * lanes, lanes)] for r in range(lanes)]
                    unp = [plsc.unpack(plsc.bitcast(p, bf16), format=plsc.PackFormat.INTERLEAVED) for p in packed]
                    lo = [unp[r][0] * ws[r] for r in range(lanes)]
                    hi = [unp[r][1] * ws[r] for r in range(lanes)]
                    while len(lo) > 1:
                        lo = [lo[k] + lo[k + 1] for k in range(0, len(lo), 2)]
                        hi = [hi[k] + hi[k + 1] for k in range(0, len(hi), 2)]
                    plsc.addupdate(out_v.at[pl.ds(c * lanes, lanes)], lo[0])
                    plsc.addupdate(out_v.at[pl.ds(words + c * lanes, lanes)], hi[0])

        def gather(half_index, rows, sem):
            return pltpu.make_async_copy(t_hbm.at[idx_v.at[half_index]], rows, sem)

        def token(t, out_v, sem_o, out_in_flight):
            gather(2 * t + 1, rows_b, sem_b).start()
            gather(2 * t, rows_a, sem_a).wait()

            @pl.when(out_in_flight)
            def _():
                pltpu.make_async_copy(out_v, o_hbm.at[base + t], sem_o).wait()

            accumulate(rows_a, out_v, t, 0, True)

            @pl.when(t + 1 < tpw)
            def _():
                gather(2 * t + 2, rows_a, sem_a).start()

            gather(2 * t + 1, rows_b, sem_b).wait()
            accumulate(rows_b, out_v, t, 1, False)
            pltpu.make_async_copy(out_v, o_hbm.at[base + t], sem_o).start()

        gather(0, rows_a, sem_a).start()

        @pl.loop(0, tpw // 2)
        def _(i):
            token(2 * i, out_a, sem_oa, i > 0)
            token(2 * i + 1, out_b, sem_ob, i > 0)

        pltpu.make_async_copy(out_a, o_hbm.at[base], sem_oa).wait()
        pltpu.make_async_copy(out_b, o_hbm.at[base], sem_ob).wait()

    return mix(table, idx2, w)


def _final_kernel(hp_ref, peer_ref, gf_ref, yin_ref, y_ref, done_ref):
    del yin_ref
    done_ref[...] = jnp.zeros(done_ref.shape, f32)
    h = hp_ref[...] + peer_ref[...]
    y_ref[...] = h * lax.rsqrt(jnp.mean(h * h, axis=-1, keepdims=True) + EPS) * gf_ref[...]


def _final(hp, peer, gf, y_all, y_block0, tm):
    t, d = hp.shape
    return pl.pallas_call(
        _final_kernel,
        grid=(t // tm,),
        in_specs=[
            pl.BlockSpec((tm, d), lambda i: (i, 0)),
            pl.BlockSpec((tm, d), lambda i: (i, 0)),
            pl.BlockSpec((1, d), lambda i: (0, 0)),
            pl.BlockSpec(memory_space=pl.ANY),
        ],
        out_specs=[pl.BlockSpec((tm, d), lambda i: (y_block0 + i, 0)),
                   pl.BlockSpec((8, LANES), lambda i: (0, 0))],
        out_shape=[jax.ShapeDtypeStruct(y_all.shape, f32), jax.ShapeDtypeStruct((8, LANES), f32)],
        input_output_aliases={3: 0},
        compiler_params=_cparams("arbitrary"),
        name="final_norm",
    )(hp, peer, gf, y_all)


def _channel_mixer_front(oa, ob, x, x_block0, ob_block0, wo_bf, g2, wqt_bf, sk_bf, u_pk, tm):
    hp, xn, eidx, gate = _retrieve(oa, ob, x, wo_bf, g2, wqt_bf, sk_bf, tm, x_block0, ob_block0)
    return hp, eidx, gate, _selected_dots(u_pk, eidx, xn)


def _channel_mixer_mix(front, after, v_pk, tm):
    _, eidx, gate, act = front
    return _weighted_rows(v_pk, eidx, _gate_weights(act, gate, after, tm))


def kernel(x_prompt, x_sample, cache_k, cache_v, state_hgrn, norm1, w_in, da_lq1, da_lk1, da_lq2, da_lk2,
           da_out_norm, hg_lb_logits, hg_out_norm, w_out, rel_bias, norm2, peer_w_q, peer_sub_keys,
           peer_u, peer_v, final_norm):
    batch, seq, d = x_prompt.shape
    dbatch, dseq, _ = x_sample.shape
    past = cache_k.shape[2]
    da_heads = cache_k.shape[3]
    hg_heads = state_hgrn.shape[2]
    depth = w_in.shape[0]
    assert depth == 1 and seq % ATT_BLOCK == 0 and seq % CHUNK == 0 and dseq % HG_SUB == 0 and dseq <= CHUNK
    assert past % CHUNK == 0 and (past + dseq - 1) // CHUNK == past // CHUNK

    l = 0
    lam_init = 0.8 - 0.6 * math.exp(-0.3 * l)
    lam = (jnp.exp(jnp.sum(da_lq1[l].astype(f32) * da_lk1[l].astype(f32)))
           - jnp.exp(jnp.sum(da_lq2[l].astype(f32) * da_lk2[l].astype(f32))) + lam_init).reshape(1)
    lb = jnp.cumsum(jax.nn.softmax(hg_lb_logits.astype(f32), axis=0), axis=0)[l].reshape(1, -1)
    da_gain = (jnp.tile(da_out_norm[l].astype(f32), 2) * (1.0 - lam_init)).reshape(1, LANES)
    hg_gain = hg_out_norm[l].astype(f32).reshape(1, HG_DIM)
    g1 = norm1[l].astype(f32).reshape(1, d)
    g2 = norm2[l].astype(f32).reshape(1, d)
    gf = final_norm.astype(f32).reshape(1, d)
    w_in_bf = w_in[l].astype(bf16)
    wkv_t_bf = w_in[l][:, cache_k.shape[3] * cache_k.shape[4]:3 * cache_k.shape[3] * cache_k.shape[4]].T.astype(bf16)
    wo_bf = w_out[l].astype(bf16)
    wqt_bf = peer_w_q[l].T.astype(bf16)
    sk_bf = peer_sub_keys[l].reshape(-1, PEER_KEYS, PEER_SUB_DIM).astype(bf16)
    u_pk = _pack_table(peer_u[l])
    v_pk = _pack_table(peer_v[l])

    bias_tiles = _prompt_bias_tiles(rel_bias)
    q_pos = past + jnp.arange(dseq)
    bias_s = _bias_of(jnp.arange(past + dseq)[None, :] - q_pos[:, None], rel_bias) * LOG2E
    bias_past, bias_new = bias_s[:, :, :past], bias_s[:, :, past:]
    zero_state = jnp.zeros((1, hg_heads, HG_DIM, HG_DIM), f32)

    dh2 = 2 * DA_HEAD_DIM
    wq = da_heads * dh2
    tm = 256
    x_all = x_prompt.reshape(batch * seq, d)
    kt_all = jnp.zeros((batch * wq, seq), f32)
    vt_all = jnp.zeros((batch * wq, seq), f32)
    y_all = jnp.zeros((batch * seq, d), f32)
    nq = seq // ATT_BLOCK
    blocks = seq // tm
    cuts = {0: 4, 1: 2}
    segments = [(b, c * (nq // cuts.get(b, 1)), nq // cuts.get(b, 1))
                for b in range(batch) for c in range(cuts.get(b, 1))]
    assert all(nq % n == 0 and (nq // n * ATT_BLOCK) % (2 * tm) == 0 for n in cuts.values())
    s_p = []
    done = [jnp.zeros((8, LANES), f32)] * 3
    fronts, peers = [], []

    def finish(k, y_all):
        return _final(fronts[k][0][0], peers[k], gf, y_all, fronts[k][1], tm)

    for s, (b, q0, n_q) in enumerate(segments):
        if q0 == 0:
            q, kt, vt, hz, kt_all, vt_all = _inproj_prompt(x_all, g1, w_in_bf, wkv_t_bf, kt_all, vt_all, done[s],
                                                           b, seq, tm)
            ob, s_new = _hgrn(hz, lb, hg_gain, zero_state, 1, seq, CHUNK, 4)
            s_p.append(s_new)
        oa = _attn_prompt(q, kt, vt, bias_tiles, da_gain, lam, q0, n_q)
        if s >= 1:
            peers.append(_channel_mixer_mix(fronts[s - 1][0], oa, v_pk, tm))
        row0 = q0 * ATT_BLOCK // tm
        fronts.append((_channel_mixer_front(oa, ob, x_all, b * blocks + row0, row0, wo_bf, g2, wqt_bf, sk_bf, u_pk,
                                            tm), b * blocks + row0))
        if s >= 2:
            y_all, token = finish(s - 2, y_all)
            done.append(token)
    peers.append(_channel_mixer_mix(fronts[-1][0], fronts[-1][0][0], v_pk, tm))
    for k in (len(segments) - 2, len(segments) - 1):
        y_all, _ = finish(k, y_all)

    xs = x_sample.reshape(dbatch * dseq, d)
    q, k_s, v_s, hz = _inproj_sample(xs, g1, w_in_bf, wq)
    kp_t = jnp.transpose(cache_k[l], (0, 2, 3, 1)).reshape(dbatch * wq, past)
    vp_t = jnp.transpose(cache_v[l], (0, 2, 3, 1)).reshape(dbatch * wq, past)
    oa = _attn_sample(q, k_s, v_s, kp_t, vp_t, bias_past, bias_new, da_gain, lam, dbatch, dseq, past)
    ob, s_s = _hgrn(hz, lb, hg_gain, state_hgrn[l].astype(f32), dbatch, dseq, dseq, 1)
    front = _channel_mixer_front(oa, ob, xs, 0, 0, wo_bf, g2, wqt_bf, sk_bf, u_pk, dbatch * dseq)
    peer = _channel_mixer_mix(front, front[0], v_pk, dbatch * dseq)
    y_s, _ = _final(front[0], peer, gf, jnp.zeros((dbatch * dseq, d), f32), 0, dbatch * dseq)

    y_prompt = y_all.reshape(batch, seq, d)
    y_sample = y_s.reshape(dbatch, dseq, d)
    k_prompt = jnp.transpose(kt_all.reshape(batch, da_heads, dh2, seq), (0, 3, 1, 2))[None]
    v_prompt = jnp.transpose(vt_all.reshape(batch, da_heads, DA_V_DIM, seq), (0, 3, 1, 2))[None]
    state_prompt = jnp.concatenate(s_p, axis=0)[None].astype(state_hgrn.dtype)
    k_sample = k_s.reshape(1, dbatch, dseq, da_heads, dh2)
    v_sample = v_s.reshape(1, dbatch, dseq, da_heads, DA_V_DIM)
    state_sample = s_s[None].astype(state_hgrn.dtype)
    return (y_prompt, y_sample, k_prompt, v_prompt, state_prompt, k_sample, v_sample, state_sample)
```

```python
import functools
import math

import jax
import jax.numpy as jnp
from jax import lax
from jax.experimental import pallas as pl
from jax.experimental.pallas import tpu as pltpu
from jax.experimental.pallas import tpu_sc as plsc

CHUNK = 64
DA_HEAD_DIM = 32
DA_V_DIM = 2 * DA_HEAD_DIM
HG_DIM = 128
REL_BUCKETS = 32
REL_MAX_DIST = 128
PEER_KEYS = 128
PEER_TOPK = 16
PEER_SUB_DIM = 64
EPS = 1e-6
NEG = -1e30
LOG2E = 1.4426950408889634

LANES = 128
ATT_BLOCK = 128
ATT_STEP = 512
HG_SUB = 16
V7X_VMEM_BYTES = 64 * 1024 * 1024
VMEM_LIMIT = V7X_VMEM_BYTES * 7 // 8

f32 = jnp.float32
bf16 = jnp.bfloat16


def _cparams(*sem):
    return pltpu.CompilerParams(dimension_semantics=sem, vmem_limit_bytes=VMEM_LIMIT)


def _inproj_prompt_kernel(x_ref, g_ref, w_ref, wkv_t_ref, kin_ref, vin_ref, after_ref,
                          q_ref, kt_ref, vt_ref, hz_ref, ktf_ref, vtf_ref):
    del kin_ref, vin_ref
    del after_ref
    x = x_ref[...]
    xn = x * lax.rsqrt(jnp.mean(x * x, axis=-1, keepdims=True) + EPS) * g_ref[...]
    xb = xn.astype(bf16)
    wq = q_ref.shape[1]
    nt = (((1,), (1,)), ((), ()))
    q_ref[...] = jnp.dot(xb, w_ref[:, 0:wq], preferred_element_type=f32).astype(bf16)
    kt = lax.dot_general(wkv_t_ref[0:wq, :], xb, nt, preferred_element_type=f32)
    ktf_ref[...] = kt
    kt_ref[...] = kt.astype(bf16)
    vt = lax.dot_general(wkv_t_ref[wq:2 * wq, :], xb, nt, preferred_element_type=f32)
    vtf_ref[...] = vt
    vt_ref[...] = vt.astype(bf16)
    hz_ref[...] = jnp.dot(xb, w_ref[:, 3 * wq:], preferred_element_type=f32)


def _inproj_prompt(x, g, w_bf, wkv_t_bf, kt_all, vt_all, after, b, seq, tm):
    d = x.shape[1]
    e = w_bf.shape[1]
    wq = wkv_t_bf.shape[0] // 2
    steps = seq // tm
    return pl.pallas_call(
        _inproj_prompt_kernel,
        grid=(steps,),
        in_specs=[
            pl.BlockSpec((tm, d), lambda i: (b * steps + i, 0)),
            pl.BlockSpec((1, d), lambda i: (0, 0)),
            pl.BlockSpec((d, e), lambda i: (0, 0)),
            pl.BlockSpec((2 * wq, d), lambda i: (0, 0)),
            pl.BlockSpec(memory_space=pl.ANY),
            pl.BlockSpec(memory_space=pl.ANY),
            pl.BlockSpec(memory_space=pl.ANY),
        ],
        out_specs=[
            pl.BlockSpec((tm, wq), lambda i: (i, 0)),
            pl.BlockSpec((wq, tm), lambda i: (0, i)),
            pl.BlockSpec((wq, tm), lambda i: (0, i)),
            pl.BlockSpec((tm, e - 3 * wq), lambda i: (i, 0)),
            pl.BlockSpec((wq, tm), lambda i: (b, i)),
            pl.BlockSpec((wq, tm), lambda i: (b, i)),
        ],
        out_shape=[
            jax.ShapeDtypeStruct((seq, wq), bf16),
            jax.ShapeDtypeStruct((wq, seq), bf16),
            jax.ShapeDtypeStruct((wq, seq), bf16),
            jax.ShapeDtypeStruct((seq, e - 3 * wq), f32),
            jax.ShapeDtypeStruct(kt_all.shape, f32),
            jax.ShapeDtypeStruct(vt_all.shape, f32),
        ],
        input_output_aliases={4: 4, 5: 5},
        compiler_params=_cparams("parallel"),
        name="inproj",
    )(x, g, w_bf, wkv_t_bf, kt_all, vt_all, after)


def _inproj_sample_kernel(x_ref, g_ref, w_ref, q_ref, k_ref, v_ref, hz_ref):
    x = x_ref[...]
    xn = x * lax.rsqrt(jnp.mean(x * x, axis=-1, keepdims=True) + EPS) * g_ref[...]
    xb = xn.astype(bf16)
    wq = q_ref.shape[1]
    q_ref[...] = jnp.dot(xb, w_ref[:, 0:wq], preferred_element_type=f32).astype(bf16)
    k_ref[...] = jnp.dot(xb, w_ref[:, wq:2 * wq], preferred_element_type=f32)
    v_ref[...] = jnp.dot(xb, w_ref[:, 2 * wq:3 * wq], preferred_element_type=f32)
    hz_ref[...] = jnp.dot(xb, w_ref[:, 3 * wq:], preferred_element_type=f32)


def _inproj_sample(x, g, w_bf, wq):
    t, d = x.shape
    e = w_bf.shape[1]
    return pl.pallas_call(
        _inproj_sample_kernel,
        grid=(1,),
        in_specs=[
            pl.BlockSpec((t, d), lambda i: (0, 0)),
            pl.BlockSpec((1, d), lambda i: (0, 0)),
            pl.BlockSpec((d, e), lambda i: (0, 0)),
        ],
        out_specs=[
            pl.BlockSpec((t, wq), lambda i: (0, 0)),
            pl.BlockSpec((t, wq), lambda i: (0, 0)),
            pl.BlockSpec((t, wq), lambda i: (0, 0)),
            pl.BlockSpec((t, e - 3 * wq), lambda i: (0, 0)),
        ],
        out_shape=[
            jax.ShapeDtypeStruct((t, wq), bf16),
            jax.ShapeDtypeStruct((t, wq), f32),
            jax.ShapeDtypeStruct((t, wq), f32),
            jax.ShapeDtypeStruct((t, e - 3 * wq), f32),
        ],
        compiler_params=_cparams("arbitrary"),
        name="inproj_sample",
    )(x, g, w_bf)


def _rel_bucket(rel):
    nb = REL_BUCKETS // 2
    max_exact = nb // 2
    ret = jnp.where(rel > 0, nb, 0)
    n = jnp.abs(rel)
    nf = jnp.maximum(n, 1).astype(f32)
    large = max_exact + (jnp.log(nf / max_exact) / math.log(REL_MAX_DIST / max_exact)
                         * (nb - max_exact)).astype(jnp.int32)
    large = jnp.minimum(large, nb - 1)
    return ret + jnp.where(n < max_exact, n, large)


def _bias_of(rel, rel_bias):
    bucket = _rel_bucket(rel)
    out = jnp.zeros((rel_bias.shape[1],) + rel.shape, f32)
    for b in range(REL_BUCKETS):
        out = out + jnp.where(bucket[None] == b, rel_bias[b].astype(f32)[:, None, None], 0.0)
    return out


def _attn_prompt_kernel(lam_ref, q_ref, kt_ref, vt_ref, bias_ref, gain_ref, o_ref, m_sc, acc_sc, sa_sc, sb_sc,
                        *, q_block0):
    qb = pl.program_id(1) + q_block0
    tq = q_ref.shape[0]
    seq = kt_ref.shape[1]
    per_step = ATT_STEP // ATT_BLOCK
    lane = lax.broadcasted_iota(jnp.int32, (tq, LANES), 1)
    feat_k = lax.broadcasted_iota(jnp.int32, (LANES, ATT_STEP), 0)
    qs = q_ref[...].astype(f32) * (DA_HEAD_DIM ** -0.5 * LOG2E)
    q4 = jnp.concatenate(
        [jnp.where((lane // DA_HEAD_DIM) == i, qs, 0.0).astype(bf16) for i in range(4)], axis=0)

    m_sc[...] = jnp.full(m_sc.shape, -jnp.inf, f32)
    acc_sc[...] = jnp.zeros(acc_sc.shape, f32)

    n_steps = qb // per_step + 1
    last = n_steps - 1
    r = qb % per_step

    def scores(k):
        start = pl.multiple_of(jnp.minimum(k * ATT_STEP, seq - ATT_STEP), ATT_STEP)
        return jnp.dot(q4, kt_ref[:, pl.ds(start, ATT_STEP)], preferred_element_type=f32)

    def half(k, cur_ref, nxt_ref):
        nxt_ref[...] = scores(k + 1)
        tile = jnp.where(k == last, 1 + r, jnp.where((k == last - 1) & (r == 0), per_step + 1, 0))
        vv = vt_ref[:, pl.ds(pl.multiple_of(k * ATT_STEP, ATT_STEP), ATT_STEP)]
        for hh in range(2):
            bias = bias_ref[0, tile, hh * tq:(hh + 1) * tq, :]
            ps, alphas = [], []
            for c in range(2):
                rows = slice((2 * hh + c) * tq, (2 * hh + c + 1) * tq)
                s = cur_ref[rows, :] + bias
                m_old = m_sc[rows]
                m_new = jnp.maximum(m_old, jnp.max(s, axis=-1, keepdims=True))
                ps.append(jnp.exp2(s - m_new[:, 0:1]).astype(bf16))
                alphas.append(jnp.exp2(m_old - m_new))
                m_sc[rows] = m_new
            rows2 = slice(2 * hh * tq, (2 * hh + 2) * tq)
            vaug = jnp.where((feat_k // DA_V_DIM) == hh, vv, jnp.ones_like(vv))
            acc_sc[rows2] = (acc_sc[rows2] * jnp.concatenate(alphas, axis=0)
                             + lax.dot_general(jnp.concatenate(ps, axis=0), vaug, (((1,), (1,)), ((), ())),
                                               preferred_element_type=f32))

    sa_sc[...] = scores(0)

    def two_steps(kk, carry):
        half(2 * kk, sa_sc, sb_sc)

        @pl.when(2 * kk + 1 < n_steps)
        def _():
            half(2 * kk + 1, sb_sc, sa_sc)

        return carry

    lax.fori_loop(0, (n_steps + 1) // 2, two_steps, 0)

    lam = lam_ref[0]
    outs = []
    for hh in range(2):
        own = (lane // DA_V_DIM) == hh
        a1 = acc_sc[2 * hh * tq:(2 * hh + 1) * tq]
        a2 = acc_sc[(2 * hh + 1) * tq:(2 * hh + 2) * tq]
        l1 = jnp.max(jnp.where(own, 0.0, a1), axis=-1, keepdims=True)
        l2 = jnp.max(jnp.where(own, 0.0, a2), axis=-1, keepdims=True)
        o = a1 / l1 - lam * (a2 / l2)
        ssq = jnp.sum(jnp.where(own, o * o, 0.0), axis=-1, keepdims=True)
        outs.append(o * lax.rsqrt(ssq * (1.0 / DA_V_DIM) + EPS))
    o = jnp.where((lane // DA_V_DIM) == 0, outs[0], outs[1])
    o_ref[...] = (o * gain_ref[...]).astype(o_ref.dtype)


def _attn_prompt(q, kt, vt, bias_tiles, gain, lam, q_block0, n_q):
    seq, w = q.shape
    pairs = w // LANES
    assert seq % ATT_STEP == 0
    n_tiles = ATT_STEP // ATT_BLOCK + 2
    return pl.pallas_call(
        functools.partial(_attn_prompt_kernel, q_block0=q_block0),
        grid=(pairs, n_q),
        in_specs=[
            pl.BlockSpec(memory_space=pltpu.SMEM),
            pl.BlockSpec((ATT_BLOCK, LANES), lambda p, i: (q_block0 + i, p)),
            pl.BlockSpec((LANES, seq), lambda p, i: (p, 0)),
            pl.BlockSpec((LANES, seq), lambda p, i: (p, 0)),
            pl.BlockSpec((1, n_tiles, 2 * ATT_BLOCK, ATT_STEP), lambda p, i: (p, 0, 0, 0)),
            pl.BlockSpec((1, LANES), lambda p, i: (0, 0)),
        ],
        out_specs=pl.BlockSpec((ATT_BLOCK, LANES), lambda p, i: (i, p)),
        out_shape=jax.ShapeDtypeStruct((n_q * ATT_BLOCK, w), bf16),
        scratch_shapes=[
            pltpu.VMEM((4 * ATT_BLOCK, LANES), f32),
            pltpu.VMEM((4 * ATT_BLOCK, LANES), f32),
            pltpu.VMEM((4 * ATT_BLOCK, ATT_STEP), f32),
            pltpu.VMEM((4 * ATT_BLOCK, ATT_STEP), f32),
        ],
        compiler_params=_cparams("parallel", "arbitrary"),
        name="attn_prompt",
    )(lam, q, kt, vt, bias_tiles, gain)


def _prompt_bias_tiles(rel_bias):
    h = rel_bias.shape[1]
    per_step = ATT_STEP // ATT_BLOCK
    i = jnp.arange(ATT_BLOCK)
    rel_diag = i[None, :] - i[:, None]
    rel_prev = rel_diag - ATT_BLOCK
    far = rel_bias[REL_BUCKETS // 2 - 1].astype(f32)
    b_diag = (_bias_of(rel_diag, rel_bias) - far[:, None, None]) * LOG2E
    b_prev = (_bias_of(rel_prev, rel_bias) - far[:, None, None]) * LOG2E
    mask = (i[None, :] // CHUNK) <= (i[:, None] // CHUNK)
    b_diag = jnp.where(mask[None], b_diag, NEG)
    zero = jnp.zeros_like(b_diag)
    dead = jnp.full_like(b_diag, NEG)
    tiles = [jnp.concatenate([zero] * per_step, axis=-1)]
    for r in range(per_step):
        blocks = [zero if j < r - 1 else b_prev if j == r - 1 else b_diag if j == r else dead
                  for j in range(per_step)]
        tiles.append(jnp.concatenate(blocks, axis=-1))
    tiles.append(jnp.concatenate([zero] * (per_step - 1) + [b_prev], axis=-1))
    tiles = jnp.stack(tiles, axis=1)
    tiles = tiles.reshape(h // 2, 2, per_step + 2, ATT_BLOCK, ATT_STEP)
    return jnp.transpose(tiles, (0, 2, 1, 3, 4)).reshape(h // 2, per_step + 2, 2 * ATT_BLOCK, ATT_STEP)


def _attn_sample_kernel(lam_ref, q_ref, kp_ref, vp_ref, kn_ref, vn_ref, bp_ref, bn_ref, gain_ref, o_ref):
    nq = q_ref.shape[0]
    pairs = q_ref.shape[1] // LANES
    lane = lax.broadcasted_iota(jnp.int32, (nq, LANES), 1)
    lam = lam_ref[0]
    nt = (((1,), (1,)), ((), ()))
    for p in range(pairs):
        sl = slice(p * LANES, (p + 1) * LANES)
        qs = q_ref[:, sl].astype(f32) * (DA_HEAD_DIM ** -0.5 * LOG2E)
        kp = kp_ref[sl, :].astype(bf16)
        vp = vp_ref[sl, :].astype(bf16)
        kn = kn_ref[:, sl].astype(bf16)
        vn = vn_ref[:, sl].astype(bf16)
        outs = []
        for hh in range(2):
            h = 2 * p + hh
            own = (lane // DA_V_DIM) == hh
            res = []
            for c in range(2):
                qm = jnp.where((lane // DA_HEAD_DIM) == 2 * hh + c, qs, 0.0).astype(bf16)
                sp = jnp.dot(qm, kp, preferred_element_type=f32) + bp_ref[h]
                sn = lax.dot_general(qm, kn, nt, preferred_element_type=f32) + bn_ref[h]
                m = jnp.maximum(jnp.max(sp, axis=-1, keepdims=True), jnp.max(sn, axis=-1, keepdims=True))
                pp = jnp.exp2(sp - m)
                pn = jnp.exp2(sn - m)
                l = jnp.sum(pp, axis=-1, keepdims=True) + jnp.sum(pn, axis=-1, keepdims=True)
                pv = (lax.dot_general(pp.astype(bf16), vp, nt, preferred_element_type=f32)
                      + jnp.dot(pn.astype(bf16), vn, preferred_element_type=f32))
                res.append(pv / l)
            o = res[0] - lam * res[1]
            ssq = jnp.sum(jnp.where(own, o * o, 0.0), axis=-1, keepdims=True)
            outs.append(o * lax.rsqrt(ssq * (1.0 / DA_V_DIM) + EPS))
        o = jnp.where((lane // DA_V_DIM) == 0, outs[0], outs[1])
        o_ref[:, sl] = (o * gain_ref[...]).astype(o_ref.dtype)


def _attn_sample(q, k_new, v_new, k_past, v_past, bias_past, bias_new, gain, lam, batch, nq, past):
    w = q.shape[1]
    h = bias_past.shape[0]
    return pl.pallas_call(
        _attn_sample_kernel,
        grid=(batch,),
        in_specs=[
            pl.BlockSpec(memory_space=pltpu.SMEM),
            pl.BlockSpec((nq, w), lambda b: (b, 0)),
            pl.BlockSpec((w, past), lambda b: (b, 0)),
            pl.BlockSpec((w, past), lambda b: (b, 0)),
            pl.BlockSpec((nq, w), lambda b: (b, 0)),
            pl.BlockSpec((nq, w), lambda b: (b, 0)),
            pl.BlockSpec((h, nq, past), lambda b: (0, 0, 0)),
            pl.BlockSpec((h, nq, nq), lambda b: (0, 0, 0)),
            pl.BlockSpec((1, LANES), lambda b: (0, 0)),
        ],
        out_specs=pl.BlockSpec((nq, w), lambda b: (b, 0)),
        out_shape=jax.ShapeDtypeStruct((batch * nq, w), bf16),
        compiler_params=_cparams("parallel"),
        name="attn_sample",
    )(lam, q, k_past, v_past, k_new, v_new, bias_past, bias_new, gain)


def _hgrn_chunk(q, f_logit, vv, g_logit, lb, gain, s0, lc):
    n_sub = lc // HG_SUB
    f = lb + (1.0 - lb) * jax.nn.sigmoid(f_logit)
    logf = jnp.log(f)
    k = 1.0 - f
    row = lax.broadcasted_iota(jnp.int32, (lc, lc), 0)
    col = lax.broadcasted_iota(jnp.int32, (lc, lc), 1)
    tril = (col <= row).astype(f32)
    b = jnp.dot(tril, logf, precision=lax.Precision.HIGHEST, preferred_element_type=f32)
    b_last = b[lc - 1:lc, :]

    o = jnp.dot((q * jnp.exp(b)).astype(bf16), s0.astype(bf16), preferred_element_type=f32)
    k_hat = k * jnp.exp(b_last - b)
    bt = b.T
    decay_col = jnp.exp(bt[:, lc - 1:lc])
    s_new = decay_col * s0 + jnp.dot(k_hat.T.astype(bf16), vv.astype(bf16), preferred_element_type=f32)

    if n_sub > 1:
        rows = []
        nt = (((1,), (1,)), ((), ()))
        for i in range(n_sub):
            lo = i * HG_SUB
            if i == 0:
                rows.append(jnp.zeros((HG_SUB, lc), f32))
                continue
            beta = b[lo - 1:lo, :]
            q_t = q[lo:lo + HG_SUB, :] * jnp.exp(b[lo:lo + HG_SUB, :] - beta)
            k_t = k * jnp.exp(jnp.minimum(beta - b, 0.0))
            rows.append(lax.dot_general(q_t.astype(bf16), k_t.astype(bf16), nt, preferred_element_type=f32))
        a_off = jnp.concatenate(rows, axis=0)
        a_off = jnp.where(col < (row // HG_SUB) * HG_SUB, a_off, 0.0)
        o = o + jnp.dot(a_off.astype(bf16), vv.astype(bf16), preferred_element_type=f32)

    ones = jnp.ones((HG_DIM, LANES), bf16)
    srow = lax.broadcasted_iota(jnp.int32, (HG_SUB, HG_DIM), 0)
    orow = lax.broadcasted_iota(jnp.int32, (HG_SUB, HG_DIM), 0)
    blocks = []
    for i in range(n_sub):
        lo = i * HG_SUB
        b_i = b[lo:lo + HG_SUB, :]
        k_i = k[lo:lo + HG_SUB, :]
        v_i = vv[lo:lo + HG_SUB, :]
        q_i = q[lo:lo + HG_SUB, :]
        d_rows = []
        for t in range(HG_SUB):
            arg = jnp.where(srow <= t, b_i[t:t + 1, :] - b_i, -jnp.inf)
            d_rows.append(q_i[t:t + 1, :] * k_i * jnp.exp(arg))
        d3 = jnp.concatenate(d_rows, axis=0)
        a_rep = jnp.dot(d3.astype(bf16), ones, preferred_element_type=f32)
        o_i = jnp.zeros((HG_SUB, HG_DIM), f32)
        for t in range(HG_SUB):
            o_t = jnp.sum(a_rep[t * HG_SUB:(t + 1) * HG_SUB, :] * v_i, axis=0, keepdims=True)
            o_i = jnp.where(orow == t, o_t, o_i)
        blocks.append(o_i)
    o = o + (jnp.concatenate(blocks, axis=0) if n_sub > 1 else blocks[0])

    on = o * lax.rsqrt(jnp.mean(o * o, axis=-1, keepdims=True) + EPS) * gain
    gate = g_logit * jax.nn.sigmoid(g_logit)
    return on * gate, s_new


def _hgrn_kernel(hz_ref, lb_ref, gain_ref, s0_ref, o_ref, s_ref, st_sc, *, lc, heads):
    j = pl.program_id(1)
    w = heads * HG_DIM

    @pl.when(j == 0)
    def _():
        st_sc[...] = s0_ref[0]

    n_chunks = hz_ref.shape[0] // lc
    for c in range(n_chunks):
        rs = slice(c * lc, (c + 1) * lc)
        for h in range(heads):
            cs = slice(h * HG_DIM, (h + 1) * HG_DIM)
            o, s_new = _hgrn_chunk(
                hz_ref[rs, h * HG_DIM:(h + 1) * HG_DIM],
                hz_ref[rs, w + h * HG_DIM:w + (h + 1) * HG_DIM],
                hz_ref[rs, 2 * w + h * HG_DIM:2 * w + (h + 1) * HG_DIM],
                hz_ref[rs, 3 * w + h * HG_DIM:3 * w + (h + 1) * HG_DIM],
                lb_ref[:, cs], gain_ref[...], st_sc[h], lc)
            st_sc[h] = s_new
            o_ref[rs, cs] = o.astype(o_ref.dtype)

    @pl.when(j == pl.num_programs(1) - 1)
    def _():
        s_ref[0] = st_sc[...]


def _hgrn(hz, lb, gain, s0, batch, seq, lc, chunks_per_step):
    t = hz.shape[0]
    w = hz.shape[1] // 4
    heads = w // HG_DIM
    rows = lc * chunks_per_step
    steps = seq // rows
    return pl.pallas_call(
        functools.partial(_hgrn_kernel, lc=lc, heads=heads),
        grid=(batch, steps),
        in_specs=[
            pl.BlockSpec((rows, 4 * w), lambda b, j: (b * steps + j, 0)),
            pl.BlockSpec((1, w), lambda b, j: (0, 0)),
            pl.BlockSpec((1, HG_DIM), lambda b, j: (0, 0)),
            pl.BlockSpec((1, heads, HG_DIM, HG_DIM), lambda b, j: (b, 0, 0, 0)),
        ],
        out_specs=[
            pl.BlockSpec((rows, w), lambda b, j: (b * steps + j, 0)),
            pl.BlockSpec((1, heads, HG_DIM, HG_DIM), lambda b, j: (b, 0, 0, 0)),
        ],
        out_shape=[
            jax.ShapeDtypeStruct((t, w), bf16),
            jax.ShapeDtypeStruct((batch, heads, HG_DIM, HG_DIM), f32),
        ],
        scratch_shapes=[pltpu.VMEM((heads, HG_DIM, HG_DIM), f32)],
        compiler_params=_cparams("parallel", "arbitrary"),
        name="hgrn2",
    )(hz, lb, gain, s0)


def _top_rows(vals, k, payload=None):
    n_rows = vals.shape[0]
    row = lax.broadcasted_iota(jnp.int32, vals.shape, 0)
    out_v, out_i = [], []
    for _ in range(k):
        m = jnp.max(vals, axis=0, keepdims=True)
        first = jnp.min(jnp.where(vals == m, row, n_rows), axis=0, keepdims=True)
        hit = row == first
        out_v.append(m)
        if payload is None:
            out_i.append(first)
        else:
            out_i.append(jnp.max(jnp.where(hit, payload, -1), axis=0, keepdims=True))
        vals = jnp.where(hit, -jnp.inf, vals)
    return jnp.concatenate(out_v, axis=0), jnp.concatenate(out_i, axis=0)


def _retrieve_kernel(oa_ref, ob_ref, x_ref, wo_ref, g_ref, wqt_ref, sk_ref,
                     hp_ref, xn_ref, idx_ref, gate_ref, qt_sc, s1v_sc, s1i_sc, idx_sc, gate_sc):
    half = oa_ref.shape[1]
    y = (jnp.dot(oa_ref[...], wo_ref[0:half, :], preferred_element_type=f32)
         + jnp.dot(ob_ref[...], wo_ref[half:, :], preferred_element_type=f32))
    hp = x_ref[...] + y
    hp_ref[...] = hp
    xn = hp * lax.rsqrt(jnp.mean(hp * hp, axis=-1, keepdims=True) + EPS) * g_ref[...]
    xn_ref[...] = xn
    nt = (((1,), (1,)), ((), ()))
    qt_sc[...] = lax.dot_general(wqt_ref[...], xn.astype(bf16), nt, preferred_element_type=f32).astype(bf16)

    n_half = sk_ref.shape[0]

    def stage1(hc, carry):
        start = pl.multiple_of(hc * PEER_SUB_DIM, PEER_SUB_DIM)
        s = jnp.dot(sk_ref[hc], qt_sc[pl.ds(start, PEER_SUB_DIM), :], preferred_element_type=f32)
        v, i = _top_rows(s, PEER_TOPK)
        s1v_sc[hc] = v
        s1i_sc[hc] = i
        return carry

    lax.fori_loop(0, n_half, stage1, 0)

    tm = x_ref.shape[0]
    sub = lax.broadcasted_iota(jnp.int32, (8, tm), 0)

    def stage2(h, carry):
        v1, i1 = s1v_sc[2 * h], s1i_sc[2 * h]
        v2, i2 = s1v_sc[2 * h + 1], s1i_sc[2 * h + 1]
        cv, ci = [], []
        for i in range(PEER_TOPK):
            nj = PEER_TOPK // (i + 1)
            rows = PEER_TOPK if nj > 8 else 8
            a = v1[i:i + 1, :] + v2[0:rows, :]
            e = i1[i:i + 1, :] * PEER_KEYS + i2[0:rows, :]
            if nj < rows:
                a = jnp.where(sub < nj, a, -jnp.inf)
            cv.append(a)
            ci.append(e)
        top_s, eidx = _top_rows(jnp.concatenate(cv, axis=0), PEER_TOPK, jnp.concatenate(ci, axis=0))
        p = jnp.exp(top_s - top_s[0:1, :])
        g = p / jnp.sum(p, axis=0, keepdims=True)
        r0 = pl.multiple_of(h * PEER_TOPK, PEER_TOPK)
        gate_sc[pl.ds(r0, PEER_TOPK), :] = g
        idx_sc[pl.ds(r0, PEER_TOPK), :] = eidx
        return carry

    lax.fori_loop(0, n_half // 2, stage2, 0)
    idx_ref[...] = pltpu.bitcast(pltpu.bitcast(idx_sc[...], f32).T, jnp.int32)
    gate_ref[...] = gate_sc[...].T


def _retrieve(oa, ob, x, wo_bf, g2, wqt_bf, sk_bf, tm, x_block0=0, ob_block0=0):
    t, half = oa.shape
    d = x.shape[1]
    e = wqt_bf.shape[0]
    n_half = sk_bf.shape[0]
    n_sel = (n_half // 2) * PEER_TOPK
    return pl.pallas_call(
        _retrieve_kernel,
        grid=(t // tm,),
        in_specs=[
            pl.BlockSpec((tm, half), lambda i: (i, 0)),
            pl.BlockSpec((tm, half), lambda i: (ob_block0 + i, 0)),
            pl.BlockSpec((tm, d), lambda i: (x_block0 + i, 0)),
            pl.BlockSpec((2 * half, d), lambda i: (0, 0)),
            pl.BlockSpec((1, d), lambda i: (0, 0)),
            pl.BlockSpec((e, d), lambda i: (0, 0)),
            pl.BlockSpec((n_half, PEER_KEYS, PEER_SUB_DIM), lambda i: (0, 0, 0)),
        ],
        out_specs=[
            pl.BlockSpec((tm, d), lambda i: (i, 0)),
            pl.BlockSpec((tm, d), lambda i: (i, 0)),
            pl.BlockSpec((tm, n_sel), lambda i: (i, 0)),
            pl.BlockSpec((tm, n_sel), lambda i: (i, 0)),
        ],
        out_shape=[
            jax.ShapeDtypeStruct((t, d), f32),
            jax.ShapeDtypeStruct((t, d), f32),
            jax.ShapeDtypeStruct((t, n_sel), jnp.int32),
            jax.ShapeDtypeStruct((t, n_sel), f32),
        ],
        scratch_shapes=[
            pltpu.VMEM((e, tm), bf16),
            pltpu.VMEM((n_half, PEER_TOPK, tm), f32),
            pltpu.VMEM((n_half, PEER_TOPK, tm), jnp.int32),
            pltpu.VMEM((n_sel, tm), jnp.int32),
            pltpu.VMEM((n_sel, tm), f32),
        ],
        compiler_params=_cparams("parallel"),
        name="retrieve",
    )(oa, ob, x, wo_bf, g2, wqt_bf, sk_bf)


def _pack_table(tab):
    half = tab.shape[1] // 2
    lo = lax.bitcast_convert_type(tab[:, :half].astype(bf16), jnp.uint16).astype(jnp.uint32)
    hi = lax.bitcast_convert_type(tab[:, half:].astype(bf16), jnp.uint16).astype(jnp.uint32)
    return lax.bitcast_convert_type(lo | (hi << 16), jnp.int32)


def _sc_layout(n_tok, n_sel, words):
    info = plsc.get_sparse_core_info()
    n_workers = info.num_cores * info.num_subcores
    lanes = info.num_lanes
    block = 8 * lanes
    assert n_tok % (2 * n_workers) == 0 and words % block == 0 and (n_sel // 2) % lanes == 0
    return info, lanes, n_tok // n_workers, n_sel // 2, block


def _selected_dots(table, idx, x):
    n_tok, n_sel = idx.shape
    words = table.shape[1]
    info, lanes, tpw, half_rows, block = _sc_layout(n_tok, n_sel, words)
    idx2 = idx.reshape(2 * n_tok, half_rows)
    mesh = plsc.VectorSubcoreMesh(core_axis_name="c", subcore_axis_name="s")

    @functools.partial(
        pl.kernel, out_type=jax.ShapeDtypeStruct((n_tok, n_sel), f32), mesh=mesh,
        compiler_params=pltpu.CompilerParams(needs_layout_passes=False),
        scratch_types=[
            pltpu.VMEM((2 * tpw, half_rows), jnp.int32),
            pltpu.VMEM((half_rows, words), jnp.int32),
            pltpu.VMEM((half_rows, words), jnp.int32),
            pltpu.VMEM((2 * words,), f32),
            pltpu.VMEM((2 * words,), f32),
            pltpu.VMEM((n_sel // 8, 8 * lanes), f32),
            pltpu.VMEM((tpw, n_sel), f32),
            pltpu.SemaphoreType.DMA, pltpu.SemaphoreType.DMA, pltpu.SemaphoreType.DMA, pltpu.SemaphoreType.DMA,
        ])
    def dots(t_hbm, i_hbm, x_hbm, o_hbm, idx_v, rows_a, rows_b, x_a, x_b, part_v, act_v, sem_a, sem_b, sem_xa, sem_xb):
        wid = lax.axis_index("s") * info.num_cores + lax.axis_index("c")
        base = wid * tpw
        pltpu.sync_copy(i_hbm.at[pl.ds(base * 2, 2 * tpw)], idx_v)

        def accumulate(rows, x_v, half):
            @pl.loop(0, words // block)
            def _(cb):
                xs = [x_v[pl.ds(p * words + cb * block + c * lanes, lanes)] for c in range(8) for p in range(2)]

                @plsc.parallel_loop(0, half_rows)
                def _(row):
                    packed = [rows[row, pl.ds(cb * block + c * lanes, lanes)] for c in range(8)]
                    unp = [plsc.unpack(plsc.bitcast(p, bf16), format=plsc.PackFormat.INTERLEAVED) for p in packed]
                    prods = [unp[c][p] * xs[2 * c + p] for c in range(8) for p in range(2)]
                    while len(prods) > 1:
                        prods = [prods[k] + prods[k + 1] for k in range(0, len(prods), 2)]
                    prow = half * half_rows + row
                    plsc.addupdate(part_v.at[prow // 8, pl.ds((prow % 8) * lanes, lanes)], prods[0])

        def gather(half_index, rows, sem):
            return pltpu.make_async_copy(t_hbm.at[idx_v.at[half_index]], rows, sem)

        def xcopy(t, x_v, sem):
            return pltpu.make_async_copy(x_hbm.at[base + t], x_v, sem)

        def token(t, x_v, sem_x, x_next, sem_xn):
            gather(2 * t + 1, rows_b, sem_b).start()

            @pl.when(t + 1 < tpw)
            def _():
                xcopy(t + 1, x_next, sem_xn).start()

            zero = jnp.zeros((lanes,), f32)
            for r in range(n_sel):
                part_v[r // 8, pl.ds((r % 8) * lanes, lanes)] = zero
            xcopy(t, x_v, sem_x).wait()
            gather(2 * t, rows_a, sem_a).wait()
            accumulate(rows_a, x_v, 0)

            @pl.when(t + 1 < tpw)
            def _():
                gather(2 * t + 2, rows_a, sem_a).start()

            gather(2 * t + 1, rows_b, sem_b).wait()
            accumulate(rows_b, x_v, 1)
            it = lax.iota(jnp.int32, lanes)
            for g in range(n_sel // lanes):
                prow = g * (lanes // 8) + it // 8
                pcol = (it % 8) * lanes
                tot = plsc.load_gather(part_v, [prow, pcol])
                for l in range(1, lanes):
                    tot = tot + plsc.load_gather(part_v, [prow, pcol + l])
                act_v[t, pl.ds(g * lanes, lanes)] = tot

        gather(0, rows_a, sem_a).start()
        xcopy(0, x_a, sem_xa).start()

        @pl.loop(0, tpw // 2)
        def _(i):
            token(2 * i, x_a, sem_xa, x_b, sem_xb)
            token(2 * i + 1, x_b, sem_xb, x_a, sem_xa)

        pltpu.sync_copy(act_v, o_hbm.at[pl.ds(base, tpw)])

    return dots(table, idx2, x)


def _gate_weights_kernel(act_ref, gate_ref, after_ref, w_ref):
    del after_ref
    a = act_ref[...]
    w_ref[...] = gate_ref[...] * (0.5 * a * (1.0 + lax.erf(a * (2.0 ** -0.5))))


def _gate_weights(act, gate, after, tm):
    t, n_sel = act.shape
    spec = pl.BlockSpec((tm, n_sel), lambda i: (i, 0))
    return pl.pallas_call(
        _gate_weights_kernel,
        grid=(t // tm,),
        in_specs=[spec, spec, pl.BlockSpec(memory_space=pl.ANY)],
        out_specs=spec,
        out_shape=jax.ShapeDtypeStruct((t, n_sel), f32),
        compiler_params=_cparams("parallel"),
        name="peer_weights",
    )(act, gate, after)


def _weighted_rows(table, idx, w):
    n_tok, n_sel = w.shape
    words = table.shape[1]
    info, lanes, tpw, half_rows, _ = _sc_layout(n_tok, n_sel, words)
    idx2 = idx.reshape(2 * n_tok, half_rows)
    mesh = plsc.VectorSubcoreMesh(core_axis_name="c", subcore_axis_name="s")

    @functools.partial(
        pl.kernel, out_type=jax.ShapeDtypeStruct((n_tok, 2 * words), f32), mesh=mesh,
        compiler_params=pltpu.CompilerParams(needs_layout_passes=False),
        scratch_types=[
            pltpu.VMEM((2 * tpw, half_rows), jnp.int32),
            pltpu.VMEM((tpw, n_sel), f32),
            pltpu.VMEM((half_rows, words), jnp.int32),
            pltpu.VMEM((half_rows, words), jnp.int32),
            pltpu.VMEM((2 * words,), f32),
            pltpu.VMEM((2 * words,), f32),
            pltpu.SemaphoreType.DMA, pltpu.SemaphoreType.DMA, pltpu.SemaphoreType.DMA, pltpu.SemaphoreType.DMA,
        ])
    def mix(t_hbm, i_hbm, w_hbm, o_hbm, idx_v, w_v, rows_a, rows_b, out_a, out_b, sem_a, sem_b, sem_oa, sem_ob):
        wid = lax.axis_index("s") * info.num_cores + lax.axis_index("c")
        base = wid * tpw
        pltpu.sync_copy(i_hbm.at[pl.ds(base * 2, 2 * tpw)], idx_v)
        pltpu.sync_copy(w_hbm.at[pl.ds(base, tpw)], w_v)

        def accumulate(rows, out_v, t, half, first):
            if first:
                zero = jnp.zeros((lanes,), f32)
                for c in range(2 * words // lanes):
                    out_v[pl.ds(c * lanes, lanes)] = zero

            @pl.loop(0, half_rows // lanes)
            def _(g):
                wv = w_v[t, pl.ds(half * half_rows + g * lanes, lanes)]
                ws = [wv.at[jnp.full((lanes,), r, jnp.int32)].get(mode="promise_in_bounds") for r in range(lanes)]

                @plsc.parallel_loop(0, words // lanes)
                def _(c):
                    packed = [rows[g * lanes + r, pl.ds(c * lanes, lanes)] for r in range(lanes)]
                    unp = [plsc.unpack(plsc.bitcast(p, bf16), format=plsc.PackFormat.INTERLEAVED) for p in packed]
                    lo = [unp[r][0] * ws[r] for r in range(lanes)]
                    hi = [unp[r][1] * ws[r] for r in range(lanes)]
                    while len(lo) > 1:
                        lo = [lo[k] + lo[k + 1] for k in range(0, len(lo), 2)]
                        hi = [hi[k] + hi[k + 1] for k in range(0, len(hi), 2)]
                    plsc.addupdate(out_v.at[pl.ds(c * lanes, lanes)], lo[0])
                    plsc.addupdate(out_v.at[pl.ds(words + c * lanes, lanes)], hi[0])

        def gather(half_index, rows, sem):
            return pltpu.make_async_copy(t_hbm.at[idx_v.at[half_index]], rows, sem)

        def token(t, out_v, sem_o, out_in_flight):
            gather(2 * t + 1, rows_b, sem_b).start()
            gather(2 * t, rows_a, sem_a).wait()

            @pl.when(out_in_flight)
            def _():
                pltpu.make_async_copy(out_v, o_hbm.at[base + t], sem_o).wait()

            accumulate(rows_a, out_v, t, 0, True)

            @pl.when(t + 1 < tpw)
            def _():
                gather(2 * t + 2, rows_a, sem_a).start()

            gather(2 * t + 1, rows_b, sem_b).wait()
            accumulate(rows_b, out_v, t, 1, False)
            pltpu.make_async_copy(out_v, o_hbm.at[base + t], sem_o).start()

        gather(0, rows_a, sem_a).start()

        @pl.loop(0, tpw // 2)
        def _(i):
            token(2 * i, out_a, sem_oa, i > 0)
            token(2 * i + 1, out_b, sem_ob, i > 0)

        pltpu.make_async_copy(out_a, o_hbm.at[base], sem_oa).wait()
        pltpu.make_async_copy(out_b, o_hbm.at[base], sem_ob).wait()

    return mix(table, idx2, w)


def _final_kernel(hp_ref, peer_ref, gf_ref, yin_ref, y_ref, done_ref):
    del yin_ref
    done_ref[...] = jnp.zeros(done_ref.shape, f32)
    h = hp_ref[...] + peer_ref[...]
    y_ref[...] = h * lax.rsqrt(jnp.mean(h * h, axis=-1, keepdims=True) + EPS) * gf_ref[...]


def _final(hp, peer, gf, y_all, y_block0, tm):
    t, d = hp.shape
    return pl.pallas_call(
        _final_kernel,
        grid=(t // tm,),
        in_specs=[
            pl.BlockSpec((tm, d), lambda i: (i, 0)),
            pl.BlockSpec((tm, d), lambda i: (i, 0)),
            pl.BlockSpec((1, d), lambda i: (0, 0)),
            pl.BlockSpec(memory_space=pl.ANY),
        ],
        out_specs=[pl.BlockSpec((tm, d), lambda i: (y_block0 + i, 0)),
                   pl.BlockSpec((8, LANES), lambda i: (0, 0))],
        out_shape=[jax.ShapeDtypeStruct(y_all.shape, f32), jax.ShapeDtypeStruct((8, LANES), f32)],
        input_output_aliases={3: 0},
        compiler_params=_cparams("arbitrary"),
        name="final_norm",
    )(hp, peer, gf, y_all)


def _channel_mixer_front(oa, ob, x, x_block0, ob_block0, wo_bf, g2, wqt_bf, sk_bf, u_pk, tm):
    hp, xn, eidx, gate = _retrieve(oa, ob, x, wo_bf, g2, wqt_bf, sk_bf, tm, x_block0, ob_block0)
    return hp, eidx, gate, _selected_dots(u_pk, eidx, xn)


def _channel_mixer_mix(front, after, v_pk, tm):
    _, eidx, gate, act = front
    return _weighted_rows(v_pk, eidx, _gate_weights(act, gate, after, tm))


def kernel(x_prompt, x_sample, cache_k, cache_v, state_hgrn, norm1, w_in, da_lq1, da_lk1, da_lq2, da_lk2,
           da_out_norm, hg_lb_logits, hg_out_norm, w_out, rel_bias, norm2, peer_w_q, peer_sub_keys,
           peer_u, peer_v, final_norm):
    batch, seq, d = x_prompt.shape
    dbatch, dseq, _ = x_sample.shape
    past = cache_k.shape[2]
    da_heads = cache_k.shape[3]
    hg_heads = state_hgrn.shape[2]
    depth = w_in.shape[0]
    assert depth == 1 and seq % ATT_BLOCK == 0 and seq % CHUNK == 0 and dseq % HG_SUB == 0 and dseq <= CHUNK
    assert past % CHUNK == 0 and (past + dseq - 1) // CHUNK == past // CHUNK

    l = 0
    lam_init = 0.8 - 0.6 * math.exp(-0.3 * l)
    lam = (jnp.exp(jnp.sum(da_lq1[l].astype(f32) * da_lk1[l].astype(f32)))
           - jnp.exp(jnp.sum(da_lq2[l].astype(f32) * da_lk2[l].astype(f32))) + lam_init).reshape(1)
    lb = jnp.cumsum(jax.nn.softmax(hg_lb_logits.astype(f32), axis=0), axis=0)[l].reshape(1, -1)
    da_gain = (jnp.tile(da_out_norm[l].astype(f32), 2) * (1.0 - lam_init)).reshape(1, LANES)
    hg_gain = hg_out_norm[l].astype(f32).reshape(1, HG_DIM)
    g1 = norm1[l].astype(f32).reshape(1, d)
    g2 = norm2[l].astype(f32).reshape(1, d)
    gf = final_norm.astype(f32).reshape(1, d)
    w_in_bf = w_in[l].astype(bf16)
    wkv_t_bf = w_in[l][:, cache_k.shape[3] * cache_k.shape[4]:3 * cache_k.shape[3] * cache_k.shape[4]].T.astype(bf16)
    wo_bf = w_out[l].astype(bf16)
    wqt_bf = peer_w_q[l].T.astype(bf16)
    sk_bf = peer_sub_keys[l].reshape(-1, PEER_KEYS, PEER_SUB_DIM).astype(bf16)
    u_pk = _pack_table(peer_u[l])
    v_pk = _pack_table(peer_v[l])

    bias_tiles = _prompt_bias_tiles(rel_bias)
    q_pos = past + jnp.arange(dseq)
    bias_s = _bias_of(jnp.arange(past + dseq)[None, :] - q_pos[:, None], rel_bias) * LOG2E
    bias_past, bias_new = bias_s[:, :, :past], bias_s[:, :, past:]
    zero_state = jnp.zeros((1, hg_heads, HG_DIM, HG_DIM), f32)

    dh2 = 2 * DA_HEAD_DIM
    wq = da_heads * dh2
    tm = 512
    x_all = x_prompt.reshape(batch * seq, d)
    kt_all = jnp.zeros((batch * wq, seq), f32)
    vt_all = jnp.zeros((batch * wq, seq), f32)
    y_all = jnp.zeros((batch * seq, d), f32)
    nq = seq // ATT_BLOCK
    blocks = seq // tm
    cuts = {0: 4, 1: 2}
    segments = [(b, c * (nq // cuts.get(b, 1)), nq // cuts.get(b, 1))
                for b in range(batch) for c in range(cuts.get(b, 1))]
    assert all(nq % n == 0 and (nq // n * ATT_BLOCK) % (2 * tm) == 0 for n in cuts.values())
    s_p = []
    done = [jnp.zeros((8, LANES), f32)] * 3
    fronts, peers = [], []

    def finish(k, y_all):
        return _final(fronts[k][0][0], peers[k], gf, y_all, fronts[k][1], tm)

    for s, (b, q0, n_q) in enumerate(segments):
        if q0 == 0:
            q, kt, vt, hz, kt_all, vt_all = _inproj_prompt(x_all, g1, w_in_bf, wkv_t_bf, kt_all, vt_all, done[s],
                                                           b, seq, tm)
            ob, s_new = _hgrn(hz, lb, hg_gain, zero_state, 1, seq, CHUNK, 4)
            s_p.append(s_new)
        oa = _attn_prompt(q, kt, vt, bias_tiles, da_gain, lam, q0, n_q)
        if s >= 1:
            peers.append(_channel_mixer_mix(fronts[s - 1][0], oa, v_pk, tm))
        row0 = q0 * ATT_BLOCK // tm
        fronts.append((_channel_mixer_front(oa, ob, x_all, b * blocks + row0, row0, wo_bf, g2, wqt_bf, sk_bf, u_pk,
                                            tm), b * blocks + row0))
        if s >= 2:
            y_all, token = finish(s - 2, y_all)
            done.append(token)
    peers.append(_channel_mixer_mix(fronts[-1][0], fronts[-1][0][0], v_pk, tm))
    for k in (len(segments) - 2, len(segments) - 1):
        y_all, _ = finish(k, y_all)

    xs = x_sample.reshape(dbatch * dseq, d)
    q, k_s, v_s, hz = _inproj_sample(xs, g1, w_in_bf, wq)
    kp_t = jnp.transpose(cache_k[l], (0, 2, 3, 1)).reshape(dbatch * wq, past)
    vp_t = jnp.transpose(cache_v[l], (0, 2, 3, 1)).reshape(dbatch * wq, past)
    oa = _attn_sample(q, k_s, v_s, kp_t, vp_t, bias_past, bias_new, da_gain, lam, dbatch, dseq, past)
    ob, s_s = _hgrn(hz, lb, hg_gain, state_hgrn[l].astype(f32), dbatch, dseq, dseq, 1)
    front = _channel_mixer_front(oa, ob, xs, 0, 0, wo_bf, g2, wqt_bf, sk_bf, u_pk, dbatch * dseq)
    peer = _channel_mixer_mix(front, front[0], v_pk, dbatch * dseq)
    y_s, _ = _final(front[0], peer, gf, jnp.zeros((dbatch * dseq, d), f32), 0, dbatch * dseq)

    y_prompt = y_all.reshape(batch, seq, d)
    y_sample = y_s.reshape(dbatch, dseq, d)
    k_prompt = jnp.transpose(kt_all.reshape(batch, da_heads, dh2, seq), (0, 3, 1, 2))[None]
    v_prompt = jnp.transpose(vt_all.reshape(batch, da_heads, DA_V_DIM, seq), (0, 3, 1, 2))[None]
    state_prompt = jnp.concatenate(s_p, axis=0)[None].astype(state_hgrn.dtype)
    k_sample = k_s.reshape(1, dbatch, dseq, da_heads, dh2)
    v_sample = v_s.reshape(1, dbatch, dseq, da_heads, DA_V_DIM)
    state_sample = s_s[None].astype(state_hgrn.dtype)
    return (y_prompt, y_sample, k_prompt, v_prompt, state_prompt, k_sample, v_sample, state_sample)
```

```python
import functools
import math

import jax
import jax.numpy as jnp
from jax import lax
from jax.experimental import pallas as pl
from jax.experimental.pallas import tpu as pltpu
from jax.experimental.pallas import tpu_sc as plsc

CHUNK = 64
DA_HEAD_DIM = 32
DA_V_DIM = 2 * DA_HEAD_DIM
HG_DIM = 128
REL_BUCKETS = 32
REL_MAX_DIST = 128
PEER_KEYS = 128
PEER_TOPK = 16
PEER_SUB_DIM = 64
EPS = 1e-6
NEG = -1e30
LOG2E = 1.4426950408889634

LANES = 128
ATT_BLOCK = 128
ATT_STEP = 512
HG_SUB = 16
V7X_VMEM_BYTES = 64 * 1024 * 1024
VMEM_LIMIT = V7X_VMEM_BYTES * 7 // 8

f32 = jnp.float32
bf16 = jnp.bfloat16


def _cparams(*sem):
    return pltpu.CompilerParams(dimension_semantics=sem, vmem_limit_bytes=VMEM_LIMIT)


def _inproj_prompt_kernel(x_ref, g_ref, w_ref, wkv_t_ref, kin_ref, vin_ref, after_ref,
                          q_ref, kt_ref, vt_ref, hz_ref, ktf_ref, vtf_ref):
    del kin_ref, vin_ref
    del after_ref
    x = x_ref[...]
    xn = x * lax.rsqrt(jnp.mean(x * x, axis=-1, keepdims=True) + EPS) * g_ref[...]
    xb = xn.astype(bf16)
    wq = q_ref.shape[1]
    nt = (((1,), (1,)), ((), ()))
    q_ref[...] = jnp.dot(xb, w_ref[:, 0:wq], preferred_element_type=f32).astype(bf16)
    kt = lax.dot_general(wkv_t_ref[0:wq, :], xb, nt, preferred_element_type=f32)
    ktf_ref[...] = kt
    kt_ref[...] = kt.astype(bf16)
    vt = lax.dot_general(wkv_t_ref[wq:2 * wq, :], xb, nt, preferred_element_type=f32)
    vtf_ref[...] = vt
    vt_ref[...] = vt.astype(bf16)
    hz_ref[...] = jnp.dot(xb, w_ref[:, 3 * wq:], preferred_element_type=f32)


def _inproj_prompt(x, g, w_bf, wkv_t_bf, kt_all, vt_all, after, b, seq, tm):
    d = x.shape[1]
    e = w_bf.shape[1]
    wq = wkv_t_bf.shape[0] // 2
    steps = seq // tm
    return pl.pallas_call(
        _inproj_prompt_kernel,
        grid=(steps,),
        in_specs=[
            pl.BlockSpec((tm, d), lambda i: (b * steps + i, 0)),
            pl.BlockSpec((1, d), lambda i: (0, 0)),
            pl.BlockSpec((d, e), lambda i: (0, 0)),
            pl.BlockSpec((2 * wq, d), lambda i: (0, 0)),
            pl.BlockSpec(memory_space=pl.ANY),
            pl.BlockSpec(memory_space=pl.ANY),
            pl.BlockSpec(memory_space=pl.ANY),
        ],
        out_specs=[
            pl.BlockSpec((tm, wq), lambda i: (i, 0)),
            pl.BlockSpec((wq, tm), lambda i: (0, i)),
            pl.BlockSpec((wq, tm), lambda i: (0, i)),
            pl.BlockSpec((tm, e - 3 * wq), lambda i: (i, 0)),
            pl.BlockSpec((wq, tm), lambda i: (b, i)),
            pl.BlockSpec((wq, tm), lambda i: (b, i)),
        ],
        out_shape=[
            jax.ShapeDtypeStruct((seq, wq), bf16),
            jax.ShapeDtypeStruct((wq, seq), bf16),
            jax.ShapeDtypeStruct((wq, seq), bf16),
            jax.ShapeDtypeStruct((seq, e - 3 * wq), f32),
            jax.ShapeDtypeStruct(kt_all.shape, f32),
            jax.ShapeDtypeStruct(vt_all.shape, f32),
        ],
        input_output_aliases={4: 4, 5: 5},
        compiler_params=_cparams("parallel"),
        name="inproj",
    )(x, g, w_bf, wkv_t_bf, kt_all, vt_all, after)


def _inproj_sample_kernel(x_ref, g_ref, w_ref, q_ref, k_ref, v_ref, hz_ref):
    x = x_ref[...]
    xn = x * lax.rsqrt(jnp.mean(x * x, axis=-1, keepdims=True) + EPS) * g_ref[...]
    xb = xn.astype(bf16)
    wq = q_ref.shape[1]
    q_ref[...] = jnp.dot(xb, w_ref[:, 0:wq], preferred_element_type=f32).astype(bf16)
    k_ref[...] = jnp.dot(xb, w_ref[:, wq:2 * wq], preferred_element_type=f32)
    v_ref[...] = jnp.dot(xb, w_ref[:, 2 * wq:3 * wq], preferred_element_type=f32)
    hz_ref[...] = jnp.dot(xb, w_ref[:, 3 * wq:], preferred_element_type=f32)


def _inproj_sample(x, g, w_bf, wq):
    t, d = x.shape
    e = w_bf.shape[1]
    return pl.pallas_call(
        _inproj_sample_kernel,
        grid=(1,),
        in_specs=[
            pl.BlockSpec((t, d), lambda i: (0, 0)),
            pl.BlockSpec((1, d), lambda i: (0, 0)),
            pl.BlockSpec((d, e), lambda i: (0, 0)),
        ],
        out_specs=[
            pl.BlockSpec((t, wq), lambda i: (0, 0)),
            pl.BlockSpec((t, wq), lambda i: (0, 0)),
            pl.BlockSpec((t, wq), lambda i: (0, 0)),
            pl.BlockSpec((t, e - 3 * wq), lambda i: (0, 0)),
        ],
        out_shape=[
            jax.ShapeDtypeStruct((t, wq), bf16),
            jax.ShapeDtypeStruct((t, wq), f32),
            jax.ShapeDtypeStruct((t, wq), f32),
            jax.ShapeDtypeStruct((t, e - 3 * wq), f32),
        ],
        compiler_params=_cparams("arbitrary"),
        name="inproj_sample",
    )(x, g, w_bf)


def _rel_bucket(rel):
    nb = REL_BUCKETS // 2
    max_exact = nb // 2
    ret = jnp.where(rel > 0, nb, 0)
    n = jnp.abs(rel)
    nf = jnp.maximum(n, 1).astype(f32)
    large = max_exact + (jnp.log(nf / max_exact) / math.log(REL_MAX_DIST / max_exact)
                         * (nb - max_exact)).astype(jnp.int32)
    large = jnp.minimum(large, nb - 1)
    return ret + jnp.where(n < max_exact, n, large)


def _bias_of(rel, rel_bias):
    bucket = _rel_bucket(rel)
    out = jnp.zeros((rel_bias.shape[1],) + rel.shape, f32)
    for b in range(REL_BUCKETS):
        out = out + jnp.where(bucket[None] == b, rel_bias[b].astype(f32)[:, None, None], 0.0)
    return out


def _attn_prompt_kernel(lam_ref, q_ref, kt_ref, vt_ref, bias_ref, gain_ref, o_ref, m_sc, acc_sc, sa_sc, sb_sc,
                        *, q_block0):
    qb = pl.program_id(1) + q_block0
    tq = q_ref.shape[0]
    seq = kt_ref.shape[1]
    per_step = ATT_STEP // ATT_BLOCK
    lane = lax.broadcasted_iota(jnp.int32, (tq, LANES), 1)
    feat_k = lax.broadcasted_iota(jnp.int32, (LANES, ATT_STEP), 0)
    qs = q_ref[...].astype(f32) * (DA_HEAD_DIM ** -0.5 * LOG2E)
    q4 = jnp.concatenate(
        [jnp.where((lane // DA_HEAD_DIM) == i, qs, 0.0).astype(bf16) for i in range(4)], axis=0)

    m_sc[...] = jnp.full(m_sc.shape, -jnp.inf, f32)
    acc_sc[...] = jnp.zeros(acc_sc.shape, f32)

    n_steps = qb // per_step + 1
    last = n_steps - 1
    r = qb % per_step

    def scores(k):
        start = pl.multiple_of(jnp.minimum(k * ATT_STEP, seq - ATT_STEP), ATT_STEP)
        return jnp.dot(q4, kt_ref[:, pl.ds(start, ATT_STEP)], preferred_element_type=f32)

    def half(k, cur_ref, nxt_ref):
        nxt_ref[...] = scores(k + 1)
        tile = jnp.where(k == last, 1 + r, jnp.where((k == last - 1) & (r == 0), per_step + 1, 0))
        vv = vt_ref[:, pl.ds(pl.multiple_of(k * ATT_STEP, ATT_STEP), ATT_STEP)]
        for hh in range(2):
            bias = bias_ref[0, tile, hh * tq:(hh + 1) * tq, :]
            ps, alphas = [], []
            for c in range(2):
                rows = slice((2 * hh + c) * tq, (2 * hh + c + 1) * tq)
                s = cur_ref[rows, :] + bias
                m_old = m_sc[rows]
                m_new = jnp.maximum(m_old, jnp.max(s, axis=-1, keepdims=True))
                ps.append(jnp.exp2(s - m_new[:, 0:1]).astype(bf16))
                alphas.append(jnp.exp2(m_old - m_new))
                m_sc[rows] = m_new
            rows2 = slice(2 * hh * tq, (2 * hh + 2) * tq)
            vaug = jnp.where((feat_k // DA_V_DIM) == hh, vv, jnp.ones_like(vv))
            acc_sc[rows2] = (acc_sc[rows2] * jnp.concatenate(alphas, axis=0)
                             + lax.dot_general(jnp.concatenate(ps, axis=0), vaug, (((1,), (1,)), ((), ())),
                                               preferred_element_type=f32))

    sa_sc[...] = scores(0)

    def two_steps(kk, carry):
        half(2 * kk, sa_sc, sb_sc)

        @pl.when(2 * kk + 1 < n_steps)
        def _():
            half(2 * kk + 1, sb_sc, sa_sc)

        return carry

    lax.fori_loop(0, (n_steps + 1) // 2, two_steps, 0)

    lam = lam_ref[0]
    outs = []
    for hh in range(2):
        own = (lane // DA_V_DIM) == hh
        a1 = acc_sc[2 * hh * tq:(2 * hh + 1) * tq]
        a2 = acc_sc[(2 * hh + 1) * tq:(2 * hh + 2) * tq]
        l1 = jnp.max(jnp.where(own, 0.0, a1), axis=-1, keepdims=True)
        l2 = jnp.max(jnp.where(own, 0.0, a2), axis=-1, keepdims=True)
        o = a1 / l1 - lam * (a2 / l2)
        ssq = jnp.sum(jnp.where(own, o * o, 0.0), axis=-1, keepdims=True)
        outs.append(o * lax.rsqrt(ssq * (1.0 / DA_V_DIM) + EPS))
    o = jnp.where((lane // DA_V_DIM) == 0, outs[0], outs[1])
    o_ref[...] = (o * gain_ref[...]).astype(o_ref.dtype)


def _attn_prompt(q, kt, vt, bias_tiles, gain, lam, q_block0, n_q):
    seq, w = q.shape
    pairs = w // LANES
    assert seq % ATT_STEP == 0
    n_tiles = ATT_STEP // ATT_BLOCK + 2
    return pl.pallas_call(
        functools.partial(_attn_prompt_kernel, q_block0=q_block0),
        grid=(pairs, n_q),
        in_specs=[
            pl.BlockSpec(memory_space=pltpu.SMEM),
            pl.BlockSpec((ATT_BLOCK, LANES), lambda p, i: (q_block0 + i, p)),
            pl.BlockSpec((LANES, seq), lambda p, i: (p, 0)),
            pl.BlockSpec((LANES, seq), lambda p, i: (p, 0)),
            pl.BlockSpec((1, n_tiles, 2 * ATT_BLOCK, ATT_STEP), lambda p, i: (p, 0, 0, 0)),
            pl.BlockSpec((1, LANES), lambda p, i: (0, 0)),
        ],
        out_specs=pl.BlockSpec((ATT_BLOCK, LANES), lambda p, i: (i, p)),
        out_shape=jax.ShapeDtypeStruct((n_q * ATT_BLOCK, w), bf16),
        scratch_shapes=[
            pltpu.VMEM((4 * ATT_BLOCK, LANES), f32),
            pltpu.VMEM((4 * ATT_BLOCK, LANES), f32),
            pltpu.VMEM((4 * ATT_BLOCK, ATT_STEP), f32),
            pltpu.VMEM((4 * ATT_BLOCK, ATT_STEP), f32),
        ],
        compiler_params=_cparams("parallel", "arbitrary"),
        name="attn_prompt",
    )(lam, q, kt, vt, bias_tiles, gain)


def _prompt_bias_tiles(rel_bias):
    h = rel_bias.shape[1]
    per_step = ATT_STEP // ATT_BLOCK
    i = jnp.arange(ATT_BLOCK)
    rel_diag = i[None, :] - i[:, None]
    rel_prev = rel_diag - ATT_BLOCK
    far = rel_bias[REL_BUCKETS // 2 - 1].astype(f32)
    b_diag = (_bias_of(rel_diag, rel_bias) - far[:, None, None]) * LOG2E
    b_prev = (_bias_of(rel_prev, rel_bias) - far[:, None, None]) * LOG2E
    mask = (i[None, :] // CHUNK) <= (i[:, None] // CHUNK)
    b_diag = jnp.where(mask[None], b_diag, NEG)
    zero = jnp.zeros_like(b_diag)
    dead = jnp.full_like(b_diag, NEG)
    tiles = [jnp.concatenate([zero] * per_step, axis=-1)]
    for r in range(per_step):
        blocks = [zero if j < r - 1 else b_prev if j == r - 1 else b_diag if j == r else dead
                  for j in range(per_step)]
        tiles.append(jnp.concatenate(blocks, axis=-1))
    tiles.append(jnp.concatenate([zero] * (per_step - 1) + [b_prev], axis=-1))
    tiles = jnp.stack(tiles, axis=1)
    tiles = tiles.reshape(h // 2, 2, per_step + 2, ATT_BLOCK, ATT_STEP)
    return jnp.transpose(tiles, (0, 2, 1, 3, 4)).reshape(h // 2, per_step + 2, 2 * ATT_BLOCK, ATT_STEP)


def _attn_sample_kernel(lam_ref, q_ref, kp_ref, vp_ref, kn_ref, vn_ref, bp_ref, bn_ref, gain_ref, o_ref):
    nq = q_ref.shape[0]
    pairs = q_ref.shape[1] // LANES
    lane = lax.broadcasted_iota(jnp.int32, (nq, LANES), 1)
    lam = lam_ref[0]
    nt = (((1,), (1,)), ((), ()))
    for p in range(pairs):
        sl = slice(p * LANES, (p + 1) * LANES)
        qs = q_ref[:, sl].astype(f32) * (DA_HEAD_DIM ** -0.5 * LOG2E)
        kp = kp_ref[sl, :].astype(bf16)
        vp = vp_ref[sl, :].astype(bf16)
        kn = kn_ref[:, sl].astype(bf16)
        vn = vn_ref[:, sl].astype(bf16)
        outs = []
        for hh in range(2):
            h = 2 * p + hh
            own = (lane // DA_V_DIM) == hh
            res = []
            for c in range(2):
                qm = jnp.where((lane // DA_HEAD_DIM) == 2 * hh + c, qs, 0.0).astype(bf16)
                sp = jnp.dot(qm, kp, preferred_element_type=f32) + bp_ref[h]
                sn = lax.dot_general(qm, kn, nt, preferred_element_type=f32) + bn_ref[h]
                m = jnp.maximum(jnp.max(sp, axis=-1, keepdims=True), jnp.max(sn, axis=-1, keepdims=True))
                pp = jnp.exp2(sp - m)
                pn = jnp.exp2(sn - m)
                l = jnp.sum(pp, axis=-1, keepdims=True) + jnp.sum(pn, axis=-1, keepdims=True)
                pv = (lax.dot_general(pp.astype(bf16), vp, nt, preferred_element_type=f32)
                      + jnp.dot(pn.astype(bf16), vn, preferred_element_type=f32))
                res.append(pv / l)
            o = res[0] - lam * res[1]
            ssq = jnp.sum(jnp.where(own, o * o, 0.0), axis=-1, keepdims=True)
            outs.append(o * lax.rsqrt(ssq * (1.0 / DA_V_DIM) + EPS))
        o = jnp.where((lane // DA_V_DIM) == 0, outs[0], outs[1])
        o_ref[:, sl] = (o * gain_ref[...]).astype(o_ref.dtype)


def _attn_sample(q, k_new, v_new, k_past, v_past, bias_past, bias_new, gain, lam, batch, nq, past):
    w = q.shape[1]
    h = bias_past.shape[0]
    return pl.pallas_call(
        _attn_sample_kernel,
        grid=(batch,),
        in_specs=[
            pl.BlockSpec(memory_space=pltpu.SMEM),
            pl.BlockSpec((nq, w), lambda b: (b, 0)),
            pl.BlockSpec((w, past), lambda b: (b, 0)),
            pl.BlockSpec((w, past), lambda b: (b, 0)),
            pl.BlockSpec((nq, w), lambda b: (b, 0)),
            pl.BlockSpec((nq, w), lambda b: (b, 0)),
            pl.BlockSpec((h, nq, past), lambda b: (0, 0, 0)),
            pl.BlockSpec((h, nq, nq), lambda b: (0, 0, 0)),
            pl.BlockSpec((1, LANES), lambda b: (0, 0)),
        ],
        out_specs=pl.BlockSpec((nq, w), lambda b: (b, 0)),
        out_shape=jax.ShapeDtypeStruct((batch * nq, w), bf16),
        compiler_params=_cparams("parallel"),
        name="attn_sample",
    )(lam, q, k_past, v_past, k_new, v_new, bias_past, bias_new, gain)


def _hgrn_chunk(q, f_logit, vv, g_logit, lb, gain, s0, lc):
    n_sub = lc // HG_SUB
    f = lb + (1.0 - lb) * jax.nn.sigmoid(f_logit)
    logf = jnp.log(f)
    k = 1.0 - f
    row = lax.broadcasted_iota(jnp.int32, (lc, lc), 0)
    col = lax.broadcasted_iota(jnp.int32, (lc, lc), 1)
    tril = (col <= row).astype(f32)
    b = jnp.dot(tril, logf, precision=lax.Precision.HIGHEST, preferred_element_type=f32)
    b_last = b[lc - 1:lc, :]

    o = jnp.dot((q * jnp.exp(b)).astype(bf16), s0.astype(bf16), preferred_element_type=f32)
    k_hat = k * jnp.exp(b_last - b)
    bt = b.T
    decay_col = jnp.exp(bt[:, lc - 1:lc])
    s_new = decay_col * s0 + jnp.dot(k_hat.T.astype(bf16), vv.astype(bf16), preferred_element_type=f32)

    if n_sub > 1:
        rows = []
        nt = (((1,), (1,)), ((), ()))
        for i in range(n_sub):
            lo = i * HG_SUB
            if i == 0:
                rows.append(jnp.zeros((HG_SUB, lc), f32))
                continue
            beta = b[lo - 1:lo, :]
            q_t = q[lo:lo + HG_SUB, :] * jnp.exp(b[lo:lo + HG_SUB, :] - beta)
            k_t = k * jnp.exp(jnp.minimum(beta - b, 0.0))
            rows.append(lax.dot_general(q_t.astype(bf16), k_t.astype(bf16), nt, preferred_element_type=f32))
        a_off = jnp.concatenate(rows, axis=0)
        a_off = jnp.where(col < (row // HG_SUB) * HG_SUB, a_off, 0.0)
        o = o + jnp.dot(a_off.astype(bf16), vv.astype(bf16), preferred_element_type=f32)

    ones = jnp.ones((HG_DIM, LANES), bf16)
    srow = lax.broadcasted_iota(jnp.int32, (HG_SUB, HG_DIM), 0)
    orow = lax.broadcasted_iota(jnp.int32, (HG_SUB, HG_DIM), 0)
    blocks = []
    for i in range(n_sub):
        lo = i * HG_SUB
        b_i = b[lo:lo + HG_SUB, :]
        k_i = k[lo:lo + HG_SUB, :]
        v_i = vv[lo:lo + HG_SUB, :]
        q_i = q[lo:lo + HG_SUB, :]
        d_rows = []
        for t in range(HG_SUB):
            arg = jnp.where(srow <= t, b_i[t:t + 1, :] - b_i, -jnp.inf)
            d_rows.append(q_i[t:t + 1, :] * k_i * jnp.exp(arg))
        d3 = jnp.concatenate(d_rows, axis=0)
        a_rep = jnp.dot(d3.astype(bf16), ones, preferred_element_type=f32)
        o_i = jnp.zeros((HG_SUB, HG_DIM), f32)
        for t in range(HG_SUB):
            o_t = jnp.sum(a_rep[t * HG_SUB:(t + 1) * HG_SUB, :] * v_i, axis=0, keepdims=True)
            o_i = jnp.where(orow == t, o_t, o_i)
        blocks.append(o_i)
    o = o + (jnp.concatenate(blocks, axis=0) if n_sub > 1 else blocks[0])

    on = o * lax.rsqrt(jnp.mean(o * o, axis=-1, keepdims=True) + EPS) * gain
    gate = g_logit * jax.nn.sigmoid(g_logit)
    return on * gate, s_new


def _hgrn_kernel(hz_ref, lb_ref, gain_ref, s0_ref, o_ref, s_ref, st_sc, *, lc, heads):
    j = pl.program_id(1)
    w = heads * HG_DIM

    @pl.when(j == 0)
    def _():
        st_sc[...] = s0_ref[0]

    n_chunks = hz_ref.shape[0] // lc
    for c in range(n_chunks):
        rs = slice(c * lc, (c + 1) * lc)
        for h in range(heads):
            cs = slice(h * HG_DIM, (h + 1) * HG_DIM)
            o, s_new = _hgrn_chunk(
                hz_ref[rs, h * HG_DIM:(h + 1) * HG_DIM],
                hz_ref[rs, w + h * HG_DIM:w + (h + 1) * HG_DIM],
                hz_ref[rs, 2 * w + h * HG_DIM:2 * w + (h + 1) * HG_DIM],
                hz_ref[rs, 3 * w + h * HG_DIM:3 * w + (h + 1) * HG_DIM],
                lb_ref[:, cs], gain_ref[...], st_sc[h], lc)
            st_sc[h] = s_new
            o_ref[rs, cs] = o.astype(o_ref.dtype)

    @pl.when(j == pl.num_programs(1) - 1)
    def _():
        s_ref[0] = st_sc[...]


def _hgrn(hz, lb, gain, s0, batch, seq, lc, chunks_per_step):
    t = hz.shape[0]
    w = hz.shape[1] // 4
    heads = w // HG_DIM
    rows = lc * chunks_per_step
    steps = seq // rows
    return pl.pallas_call(
        functools.partial(_hgrn_kernel, lc=lc, heads=heads),
        grid=(batch, steps),
        in_specs=[
            pl.BlockSpec((rows, 4 * w), lambda b, j: (b * steps + j, 0)),
            pl.BlockSpec((1, w), lambda b, j: (0, 0)),
            pl.BlockSpec((1, HG_DIM), lambda b, j: (0, 0)),
            pl.BlockSpec((1, heads, HG_DIM, HG_DIM), lambda b, j: (b, 0, 0, 0)),
        ],
        out_specs=[
            pl.BlockSpec((rows, w), lambda b, j: (b * steps + j, 0)),
            pl.BlockSpec((1, heads, HG_DIM, HG_DIM), lambda b, j: (b, 0, 0, 0)),
        ],
        out_shape=[
            jax.ShapeDtypeStruct((t, w), bf16),
            jax.ShapeDtypeStruct((batch, heads, HG_DIM, HG_DIM), f32),
        ],
        scratch_shapes=[pltpu.VMEM((heads, HG_DIM, HG_DIM), f32)],
        compiler_params=_cparams("parallel", "arbitrary"),
        name="hgrn2",
    )(hz, lb, gain, s0)


def _top_rows(vals, k, payload=None):
    n_rows = vals.shape[0]
    row = lax.broadcasted_iota(jnp.int32, vals.shape, 0)
    out_v, out_i = [], []
    for _ in range(k):
        m = jnp.max(vals, axis=0, keepdims=True)
        first = jnp.min(jnp.where(vals == m, row, n_rows), axis=0, keepdims=True)
        hit = row == first
        out_v.append(m)
        if payload is None:
            out_i.append(first)
        else:
            out_i.append(jnp.max(jnp.where(hit, payload, -1), axis=0, keepdims=True))
        vals = jnp.where(hit, -jnp.inf, vals)
    return jnp.concatenate(out_v, axis=0), jnp.concatenate(out_i, axis=0)


def _retrieve_kernel(oa_ref, ob_ref, x_ref, wo_ref, g_ref, wqt_ref, sk_ref,
                     hp_ref, xn_ref, idx_ref, gate_ref, qt_sc, s1v_sc, s1i_sc, idx_sc, gate_sc):
    half = oa_ref.shape[1]
    y = (jnp.dot(oa_ref[...], wo_ref[0:half, :], preferred_element_type=f32)
         + jnp.dot(ob_ref[...], wo_ref[half:, :], preferred_element_type=f32))
    hp = x_ref[...] + y
    hp_ref[...] = hp
    xn = hp * lax.rsqrt(jnp.mean(hp * hp, axis=-1, keepdims=True) + EPS) * g_ref[...]
    xn_ref[...] = xn
    nt = (((1,), (1,)), ((), ()))
    qt_sc[...] = lax.dot_general(wqt_ref[...], xn.astype(bf16), nt, preferred_element_type=f32).astype(bf16)

    n_half = sk_ref.shape[0]

    def stage1(hc, carry):
        start = pl.multiple_of(hc * PEER_SUB_DIM, PEER_SUB_DIM)
        s = jnp.dot(sk_ref[hc], qt_sc[pl.ds(start, PEER_SUB_DIM), :], preferred_element_type=f32)
        v, i = _top_rows(s, PEER_TOPK)
        s1v_sc[hc] = v
        s1i_sc[hc] = i
        return carry

    lax.fori_loop(0, n_half, stage1, 0)

    tm = x_ref.shape[0]
    sub = lax.broadcasted_iota(jnp.int32, (8, tm), 0)

    def stage2(h, carry):
        v1, i1 = s1v_sc[2 * h], s1i_sc[2 * h]
        v2, i2 = s1v_sc[2 * h + 1], s1i_sc[2 * h + 1]
        cv, ci = [], []
        for i in range(PEER_TOPK):
            nj = PEER_TOPK // (i + 1)
            rows = PEER_TOPK if nj > 8 else 8
            a = v1[i:i + 1, :] + v2[0:rows, :]
            e = i1[i:i + 1, :] * PEER_KEYS + i2[0:rows, :]
            if nj < rows:
                a = jnp.where(sub < nj, a, -jnp.inf)
            cv.append(a)
            ci.append(e)
        top_s, eidx = _top_rows(jnp.concatenate(cv, axis=0), PEER_TOPK, jnp.concatenate(ci, axis=0))
        p = jnp.exp(top_s - top_s[0:1, :])
        g = p / jnp.sum(p, axis=0, keepdims=True)
        r0 = pl.multiple_of(h * PEER_TOPK, PEER_TOPK)
        gate_sc[pl.ds(r0, PEER_TOPK), :] = g
        idx_sc[pl.ds(r0, PEER_TOPK), :] = eidx
        return carry

    lax.fori_loop(0, n_half // 2, stage2, 0)
    idx_ref[...] = pltpu.bitcast(pltpu.bitcast(idx_sc[...], f32).T, jnp.int32)
    gate_ref[...] = gate_sc[...].T


def _retrieve(oa, ob, x, wo_bf, g2, wqt_bf, sk_bf, tm, x_block0=0, ob_block0=0):
    t, half = oa.shape
    d = x.shape[1]
    e = wqt_bf.shape[0]
    n_half = sk_bf.shape[0]
    n_sel = (n_half // 2) * PEER_TOPK
    return pl.pallas_call(
        _retrieve_kernel,
        grid=(t // tm,),
        in_specs=[
            pl.BlockSpec((tm, half), lambda i: (i, 0)),
            pl.BlockSpec((tm, half), lambda i: (ob_block0 + i, 0)),
            pl.BlockSpec((tm, d), lambda i: (x_block0 + i, 0)),
            pl.BlockSpec((2 * half, d), lambda i: (0, 0)),
            pl.BlockSpec((1, d), lambda i: (0, 0)),
            pl.BlockSpec((e, d), lambda i: (0, 0)),
            pl.BlockSpec((n_half, PEER_KEYS, PEER_SUB_DIM), lambda i: (0, 0, 0)),
        ],
        out_specs=[
            pl.BlockSpec((tm, d), lambda i: (i, 0)),
            pl.BlockSpec((tm, d), lambda i: (i, 0)),
            pl.BlockSpec((tm, n_sel), lambda i: (i, 0)),
            pl.BlockSpec((tm, n_sel), lambda i: (i, 0)),
        ],
        out_shape=[
            jax.ShapeDtypeStruct((t, d), f32),
            jax.ShapeDtypeStruct((t, d), f32),
            jax.ShapeDtypeStruct((t, n_sel), jnp.int32),
            jax.ShapeDtypeStruct((t, n_sel), f32),
        ],
        scratch_shapes=[
            pltpu.VMEM((e, tm), bf16),
            pltpu.VMEM((n_half, PEER_TOPK, tm), f32),
            pltpu.VMEM((n_half, PEER_TOPK, tm), jnp.int32),
            pltpu.VMEM((n_sel, tm), jnp.int32),
            pltpu.VMEM((n_sel, tm), f32),
        ],
        compiler_params=_cparams("parallel"),
        name="retrieve",
    )(oa, ob, x, wo_bf, g2, wqt_bf, sk_bf)


def _pack_table(tab):
    half = tab.shape[1] // 2
    lo = lax.bitcast_convert_type(tab[:, :half].astype(bf16), jnp.uint16).astype(jnp.uint32)
    hi = lax.bitcast_convert_type(tab[:, half:].astype(bf16), jnp.uint16).astype(jnp.uint32)
    return lax.bitcast_convert_type(lo | (hi << 16), jnp.int32)


def _sc_layout(n_tok, n_sel, words):
    info = plsc.get_sparse_core_info()
    n_workers = info.num_cores * info.num_subcores
    lanes = info.num_lanes
    block = 8 * lanes
    assert n_tok % (2 * n_workers) == 0 and words % block == 0 and (n_sel // 2) % lanes == 0
    return info, lanes, n_tok // n_workers, n_sel // 2, block


def _selected_dots(table, idx, x):
    n_tok, n_sel = idx.shape
    words = table.shape[1]
    info, lanes, tpw, half_rows, block = _sc_layout(n_tok, n_sel, words)
    idx2 = idx.reshape(2 * n_tok, half_rows)
    mesh = plsc.VectorSubcoreMesh(core_axis_name="c", subcore_axis_name="s")

    @functools.partial(
        pl.kernel, out_type=jax.ShapeDtypeStruct((n_tok, n_sel), f32), mesh=mesh,
        compiler_params=pltpu.CompilerParams(needs_layout_passes=False),
        scratch_types=[
            pltpu.VMEM((2 * tpw, half_rows), jnp.int32),
            pltpu.VMEM((half_rows, words), jnp.int32),
            pltpu.VMEM((half_rows, words), jnp.int32),
            pltpu.VMEM((2 * words,), f32),
            pltpu.VMEM((2 * words,), f32),
            pltpu.VMEM((n_sel // 8, 8 * lanes), f32),
            pltpu.VMEM((tpw, n_sel), f32),
            pltpu.SemaphoreType.DMA, pltpu.SemaphoreType.DMA, pltpu.SemaphoreType.DMA, pltpu.SemaphoreType.DMA,
        ])
    def dots(t_hbm, i_hbm, x_hbm, o_hbm, idx_v, rows_a, rows_b, x_a, x_b, part_v, act_v, sem_a, sem_b, sem_xa, sem_xb):
        wid = lax.axis_index("s") * info.num_cores + lax.axis_index("c")
        base = wid * tpw
        pltpu.sync_copy(i_hbm.at[pl.ds(base * 2, 2 * tpw)], idx_v)

        def accumulate(rows, x_v, half):
            @pl.loop(0, words // block)
            def _(cb):
                xs = [x_v[pl.ds(p * words + cb * block + c * lanes, lanes)] for c in range(8) for p in range(2)]

                @plsc.parallel_loop(0, half_rows)
                def _(row):
                    packed = [rows[row, pl.ds(cb * block + c * lanes, lanes)] for c in range(8)]
                    unp = [plsc.unpack(plsc.bitcast(p, bf16), format=plsc.PackFormat.INTERLEAVED) for p in packed]
                    prods = [unp[c][p] * xs[2 * c + p] for c in range(8) for p in range(2)]
                    while len(prods) > 1:
                        prods = [prods[k] + prods[k + 1] for k in range(0, len(prods), 2)]
                    prow = half * half_rows + row
                    plsc.addupdate(part_v.at[prow // 8, pl.ds((prow % 8) * lanes, lanes)], prods[0])

        def gather(half_index, rows, sem):
            return pltpu.make_async_copy(t_hbm.at[idx_v.at[half_index]], rows, sem)

        def xcopy(t, x_v, sem):
            return pltpu.make_async_copy(x_hbm.at[base + t], x_v, sem)

        def token(t, x_v, sem_x, x_next, sem_xn):
            gather(2 * t + 1, rows_b, sem_b).start()

            @pl.when(t + 1 < tpw)
            def _():
                xcopy(t + 1, x_next, sem_xn).start()

            zero = jnp.zeros((lanes,), f32)
            for r in range(n_sel):
                part_v[r // 8, pl.ds((r % 8) * lanes, lanes)] = zero
            xcopy(t, x_v, sem_x).wait()
            gather(2 * t, rows_a, sem_a).wait()
            accumulate(rows_a, x_v, 0)

            @pl.when(t + 1 < tpw)
            def _():
                gather(2 * t + 2, rows_a, sem_a).start()

            gather(2 * t + 1, rows_b, sem_b).wait()
            accumulate(rows_b, x_v, 1)
            it = lax.iota(jnp.int32, lanes)
            for g in range(n_sel // lanes):
                prow = g * (lanes // 8) + it // 8
                pcol = (it % 8) * lanes
                tot = plsc.load_gather(part_v, [prow, pcol])
                for l in range(1, lanes):
                    tot = tot + plsc.load_gather(part_v, [prow, pcol + l])
                act_v[t, pl.ds(g * lanes, lanes)] = tot

        gather(0, rows_a, sem_a).start()
        xcopy(0, x_a, sem_xa).start()

        @pl.loop(0, tpw // 2)
        def _(i):
            token(2 * i, x_a, sem_xa, x_b, sem_xb)
            token(2 * i + 1, x_b, sem_xb, x_a, sem_xa)

        pltpu.sync_copy(act_v, o_hbm.at[pl.ds(base, tpw)])

    return dots(table, idx2, x)


def _gate_weights_kernel(act_ref, gate_ref, after_ref, w_ref):
    del after_ref
    a = act_ref[...]
    w_ref[...] = gate_ref[...] * (0.5 * a * (1.0 + lax.erf(a * (2.0 ** -0.5))))


def _gate_weights(act, gate, after, tm):
    t, n_sel = act.shape
    spec = pl.BlockSpec((tm, n_sel), lambda i: (i, 0))
    return pl.pallas_call(
        _gate_weights_kernel,
        grid=(t // tm,),
        in_specs=[spec, spec, pl.BlockSpec(memory_space=pl.ANY)],
        out_specs=spec,
        out_shape=jax.ShapeDtypeStruct((t, n_sel), f32),
        compiler_params=_cparams("parallel"),
        name="peer_weights",
    )(act, gate, after)


def _weighted_rows(table, idx, w):
    n_tok, n_sel = w.shape
    words = table.shape[1]
    info, lanes, tpw, half_rows, _ = _sc_layout(n_tok, n_sel, words)
    idx2 = idx.reshape(2 * n_tok, half_rows)
    mesh = plsc.VectorSubcoreMesh(core_axis_name="c", subcore_axis_name="s")

    @functools.partial(
        pl.kernel, out_type=jax.ShapeDtypeStruct((n_tok, 2 * words), f32), mesh=mesh,
        compiler_params=pltpu.CompilerParams(needs_layout_passes=False),
        scratch_types=[
            pltpu.VMEM((2 * tpw, half_rows), jnp.int32),
            pltpu.VMEM((tpw, n_sel), f32),
            pltpu.VMEM((half_rows, words), jnp.int32),
            pltpu.VMEM((half_rows, words), jnp.int32),
            pltpu.VMEM((2 * words,), f32),
            pltpu.VMEM((2 * words,), f32),
            pltpu.SemaphoreType.DMA, pltpu.SemaphoreType.DMA, pltpu.SemaphoreType.DMA, pltpu.SemaphoreType.DMA,
        ])
    def mix(t_hbm, i_hbm, w_hbm, o_hbm, idx_v, w_v, rows_a, rows_b, out_a, out_b, sem_a, sem_b, sem_oa, sem_ob):
        wid = lax.axis_index("s") * info.num_cores + lax.axis_index("c")
        base = wid * tpw
        pltpu.sync_copy(i_hbm.at[pl.ds(base * 2, 2 * tpw)], idx_v)
        pltpu.sync_copy(w_hbm.at[pl.ds(base, tpw)], w_v)

        def accumulate(rows, out_v, t, half, first):
            if first:
                zero = jnp.zeros((lanes,), f32)
                for c in range(2 * words // lanes):
                    out_v[pl.ds(c * lanes, lanes)] = zero

            @pl.loop(0, half_rows // lanes)
            def _(g):
                wv = w_v[t, pl.ds(half * half_rows + g * lanes, lanes)]
                ws = [wv.at[jnp.full((lanes,), r, jnp.int32)].get(mode="promise_in_bounds") for r in range(lanes)]

                @plsc.parallel_loop(0, words // lanes)
                def _(c):
                    packed = [rows[g * lanes + r, pl.ds(c * lanes, lanes)] for r in range(lanes)]
                    unp = [plsc.unpack(plsc.bitcast(p, bf16), format=plsc.PackFormat.INTERLEAVED) for p in packed]
                    lo = [unp[r][0] * ws[r] for r in range(lanes)]
                    hi = [unp[r][1] * ws[r] for r in range(lanes)]
                    while len(lo) > 1:
                        lo = [lo[k] + lo[k + 1] for k in range(0, len(lo), 2)]
                        hi = [hi[k] + hi[k + 1] for k in range(0, len(hi), 2)]
                    plsc.addupdate(out_v.at[pl.ds(c * lanes, lanes)], lo[0])
                    plsc.addupdate(out_v.at[pl.ds(words + c * lanes, lanes)], hi[0])

        def gather(half_index, rows, sem):
            return pltpu.make_async_copy(t_hbm.at[idx_v.at[half_index]], rows, sem)

        def token(t, out_v, sem_o, out_in_flight):
            gather(2 * t + 1, rows_b, sem_b).start()
            gather(2 * t, rows_a, sem_a).wait()

            @pl.when(out_in_flight)
            def _():
                pltpu.make_async_copy(out_v, o_hbm.at[base + t], sem_o).wait()

            accumulate(rows_a, out_v, t, 0, True)

            @pl.when(t + 1 < tpw)
            def _():
                gather(2 * t + 2, rows_a, sem_a).start()

            gather(2 * t + 1, rows_b, sem_b).wait()
            accumulate(rows_b, out_v, t, 1, False)
            pltpu.make_async_copy(out_v, o_hbm.at[base + t], sem_o).start()

        gather(0, rows_a, sem_a).start()

        @pl.loop(0, tpw // 2)
        def _(i):
            token(2 * i, out_a, sem_oa, i > 0)
            token(2 * i + 1, out_b, sem_ob, i > 0)

        pltpu.make_async_copy(out_a, o_hbm.at[base], sem_oa).wait()
        pltpu.make_async_copy(out_b, o_hbm.at[base], sem_ob).wait()

    return mix(table, idx2, w)


def _final_kernel(hp_ref, peer_ref, gf_ref, yin_ref, y_ref, done_ref):
    del yin_ref
    done_ref[...] = jnp.zeros(done_ref.shape, f32)
    h = hp_ref[...] + peer_ref[...]
    y_ref[...] = h * lax.rsqrt(jnp.mean(h * h, axis=-1, keepdims=True) + EPS) * gf_ref[...]


def _final(hp, peer, gf, y_all, y_block0, tm):
    t, d = hp.shape
    return pl.pallas_call(
        _final_kernel,
        grid=(t // tm,),
        in_specs=[
            pl.BlockSpec((tm, d), lambda i: (i, 0)),
            pl.BlockSpec((tm, d), lambda i: (i, 0)),
            pl.BlockSpec((1, d), lambda i: (0, 0)),
            pl.BlockSpec(memory_space=pl.ANY),
        ],
        out_specs=[pl.BlockSpec((tm, d), lambda i: (y_block0 + i, 0)),
                   pl.BlockSpec((8, LANES), lambda i: (0, 0))],
        out_shape=[jax.ShapeDtypeStruct(y_all.shape, f32), jax.ShapeDtypeStruct((8, LANES), f32)],
        input_output_aliases={3: 0},
        compiler_params=_cparams("arbitrary"),
        name="final_norm",
    )(hp, peer, gf, y_all)


def _channel_mixer_front(oa, ob, x, x_block0, ob_block0, wo_bf, g2, wqt_bf, sk_bf, u_pk, tm):
    hp, xn, eidx, gate = _retrieve(oa, ob, x, wo_bf, g2, wqt_bf, sk_bf, tm, x_block0, ob_block0)
    return hp, eidx, gate, _selected_dots(u_pk, eidx, xn)


def _channel_mixer_mix(front, after, v_pk, tm):
    _, eidx, gate, act = front
    return _weighted_rows(v_pk, eidx, _gate_weights(act, gate, after, tm))


def kernel(x_prompt, x_sample, cache_k, cache_v, state_hgrn, norm1, w_in, da_lq1, da_lk1, da_lq2, da_lk2,
           da_out_norm, hg_lb_logits, hg_out_norm, w_out, rel_bias, norm2, peer_w_q, peer_sub_keys,
           peer_u, peer_v, final_norm):
    batch, seq, d = x_prompt.shape
    dbatch, dseq, _ = x_sample.shape
    past = cache_k.shape[2]
    da_heads = cache_k.shape[3]
    hg_heads = state_hgrn.shape[2]
    depth = w_in.shape[0]
    assert depth == 1 and seq % ATT_BLOCK == 0 and seq % CHUNK == 0 and dseq % HG_SUB == 0 and dseq <= CHUNK
    assert past % CHUNK == 0 and (past + dseq - 1) // CHUNK == past // CHUNK

    l = 0
    lam_init = 0.8 - 0.6 * math.exp(-0.3 * l)
    lam = (jnp.exp(jnp.sum(da_lq1[l].astype(f32) * da_lk1[l].astype(f32)))
           - jnp.exp(jnp.sum(da_lq2[l].astype(f32) * da_lk2[l].astype(f32))) + lam_init).reshape(1)
    lb = jnp.cumsum(jax.nn.softmax(hg_lb_logits.astype(f32), axis=0), axis=0)[l].reshape(1, -1)
    da_gain = (jnp.tile(da_out_norm[l].astype(f32), 2) * (1.0 - lam_init)).reshape(1, LANES)
    hg_gain = hg_out_norm[l].astype(f32).reshape(1, HG_DIM)
    g1 = norm1[l].astype(f32).reshape(1, d)
    g2 = norm2[l].astype(f32).reshape(1, d)
    gf = final_norm.astype(f32).reshape(1, d)
    w_in_bf = w_in[l].astype(bf16)
    wkv_t_bf = w_in[l][:, cache_k.shape[3] * cache_k.shape[4]:3 * cache_k.shape[3] * cache_k.shape[4]].T.astype(bf16)
    wo_bf = w_out[l].astype(bf16)
    wqt_bf = peer_w_q[l].T.astype(bf16)
    sk_bf = peer_sub_keys[l].reshape(-1, PEER_KEYS, PEER_SUB_DIM).astype(bf16)
    u_pk = _pack_table(peer_u[l])
    v_pk = _pack_table(peer_v[l])

    bias_tiles = _prompt_bias_tiles(rel_bias)
    q_pos = past + jnp.arange(dseq)
    bias_s = _bias_of(jnp.arange(past + dseq)[None, :] - q_pos[:, None], rel_bias) * LOG2E
    bias_past, bias_new = bias_s[:, :, :past], bias_s[:, :, past:]
    zero_state = jnp.zeros((1, hg_heads, HG_DIM, HG_DIM), f32)

    dh2 = 2 * DA_HEAD_DIM
    wq = da_heads * dh2
    tm = 512
    x_all = x_prompt.reshape(batch * seq, d)
    kt_all = jnp.zeros((batch * wq, seq), f32)
    vt_all = jnp.zeros((batch * wq, seq), f32)
    y_all = jnp.zeros((batch * seq, d), f32)
    nq = seq // ATT_BLOCK
    blocks = seq // tm
    cuts = {0: 4, 1: 2}
    segments = [(b, c * (nq // cuts.get(b, 1)), nq // cuts.get(b, 1))
                for b in range(batch) for c in range(cuts.get(b, 1))]
    assert all(nq % n == 0 and (nq // n * ATT_BLOCK) % (2 * tm) == 0 for n in cuts.values())
    s_p = []
    done = [jnp.zeros((8, LANES), f32)] * 5
    fronts, peers = [], []

    def finish(k, y_all):
        return _final(fronts[k][0][0], peers[k], gf, y_all, fronts[k][1], tm)

    for s, (b, q0, n_q) in enumerate(segments):
        if q0 == 0:
            q, kt, vt, hz, kt_all, vt_all = _inproj_prompt(x_all, g1, w_in_bf, wkv_t_bf, kt_all, vt_all, done[s],
                                                           b, seq, tm)
            ob, s_new = _hgrn(hz, lb, hg_gain, zero_state, 1, seq, CHUNK, 4)
            s_p.append(s_new)
        oa = _attn_prompt(q, kt, vt, bias_tiles, da_gain, lam, q0, n_q)
        if s >= 1:
            peers.append(_channel_mixer_mix(fronts[s - 1][0], oa, v_pk, tm))
        row0 = q0 * ATT_BLOCK // tm
        fronts.append((_channel_mixer_front(oa, ob, x_all, b * blocks + row0, row0, wo_bf, g2, wqt_bf, sk_bf, u_pk,
                                            tm), b * blocks + row0))
        if s >= 2:
            y_all, token = finish(s - 2, y_all)
            done.append(token)
    peers.append(_channel_mixer_mix(fronts[-1][0], fronts[-1][0][0], v_pk, tm))
    for k in (len(segments) - 2, len(segments) - 1):
        y_all, _ = finish(k, y_all)

    xs = x_sample.reshape(dbatch * dseq, d)
    q, k_s, v_s, hz = _inproj_sample(xs, g1, w_in_bf, wq)
    kp_t = jnp.transpose(cache_k[l], (0, 2, 3, 1)).reshape(dbatch * wq, past)
    vp_t = jnp.transpose(cache_v[l], (0, 2, 3, 1)).reshape(dbatch * wq, past)
    oa = _attn_sample(q, k_s, v_s, kp_t, vp_t, bias_past, bias_new, da_gain, lam, dbatch, dseq, past)
    ob, s_s = _hgrn(hz, lb, hg_gain, state_hgrn[l].astype(f32), dbatch, dseq, dseq, 1)
    front = _channel_mixer_front(oa, ob, xs, 0, 0, wo_bf, g2, wqt_bf, sk_bf, u_pk, dbatch * dseq)
    peer = _channel_mixer_mix(front, front[0], v_pk, dbatch * dseq)
    y_s, _ = _final(front[0], peer, gf, jnp.zeros((dbatch * dseq, d), f32), 0, dbatch * dseq)

    y_prompt = y_all.reshape(batch, seq, d)
    y_sample = y_s.reshape(dbatch, dseq, d)
    k_prompt = jnp.transpose(kt_all.reshape(batch, da_heads, dh2, seq), (0, 3, 1, 2))[None]
    v_prompt = jnp.transpose(vt_all.reshape(batch, da_heads, DA_V_DIM, seq), (0, 3, 1, 2))[None]
    state_prompt = jnp.concatenate(s_p, axis=0)[None].astype(state_hgrn.dtype)
    k_sample = k_s.reshape(1, dbatch, dseq, da_heads, dh2)
    v_sample = v_s.reshape(1, dbatch, dseq, da_heads, DA_V_DIM)
    state_sample = s_s[None].astype(state_hgrn.dtype)
    return (y_prompt, y_sample, k_prompt, v_prompt, state_prompt, k_sample, v_sample, state_sample)
```

```python
import functools
import math

import jax
import jax.numpy as jnp
from jax import lax
from jax.experimental import pallas as pl
from jax.experimental.pallas import tpu as pltpu
from jax.experimental.pallas import tpu_sc as plsc

CHUNK = 64
DA_HEAD_DIM = 32
DA_V_DIM = 2 * DA_HEAD_DIM
HG_DIM = 128
REL_BUCKETS = 32
REL_MAX_DIST = 128
PEER_KEYS = 128
PEER_TOPK = 16
PEER_SUB_DIM = 64
EPS = 1e-6
NEG = -1e30
LOG2E = 1.4426950408889634

LANES = 128
ATT_BLOCK = 128
ATT_STEP = 512
HG_SUB = 16
V7X_VMEM_BYTES = 64 * 1024 * 1024
VMEM_LIMIT = V7X_VMEM_BYTES * 7 // 8

f32 = jnp.float32
bf16 = jnp.bfloat16


def _cparams(*sem):
    return pltpu.CompilerParams(dimension_semantics=sem, vmem_limit_bytes=VMEM_LIMIT)


def _inproj_prompt_kernel(x_ref, g_ref, w_ref, wkv_t_ref, kin_ref, vin_ref, after_ref,
                          q_ref, kt_ref, vt_ref, hz_ref, ktf_ref, vtf_ref):
    del kin_ref, vin_ref
    del after_ref
    x = x_ref[...]
    xn = x * lax.rsqrt(jnp.mean(x * x, axis=-1, keepdims=True) + EPS) * g_ref[...]
    xb = xn.astype(bf16)
    wq = q_ref.shape[1]
    nt = (((1,), (1,)), ((), ()))
    q_ref[...] = jnp.dot(xb, w_ref[:, 0:wq], preferred_element_type=f32).astype(bf16)
    kt = lax.dot_general(wkv_t_ref[0:wq, :], xb, nt, preferred_element_type=f32)
    ktf_ref[...] = kt
    kt_ref[...] = kt.astype(bf16)
    vt = lax.dot_general(wkv_t_ref[wq:2 * wq, :], xb, nt, preferred_element_type=f32)
    vtf_ref[...] = vt
    vt_ref[...] = vt.astype(bf16)
    hz_ref[...] = jnp.dot(xb, w_ref[:, 3 * wq:], preferred_element_type=f32)


def _inproj_prompt(x, g, w_bf, wkv_t_bf, kt_all, vt_all, after, b, seq, tm):
    d = x.shape[1]
    e = w_bf.shape[1]
    wq = wkv_t_bf.shape[0] // 2
    steps = seq // tm
    return pl.pallas_call(
        _inproj_prompt_kernel,
        grid=(steps,),
        in_specs=[
            pl.BlockSpec((tm, d), lambda i: (b * steps + i, 0)),
            pl.BlockSpec((1, d), lambda i: (0, 0)),
            pl.BlockSpec((d, e), lambda i: (0, 0)),
            pl.BlockSpec((2 * wq, d), lambda i: (0, 0)),
            pl.BlockSpec(memory_space=pl.ANY),
            pl.BlockSpec(memory_space=pl.ANY),
            pl.BlockSpec(memory_space=pl.ANY),
        ],
        out_specs=[
            pl.BlockSpec((tm, wq), lambda i: (i, 0)),
            pl.BlockSpec((wq, tm), lambda i: (0, i)),
            pl.BlockSpec((wq, tm), lambda i: (0, i)),
            pl.BlockSpec((tm, e - 3 * wq), lambda i: (i, 0)),
            pl.BlockSpec((wq, tm), lambda i: (b, i)),
            pl.BlockSpec((wq, tm), lambda i: (b, i)),
        ],
        out_shape=[
            jax.ShapeDtypeStruct((seq, wq), bf16),
            jax.ShapeDtypeStruct((wq, seq), bf16),
            jax.ShapeDtypeStruct((wq, seq), bf16),
            jax.ShapeDtypeStruct((seq, e - 3 * wq), f32),
            jax.ShapeDtypeStruct(kt_all.shape, f32),
            jax.ShapeDtypeStruct(vt_all.shape, f32),
        ],
        input_output_aliases={4: 4, 5: 5},
        compiler_params=_cparams("parallel"),
        name="inproj",
    )(x, g, w_bf, wkv_t_bf, kt_all, vt_all, after)


def _inproj_sample_kernel(x_ref, g_ref, w_ref, q_ref, k_ref, v_ref, hz_ref):
    x = x_ref[...]
    xn = x * lax.rsqrt(jnp.mean(x * x, axis=-1, keepdims=True) + EPS) * g_ref[...]
    xb = xn.astype(bf16)
    wq = q_ref.shape[1]
    q_ref[...] = jnp.dot(xb, w_ref[:, 0:wq], preferred_element_type=f32).astype(bf16)
    k_ref[...] = jnp.dot(xb, w_ref[:, wq:2 * wq], preferred_element_type=f32)
    v_ref[...] = jnp.dot(xb, w_ref[:, 2 * wq:3 * wq], preferred_element_type=f32)
    hz_ref[...] = jnp.dot(xb, w_ref[:, 3 * wq:], preferred_element_type=f32)


def _inproj_sample(x, g, w_bf, wq):
    t, d = x.shape
    e = w_bf.shape[1]
    return pl.pallas_call(
        _inproj_sample_kernel,
        grid=(1,),
        in_specs=[
            pl.BlockSpec((t, d), lambda i: (0, 0)),
            pl.BlockSpec((1, d), lambda i: (0, 0)),
            pl.BlockSpec((d, e), lambda i: (0, 0)),
        ],
        out_specs=[
            pl.BlockSpec((t, wq), lambda i: (0, 0)),
            pl.BlockSpec((t, wq), lambda i: (0, 0)),
            pl.BlockSpec((t, wq), lambda i: (0, 0)),
            pl.BlockSpec((t, e - 3 * wq), lambda i: (0, 0)),
        ],
        out_shape=[
            jax.ShapeDtypeStruct((t, wq), bf16),
            jax.ShapeDtypeStruct((t, wq), f32),
            jax.ShapeDtypeStruct((t, wq), f32),
            jax.ShapeDtypeStruct((t, e - 3 * wq), f32),
        ],
        compiler_params=_cparams("arbitrary"),
        name="inproj_sample",
    )(x, g, w_bf)


def _rel_bucket(rel):
    nb = REL_BUCKETS // 2
    max_exact = nb // 2
    ret = jnp.where(rel > 0, nb, 0)
    n = jnp.abs(rel)
    nf = jnp.maximum(n, 1).astype(f32)
    large = max_exact + (jnp.log(nf / max_exact) / math.log(REL_MAX_DIST / max_exact)
                         * (nb - max_exact)).astype(jnp.int32)
    large = jnp.minimum(large, nb - 1)
    return ret + jnp.where(n < max_exact, n, large)


def _bias_of(rel, rel_bias):
    bucket = _rel_bucket(rel)
    out = jnp.zeros((rel_bias.shape[1],) + rel.shape, f32)
    for b in range(REL_BUCKETS):
        out = out + jnp.where(bucket[None] == b, rel_bias[b].astype(f32)[:, None, None], 0.0)
    return out


def _attn_prompt_kernel(lam_ref, q_ref, kt_ref, vt_ref, bias_ref, gain_ref, o_ref, m_sc, acc_sc, sa_sc, sb_sc,
                        *, q_block0):
    qb = pl.program_id(1) + q_block0
    tq = q_ref.shape[0]
    seq = kt_ref.shape[1]
    per_step = ATT_STEP // ATT_BLOCK
    lane = lax.broadcasted_iota(jnp.int32, (tq, LANES), 1)
    feat_k = lax.broadcasted_iota(jnp.int32, (LANES, ATT_STEP), 0)
    qs = q_ref[...].astype(f32) * (DA_HEAD_DIM ** -0.5 * LOG2E)
    q4 = jnp.concatenate(
        [jnp.where((lane // DA_HEAD_DIM) == i, qs, 0.0).astype(bf16) for i in range(4)], axis=0)

    m_sc[...] = jnp.full(m_sc.shape, -jnp.inf, f32)
    acc_sc[...] = jnp.zeros(acc_sc.shape, f32)

    n_steps = qb // per_step + 1
    last = n_steps - 1
    r = qb % per_step

    def scores(k):
        start = pl.multiple_of(jnp.minimum(k * ATT_STEP, seq - ATT_STEP), ATT_STEP)
        return jnp.dot(q4, kt_ref[:, pl.ds(start, ATT_STEP)], preferred_element_type=f32)

    def half(k, cur_ref, nxt_ref):
        nxt_ref[...] = scores(k + 1)
        tile = jnp.where(k == last, 1 + r, jnp.where((k == last - 1) & (r == 0), per_step + 1, 0))
        vv = vt_ref[:, pl.ds(pl.multiple_of(k * ATT_STEP, ATT_STEP), ATT_STEP)]
        for hh in range(2):
            bias = bias_ref[0, tile, hh * tq:(hh + 1) * tq, :]
            ps, alphas = [], []
            for c in range(2):
                rows = slice((2 * hh + c) * tq, (2 * hh + c + 1) * tq)
                s = cur_ref[rows, :] + bias
                m_old = m_sc[rows]
                m_new = jnp.maximum(m_old, jnp.max(s, axis=-1, keepdims=True))
                ps.append(jnp.exp2(s - m_new[:, 0:1]).astype(bf16))
                alphas.append(jnp.exp2(m_old - m_new))
                m_sc[rows] = m_new
            rows2 = slice(2 * hh * tq, (2 * hh + 2) * tq)
            vaug = jnp.where((feat_k // DA_V_DIM) == hh, vv, jnp.ones_like(vv))
            acc_sc[rows2] = (acc_sc[rows2] * jnp.concatenate(alphas, axis=0)
                             + lax.dot_general(jnp.concatenate(ps, axis=0), vaug, (((1,), (1,)), ((), ())),
                                               preferred_element_type=f32))

    sa_sc[...] = scores(0)

    def two_steps(kk, carry):
        half(2 * kk, sa_sc, sb_sc)

        @pl.when(2 * kk + 1 < n_steps)
        def _():
            half(2 * kk + 1, sb_sc, sa_sc)

        return carry

    lax.fori_loop(0, (n_steps + 1) // 2, two_steps, 0)

    lam = lam_ref[0]
    outs = []
    for hh in range(2):
        own = (lane // DA_V_DIM) == hh
        a1 = acc_sc[2 * hh * tq:(2 * hh + 1) * tq]
        a2 = acc_sc[(2 * hh + 1) * tq:(2 * hh + 2) * tq]
        l1 = jnp.max(jnp.where(own, 0.0, a1), axis=-1, keepdims=True)
        l2 = jnp.max(jnp.where(own, 0.0, a2), axis=-1, keepdims=True)
        o = a1 / l1 - lam * (a2 / l2)
        ssq = jnp.sum(jnp.where(own, o * o, 0.0), axis=-1, keepdims=True)
        outs.append(o * lax.rsqrt(ssq * (1.0 / DA_V_DIM) + EPS))
    o = jnp.where((lane // DA_V_DIM) == 0, outs[0], outs[1])
    o_ref[...] = (o * gain_ref[...]).astype(o_ref.dtype)


def _attn_prompt(q, kt, vt, bias_tiles, gain, lam, q_block0, n_q):
    seq, w = q.shape
    pairs = w // LANES
    assert seq % ATT_STEP == 0
    n_tiles = ATT_STEP // ATT_BLOCK + 2
    return pl.pallas_call(
        functools.partial(_attn_prompt_kernel, q_block0=q_block0),
        grid=(pairs, n_q),
        in_specs=[
            pl.BlockSpec(memory_space=pltpu.SMEM),
            pl.BlockSpec((ATT_BLOCK, LANES), lambda p, i: (q_block0 + i, p)),
            pl.BlockSpec((LANES, seq), lambda p, i: (p, 0)),
            pl.BlockSpec((LANES, seq), lambda p, i: (p, 0)),
            pl.BlockSpec((1, n_tiles, 2 * ATT_BLOCK, ATT_STEP), lambda p, i: (p, 0, 0, 0)),
            pl.BlockSpec((1, LANES), lambda p, i: (0, 0)),
        ],
        out_specs=pl.BlockSpec((ATT_BLOCK, LANES), lambda p, i: (i, p)),
        out_shape=jax.ShapeDtypeStruct((n_q * ATT_BLOCK, w), bf16),
        scratch_shapes=[
            pltpu.VMEM((4 * ATT_BLOCK, LANES), f32),
            pltpu.VMEM((4 * ATT_BLOCK, LANES), f32),
            pltpu.VMEM((4 * ATT_BLOCK, ATT_STEP), f32),
            pltpu.VMEM((4 * ATT_BLOCK, ATT_STEP), f32),
        ],
        compiler_params=_cparams("parallel", "arbitrary"),
        name="attn_prompt",
    )(lam, q, kt, vt, bias_tiles, gain)


def _prompt_bias_tiles(rel_bias):
    h = rel_bias.shape[1]
    per_step = ATT_STEP // ATT_BLOCK
    i = jnp.arange(ATT_BLOCK)
    rel_diag = i[None, :] - i[:, None]
    rel_prev = rel_diag - ATT_BLOCK
    far = rel_bias[REL_BUCKETS // 2 - 1].astype(f32)
    b_diag = (_bias_of(rel_diag, rel_bias) - far[:, None, None]) * LOG2E
    b_prev = (_bias_of(rel_prev, rel_bias) - far[:, None, None]) * LOG2E
    mask = (i[None, :] // CHUNK) <= (i[:, None] // CHUNK)
    b_diag = jnp.where(mask[None], b_diag, NEG)
    zero = jnp.zeros_like(b_diag)
    dead = jnp.full_like(b_diag, NEG)
    tiles = [jnp.concatenate([zero] * per_step, axis=-1)]
    for r in range(per_step):
        blocks = [zero if j < r - 1 else b_prev if j == r - 1 else b_diag if j == r else dead
                  for j in range(per_step)]
        tiles.append(jnp.concatenate(blocks, axis=-1))
    tiles.append(jnp.concatenate([zero] * (per_step - 1) + [b_prev], axis=-1))
    tiles = jnp.stack(tiles, axis=1)
    tiles = tiles.reshape(h // 2, 2, per_step + 2, ATT_BLOCK, ATT_STEP)
    return jnp.transpose(tiles, (0, 2, 1, 3, 4)).reshape(h // 2, per_step + 2, 2 * ATT_BLOCK, ATT_STEP)


def _attn_sample_kernel(lam_ref, q_ref, kp_ref, vp_ref, kn_ref, vn_ref, bp_ref, bn_ref, gain_ref, o_ref):
    nq = q_ref.shape[0]
    pairs = q_ref.shape[1] // LANES
    lane = lax.broadcasted_iota(jnp.int32, (nq, LANES), 1)
    lam = lam_ref[0]
    nt = (((1,), (1,)), ((), ()))
    for p in range(pairs):
        sl = slice(p * LANES, (p + 1) * LANES)
        qs = q_ref[:, sl].astype(f32) * (DA_HEAD_DIM ** -0.5 * LOG2E)
        kp = kp_ref[sl, :].astype(bf16)
        vp = vp_ref[sl, :].astype(bf16)
        kn = kn_ref[:, sl].astype(bf16)
        vn = vn_ref[:, sl].astype(bf16)
        outs = []
        for hh in range(2):
            h = 2 * p + hh
            own = (lane // DA_V_DIM) == hh
            res = []
            for c in range(2):
                qm = jnp.where((lane // DA_HEAD_DIM) == 2 * hh + c, qs, 0.0).astype(bf16)
                sp = jnp.dot(qm, kp, preferred_element_type=f32) + bp_ref[h]
                sn = lax.dot_general(qm, kn, nt, preferred_element_type=f32) + bn_ref[h]
                m = jnp.maximum(jnp.max(sp, axis=-1, keepdims=True), jnp.max(sn, axis=-1, keepdims=True))
                pp = jnp.exp2(sp - m)
                pn = jnp.exp2(sn - m)
                l = jnp.sum(pp, axis=-1, keepdims=True) + jnp.sum(pn, axis=-1, keepdims=True)
                pv = (lax.dot_general(pp.astype(bf16), vp, nt, preferred_element_type=f32)
                      + jnp.dot(pn.astype(bf16), vn, preferred_element_type=f32))
                res.append(pv / l)
            o = res[0] - lam * res[1]
            ssq = jnp.sum(jnp.where(own, o * o, 0.0), axis=-1, keepdims=True)
            outs.append(o * lax.rsqrt(ssq * (1.0 / DA_V_DIM) + EPS))
        o = jnp.where((lane // DA_V_DIM) == 0, outs[0], outs[1])
        o_ref[:, sl] = (o * gain_ref[...]).astype(o_ref.dtype)


def _attn_sample(q, k_new, v_new, k_past, v_past, bias_past, bias_new, gain, lam, batch, nq, past):
    w = q.shape[1]
    h = bias_past.shape[0]
    return pl.pallas_call(
        _attn_sample_kernel,
        grid=(batch,),
        in_specs=[
            pl.BlockSpec(memory_space=pltpu.SMEM),
            pl.BlockSpec((nq, w), lambda b: (b, 0)),
            pl.BlockSpec((w, past), lambda b: (b, 0)),
            pl.BlockSpec((w, past), lambda b: (b, 0)),
            pl.BlockSpec((nq, w), lambda b: (b, 0)),
            pl.BlockSpec((nq, w), lambda b: (b, 0)),
            pl.BlockSpec((h, nq, past), lambda b: (0, 0, 0)),
            pl.BlockSpec((h, nq, nq), lambda b: (0, 0, 0)),
            pl.BlockSpec((1, LANES), lambda b: (0, 0)),
        ],
        out_specs=pl.BlockSpec((nq, w), lambda b: (b, 0)),
        out_shape=jax.ShapeDtypeStruct((batch * nq, w), bf16),
        compiler_params=_cparams("parallel"),
        name="attn_sample",
    )(lam, q, k_past, v_past, k_new, v_new, bias_past, bias_new, gain)


def _hgrn_chunk(q, f_logit, vv, g_logit, lb, gain, s0, lc):
    n_sub = lc // HG_SUB
    f = lb + (1.0 - lb) * jax.nn.sigmoid(f_logit)
    logf = jnp.log(f)
    k = 1.0 - f
    row = lax.broadcasted_iota(jnp.int32, (lc, lc), 0)
    col = lax.broadcasted_iota(jnp.int32, (lc, lc), 1)
    tril = (col <= row).astype(f32)
    b = jnp.dot(tril, logf, precision=lax.Precision.HIGHEST, preferred_element_type=f32)
    b_last = b[lc - 1:lc, :]

    o = jnp.dot((q * jnp.exp(b)).astype(bf16), s0.astype(bf16), preferred_element_type=f32)
    k_hat = k * jnp.exp(b_last - b)
    bt = b.T
    decay_col = jnp.exp(bt[:, lc - 1:lc])
    s_new = decay_col * s0 + jnp.dot(k_hat.T.astype(bf16), vv.astype(bf16), preferred_element_type=f32)

    if n_sub > 1:
        rows = []
        nt = (((1,), (1,)), ((), ()))
        for i in range(n_sub):
            lo = i * HG_SUB
            if i == 0:
                rows.append(jnp.zeros((HG_SUB, lc), f32))
                continue
            beta = b[lo - 1:lo, :]
            q_t = q[lo:lo + HG_SUB, :] * jnp.exp(b[lo:lo + HG_SUB, :] - beta)
            k_t = k * jnp.exp(jnp.minimum(beta - b, 0.0))
            rows.append(lax.dot_general(q_t.astype(bf16), k_t.astype(bf16), nt, preferred_element_type=f32))
        a_off = jnp.concatenate(rows, axis=0)
        a_off = jnp.where(col < (row // HG_SUB) * HG_SUB, a_off, 0.0)
        o = o + jnp.dot(a_off.astype(bf16), vv.astype(bf16), preferred_element_type=f32)

    ones = jnp.ones((HG_DIM, LANES), bf16)
    srow = lax.broadcasted_iota(jnp.int32, (HG_SUB, HG_DIM), 0)
    orow = lax.broadcasted_iota(jnp.int32, (HG_SUB, HG_DIM), 0)
    blocks = []
    for i in range(n_sub):
        lo = i * HG_SUB
        b_i = b[lo:lo + HG_SUB, :]
        k_i = k[lo:lo + HG_SUB, :]
        v_i = vv[lo:lo + HG_SUB, :]
        q_i = q[lo:lo + HG_SUB, :]
        d_rows = []
        for t in range(HG_SUB):
            arg = jnp.where(srow <= t, b_i[t:t + 1, :] - b_i, -jnp.inf)
            d_rows.append(q_i[t:t + 1, :] * k_i * jnp.exp(arg))
        d3 = jnp.concatenate(d_rows, axis=0)
        a_rep = jnp.dot(d3.astype(bf16), ones, preferred_element_type=f32)
        o_i = jnp.zeros((HG_SUB, HG_DIM), f32)
        for t in range(HG_SUB):
            o_t = jnp.sum(a_rep[t * HG_SUB:(t + 1) * HG_SUB, :] * v_i, axis=0, keepdims=True)
            o_i = jnp.where(orow == t, o_t, o_i)
        blocks.append(o_i)
    o = o + (jnp.concatenate(blocks, axis=0) if n_sub > 1 else blocks[0])

    on = o * lax.rsqrt(jnp.mean(o * o, axis=-1, keepdims=True) + EPS) * gain
    gate = g_logit * jax.nn.sigmoid(g_logit)
    return on * gate, s_new


def _hgrn_kernel(hz_ref, lb_ref, gain_ref, s0_ref, o_ref, s_ref, st_sc, *, lc, heads):
    j = pl.program_id(1)
    w = heads * HG_DIM

    @pl.when(j == 0)
    def _():
        st_sc[...] = s0_ref[0]

    n_chunks = hz_ref.shape[0] // lc
    for c in range(n_chunks):
        rs = slice(c * lc, (c + 1) * lc)
        for h in range(heads):
            cs = slice(h * HG_DIM, (h + 1) * HG_DIM)
            o, s_new = _hgrn_chunk(
                hz_ref[rs, h * HG_DIM:(h + 1) * HG_DIM],
                hz_ref[rs, w + h * HG_DIM:w + (h + 1) * HG_DIM],
                hz_ref[rs, 2 * w + h * HG_DIM:2 * w + (h + 1) * HG_DIM],
                hz_ref[rs, 3 * w + h * HG_DIM:3 * w + (h + 1) * HG_DIM],
                lb_ref[:, cs], gain_ref[...], st_sc[h], lc)
            st_sc[h] = s_new
            o_ref[rs, cs] = o.astype(o_ref.dtype)

    @pl.when(j == pl.num_programs(1) - 1)
    def _():
        s_ref[0] = st_sc[...]


def _hgrn(hz, lb, gain, s0, batch, seq, lc, chunks_per_step):
    t = hz.shape[0]
    w = hz.shape[1] // 4
    heads = w // HG_DIM
    rows = lc * chunks_per_step
    steps = seq // rows
    return pl.pallas_call(
        functools.partial(_hgrn_kernel, lc=lc, heads=heads),
        grid=(batch, steps),
        in_specs=[
            pl.BlockSpec((rows, 4 * w), lambda b, j: (b * steps + j, 0)),
            pl.BlockSpec((1, w), lambda b, j: (0, 0)),
            pl.BlockSpec((1, HG_DIM), lambda b, j: (0, 0)),
            pl.BlockSpec((1, heads, HG_DIM, HG_DIM), lambda b, j: (b, 0, 0, 0)),
        ],
        out_specs=[
            pl.BlockSpec((rows, w), lambda b, j: (b * steps + j, 0)),
            pl.BlockSpec((1, heads, HG_DIM, HG_DIM), lambda b, j: (b, 0, 0, 0)),
        ],
        out_shape=[
            jax.ShapeDtypeStruct((t, w), bf16),
            jax.ShapeDtypeStruct((batch, heads, HG_DIM, HG_DIM), f32),
        ],
        scratch_shapes=[pltpu.VMEM((heads, HG_DIM, HG_DIM), f32)],
        compiler_params=_cparams("parallel", "arbitrary"),
        name="hgrn2",
    )(hz, lb, gain, s0)


def _top_rows(vals, k, payload=None):
    n_rows = vals.shape[0]
    row = lax.broadcasted_iota(jnp.int32, vals.shape, 0)
    out_v, out_i = [], []
    for _ in range(k):
        m = jnp.max(vals, axis=0, keepdims=True)
        first = jnp.min(jnp.where(vals == m, row, n_rows), axis=0, keepdims=True)
        hit = row == first
        out_v.append(m)
        if payload is None:
            out_i.append(first)
        else:
            out_i.append(jnp.max(jnp.where(hit, payload, -1), axis=0, keepdims=True))
        vals = jnp.where(hit, -jnp.inf, vals)
    return jnp.concatenate(out_v, axis=0), jnp.concatenate(out_i, axis=0)


def _retrieve_kernel(oa_ref, ob_ref, x_ref, wo_ref, g_ref, wqt_ref, sk_ref,
                     hp_ref, xn_ref, idx_ref, gate_ref, qt_sc, s1v_sc, s1i_sc, idx_sc, gate_sc):
    half = oa_ref.shape[1]
    y = (jnp.dot(oa_ref[...], wo_ref[0:half, :], preferred_element_type=f32)
         + jnp.dot(ob_ref[...], wo_ref[half:, :], preferred_element_type=f32))
    hp = x_ref[...] + y
    hp_ref[...] = hp
    xn = hp * lax.rsqrt(jnp.mean(hp * hp, axis=-1, keepdims=True) + EPS) * g_ref[...]
    xn_ref[...] = xn
    nt = (((1,), (1,)), ((), ()))
    qt_sc[...] = lax.dot_general(wqt_ref[...], xn.astype(bf16), nt, preferred_element_type=f32).astype(bf16)

    n_half = sk_ref.shape[0]

    def stage1(hc, carry):
        start = pl.multiple_of(hc * PEER_SUB_DIM, PEER_SUB_DIM)
        s = jnp.dot(sk_ref[hc], qt_sc[pl.ds(start, PEER_SUB_DIM), :], preferred_element_type=f32)
        v, i = _top_rows(s, PEER_TOPK)
        s1v_sc[hc] = v
        s1i_sc[hc] = i
        return carry

    lax.fori_loop(0, n_half, stage1, 0)

    tm = x_ref.shape[0]
    sub = lax.broadcasted_iota(jnp.int32, (8, tm), 0)

    def stage2(h, carry):
        v1, i1 = s1v_sc[2 * h], s1i_sc[2 * h]
        v2, i2 = s1v_sc[2 * h + 1], s1i_sc[2 * h + 1]
        cv, ci = [], []
        for i in range(PEER_TOPK):
            nj = PEER_TOPK // (i + 1)
            rows = PEER_TOPK if nj > 8 else 8
            a = v1[i:i + 1, :] + v2[0:rows, :]
            e = i1[i:i + 1, :] * PEER_KEYS + i2[0:rows, :]
            if nj < rows:
                a = jnp.where(sub < nj, a, -jnp.inf)
            cv.append(a)
            ci.append(e)
        top_s, eidx = _top_rows(jnp.concatenate(cv, axis=0), PEER_TOPK, jnp.concatenate(ci, axis=0))
        p = jnp.exp(top_s - top_s[0:1, :])
        g = p / jnp.sum(p, axis=0, keepdims=True)
        r0 = pl.multiple_of(h * PEER_TOPK, PEER_TOPK)
        gate_sc[pl.ds(r0, PEER_TOPK), :] = g
        idx_sc[pl.ds(r0, PEER_TOPK), :] = eidx
        return carry

    lax.fori_loop(0, n_half // 2, stage2, 0)
    idx_ref[...] = pltpu.bitcast(pltpu.bitcast(idx_sc[...], f32).T, jnp.int32)
    gate_ref[...] = gate_sc[...].T


def _retrieve(oa, ob, x, wo_bf, g2, wqt_bf, sk_bf, tm, x_block0=0, ob_block0=0):
    t, half = oa.shape
    d = x.shape[1]
    e = wqt_bf.shape[0]
    n_half = sk_bf.shape[0]
    n_sel = (n_half // 2) * PEER_TOPK
    return pl.pallas_call(
        _retrieve_kernel,
        grid=(t // tm,),
        in_specs=[
            pl.BlockSpec((tm, half), lambda i: (i, 0)),
            pl.BlockSpec((tm, half), lambda i: (ob_block0 + i, 0)),
            pl.BlockSpec((tm, d), lambda i: (x_block0 + i, 0)),
            pl.BlockSpec((2 * half, d), lambda i: (0, 0)),
            pl.BlockSpec((1, d), lambda i: (0, 0)),
            pl.BlockSpec((e, d), lambda i: (0, 0)),
            pl.BlockSpec((n_half, PEER_KEYS, PEER_SUB_DIM), lambda i: (0, 0, 0)),
        ],
        out_specs=[
            pl.BlockSpec((tm, d), lambda i: (i, 0)),
            pl.BlockSpec((tm, d), lambda i: (i, 0)),
            pl.BlockSpec((tm, n_sel), lambda i: (i, 0)),
            pl.BlockSpec((tm, n_sel), lambda i: (i, 0)),
        ],
        out_shape=[
            jax.ShapeDtypeStruct((t, d), f32),
            jax.ShapeDtypeStruct((t, d), f32),
            jax.ShapeDtypeStruct((t, n_sel), jnp.int32),
            jax.ShapeDtypeStruct((t, n_sel), f32),
        ],
        scratch_shapes=[
            pltpu.VMEM((e, tm), bf16),
            pltpu.VMEM((n_half, PEER_TOPK, tm), f32),
            pltpu.VMEM((n_half, PEER_TOPK, tm), jnp.int32),
            pltpu.VMEM((n_sel, tm), jnp.int32),
            pltpu.VMEM((n_sel, tm), f32),
        ],
        compiler_params=_cparams("parallel"),
        name="retrieve",
    )(oa, ob, x, wo_bf, g2, wqt_bf, sk_bf)


def _pack_table(tab):
    half = tab.shape[1] // 2
    lo = lax.bitcast_convert_type(tab[:, :half].astype(bf16), jnp.uint16).astype(jnp.uint32)
    hi = lax.bitcast_convert_type(tab[:, half:].astype(bf16), jnp.uint16).astype(jnp.uint32)
    return lax.bitcast_convert_type(lo | (hi << 16), jnp.int32)


def _sc_layout(n_tok, n_sel, words):
    info = plsc.get_sparse_core_info()
    n_workers = info.num_cores * info.num_subcores
    lanes = info.num_lanes
    block = 8 * lanes
    assert n_tok % (2 * n_workers) == 0 and words % block == 0 and (n_sel // 2) % lanes == 0
    return info, lanes, n_tok // n_workers, n_sel // 2, block


def _selected_dots(table, idx, x):
    n_tok, n_sel = idx.shape
    words = table.shape[1]
    info, lanes, tpw, half_rows, block = _sc_layout(n_tok, n_sel, words)
    idx2 = idx.reshape(2 * n_tok, half_rows)
    mesh = plsc.VectorSubcoreMesh(core_axis_name="c", subcore_axis_name="s")

    @functools.partial(
        pl.kernel, out_type=jax.ShapeDtypeStruct((n_tok, n_sel), f32), mesh=mesh,
        compiler_params=pltpu.CompilerParams(needs_layout_passes=False),
        scratch_types=[
            pltpu.VMEM((2 * tpw, half_rows), jnp.int32),
            pltpu.VMEM((half_rows, words), jnp.int32),
            pltpu.VMEM((half_rows, words), jnp.int32),
            pltpu.VMEM((2 * words,), f32),
            pltpu.VMEM((2 * words,), f32),
            pltpu.VMEM((n_sel // 8, 8 * lanes), f32),
            pltpu.VMEM((tpw, n_sel), f32),
            pltpu.SemaphoreType.DMA, pltpu.SemaphoreType.DMA, pltpu.SemaphoreType.DMA, pltpu.SemaphoreType.DMA,
        ])
    def dots(t_hbm, i_hbm, x_hbm, o_hbm, idx_v, rows_a, rows_b, x_a, x_b, part_v, act_v, sem_a, sem_b, sem_xa, sem_xb):
        wid = lax.axis_index("s") * info.num_cores + lax.axis_index("c")
        base = wid * tpw
        pltpu.sync_copy(i_hbm.at[pl.ds(base * 2, 2 * tpw)], idx_v)

        def accumulate(rows, x_v, half):
            @pl.loop(0, words // block)
            def _(cb):
                xs = [x_v[pl.ds(p * words + cb * block + c * lanes, lanes)] for c in range(8) for p in range(2)]

                @plsc.parallel_loop(0, half_rows)
                def _(row):
                    packed = [rows[row, pl.ds(cb * block + c * lanes, lanes)] for c in range(8)]
                    unp = [plsc.unpack(plsc.bitcast(p, bf16), format=plsc.PackFormat.INTERLEAVED) for p in packed]
                    prods = [unp[c][p] * xs[2 * c + p] for c in range(8) for p in range(2)]
                    while len(prods) > 1:
                        prods = [prods[k] + prods[k + 1] for k in range(0, len(prods), 2)]
                    prow = half * half_rows + row
                    plsc.addupdate(part_v.at[prow // 8, pl.ds((prow % 8) * lanes, lanes)], prods[0])

        def gather(half_index, rows, sem):
            return pltpu.make_async_copy(t_hbm.at[idx_v.at[half_index]], rows, sem)

        def xcopy(t, x_v, sem):
            return pltpu.make_async_copy(x_hbm.at[base + t], x_v, sem)

        def token(t, x_v, sem_x, x_next, sem_xn):
            gather(2 * t + 1, rows_b, sem_b).start()

            @pl.when(t + 1 < tpw)
            def _():
                xcopy(t + 1, x_next, sem_xn).start()

            zero = jnp.zeros((lanes,), f32)
            for r in range(n_sel):
                part_v[r // 8, pl.ds((r % 8) * lanes, lanes)] = zero
            xcopy(t, x_v, sem_x).wait()
            gather(2 * t, rows_a, sem_a).wait()
            accumulate(rows_a, x_v, 0)

            @pl.when(t + 1 < tpw)
            def _():
                gather(2 * t + 2, rows_a, sem_a).start()

            gather(2 * t + 1, rows_b, sem_b).wait()
            accumulate(rows_b, x_v, 1)
            it = lax.iota(jnp.int32, lanes)
            for g in range(n_sel // lanes):
                prow = g * (lanes // 8) + it // 8
                pcol = (it % 8) * lanes
                tot = plsc.load_gather(part_v, [prow, pcol])
                for l in range(1, lanes):
                    tot = tot + plsc.load_gather(part_v, [prow, pcol + l])
                act_v[t, pl.ds(g * lanes, lanes)] = tot

        gather(0, rows_a, sem_a).start()
        xcopy(0, x_a, sem_xa).start()

        @pl.loop(0, tpw // 2)
        def _(i):
            token(2 * i, x_a, sem_xa, x_b, sem_xb)
            token(2 * i + 1, x_b, sem_xb, x_a, sem_xa)

        pltpu.sync_copy(act_v, o_hbm.at[pl.ds(base, tpw)])

    return dots(table, idx2, x)


def _gate_weights_kernel(act_ref, gate_ref, after_ref, w_ref):
    del after_ref
    a = act_ref[...]
    w_ref[...] = gate_ref[...] * (0.5 * a * (1.0 + lax.erf(a * (2.0 ** -0.5))))


def _gate_weights(act, gate, after, tm):
    t, n_sel = act.shape
    spec = pl.BlockSpec((tm, n_sel), lambda i: (i, 0))
    return pl.pallas_call(
        _gate_weights_kernel,
        grid=(t // tm,),
        in_specs=[spec, spec, pl.BlockSpec(memory_space=pl.ANY)],
        out_specs=spec,
        out_shape=jax.ShapeDtypeStruct((t, n_sel), f32),
        compiler_params=_cparams("parallel"),
        name="peer_weights",
    )(act, gate, after)


def _weighted_rows(table, idx, w):
    n_tok, n_sel = w.shape
    words = table.shape[1]
    info, lanes, tpw, half_rows, _ = _sc_layout(n_tok, n_sel, words)
    idx2 = idx.reshape(2 * n_tok, half_rows)
    mesh = plsc.VectorSubcoreMesh(core_axis_name="c", subcore_axis_name="s")

    @functools.partial(
        pl.kernel, out_type=jax.ShapeDtypeStruct((n_tok, 2 * words), f32), mesh=mesh,
        compiler_params=pltpu.CompilerParams(needs_layout_passes=False),
        scratch_types=[
            pltpu.VMEM((2 * tpw, half_rows), jnp.int32),
            pltpu.VMEM((tpw, n_sel), f32),
            pltpu.VMEM((half_rows, words), jnp.int32),
            pltpu.VMEM((half_rows, words), jnp.int32),
            pltpu.VMEM((2 * words,), f32),
            pltpu.VMEM((2 * words,), f32),
            pltpu.SemaphoreType.DMA, pltpu.SemaphoreType.DMA, pltpu.SemaphoreType.DMA, pltpu.SemaphoreType.DMA,
        ])
    def mix(t_hbm, i_hbm, w_hbm, o_hbm, idx_v, w_v, rows_a, rows_b, out_a, out_b, sem_a, sem_b, sem_oa, sem_ob):
        wid = lax.axis_index("s") * info.num_cores + lax.axis_index("c")
        base = wid * tpw
        pltpu.sync_copy(i_hbm.at[pl.ds(base * 2, 2 * tpw)], idx_v)
        pltpu.sync_copy(w_hbm.at[pl.ds(base, tpw)], w_v)

        def accumulate(rows, out_v, t, half, first):
            if first:
                zero = jnp.zeros((lanes,), f32)
                for c in range(2 * words // lanes):
                    out_v[pl.ds(c * lanes, lanes)] = zero

            @pl.loop(0, half_rows // lanes)
            def _(g):
                wv = w_v[t, pl.ds(half * half_rows + g * lanes, lanes)]
                ws = [wv.at[jnp.full((lanes,), r, jnp.int32)].get(mode="promise_in_bounds") for r in range(lanes)]

                @plsc.parallel_loop(0, words // lanes)
                def _(c):
                    packed = [rows[g * lanes + r, pl.ds(c * lanes, lanes)] for r in range(lanes)]
                    unp = [plsc.unpack(plsc.bitcast(p, bf16), format=plsc.PackFormat.INTERLEAVED) for p in packed]
                    lo = [unp[r][0] * ws[r] for r in range(lanes)]
                    hi = [unp[r][1] * ws[r] for r in range(lanes)]
                    while len(lo) > 1:
                        lo = [lo[k] + lo[k + 1] for k in range(0, len(lo), 2)]
                        hi = [hi[k] + hi[k + 1] for k in range(0, len(hi), 2)]
                    plsc.addupdate(out_v.at[pl.ds(c * lanes, lanes)], lo[0])
                    plsc.addupdate(out_v.at[pl.ds(words + c * lanes, lanes)], hi[0])

        def gather(half_index, rows, sem):
            return pltpu.make_async_copy(t_hbm.at[idx_v.at[half_index]], rows, sem)

        def token(t, out_v, sem_o, out_in_flight):
            gather(2 * t + 1, rows_b, sem_b).start()
            gather(2 * t, rows_a, sem_a).wait()

            @pl.when(out_in_flight)
            def _():
                pltpu.make_async_copy(out_v, o_hbm.at[base + t], sem_o).wait()

            accumulate(rows_a, out_v, t, 0, True)

            @pl.when(t + 1 < tpw)
            def _():
                gather(2 * t + 2, rows_a, sem_a).start()

            gather(2 * t + 1, rows_b, sem_b).wait()
            accumulate(rows_b, out_v, t, 1, False)
            pltpu.make_async_copy(out_v, o_hbm.at[base + t], sem_o).start()

        gather(0, rows_a, sem_a).start()

        @pl.loop(0, tpw // 2)
        def _(i):
            token(2 * i, out_a, sem_oa, i > 0)
            token(2 * i + 1, out_b, sem_ob, i > 0)

        pltpu.make_async_copy(out_a, o_hbm.at[base], sem_oa).wait()
        pltpu.make_async_copy(out_b, o_hbm.at[base], sem_ob).wait()

    return mix(table, idx2, w)


def _final_kernel(hp_ref, peer_ref, gf_ref, yin_ref, y_ref, done_ref):
    del yin_ref
    done_ref[...] = jnp.zeros(done_ref.shape, f32)
    h = hp_ref[...] + peer_ref[...]
    y_ref[...] = h * lax.rsqrt(jnp.mean(h * h, axis=-1, keepdims=True) + EPS) * gf_ref[...]


def _final(hp, peer, gf, y_all, y_block0, tm):
    t, d = hp.shape
    return pl.pallas_call(
        _final_kernel,
        grid=(t // tm,),
        in_specs=[
            pl.BlockSpec((tm, d), lambda i: (i, 0)),
            pl.BlockSpec((tm, d), lambda i: (i, 0)),
            pl.BlockSpec((1, d), lambda i: (0, 0)),
            pl.BlockSpec(memory_space=pl.ANY),
        ],
        out_specs=[pl.BlockSpec((tm, d), lambda i: (y_block0 + i, 0)),
                   pl.BlockSpec((8, LANES), lambda i: (0, 0))],
        out_shape=[jax.ShapeDtypeStruct(y_all.shape, f32), jax.ShapeDtypeStruct((8, LANES), f32)],
        input_output_aliases={3: 0},
        compiler_params=_cparams("arbitrary"),
        name="final_norm",
    )(hp, peer, gf, y_all)


def _channel_mixer_front(oa, ob, x, x_block0, ob_block0, wo_bf, g2, wqt_bf, sk_bf, u_pk, tm):
    hp, xn, eidx, gate = _retrieve(oa, ob, x, wo_bf, g2, wqt_bf, sk_bf, tm, x_block0, ob_block0)
    return hp, eidx, gate, _selected_dots(u_pk, eidx, xn)


def _channel_mixer_mix(front, after, v_pk, tm):
    _, eidx, gate, act = front
    return _weighted_rows(v_pk, eidx, _gate_weights(act, gate, after, tm))


def kernel(x_prompt, x_sample, cache_k, cache_v, state_hgrn, norm1, w_in, da_lq1, da_lk1, da_lq2, da_lk2,
           da_out_norm, hg_lb_logits, hg_out_norm, w_out, rel_bias, norm2, peer_w_q, peer_sub_keys,
           peer_u, peer_v, final_norm):
    batch, seq, d = x_prompt.shape
    dbatch, dseq, _ = x_sample.shape
    past = cache_k.shape[2]
    da_heads = cache_k.shape[3]
    hg_heads = state_hgrn.shape[2]
    depth = w_in.shape[0]
    assert depth == 1 and seq % ATT_BLOCK == 0 and seq % CHUNK == 0 and dseq % HG_SUB == 0 and dseq <= CHUNK
    assert past % CHUNK == 0 and (past + dseq - 1) // CHUNK == past // CHUNK

    l = 0
    lam_init = 0.8 - 0.6 * math.exp(-0.3 * l)
    lam = (jnp.exp(jnp.sum(da_lq1[l].astype(f32) * da_lk1[l].astype(f32)))
           - jnp.exp(jnp.sum(da_lq2[l].astype(f32) * da_lk2[l].astype(f32))) + lam_init).reshape(1)
    lb = jnp.cumsum(jax.nn.softmax(hg_lb_logits.astype(f32), axis=0), axis=0)[l].reshape(1, -1)
    da_gain = (jnp.tile(da_out_norm[l].astype(f32), 2) * (1.0 - lam_init)).reshape(1, LANES)
    hg_gain = hg_out_norm[l].astype(f32).reshape(1, HG_DIM)
    g1 = norm1[l].astype(f32).reshape(1, d)
    g2 = norm2[l].astype(f32).reshape(1, d)
    gf = final_norm.astype(f32).reshape(1, d)
    w_in_bf = w_in[l].astype(bf16)
    wkv_t_bf = w_in[l][:, cache_k.shape[3] * cache_k.shape[4]:3 * cache_k.shape[3] * cache_k.shape[4]].T.astype(bf16)
    wo_bf = w_out[l].astype(bf16)
    wqt_bf = peer_w_q[l].T.astype(bf16)
    sk_bf = peer_sub_keys[l].reshape(-1, PEER_KEYS, PEER_SUB_DIM).astype(bf16)
    u_pk = _pack_table(peer_u[l])
    v_pk = _pack_table(peer_v[l])

    bias_tiles = _prompt_bias_tiles(rel_bias)
    q_pos = past + jnp.arange(dseq)
    bias_s = _bias_of(jnp.arange(past + dseq)[None, :] - q_pos[:, None], rel_bias) * LOG2E
    bias_past, bias_new = bias_s[:, :, :past], bias_s[:, :, past:]
    zero_state = jnp.zeros((1, hg_heads, HG_DIM, HG_DIM), f32)

    dh2 = 2 * DA_HEAD_DIM
    wq = da_heads * dh2
    tm = 512
    x_all = x_prompt.reshape(batch * seq, d)
    kt_all = jnp.zeros((batch * wq, seq), f32)
    vt_all = jnp.zeros((batch * wq, seq), f32)
    y_all = jnp.zeros((batch * seq, d), f32)
    nq = seq // ATT_BLOCK
    blocks = seq // tm
    cuts = {0: 4, 1: 2}
    segments = [(b, c * (nq // cuts.get(b, 1)), nq // cuts.get(b, 1))
                for b in range(batch) for c in range(cuts.get(b, 1))]
    assert all(nq % n == 0 and (nq // n * ATT_BLOCK) % (2 * tm) == 0 for n in cuts.values())
    s_p = []
    done = [jnp.zeros((8, LANES), f32)] * 3
    fronts, peers = [], []

    def finish(k, y_all):
        return _final(fronts[k][0][0], peers[k], gf, y_all, fronts[k][1], tm)

    for s, (b, q0, n_q) in enumerate(segments):
        if q0 == 0:
            pin = done[s] if b > len(cuts) else done[0]
            q, kt, vt, hz, kt_all, vt_all = _inproj_prompt(x_all, g1, w_in_bf, wkv_t_bf, kt_all, vt_all, pin,
                                                           b, seq, tm)
            ob, s_new = _hgrn(hz, lb, hg_gain, zero_state, 1, seq, CHUNK, 4)
            s_p.append(s_new)
        oa = _attn_prompt(q, kt, vt, bias_tiles, da_gain, lam, q0, n_q)
        if s >= 1:
            peers.append(_channel_mixer_mix(fronts[s - 1][0], oa, v_pk, tm))
        row0 = q0 * ATT_BLOCK // tm
        fronts.append((_channel_mixer_front(oa, ob, x_all, b * blocks + row0, row0, wo_bf, g2, wqt_bf, sk_bf, u_pk,
                                            tm), b * blocks + row0))
        if s >= 2:
            y_all, token = finish(s - 2, y_all)
            done.append(token)
    peers.append(_channel_mixer_mix(fronts[-1][0], fronts[-1][0][0], v_pk, tm))
    for k in (len(segments) - 2, len(segments) - 1):
        y_all, _ = finish(k, y_all)

    xs = x_sample.reshape(dbatch * dseq, d)
    q, k_s, v_s, hz = _inproj_sample(xs, g1, w_in_bf, wq)
    kp_t = jnp.transpose(cache_k[l], (0, 2, 3, 1)).reshape(dbatch * wq, past)
    vp_t = jnp.transpose(cache_v[l], (0, 2, 3, 1)).reshape(dbatch * wq, past)
    oa = _attn_sample(q, k_s, v_s, kp_t, vp_t, bias_past, bias_new, da_gain, lam, dbatch, dseq, past)
    ob, s_s = _hgrn(hz, lb, hg_gain, state_hgrn[l].astype(f32), dbatch, dseq, dseq, 1)
    front = _channel_mixer_front(oa, ob, xs, 0, 0, wo_bf, g2, wqt_bf, sk_bf, u_pk, dbatch * dseq)
    peer = _channel_mixer_mix(front, front[0], v_pk, dbatch * dseq)
    y_s, _ = _final(front[0], peer, gf, jnp.zeros((dbatch * dseq, d), f32), 0, dbatch * dseq)

    y_prompt = y_all.reshape(batch, seq, d)
    y_sample = y_s.reshape(dbatch, dseq, d)
    k_prompt = jnp.transpose(kt_all.reshape(batch, da_heads, dh2, seq), (0, 3, 1, 2))[None]
    v_prompt = jnp.transpose(vt_all.reshape(batch, da_heads, DA_V_DIM, seq), (0, 3, 1, 2))[None]
    state_prompt = jnp.concatenate(s_p, axis=0)[None].astype(state_hgrn.dtype)
    k_sample = k_s.reshape(1, dbatch, dseq, da_heads, dh2)
    v_sample = v_s.reshape(1, dbatch, dseq, da_heads, DA_V_DIM)
    state_sample = s_s[None].astype(state_hgrn.dtype)
    return (y_prompt, y_sample, k_prompt, v_prompt, state_prompt, k_sample, v_sample, state_sample)
```

```python
import functools
import math

import jax
import jax.numpy as jnp
from jax import lax
from jax.experimental import pallas as pl
from jax.experimental.pallas import tpu as pltpu
from jax.experimental.pallas import tpu_sc as plsc

CHUNK = 64
DA_HEAD_DIM = 32
DA_V_DIM = 2 * DA_HEAD_DIM
HG_DIM = 128
REL_BUCKETS = 32
REL_MAX_DIST = 128
PEER_KEYS = 128
PEER_TOPK = 16
PEER_SUB_DIM = 64
EPS = 1e-6
NEG = -1e30
LOG2E = 1.4426950408889634

LANES = 128
ATT_BLOCK = 128
ATT_STEP = 512
HG_SUB = 16
V7X_VMEM_BYTES = 64 * 1024 * 1024
VMEM_LIMIT = V7X_VMEM_BYTES * 7 // 8

f32 = jnp.float32
bf16 = jnp.bfloat16


def _cparams(*sem):
    return pltpu.CompilerParams(dimension_semantics=sem, vmem_limit_bytes=VMEM_LIMIT)


def _inproj_prompt_kernel(x_ref, g_ref, w_ref, wkv_t_ref, kin_ref, vin_ref, after_ref,
                          q_ref, kt_ref, vt_ref, hz_ref, ktf_ref, vtf_ref):
    del kin_ref, vin_ref
    del after_ref
    x = x_ref[...]
    xn = x * lax.rsqrt(jnp.mean(x * x, axis=-1, keepdims=True) + EPS) * g_ref[...]
    xb = xn.astype(bf16)
    wq = q_ref.shape[1]
    nt = (((1,), (1,)), ((), ()))
    q_ref[...] = jnp.dot(xb, w_ref[:, 0:wq], preferred_element_type=f32).astype(bf16)
    kt = lax.dot_general(wkv_t_ref[0:wq, :], xb, nt, preferred_element_type=f32)
    ktf_ref[...] = kt
    kt_ref[...] = kt.astype(bf16)
    vt = lax.dot_general(wkv_t_ref[wq:2 * wq, :], xb, nt, preferred_element_type=f32)
    vtf_ref[...] = vt
    vt_ref[...] = vt.astype(bf16)
    hz_ref[...] = jnp.dot(xb, w_ref[:, 3 * wq:], preferred_element_type=f32)


def _inproj_prompt(x, g, w_bf, wkv_t_bf, kt_all, vt_all, after, b, seq, tm):
    d = x.shape[1]
    e = w_bf.shape[1]
    wq = wkv_t_bf.shape[0] // 2
    steps = seq // tm
    return pl.pallas_call(
        _inproj_prompt_kernel,
        grid=(steps,),
        in_specs=[
            pl.BlockSpec((tm, d), lambda i: (b * steps + i, 0)),
            pl.BlockSpec((1, d), lambda i: (0, 0)),
            pl.BlockSpec((d, e), lambda i: (0, 0)),
            pl.BlockSpec((2 * wq, d), lambda i: (0, 0)),
            pl.BlockSpec(memory_space=pl.ANY),
            pl.BlockSpec(memory_space=pl.ANY),
            pl.BlockSpec(memory_space=pl.ANY),
        ],
        out_specs=[
            pl.BlockSpec((tm, wq), lambda i: (i, 0)),
            pl.BlockSpec((wq, tm), lambda i: (0, i)),
            pl.BlockSpec((wq, tm), lambda i: (0, i)),
            pl.BlockSpec((tm, e - 3 * wq), lambda i: (i, 0)),
            pl.BlockSpec((wq, tm), lambda i: (b, i)),
            pl.BlockSpec((wq, tm), lambda i: (b, i)),
        ],
        out_shape=[
            jax.ShapeDtypeStruct((seq, wq), bf16),
            jax.ShapeDtypeStruct((wq, seq), bf16),
            jax.ShapeDtypeStruct((wq, seq), bf16),
            jax.ShapeDtypeStruct((seq, e - 3 * wq), f32),
            jax.ShapeDtypeStruct(kt_all.shape, f32),
            jax.ShapeDtypeStruct(vt_all.shape, f32),
        ],
        input_output_aliases={4: 4, 5: 5},
        compiler_params=_cparams("parallel"),
        name="inproj",
    )(x, g, w_bf, wkv_t_bf, kt_all, vt_all, after)


def _inproj_sample_kernel(x_ref, g_ref, w_ref, q_ref, k_ref, v_ref, hz_ref):
    x = x_ref[...]
    xn = x * lax.rsqrt(jnp.mean(x * x, axis=-1, keepdims=True) + EPS) * g_ref[...]
    xb = xn.astype(bf16)
    wq = q_ref.shape[1]
    q_ref[...] = jnp.dot(xb, w_ref[:, 0:wq], preferred_element_type=f32).astype(bf16)
    k_ref[...] = jnp.dot(xb, w_ref[:, wq:2 * wq], preferred_element_type=f32)
    v_ref[...] = jnp.dot(xb, w_ref[:, 2 * wq:3 * wq], preferred_element_type=f32)
    hz_ref[...] = jnp.dot(xb, w_ref[:, 3 * wq:], preferred_element_type=f32)


def _inproj_sample(x, g, w_bf, wq):
    t, d = x.shape
    e = w_bf.shape[1]
    return pl.pallas_call(
        _inproj_sample_kernel,
        grid=(1,),
        in_specs=[
            pl.BlockSpec((t, d), lambda i: (0, 0)),
            pl.BlockSpec((1, d), lambda i: (0, 0)),
            pl.BlockSpec((d, e), lambda i: (0, 0)),
        ],
        out_specs=[
            pl.BlockSpec((t, wq), lambda i: (0, 0)),
            pl.BlockSpec((t, wq), lambda i: (0, 0)),
            pl.BlockSpec((t, wq), lambda i: (0, 0)),
            pl.BlockSpec((t, e - 3 * wq), lambda i: (0, 0)),
        ],
        out_shape=[
            jax.ShapeDtypeStruct((t, wq), bf16),
            jax.ShapeDtypeStruct((t, wq), f32),
            jax.ShapeDtypeStruct((t, wq), f32),
            jax.ShapeDtypeStruct((t, e - 3 * wq), f32),
        ],
        compiler_params=_cparams("arbitrary"),
        name="inproj_sample",
    )(x, g, w_bf)


def _rel_bucket(rel):
    nb = REL_BUCKETS // 2
    max_exact = nb // 2
    ret = jnp.where(rel > 0, nb, 0)
    n = jnp.abs(rel)
    nf = jnp.maximum(n, 1).astype(f32)
    large = max_exact + (jnp.log(nf / max_exact) / math.log(REL_MAX_DIST / max_exact)
                         * (nb - max_exact)).astype(jnp.int32)
    large = jnp.minimum(large, nb - 1)
    return ret + jnp.where(n < max_exact, n, large)


def _bias_of(rel, rel_bias):
    bucket = _rel_bucket(rel)
    out = jnp.zeros((rel_bias.shape[1],) + rel.shape, f32)
    for b in range(REL_BUCKETS):
        out = out + jnp.where(bucket[None] == b, rel_bias[b].astype(f32)[:, None, None], 0.0)
    return out


def _attn_prompt_kernel(lam_ref, q_ref, kt_ref, vt_ref, bias_ref, gain_ref, o_ref, m_sc, acc_sc, sa_sc, sb_sc,
                        *, q_block0):
    qb = pl.program_id(1) + q_block0
    tq = q_ref.shape[0]
    seq = kt_ref.shape[1]
    per_step = ATT_STEP // ATT_BLOCK
    lane = lax.broadcasted_iota(jnp.int32, (tq, LANES), 1)
    feat_k = lax.broadcasted_iota(jnp.int32, (LANES, ATT_STEP), 0)
    qs = q_ref[...].astype(f32) * (DA_HEAD_DIM ** -0.5 * LOG2E)
    q4 = jnp.concatenate(
        [jnp.where((lane // DA_HEAD_DIM) == i, qs, 0.0).astype(bf16) for i in range(4)], axis=0)

    m_sc[...] = jnp.full(m_sc.shape, -jnp.inf, f32)
    acc_sc[...] = jnp.zeros(acc_sc.shape, f32)

    n_steps = qb // per_step + 1
    last = n_steps - 1
    r = qb % per_step

    def scores(k):
        start = pl.multiple_of(jnp.minimum(k * ATT_STEP, seq - ATT_STEP), ATT_STEP)
        return jnp.dot(q4, kt_ref[:, pl.ds(start, ATT_STEP)], preferred_element_type=f32)

    def half(k, cur_ref, nxt_ref):
        nxt_ref[...] = scores(k + 1)
        tile = jnp.where(k == last, 1 + r, jnp.where((k == last - 1) & (r == 0), per_step + 1, 0))
        vv = vt_ref[:, pl.ds(pl.multiple_of(k * ATT_STEP, ATT_STEP), ATT_STEP)]
        for hh in range(2):
            bias = bias_ref[0, tile, hh * tq:(hh + 1) * tq, :]
            ps, alphas = [], []
            for c in range(2):
                rows = slice((2 * hh + c) * tq, (2 * hh + c + 1) * tq)
                s = cur_ref[rows, :] + bias
                m_old = m_sc[rows]
                m_new = jnp.maximum(m_old, jnp.max(s, axis=-1, keepdims=True))
                ps.append(jnp.exp2(s - m_new[:, 0:1]).astype(bf16))
                alphas.append(jnp.exp2(m_old - m_new))
                m_sc[rows] = m_new
            rows2 = slice(2 * hh * tq, (2 * hh + 2) * tq)
            vaug = jnp.where((feat_k // DA_V_DIM) == hh, vv, jnp.ones_like(vv))
            acc_sc[rows2] = (acc_sc[rows2] * jnp.concatenate(alphas, axis=0)
                             + lax.dot_general(jnp.concatenate(ps, axis=0), vaug, (((1,), (1,)), ((), ())),
                                               preferred_element_type=f32))

    sa_sc[...] = scores(0)

    def two_steps(kk, carry):
        half(2 * kk, sa_sc, sb_sc)

        @pl.when(2 * kk + 1 < n_steps)
        def _():
            half(2 * kk + 1, sb_sc, sa_sc)

        return carry

    lax.fori_loop(0, (n_steps + 1) // 2, two_steps, 0)

    lam = lam_ref[0]
    outs = []
    for hh in range(2):
        own = (lane // DA_V_DIM) == hh
        a1 = acc_sc[2 * hh * tq:(2 * hh + 1) * tq]
        a2 = acc_sc[(2 * hh + 1) * tq:(2 * hh + 2) * tq]
        l1 = jnp.max(jnp.where(own, 0.0, a1), axis=-1, keepdims=True)
        l2 = jnp.max(jnp.where(own, 0.0, a2), axis=-1, keepdims=True)
        o = a1 / l1 - lam * (a2 / l2)
        ssq = jnp.sum(jnp.where(own, o * o, 0.0), axis=-1, keepdims=True)
        outs.append(o * lax.rsqrt(ssq * (1.0 / DA_V_DIM) + EPS))
    o = jnp.where((lane // DA_V_DIM) == 0, outs[0], outs[1])
    o_ref[...] = (o * gain_ref[...]).astype(o_ref.dtype)


def _attn_prompt(q, kt, vt, bias_tiles, gain, lam, q_block0, n_q):
    seq, w = q.shape
    pairs = w // LANES
    assert seq % ATT_STEP == 0
    n_tiles = ATT_STEP // ATT_BLOCK + 2
    return pl.pallas_call(
        functools.partial(_attn_prompt_kernel, q_block0=q_block0),
        grid=(pairs, n_q),
        in_specs=[
            pl.BlockSpec(memory_space=pltpu.SMEM),
            pl.BlockSpec((ATT_BLOCK, LANES), lambda p, i: (q_block0 + i, p)),
            pl.BlockSpec((LANES, seq), lambda p, i: (p, 0)),
            pl.BlockSpec((LANES, seq), lambda p, i: (p, 0)),
            pl.BlockSpec((1, n_tiles, 2 * ATT_BLOCK, ATT_STEP), lambda p, i: (p, 0, 0, 0)),
            pl.BlockSpec((1, LANES), lambda p, i: (0, 0)),
        ],
        out_specs=pl.BlockSpec((ATT_BLOCK, LANES), lambda p, i: (i, p)),
        out_shape=jax.ShapeDtypeStruct((n_q * ATT_BLOCK, w), bf16),
        scratch_shapes=[
            pltpu.VMEM((4 * ATT_BLOCK, LANES), f32),
            pltpu.VMEM((4 * ATT_BLOCK, LANES), f32),
            pltpu.VMEM((4 * ATT_BLOCK, ATT_STEP), f32),
            pltpu.VMEM((4 * ATT_BLOCK, ATT_STEP), f32),
        ],
        compiler_params=_cparams("parallel", "arbitrary"),
        name="attn_prompt",
    )(lam, q, kt, vt, bias_tiles, gain)


def _prompt_bias_tiles(rel_bias):
    h = rel_bias.shape[1]
    per_step = ATT_STEP // ATT_BLOCK
    i = jnp.arange(ATT_BLOCK)
    rel_diag = i[None, :] - i[:, None]
    rel_prev = rel_diag - ATT_BLOCK
    far = rel_bias[REL_BUCKETS // 2 - 1].astype(f32)
    b_diag = (_bias_of(rel_diag, rel_bias) - far[:, None, None]) * LOG2E
    b_prev = (_bias_of(rel_prev, rel_bias) - far[:, None, None]) * LOG2E
    mask = (i[None, :] // CHUNK) <= (i[:, None] // CHUNK)
    b_diag = jnp.where(mask[None], b_diag, NEG)
    zero = jnp.zeros_like(b_diag)
    dead = jnp.full_like(b_diag, NEG)
    tiles = [jnp.concatenate([zero] * per_step, axis=-1)]
    for r in range(per_step):
        blocks = [zero if j < r - 1 else b_prev if j == r - 1 else b_diag if j == r else dead
                  for j in range(per_step)]
        tiles.append(jnp.concatenate(blocks, axis=-1))
    tiles.append(jnp.concatenate([zero] * (per_step - 1) + [b_prev], axis=-1))
    tiles = jnp.stack(tiles, axis=1)
    tiles = tiles.reshape(h // 2, 2, per_step + 2, ATT_BLOCK, ATT_STEP)
    return jnp.transpose(tiles, (0, 2, 1, 3, 4)).reshape(h // 2, per_step + 2, 2 * ATT_BLOCK, ATT_STEP)


def _attn_sample_kernel(lam_ref, q_ref, kp_ref, vp_ref, kn_ref, vn_ref, bp_ref, bn_ref, gain_ref, o_ref):
    nq = q_ref.shape[0]
    pairs = q_ref.shape[1] // LANES
    lane = lax.broadcasted_iota(jnp.int32, (nq, LANES), 1)
    lam = lam_ref[0]
    nt = (((1,), (1,)), ((), ()))
    for p in range(pairs):
        sl = slice(p * LANES, (p + 1) * LANES)
        qs = q_ref[:, sl].astype(f32) * (DA_HEAD_DIM ** -0.5 * LOG2E)
        kp = kp_ref[sl, :].astype(bf16)
        vp = vp_ref[sl, :].astype(bf16)
        kn = kn_ref[:, sl].astype(bf16)
        vn = vn_ref[:, sl].astype(bf16)
        outs = []
        for hh in range(2):
            h = 2 * p + hh
            own = (lane // DA_V_DIM) == hh
            res = []
            for c in range(2):
                qm = jnp.where((lane // DA_HEAD_DIM) == 2 * hh + c, qs, 0.0).astype(bf16)
                sp = jnp.dot(qm, kp, preferred_element_type=f32) + bp_ref[h]
                sn = lax.dot_general(qm, kn, nt, preferred_element_type=f32) + bn_ref[h]
                m = jnp.maximum(jnp.max(sp, axis=-1, keepdims=True), jnp.max(sn, axis=-1, keepdims=True))
                pp = jnp.exp2(sp - m)
                pn = jnp.exp2(sn - m)
                l = jnp.sum(pp, axis=-1, keepdims=True) + jnp.sum(pn, axis=-1, keepdims=True)
                pv = (lax.dot_general(pp.astype(bf16), vp, nt, preferred_element_type=f32)
                      + jnp.dot(pn.astype(bf16), vn, preferred_element_type=f32))
                res.append(pv / l)
            o = res[0] - lam * res[1]
            ssq = jnp.sum(jnp.where(own, o * o, 0.0), axis=-1, keepdims=True)
            outs.append(o * lax.rsqrt(ssq * (1.0 / DA_V_DIM) + EPS))
        o = jnp.where((lane // DA_V_DIM) == 0, outs[0], outs[1])
        o_ref[:, sl] = (o * gain_ref[...]).astype(o_ref.dtype)


def _attn_sample(q, k_new, v_new, k_past, v_past, bias_past, bias_new, gain, lam, batch, nq, past):
    w = q.shape[1]
    h = bias_past.shape[0]
    return pl.pallas_call(
        _attn_sample_kernel,
        grid=(batch,),
        in_specs=[
            pl.BlockSpec(memory_space=pltpu.SMEM),
            pl.BlockSpec((nq, w), lambda b: (b, 0)),
            pl.BlockSpec((w, past), lambda b: (b, 0)),
            pl.BlockSpec((w, past), lambda b: (b, 0)),
            pl.BlockSpec((nq, w), lambda b: (b, 0)),
            pl.BlockSpec((nq, w), lambda b: (b, 0)),
            pl.BlockSpec((h, nq, past), lambda b: (0, 0, 0)),
            pl.BlockSpec((h, nq, nq), lambda b: (0, 0, 0)),
            pl.BlockSpec((1, LANES), lambda b: (0, 0)),
        ],
        out_specs=pl.BlockSpec((nq, w), lambda b: (b, 0)),
        out_shape=jax.ShapeDtypeStruct((batch * nq, w), bf16),
        compiler_params=_cparams("parallel"),
        name="attn_sample",
    )(lam, q, k_past, v_past, k_new, v_new, bias_past, bias_new, gain)


def _hgrn_chunk(q, f_logit, vv, g_logit, lb, gain, s0, lc):
    n_sub = lc // HG_SUB
    f = lb + (1.0 - lb) * jax.nn.sigmoid(f_logit)
    logf = jnp.log(f)
    k = 1.0 - f
    row = lax.broadcasted_iota(jnp.int32, (lc, lc), 0)
    col = lax.broadcasted_iota(jnp.int32, (lc, lc), 1)
    tril = (col <= row).astype(f32)
    b = jnp.dot(tril, logf, precision=lax.Precision.HIGHEST, preferred_element_type=f32)
    b_last = b[lc - 1:lc, :]

    o = jnp.dot((q * jnp.exp(b)).astype(bf16), s0.astype(bf16), preferred_element_type=f32)
    k_hat = k * jnp.exp(b_last - b)
    bt = b.T
    decay_col = jnp.exp(bt[:, lc - 1:lc])
    s_new = decay_col * s0 + jnp.dot(k_hat.T.astype(bf16), vv.astype(bf16), preferred_element_type=f32)

    if n_sub > 1:
        rows = []
        nt = (((1,), (1,)), ((), ()))
        for i in range(n_sub):
            lo = i * HG_SUB
            if i == 0:
                rows.append(jnp.zeros((HG_SUB, lc), f32))
                continue
            beta = b[lo - 1:lo, :]
            q_t = q[lo:lo + HG_SUB, :] * jnp.exp(b[lo:lo + HG_SUB, :] - beta)
            k_t = k * jnp.exp(jnp.minimum(beta - b, 0.0))
            rows.append(lax.dot_general(q_t.astype(bf16), k_t.astype(bf16), nt, preferred_element_type=f32))
        a_off = jnp.concatenate(rows, axis=0)
        a_off = jnp.where(col < (row // HG_SUB) * HG_SUB, a_off, 0.0)
        o = o + jnp.dot(a_off.astype(bf16), vv.astype(bf16), preferred_element_type=f32)

    ones = jnp.ones((HG_DIM, LANES), bf16)
    srow = lax.broadcasted_iota(jnp.int32, (HG_SUB, HG_DIM), 0)
    orow = lax.broadcasted_iota(jnp.int32, (HG_SUB, HG_DIM), 0)
    blocks = []
    for i in range(n_sub):
        lo = i * HG_SUB
        b_i = b[lo:lo + HG_SUB, :]
        k_i = k[lo:lo + HG_SUB, :]
        v_i = vv[lo:lo + HG_SUB, :]
        q_i = q[lo:lo + HG_SUB, :]
        d_rows = []
        for t in range(HG_SUB):
            arg = jnp.where(srow <= t, b_i[t:t + 1, :] - b_i, -jnp.inf)
            d_rows.append(q_i[t:t + 1, :] * k_i * jnp.exp(arg))
        d3 = jnp.concatenate(d_rows, axis=0)
        a_rep = jnp.dot(d3.astype(bf16), ones, preferred_element_type=f32)
        o_i = jnp.zeros((HG_SUB, HG_DIM), f32)
        for t in range(HG_SUB):
            o_t = jnp.sum(a_rep[t * HG_SUB:(t + 1) * HG_SUB, :] * v_i, axis=0, keepdims=True)
            o_i = jnp.where(orow == t, o_t, o_i)
        blocks.append(o_i)
    o = o + (jnp.concatenate(blocks, axis=0) if n_sub > 1 else blocks[0])

    on = o * lax.rsqrt(jnp.mean(o * o, axis=-1, keepdims=True) + EPS) * gain
    gate = g_logit * jax.nn.sigmoid(g_logit)
    return on * gate, s_new


def _hgrn_kernel(hz_ref, lb_ref, gain_ref, s0_ref, o_ref, s_ref, st_sc, *, lc, heads):
    j = pl.program_id(1)
    w = heads * HG_DIM

    @pl.when(j == 0)
    def _():
        st_sc[...] = s0_ref[0]

    n_chunks = hz_ref.shape[0] // lc
    for c in range(n_chunks):
        rs = slice(c * lc, (c + 1) * lc)
        for h in range(heads):
            cs = slice(h * HG_DIM, (h + 1) * HG_DIM)
            o, s_new = _hgrn_chunk(
                hz_ref[rs, h * HG_DIM:(h + 1) * HG_DIM],
                hz_ref[rs, w + h * HG_DIM:w + (h + 1) * HG_DIM],
                hz_ref[rs, 2 * w + h * HG_DIM:2 * w + (h + 1) * HG_DIM],
                hz_ref[rs, 3 * w + h * HG_DIM:3 * w + (h + 1) * HG_DIM],
                lb_ref[:, cs], gain_ref[...], st_sc[h], lc)
            st_sc[h] = s_new
            o_ref[rs, cs] = o.astype(o_ref.dtype)

    @pl.when(j == pl.num_programs(1) - 1)
    def _():
        s_ref[0] = st_sc[...]


def _hgrn(hz, lb, gain, s0, batch, seq, lc, chunks_per_step):
    t = hz.shape[0]
    w = hz.shape[1] // 4
    heads = w // HG_DIM
    rows = lc * chunks_per_step
    steps = seq // rows
    return pl.pallas_call(
        functools.partial(_hgrn_kernel, lc=lc, heads=heads),
        grid=(batch, steps),
        in_specs=[
            pl.BlockSpec((rows, 4 * w), lambda b, j: (b * steps + j, 0)),
            pl.BlockSpec((1, w), lambda b, j: (0, 0)),
            pl.BlockSpec((1, HG_DIM), lambda b, j: (0, 0)),
            pl.BlockSpec((1, heads, HG_DIM, HG_DIM), lambda b, j: (b, 0, 0, 0)),
        ],
        out_specs=[
            pl.BlockSpec((rows, w), lambda b, j: (b * steps + j, 0)),
            pl.BlockSpec((1, heads, HG_DIM, HG_DIM), lambda b, j: (b, 0, 0, 0)),
        ],
        out_shape=[
            jax.ShapeDtypeStruct((t, w), bf16),
            jax.ShapeDtypeStruct((batch, heads, HG_DIM, HG_DIM), f32),
        ],
        scratch_shapes=[pltpu.VMEM((heads, HG_DIM, HG_DIM), f32)],
        compiler_params=_cparams("parallel", "arbitrary"),
        name="hgrn2",
    )(hz, lb, gain, s0)


def _top_rows(vals, k, payload=None):
    n_rows = vals.shape[0]
    row = lax.broadcasted_iota(jnp.int32, vals.shape, 0)
    out_v, out_i = [], []
    for _ in range(k):
        m = jnp.max(vals, axis=0, keepdims=True)
        first = jnp.min(jnp.where(vals == m, row, n_rows), axis=0, keepdims=True)
        hit = row == first
        out_v.append(m)
        if payload is None:
            out_i.append(first)
        else:
            out_i.append(jnp.max(jnp.where(hit, payload, -1), axis=0, keepdims=True))
        vals = jnp.where(hit, -jnp.inf, vals)
    return jnp.concatenate(out_v, axis=0), jnp.concatenate(out_i, axis=0)


def _retrieve_kernel(oa_ref, ob_ref, x_ref, wo_ref, g_ref, wqt_ref, sk_ref,
                     hp_ref, xn_ref, idx_ref, gate_ref, qt_sc, s1v_sc, s1i_sc, idx_sc, gate_sc):
    half = oa_ref.shape[1]
    y = (jnp.dot(oa_ref[...], wo_ref[0:half, :], preferred_element_type=f32)
         + jnp.dot(ob_ref[...], wo_ref[half:, :], preferred_element_type=f32))
    hp = x_ref[...] + y
    hp_ref[...] = hp
    xn = hp * lax.rsqrt(jnp.mean(hp * hp, axis=-1, keepdims=True) + EPS) * g_ref[...]
    xn_ref[...] = xn
    nt = (((1,), (1,)), ((), ()))
    qt_sc[...] = lax.dot_general(wqt_ref[...], xn.astype(bf16), nt, preferred_element_type=f32).astype(bf16)

    n_half = sk_ref.shape[0]

    def stage1(hc, carry):
        start = pl.multiple_of(hc * PEER_SUB_DIM, PEER_SUB_DIM)
        s = jnp.dot(sk_ref[hc], qt_sc[pl.ds(start, PEER_SUB_DIM), :], preferred_element_type=f32)
        v, i = _top_rows(s, PEER_TOPK)
        s1v_sc[hc] = v
        s1i_sc[hc] = i
        return carry

    lax.fori_loop(0, n_half, stage1, 0)

    tm = x_ref.shape[0]
    sub = lax.broadcasted_iota(jnp.int32, (8, tm), 0)

    def stage2(h, carry):
        v1, i1 = s1v_sc[2 * h], s1i_sc[2 * h]
        v2, i2 = s1v_sc[2 * h + 1], s1i_sc[2 * h + 1]
        cv, ci = [], []
        for i in range(PEER_TOPK):
            nj = PEER_TOPK // (i + 1)
            rows = PEER_TOPK if nj > 8 else 8
            a = v1[i:i + 1, :] + v2[0:rows, :]
            e = i1[i:i + 1, :] * PEER_KEYS + i2[0:rows, :]
            if nj < rows:
                a = jnp.where(sub < nj, a, -jnp.inf)
            cv.append(a)
            ci.append(e)
        top_s, eidx = _top_rows(jnp.concatenate(cv, axis=0), PEER_TOPK, jnp.concatenate(ci, axis=0))
        p = jnp.exp(top_s - top_s[0:1, :])
        g = p / jnp.sum(p, axis=0, keepdims=True)
        r0 = pl.multiple_of(h * PEER_TOPK, PEER_TOPK)
        gate_sc[pl.ds(r0, PEER_TOPK), :] = g
        idx_sc[pl.ds(r0, PEER_TOPK), :] = eidx
        return carry

    lax.fori_loop(0, n_half // 2, stage2, 0)
    idx_ref[...] = pltpu.bitcast(pltpu.bitcast(idx_sc[...], f32).T, jnp.int32)
    gate_ref[...] = gate_sc[...].T


def _retrieve(oa, ob, x, wo_bf, g2, wqt_bf, sk_bf, tm, x_block0=0, ob_block0=0):
    t, half = oa.shape
    d = x.shape[1]
    e = wqt_bf.shape[0]
    n_half = sk_bf.shape[0]
    n_sel = (n_half // 2) * PEER_TOPK
    return pl.pallas_call(
        _retrieve_kernel,
        grid=(t // tm,),
        in_specs=[
            pl.BlockSpec((tm, half), lambda i: (i, 0)),
            pl.BlockSpec((tm, half), lambda i: (ob_block0 + i, 0)),
            pl.BlockSpec((tm, d), lambda i: (x_block0 + i, 0)),
            pl.BlockSpec((2 * half, d), lambda i: (0, 0)),
            pl.BlockSpec((1, d), lambda i: (0, 0)),
            pl.BlockSpec((e, d), lambda i: (0, 0)),
            pl.BlockSpec((n_half, PEER_KEYS, PEER_SUB_DIM), lambda i: (0, 0, 0)),
        ],
        out_specs=[
            pl.BlockSpec((tm, d), lambda i: (i, 0)),
            pl.BlockSpec((tm, d), lambda i: (i, 0)),
            pl.BlockSpec((tm, n_sel), lambda i: (i, 0)),
            pl.BlockSpec((tm, n_sel), lambda i: (i, 0)),
        ],
        out_shape=[
            jax.ShapeDtypeStruct((t, d), f32),
            jax.ShapeDtypeStruct((t, d), f32),
            jax.ShapeDtypeStruct((t, n_sel), jnp.int32),
            jax.ShapeDtypeStruct((t, n_sel), f32),
        ],
        scratch_shapes=[
            pltpu.VMEM((e, tm), bf16),
            pltpu.VMEM((n_half, PEER_TOPK, tm), f32),
            pltpu.VMEM((n_half, PEER_TOPK, tm), jnp.int32),
            pltpu.VMEM((n_sel, tm), jnp.int32),
            pltpu.VMEM((n_sel, tm), f32),
        ],
        compiler_params=_cparams("parallel"),
        name="retrieve",
    )(oa, ob, x, wo_bf, g2, wqt_bf, sk_bf)


def _pack_table(tab):
    half = tab.shape[1] // 2
    lo = lax.bitcast_convert_type(tab[:, :half].astype(bf16), jnp.uint16).astype(jnp.uint32)
    hi = lax.bitcast_convert_type(tab[:, half:].astype(bf16), jnp.uint16).astype(jnp.uint32)
    return lax.bitcast_convert_type(lo | (hi << 16), jnp.int32)


def _sc_layout(n_tok, n_sel, words):
    info = plsc.get_sparse_core_info()
    n_workers = info.num_cores * info.num_subcores
    lanes = info.num_lanes
    block = 8 * lanes
    assert n_tok % (2 * n_workers) == 0 and words % block == 0 and (n_sel // 2) % lanes == 0
    return info, lanes, n_tok // n_workers, n_sel // 2, block


def _selected_dots(table, idx, x):
    n_tok, n_sel = idx.shape
    words = table.shape[1]
    info, lanes, tpw, half_rows, block = _sc_layout(n_tok, n_sel, words)
    idx2 = idx.reshape(2 * n_tok, half_rows)
    mesh = plsc.VectorSubcoreMesh(core_axis_name="c", subcore_axis_name="s")

    @functools.partial(
        pl.kernel, out_type=jax.ShapeDtypeStruct((n_tok, n_sel), f32), mesh=mesh,
        compiler_params=pltpu.CompilerParams(needs_layout_passes=False),
        scratch_types=[
            pltpu.VMEM((2 * tpw, half_rows), jnp.int32),
            pltpu.VMEM((half_rows, words), jnp.int32),
            pltpu.VMEM((half_rows, words), jnp.int32),
            pltpu.VMEM((2 * words,), f32),
            pltpu.VMEM((2 * words,), f32),
            pltpu.VMEM((n_sel // 8, 8 * lanes), f32),
            pltpu.VMEM((tpw, n_sel), f32),
            pltpu.SemaphoreType.DMA, pltpu.SemaphoreType.DMA, pltpu.SemaphoreType.DMA, pltpu.SemaphoreType.DMA,
        ])
    def dots(t_hbm, i_hbm, x_hbm, o_hbm, idx_v, rows_a, rows_b, x_a, x_b, part_v, act_v, sem_a, sem_b, sem_xa, sem_xb):
        wid = lax.axis_index("s") * info.num_cores + lax.axis_index("c")
        base = wid * tpw
        pltpu.sync_copy(i_hbm.at[pl.ds(base * 2, 2 * tpw)], idx_v)

        def accumulate(rows, x_v, half):
            @pl.loop(0, words // block)
            def _(cb):
                xs = [x_v[pl.ds(p * words + cb * block + c * lanes, lanes)] for c in range(8) for p in range(2)]

                @plsc.parallel_loop(0, half_rows)
                def _(row):
                    packed = [rows[row, pl.ds(cb * block + c * lanes, lanes)] for c in range(8)]
                    unp = [plsc.unpack(plsc.bitcast(p, bf16), format=plsc.PackFormat.INTERLEAVED) for p in packed]
                    prods = [unp[c][p] * xs[2 * c + p] for c in range(8) for p in range(2)]
                    while len(prods) > 1:
                        prods = [prods[k] + prods[k + 1] for k in range(0, len(prods), 2)]
                    prow = half * half_rows + row
                    plsc.addupdate(part_v.at[prow // 8, pl.ds((prow % 8) * lanes, lanes)], prods[0])

        def gather(half_index, rows, sem):
            return pltpu.make_async_copy(t_hbm.at[idx_v.at[half_index]], rows, sem)

        def xcopy(t, x_v, sem):
            return pltpu.make_async_copy(x_hbm.at[base + t], x_v, sem)

        def token(t, x_v, sem_x, x_next, sem_xn):
            gather(2 * t + 1, rows_b, sem_b).start()

            @pl.when(t + 1 < tpw)
            def _():
                xcopy(t + 1, x_next, sem_xn).start()

            zero = jnp.zeros((lanes,), f32)
            for r in range(n_sel):
                part_v[r // 8, pl.ds((r % 8) * lanes, lanes)] = zero
            xcopy(t, x_v, sem_x).wait()
            gather(2 * t, rows_a, sem_a).wait()
            accumulate(rows_a, x_v, 0)

            @pl.when(t + 1 < tpw)
            def _():
                gather(2 * t + 2, rows_a, sem_a).start()

            gather(2 * t + 1, rows_b, sem_b).wait()
            accumulate(rows_b, x_v, 1)
            it = lax.iota(jnp.int32, lanes)
            for g in range(n_sel // lanes):
                prow = g * (lanes // 8) + it // 8
                pcol = (it % 8) * lanes
                tot = plsc.load_gather(part_v, [prow, pcol])
                for l in range(1, lanes):
                    tot = tot + plsc.load_gather(part_v, [prow, pcol + l])
                act_v[t, pl.ds(g * lanes, lanes)] = tot

        gather(0, rows_a, sem_a).start()
        xcopy(0, x_a, sem_xa).start()

        @pl.loop(0, tpw // 2)
        def _(i):
            token(2 * i, x_a, sem_xa, x_b, sem_xb)
            token(2 * i + 1, x_b, sem_xb, x_a, sem_xa)

        pltpu.sync_copy(act_v, o_hbm.at[pl.ds(base, tpw)])

    return dots(table, idx2, x)


def _gate_weights_kernel(act_ref, gate_ref, after_ref, w_ref):
    del after_ref
    a = act_ref[...]
    w_ref[...] = gate_ref[...] * (0.5 * a * (1.0 + lax.erf(a * (2.0 ** -0.5))))


def _gate_weights(act, gate, after, tm):
    t, n_sel = act.shape
    spec = pl.BlockSpec((tm, n_sel), lambda i: (i, 0))
    return pl.pallas_call(
        _gate_weights_kernel,
        grid=(t // tm,),
        in_specs=[spec, spec, pl.BlockSpec(memory_space=pl.ANY)],
        out_specs=spec,
        out_shape=jax.ShapeDtypeStruct((t, n_sel), f32),
        compiler_params=_cparams("parallel"),
        name="peer_weights",
    )(act, gate, after)


def _weighted_rows(table, idx, w):
    n_tok, n_sel = w.shape
    words = table.shape[1]
    info, lanes, tpw, half_rows, _ = _sc_layout(n_tok, n_sel, words)
    idx2 = idx.reshape(2 * n_tok, half_rows)
    mesh = plsc.VectorSubcoreMesh(core_axis_name="c", subcore_axis_name="s")

    @functools.partial(
        pl.kernel, out_type=jax.ShapeDtypeStruct((n_tok, 2 * words), f32), mesh=mesh,
        compiler_params=pltpu.CompilerParams(needs_layout_passes=False),
        scratch_types=[
            pltpu.VMEM((2 * tpw, half_rows), jnp.int32),
            pltpu.VMEM((tpw, n_sel), f32),
            pltpu.VMEM((half_rows, words), jnp.int32),
            pltpu.VMEM((half_rows, words), jnp.int32),
            pltpu.VMEM((2 * words,), f32),
            pltpu.VMEM((2 * words,), f32),
            pltpu.SemaphoreType.DMA, pltpu.SemaphoreType.DMA, pltpu.SemaphoreType.DMA, pltpu.SemaphoreType.DMA,
        ])
    def mix(t_hbm, i_hbm, w_hbm, o_hbm, idx_v, w_v, rows_a, rows_b, out_a, out_b, sem_a, sem_b, sem_oa, sem_ob):
        wid = lax.axis_index("s") * info.num_cores + lax.axis_index("c")
        base = wid * tpw
        pltpu.sync_copy(i_hbm.at[pl.ds(base * 2, 2 * tpw)], idx_v)
        pltpu.sync_copy(w_hbm.at[pl.ds(base, tpw)], w_v)

        def accumulate(rows, out_v, t, half, first):
            if first:
                zero = jnp.zeros((lanes,), f32)
                for c in range(2 * words // lanes):
                    out_v[pl.ds(c * lanes, lanes)] = zero

            @pl.loop(0, half_rows // lanes)
            def _(g):
                wv = w_v[t, pl.ds(half * half_rows + g * lanes, lanes)]
                ws = [wv.at[jnp.full((lanes,), r, jnp.int32)].get(mode="promise_in_bounds") for r in range(lanes)]

                @plsc.parallel_loop(0, words // lanes)
                def _(c):
                    packed = [rows[g * lanes + r, pl.ds(c * lanes, lanes)] for r in range(lanes)]
                    unp = [plsc.unpack(plsc.bitcast(p, bf16), format=plsc.PackFormat.INTERLEAVED) for p in packed]
                    lo = [unp[r][0] * ws[r] for r in range(lanes)]
                    hi = [unp[r][1] * ws[r] for r in range(lanes)]
                    while len(lo) > 1:
                        lo = [lo[k] + lo[k + 1] for k in range(0, len(lo), 2)]
                        hi = [hi[k] + hi[k + 1] for k in range(0, len(hi), 2)]
                    plsc.addupdate(out_v.at[pl.ds(c * lanes, lanes)], lo[0])
                    plsc.addupdate(out_v.at[pl.ds(words + c * lanes, lanes)], hi[0])

        def gather(half_index, rows, sem):
            return pltpu.make_async_copy(t_hbm.at[idx_v.at[half_index]], rows, sem)

        def token(t, out_v, sem_o, out_in_flight):
            gather(2 * t + 1, rows_b, sem_b).start()
            gather(2 * t, rows_a, sem_a).wait()

            @pl.when(out_in_flight)
            def _():
                pltpu.make_async_copy(out_v, o_hbm.at[base + t], sem_o).wait()

            accumulate(rows_a, out_v, t, 0, True)

            @pl.when(t + 1 < tpw)
            def _():
                gather(2 * t + 2, rows_a, sem_a).start()

            gather(2 * t + 1, rows_b, sem_b).wait()
            accumulate(rows_b, out_v, t, 1, False)
            pltpu.make_async_copy(out_v, o_hbm.at[base + t], sem_o).start()

        gather(0, rows_a, sem_a).start()

        @pl.loop(0, tpw // 2)
        def _(i):
            token(2 * i, out_a, sem_oa, i > 0)
            token(2 * i + 1, out_b, sem_ob, i > 0)

        pltpu.make_async_copy(out_a, o_hbm.at[base], sem_oa).wait()
        pltpu.make_async_copy(out_b, o_hbm.at[base], sem_ob).wait()

    return mix(table, idx2, w)


def _final_kernel(hp_ref, peer_ref, gf_ref, yin_ref, y_ref, done_ref):
    del yin_ref
    done_ref[...] = jnp.zeros(done_ref.shape, f32)
    h = hp_ref[...] + peer_ref[...]
    y_ref[...] = h * lax.rsqrt(jnp.mean(h * h, axis=-1, keepdims=True) + EPS) * gf_ref[...]


def _final(hp, peer, gf, y_all, y_block0, tm):
    t, d = hp.shape
    return pl.pallas_call(
        _final_kernel,
        grid=(t // tm,),
        in_specs=[
            pl.BlockSpec((tm, d), lambda i: (i, 0)),
            pl.BlockSpec((tm, d), lambda i: (i, 0)),
            pl.BlockSpec((1, d), lambda i: (0, 0)),
            pl.BlockSpec(memory_space=pl.ANY),
        ],
        out_specs=[pl.BlockSpec((tm, d), lambda i: (y_block0 + i, 0)),
                   pl.BlockSpec((8, LANES), lambda i: (0, 0))],
        out_shape=[jax.ShapeDtypeStruct(y_all.shape, f32), jax.ShapeDtypeStruct((8, LANES), f32)],
        input_output_aliases={3: 0},
        compiler_params=_cparams("arbitrary"),
        name="final_norm",
    )(hp, peer, gf, y_all)


def _channel_mixer_front(oa, ob, x, x_block0, ob_block0, wo_bf, g2, wqt_bf, sk_bf, u_pk, tm):
    hp, xn, eidx, gate = _retrieve(oa, ob, x, wo_bf, g2, wqt_bf, sk_bf, tm, x_block0, ob_block0)
    return hp, eidx, gate, _selected_dots(u_pk, eidx, xn)


def _channel_mixer_mix(front, after, v_pk, tm):
    _, eidx, gate, act = front
    return _weighted_rows(v_pk, eidx, _gate_weights(act, gate, after, tm))


def kernel(x_prompt, x_sample, cache_k, cache_v, state_hgrn, norm1, w_in, da_lq1, da_lk1, da_lq2, da_lk2,
           da_out_norm, hg_lb_logits, hg_out_norm, w_out, rel_bias, norm2, peer_w_q, peer_sub_keys,
           peer_u, peer_v, final_norm):
    batch, seq, d = x_prompt.shape
    dbatch, dseq, _ = x_sample.shape
    past = cache_k.shape[2]
    da_heads = cache_k.shape[3]
    hg_heads = state_hgrn.shape[2]
    depth = w_in.shape[0]
    assert depth == 1 and seq % ATT_BLOCK == 0 and seq % CHUNK == 0 and dseq % HG_SUB == 0 and dseq <= CHUNK
    assert past % CHUNK == 0 and (past + dseq - 1) // CHUNK == past // CHUNK

    l = 0
    lam_init = 0.8 - 0.6 * math.exp(-0.3 * l)
    lam = (jnp.exp(jnp.sum(da_lq1[l].astype(f32) * da_lk1[l].astype(f32)))
           - jnp.exp(jnp.sum(da_lq2[l].astype(f32) * da_lk2[l].astype(f32))) + lam_init).reshape(1)
    lb = jnp.cumsum(jax.nn.softmax(hg_lb_logits.astype(f32), axis=0), axis=0)[l].reshape(1, -1)
    da_gain = (jnp.tile(da_out_norm[l].astype(f32), 2) * (1.0 - lam_init)).reshape(1, LANES)
    hg_gain = hg_out_norm[l].astype(f32).reshape(1, HG_DIM)
    g1 = norm1[l].astype(f32).reshape(1, d)
    g2 = norm2[l].astype(f32).reshape(1, d)
    gf = final_norm.astype(f32).reshape(1, d)
    w_in_bf = w_in[l].astype(bf16)
    wkv_t_bf = w_in[l][:, cache_k.shape[3] * cache_k.shape[4]:3 * cache_k.shape[3] * cache_k.shape[4]].T.astype(bf16)
    wo_bf = w_out[l].astype(bf16)
    wqt_bf = peer_w_q[l].T.astype(bf16)
    sk_bf = peer_sub_keys[l].reshape(-1, PEER_KEYS, PEER_SUB_DIM).astype(bf16)
    u_pk = _pack_table(peer_u[l])
    v_pk = _pack_table(peer_v[l])

    bias_tiles = _prompt_bias_tiles(rel_bias)
    q_pos = past + jnp.arange(dseq)
    bias_s = _bias_of(jnp.arange(past + dseq)[None, :] - q_pos[:, None], rel_bias) * LOG2E
    bias_past, bias_new = bias_s[:, :, :past], bias_s[:, :, past:]
    zero_state = jnp.zeros((1, hg_heads, HG_DIM, HG_DIM), f32)

    dh2 = 2 * DA_HEAD_DIM
    wq = da_heads * dh2
    tm = 512
    x_all = x_prompt.reshape(batch * seq, d)
    kt_all = jnp.zeros((batch * wq, seq), f32)
    vt_all = jnp.zeros((batch * wq, seq), f32)
    y_all = jnp.zeros((batch * seq, d), f32)
    nq = seq // ATT_BLOCK
    blocks = seq // tm
    cuts = {0: 4, 1: 2, 2: 2}
    segments = [(b, c * (nq // cuts.get(b, 1)), nq // cuts.get(b, 1))
                for b in range(batch) for c in range(cuts.get(b, 1))]
    assert all(nq % n == 0 and (nq // n * ATT_BLOCK) % (2 * tm) == 0 for n in cuts.values())
    s_p = []
    done = [jnp.zeros((8, LANES), f32)] * 3
    fronts, peers = [], []

    def finish(k, y_all):
        return _final(fronts[k][0][0], peers[k], gf, y_all, fronts[k][1], tm)

    for s, (b, q0, n_q) in enumerate(segments):
        if q0 == 0:
            q, kt, vt, hz, kt_all, vt_all = _inproj_prompt(x_all, g1, w_in_bf, wkv_t_bf, kt_all, vt_all, done[s],
                                                           b, seq, tm)
            ob, s_new = _hgrn(hz, lb, hg_gain, zero_state, 1, seq, CHUNK, 4)
            s_p.append(s_new)
        oa = _attn_prompt(q, kt, vt, bias_tiles, da_gain, lam, q0, n_q)
        if s >= 1:
            peers.append(_channel_mixer_mix(fronts[s - 1][0], oa, v_pk, tm))
        row0 = q0 * ATT_BLOCK // tm
        fronts.append((_channel_mixer_front(oa, ob, x_all, b * blocks + row0, row0, wo_bf, g2, wqt_bf, sk_bf, u_pk,
                                            tm), b * blocks + row0))
        if s >= 2:
            y_all, token = finish(s - 2, y_all)
            done.append(token)
    peers.append(_channel_mixer_mix(fronts[-1][0], fronts[-1][0][0], v_pk, tm))
    for k in (len(segments) - 2, len(segments) - 1):
        y_all, _ = finish(k, y_all)

    xs = x_sample.reshape(dbatch * dseq, d)
    q, k_s, v_s, hz = _inproj_sample(xs, g1, w_in_bf, wq)
    kp_t = jnp.transpose(cache_k[l], (0, 2, 3, 1)).reshape(dbatch * wq, past)
    vp_t = jnp.transpose(cache_v[l], (0, 2, 3, 1)).reshape(dbatch * wq, past)
    oa = _attn_sample(q, k_s, v_s, kp_t, vp_t, bias_past, bias_new, da_gain, lam, dbatch, dseq, past)
    ob, s_s = _hgrn(hz, lb, hg_gain, state_hgrn[l].astype(f32), dbatch, dseq, dseq, 1)
    front = _channel_mixer_front(oa, ob, xs, 0, 0, wo_bf, g2, wqt_bf, sk_bf, u_pk, dbatch * dseq)
    peer = _channel_mixer_mix(front, front[0], v_pk, dbatch * dseq)
    y_s, _ = _final(front[0], peer, gf, jnp.zeros((dbatch * dseq, d), f32), 0, dbatch * dseq)

    y_prompt = y_all.reshape(batch, seq, d)
    y_sample = y_s.reshape(dbatch, dseq, d)
    k_prompt = jnp.transpose(kt_all.reshape(batch, da_heads, dh2, seq), (0, 3, 1, 2))[None]
    v_prompt = jnp.transpose(vt_all.reshape(batch, da_heads, DA_V_DIM, seq), (0, 3, 1, 2))[None]
    state_prompt = jnp.concatenate(s_p, axis=0)[None].astype(state_hgrn.dtype)
    k_sample = k_s.reshape(1, dbatch, dseq, da_heads, dh2)
    v_sample = v_s.reshape(1, dbatch, dseq, da_heads, DA_V_DIM)
    state_sample = s_s[None].astype(state_hgrn.dtype)
    return (y_prompt, y_sample, k_prompt, v_prompt, state_prompt, k_sample, v_sample, state_sample)
```

```python
import functools
import math

import jax
import jax.numpy as jnp
from jax import lax
from jax.experimental import pallas as pl
from jax.experimental.pallas import tpu as pltpu
from jax.experimental.pallas import tpu_sc as plsc

CHUNK = 64
DA_HEAD_DIM = 32
DA_V_DIM = 2 * DA_HEAD_DIM
HG_DIM = 128
REL_BUCKETS = 32
REL_MAX_DIST = 128
PEER_KEYS = 128
PEER_TOPK = 16
PEER_SUB_DIM = 64
EPS = 1e-6
NEG = -1e30
LOG2E = 1.4426950408889634

LANES = 128
ATT_BLOCK = 128
ATT_STEP = 512
HG_SUB = 16
V7X_VMEM_BYTES = 64 * 1024 * 1024
VMEM_LIMIT = V7X_VMEM_BYTES * 7 // 8

f32 = jnp.float32
bf16 = jnp.bfloat16


def _cparams(*sem):
    return pltpu.CompilerParams(dimension_semantics=sem, vmem_limit_bytes=VMEM_LIMIT)


def _inproj_prompt_kernel(x_ref, g_ref, w_ref, wkv_t_ref, kin_ref, vin_ref, after_ref,
                          q_ref, kt_ref, vt_ref, hz_ref, ktf_ref, vtf_ref):
    del kin_ref, vin_ref
    del after_ref
    x = x_ref[...]
    xn = x * lax.rsqrt(jnp.mean(x * x, axis=-1, keepdims=True) + EPS) * g_ref[...]
    xb = xn.astype(bf16)
    wq = q_ref.shape[1]
    nt = (((1,), (1,)), ((), ()))
    q_ref[...] = jnp.dot(xb, w_ref[:, 0:wq], preferred_element_type=f32).astype(bf16)
    kt = lax.dot_general(wkv_t_ref[0:wq, :], xb, nt, preferred_element_type=f32)
    ktf_ref[...] = kt
    kt_ref[...] = kt.astype(bf16)
    vt = lax.dot_general(wkv_t_ref[wq:2 * wq, :], xb, nt, preferred_element_type=f32)
    vtf_ref[...] = vt
    vt_ref[...] = vt.astype(bf16)
    hz_ref[...] = jnp.dot(xb, w_ref[:, 3 * wq:], preferred_element_type=f32)


def _inproj_prompt(x, g, w_bf, wkv_t_bf, kt_all, vt_all, after, b, seq, tm):
    d = x.shape[1]
    e = w_bf.shape[1]
    wq = wkv_t_bf.shape[0] // 2
    steps = seq // tm
    return pl.pallas_call(
        _inproj_prompt_kernel,
        grid=(steps,),
        in_specs=[
            pl.BlockSpec((tm, d), lambda i: (b * steps + i, 0)),
            pl.BlockSpec((1, d), lambda i: (0, 0)),
            pl.BlockSpec((d, e), lambda i: (0, 0)),
            pl.BlockSpec((2 * wq, d), lambda i: (0, 0)),
            pl.BlockSpec(memory_space=pl.ANY),
            pl.BlockSpec(memory_space=pl.ANY),
            pl.BlockSpec(memory_space=pl.ANY),
        ],
        out_specs=[
            pl.BlockSpec((tm, wq), lambda i: (i, 0)),
            pl.BlockSpec((wq, tm), lambda i: (0, i)),
            pl.BlockSpec((wq, tm), lambda i: (0, i)),
            pl.BlockSpec((tm, e - 3 * wq), lambda i: (i, 0)),
            pl.BlockSpec((wq, tm), lambda i: (b, i)),
            pl.BlockSpec((wq, tm), lambda i: (b, i)),
        ],
        out_shape=[
            jax.ShapeDtypeStruct((seq, wq), bf16),
            jax.ShapeDtypeStruct((wq, seq), bf16),
            jax.ShapeDtypeStruct((wq, seq), bf16),
            jax.ShapeDtypeStruct((seq, e - 3 * wq), f32),
            jax.ShapeDtypeStruct(kt_all.shape, f32),
            jax.ShapeDtypeStruct(vt_all.shape, f32),
        ],
        input_output_aliases={4: 4, 5: 5},
        compiler_params=_cparams("parallel"),
        name="inproj",
    )(x, g, w_bf, wkv_t_bf, kt_all, vt_all, after)


def _inproj_sample_kernel(x_ref, g_ref, w_ref, q_ref, k_ref, v_ref, hz_ref):
    x = x_ref[...]
    xn = x * lax.rsqrt(jnp.mean(x * x, axis=-1, keepdims=True) + EPS) * g_ref[...]
    xb = xn.astype(bf16)
    wq = q_ref.shape[1]
    q_ref[...] = jnp.dot(xb, w_ref[:, 0:wq], preferred_element_type=f32).astype(bf16)
    k_ref[...] = jnp.dot(xb, w_ref[:, wq:2 * wq], preferred_element_type=f32)
    v_ref[...] = jnp.dot(xb, w_ref[:, 2 * wq:3 * wq], preferred_element_type=f32)
    hz_ref[...] = jnp.dot(xb, w_ref[:, 3 * wq:], preferred_element_type=f32)


def _inproj_sample(x, g, w_bf, wq):
    t, d = x.shape
    e = w_bf.shape[1]
    return pl.pallas_call(
        _inproj_sample_kernel,
        grid=(1,),
        in_specs=[
            pl.BlockSpec((t, d), lambda i: (0, 0)),
            pl.BlockSpec((1, d), lambda i: (0, 0)),
            pl.BlockSpec((d, e), lambda i: (0, 0)),
        ],
        out_specs=[
            pl.BlockSpec((t, wq), lambda i: (0, 0)),
            pl.BlockSpec((t, wq), lambda i: (0, 0)),
            pl.BlockSpec((t, wq), lambda i: (0, 0)),
            pl.BlockSpec((t, e - 3 * wq), lambda i: (0, 0)),
        ],
        out_shape=[
            jax.ShapeDtypeStruct((t, wq), bf16),
            jax.ShapeDtypeStruct((t, wq), f32),
            jax.ShapeDtypeStruct((t, wq), f32),
            jax.ShapeDtypeStruct((t, e - 3 * wq), f32),
        ],
        compiler_params=_cparams("arbitrary"),
        name="inproj_sample",
    )(x, g, w_bf)


def _rel_bucket(rel):
    nb = REL_BUCKETS // 2
    max_exact = nb // 2
    ret = jnp.where(rel > 0, nb, 0)
    n = jnp.abs(rel)
    nf = jnp.maximum(n, 1).astype(f32)
    large = max_exact + (jnp.log(nf / max_exact) / math.log(REL_MAX_DIST / max_exact)
                         * (nb - max_exact)).astype(jnp.int32)
    large = jnp.minimum(large, nb - 1)
    return ret + jnp.where(n < max_exact, n, large)


def _bias_of(rel, rel_bias):
    bucket = _rel_bucket(rel)
    out = jnp.zeros((rel_bias.shape[1],) + rel.shape, f32)
    for b in range(REL_BUCKETS):
        out = out + jnp.where(bucket[None] == b, rel_bias[b].astype(f32)[:, None, None], 0.0)
    return out


def _attn_prompt_kernel(lam_ref, q_ref, kt_ref, vt_ref, bias_ref, gain_ref, o_ref, m_sc, acc_sc, sa_sc, sb_sc,
                        *, q_block0):
    qb = pl.program_id(1) + q_block0
    tq = q_ref.shape[0]
    seq = kt_ref.shape[1]
    per_step = ATT_STEP // ATT_BLOCK
    lane = lax.broadcasted_iota(jnp.int32, (tq, LANES), 1)
    feat_k = lax.broadcasted_iota(jnp.int32, (LANES, ATT_STEP), 0)
    qs = q_ref[...].astype(f32) * (DA_HEAD_DIM ** -0.5 * LOG2E)
    q4 = jnp.concatenate(
        [jnp.where((lane // DA_HEAD_DIM) == i, qs, 0.0).astype(bf16) for i in range(4)], axis=0)

    m_sc[...] = jnp.full(m_sc.shape, -jnp.inf, f32)
    acc_sc[...] = jnp.zeros(acc_sc.shape, f32)

    n_steps = qb // per_step + 1
    last = n_steps - 1
    r = qb % per_step

    def scores(k):
        start = pl.multiple_of(jnp.minimum(k * ATT_STEP, seq - ATT_STEP), ATT_STEP)
        return jnp.dot(q4, kt_ref[:, pl.ds(start, ATT_STEP)], preferred_element_type=f32)

    def half(k, cur_ref, nxt_ref):
        nxt_ref[...] = scores(k + 1)
        tile = jnp.where(k == last, 1 + r, jnp.where((k == last - 1) & (r == 0), per_step + 1, 0))
        vv = vt_ref[:, pl.ds(pl.multiple_of(k * ATT_STEP, ATT_STEP), ATT_STEP)]
        for hh in range(2):
            bias = bias_ref[0, tile, hh * tq:(hh + 1) * tq, :]
            ps, alphas = [], []
            for c in range(2):
                rows = slice((2 * hh + c) * tq, (2 * hh + c + 1) * tq)
                s = cur_ref[rows, :] + bias
                m_old = m_sc[rows]
                m_new = jnp.maximum(m_old, jnp.max(s, axis=-1, keepdims=True))
                ps.append(jnp.exp2(s - m_new[:, 0:1]).astype(bf16))
                alphas.append(jnp.exp2(m_old - m_new))
                m_sc[rows] = m_new
            rows2 = slice(2 * hh * tq, (2 * hh + 2) * tq)
            vaug = jnp.where((feat_k // DA_V_DIM) == hh, vv, jnp.ones_like(vv))
            acc_sc[rows2] = (acc_sc[rows2] * jnp.concatenate(alphas, axis=0)
                             + lax.dot_general(jnp.concatenate(ps, axis=0), vaug, (((1,), (1,)), ((), ())),
                                               preferred_element_type=f32))

    sa_sc[...] = scores(0)

    def two_steps(kk, carry):
        half(2 * kk, sa_sc, sb_sc)

        @pl.when(2 * kk + 1 < n_steps)
        def _():
            half(2 * kk + 1, sb_sc, sa_sc)

        return carry

    lax.fori_loop(0, (n_steps + 1) // 2, two_steps, 0)

    lam = lam_ref[0]
    outs = []
    for hh in range(2):
        own = (lane // DA_V_DIM) == hh
        a1 = acc_sc[2 * hh * tq:(2 * hh + 1) * tq]
        a2 = acc_sc[(2 * hh + 1) * tq:(2 * hh + 2) * tq]
        l1 = jnp.max(jnp.where(own, 0.0, a1), axis=-1, keepdims=True)
        l2 = jnp.max(jnp.where(own, 0.0, a2), axis=-1, keepdims=True)
        o = a1 / l1 - lam * (a2 / l2)
        ssq = jnp.sum(jnp.where(own, o * o, 0.0), axis=-1, keepdims=True)
        outs.append(o * lax.rsqrt(ssq * (1.0 / DA_V_DIM) + EPS))
    o = jnp.where((lane // DA_V_DIM) == 0, outs[0], outs[1])
    o_ref[...] = (o * gain_ref[...]).astype(o_ref.dtype)


def _attn_prompt(q, kt, vt, bias_tiles, gain, lam, q_block0, n_q):
    seq, w = q.shape
    pairs = w // LANES
    assert seq % ATT_STEP == 0
    n_tiles = ATT_STEP // ATT_BLOCK + 2
    return pl.pallas_call(
        functools.partial(_attn_prompt_kernel, q_block0=q_block0),
        grid=(pairs, n_q),
        in_specs=[
            pl.BlockSpec(memory_space=pltpu.SMEM),
            pl.BlockSpec((ATT_BLOCK, LANES), lambda p, i: (q_block0 + i, p)),
            pl.BlockSpec((LANES, seq), lambda p, i: (p, 0)),
            pl.BlockSpec((LANES, seq), lambda p, i: (p, 0)),
            pl.BlockSpec((1, n_tiles, 2 * ATT_BLOCK, ATT_STEP), lambda p, i: (p, 0, 0, 0)),
            pl.BlockSpec((1, LANES), lambda p, i: (0, 0)),
        ],
        out_specs=pl.BlockSpec((ATT_BLOCK, LANES), lambda p, i: (i, p)),
        out_shape=jax.ShapeDtypeStruct((n_q * ATT_BLOCK, w), bf16),
        scratch_shapes=[
            pltpu.VMEM((4 * ATT_BLOCK, LANES), f32),
            pltpu.VMEM((4 * ATT_BLOCK, LANES), f32),
            pltpu.VMEM((4 * ATT_BLOCK, ATT_STEP), f32),
            pltpu.VMEM((4 * ATT_BLOCK, ATT_STEP), f32),
        ],
        compiler_params=_cparams("parallel", "arbitrary"),
        name="attn_prompt",
    )(lam, q, kt, vt, bias_tiles, gain)


def _prompt_bias_tiles(rel_bias):
    h = rel_bias.shape[1]
    per_step = ATT_STEP // ATT_BLOCK
    i = jnp.arange(ATT_BLOCK)
    rel_diag = i[None, :] - i[:, None]
    rel_prev = rel_diag - ATT_BLOCK
    far = rel_bias[REL_BUCKETS // 2 - 1].astype(f32)
    b_diag = (_bias_of(rel_diag, rel_bias) - far[:, None, None]) * LOG2E
    b_prev = (_bias_of(rel_prev, rel_bias) - far[:, None, None]) * LOG2E
    mask = (i[None, :] // CHUNK) <= (i[:, None] // CHUNK)
    b_diag = jnp.where(mask[None], b_diag, NEG)
    zero = jnp.zeros_like(b_diag)
    dead = jnp.full_like(b_diag, NEG)
    tiles = [jnp.concatenate([zero] * per_step, axis=-1)]
    for r in range(per_step):
        blocks = [zero if j < r - 1 else b_prev if j == r - 1 else b_diag if j == r else dead
                  for j in range(per_step)]
        tiles.append(jnp.concatenate(blocks, axis=-1))
    tiles.append(jnp.concatenate([zero] * (per_step - 1) + [b_prev], axis=-1))
    tiles = jnp.stack(tiles, axis=1)
    tiles = tiles.reshape(h // 2, 2, per_step + 2, ATT_BLOCK, ATT_STEP)
    return jnp.transpose(tiles, (0, 2, 1, 3, 4)).reshape(h // 2, per_step + 2, 2 * ATT_BLOCK, ATT_STEP)


def _attn_sample_kernel(lam_ref, q_ref, kp_ref, vp_ref, kn_ref, vn_ref, bp_ref, bn_ref, gain_ref, o_ref):
    nq = q_ref.shape[0]
    pairs = q_ref.shape[1] // LANES
    lane = lax.broadcasted_iota(jnp.int32, (nq, LANES), 1)
    lam = lam_ref[0]
    nt = (((1,), (1,)), ((), ()))
    for p in range(pairs):
        sl = slice(p * LANES, (p + 1) * LANES)
        qs = q_ref[:, sl].astype(f32) * (DA_HEAD_DIM ** -0.5 * LOG2E)
        kp = kp_ref[sl, :].astype(bf16)
        vp = vp_ref[sl, :].astype(bf16)
        kn = kn_ref[:, sl].astype(bf16)
        vn = vn_ref[:, sl].astype(bf16)
        outs = []
        for hh in range(2):
            h = 2 * p + hh
            own = (lane // DA_V_DIM) == hh
            res = []
            for c in range(2):
                qm = jnp.where((lane // DA_HEAD_DIM) == 2 * hh + c, qs, 0.0).astype(bf16)
                sp = jnp.dot(qm, kp, preferred_element_type=f32) + bp_ref[h]
                sn = lax.dot_general(qm, kn, nt, preferred_element_type=f32) + bn_ref[h]
                m = jnp.maximum(jnp.max(sp, axis=-1, keepdims=True), jnp.max(sn, axis=-1, keepdims=True))
                pp = jnp.exp2(sp - m)
                pn = jnp.exp2(sn - m)
                l = jnp.sum(pp, axis=-1, keepdims=True) + jnp.sum(pn, axis=-1, keepdims=True)
                pv = (lax.dot_general(pp.astype(bf16), vp, nt, preferred_element_type=f32)
                      + jnp.dot(pn.astype(bf16), vn, preferred_element_type=f32))
                res.append(pv / l)
            o = res[0] - lam * res[1]
            ssq = jnp.sum(jnp.where(own, o * o, 0.0), axis=-1, keepdims=True)
            outs.append(o * lax.rsqrt(ssq * (1.0 / DA_V_DIM) + EPS))
        o = jnp.where((lane // DA_V_DIM) == 0, outs[0], outs[1])
        o_ref[:, sl] = (o * gain_ref[...]).astype(o_ref.dtype)


def _attn_sample(q, k_new, v_new, k_past, v_past, bias_past, bias_new, gain, lam, batch, nq, past):
    w = q.shape[1]
    h = bias_past.shape[0]
    return pl.pallas_call(
        _attn_sample_kernel,
        grid=(batch,),
        in_specs=[
            pl.BlockSpec(memory_space=pltpu.SMEM),
            pl.BlockSpec((nq, w), lambda b: (b, 0)),
            pl.BlockSpec((w, past), lambda b: (b, 0)),
            pl.BlockSpec((w, past), lambda b: (b, 0)),
            pl.BlockSpec((nq, w), lambda b: (b, 0)),
            pl.BlockSpec((nq, w), lambda b: (b, 0)),
            pl.BlockSpec((h, nq, past), lambda b: (0, 0, 0)),
            pl.BlockSpec((h, nq, nq), lambda b: (0, 0, 0)),
            pl.BlockSpec((1, LANES), lambda b: (0, 0)),
        ],
        out_specs=pl.BlockSpec((nq, w), lambda b: (b, 0)),
        out_shape=jax.ShapeDtypeStruct((batch * nq, w), bf16),
        compiler_params=_cparams("parallel"),
        name="attn_sample",
    )(lam, q, k_past, v_past, k_new, v_new, bias_past, bias_new, gain)


def _hgrn_chunk(q, f_logit, vv, g_logit, lb, gain, s0, lc):
    n_sub = lc // HG_SUB
    f = lb + (1.0 - lb) * jax.nn.sigmoid(f_logit)
    logf = jnp.log(f)
    k = 1.0 - f
    row = lax.broadcasted_iota(jnp.int32, (lc, lc), 0)
    col = lax.broadcasted_iota(jnp.int32, (lc, lc), 1)
    tril = (col <= row).astype(f32)
    b = jnp.dot(tril, logf, precision=lax.Precision.HIGHEST, preferred_element_type=f32)
    b_last = b[lc - 1:lc, :]

    o = jnp.dot((q * jnp.exp(b)).astype(bf16), s0.astype(bf16), preferred_element_type=f32)
    k_hat = k * jnp.exp(b_last - b)
    bt = b.T
    decay_col = jnp.exp(bt[:, lc - 1:lc])
    s_new = decay_col * s0 + jnp.dot(k_hat.T.astype(bf16), vv.astype(bf16), preferred_element_type=f32)

    if n_sub > 1:
        rows = []
        nt = (((1,), (1,)), ((), ()))
        for i in range(n_sub):
            lo = i * HG_SUB
            if i == 0:
                rows.append(jnp.zeros((HG_SUB, lc), f32))
                continue
            beta = b[lo - 1:lo, :]
            q_t = q[lo:lo + HG_SUB, :] * jnp.exp(b[lo:lo + HG_SUB, :] - beta)
            k_t = k * jnp.exp(jnp.minimum(beta - b, 0.0))
            rows.append(lax.dot_general(q_t.astype(bf16), k_t.astype(bf16), nt, preferred_element_type=f32))
        a_off = jnp.concatenate(rows, axis=0)
        a_off = jnp.where(col < (row // HG_SUB) * HG_SUB, a_off, 0.0)
        o = o + jnp.dot(a_off.astype(bf16), vv.astype(bf16), preferred_element_type=f32)

    ones = jnp.ones((HG_DIM, LANES), bf16)
    srow = lax.broadcasted_iota(jnp.int32, (HG_SUB, HG_DIM), 0)
    orow = lax.broadcasted_iota(jnp.int32, (HG_SUB, HG_DIM), 0)
    blocks = []
    for i in range(n_sub):
        lo = i * HG_SUB
        b_i = b[lo:lo + HG_SUB, :]
        k_i = k[lo:lo + HG_SUB, :]
        v_i = vv[lo:lo + HG_SUB, :]
        q_i = q[lo:lo + HG_SUB, :]
        d_rows = []
        for t in range(HG_SUB):
            arg = jnp.where(srow <= t, b_i[t:t + 1, :] - b_i, -jnp.inf)
            d_rows.append(q_i[t:t + 1, :] * k_i * jnp.exp(arg))
        d3 = jnp.concatenate(d_rows, axis=0)
        a_rep = jnp.dot(d3.astype(bf16), ones, preferred_element_type=f32)
        o_i = jnp.zeros((HG_SUB, HG_DIM), f32)
        for t in range(HG_SUB):
            o_t = jnp.sum(a_rep[t * HG_SUB:(t + 1) * HG_SUB, :] * v_i, axis=0, keepdims=True)
            o_i = jnp.where(orow == t, o_t, o_i)
        blocks.append(o_i)
    o = o + (jnp.concatenate(blocks, axis=0) if n_sub > 1 else blocks[0])

    on = o * lax.rsqrt(jnp.mean(o * o, axis=-1, keepdims=True) + EPS) * gain
    gate = g_logit * jax.nn.sigmoid(g_logit)
    return on * gate, s_new


def _hgrn_kernel(hz_ref, lb_ref, gain_ref, s0_ref, o_ref, s_ref, st_sc, *, lc, heads):
    j = pl.program_id(1)
    w = heads * HG_DIM

    @pl.when(j == 0)
    def _():
        st_sc[...] = s0_ref[0]

    n_chunks = hz_ref.shape[0] // lc
    for c in range(n_chunks):
        rs = slice(c * lc, (c + 1) * lc)
        for h in range(heads):
            cs = slice(h * HG_DIM, (h + 1) * HG_DIM)
            o, s_new = _hgrn_chunk(
                hz_ref[rs, h * HG_DIM:(h + 1) * HG_DIM],
                hz_ref[rs, w + h * HG_DIM:w + (h + 1) * HG_DIM],
                hz_ref[rs, 2 * w + h * HG_DIM:2 * w + (h + 1) * HG_DIM],
                hz_ref[rs, 3 * w + h * HG_DIM:3 * w + (h + 1) * HG_DIM],
                lb_ref[:, cs], gain_ref[...], st_sc[h], lc)
            st_sc[h] = s_new
            o_ref[rs, cs] = o.astype(o_ref.dtype)

    @pl.when(j == pl.num_programs(1) - 1)
    def _():
        s_ref[0] = st_sc[...]


def _hgrn(hz, lb, gain, s0, batch, seq, lc, chunks_per_step):
    t = hz.shape[0]
    w = hz.shape[1] // 4
    heads = w // HG_DIM
    rows = lc * chunks_per_step
    steps = seq // rows
    return pl.pallas_call(
        functools.partial(_hgrn_kernel, lc=lc, heads=heads),
        grid=(batch, steps),
        in_specs=[
            pl.BlockSpec((rows, 4 * w), lambda b, j: (b * steps + j, 0)),
            pl.BlockSpec((1, w), lambda b, j: (0, 0)),
            pl.BlockSpec((1, HG_DIM), lambda b, j: (0, 0)),
            pl.BlockSpec((1, heads, HG_DIM, HG_DIM), lambda b, j: (b, 0, 0, 0)),
        ],
        out_specs=[
            pl.BlockSpec((rows, w), lambda b, j: (b * steps + j, 0)),
            pl.BlockSpec((1, heads, HG_DIM, HG_DIM), lambda b, j: (b, 0, 0, 0)),
        ],
        out_shape=[
            jax.ShapeDtypeStruct((t, w), bf16),
            jax.ShapeDtypeStruct((batch, heads, HG_DIM, HG_DIM), f32),
        ],
        scratch_shapes=[pltpu.VMEM((heads, HG_DIM, HG_DIM), f32)],
        compiler_params=_cparams("parallel", "arbitrary"),
        name="hgrn2",
    )(hz, lb, gain, s0)


def _top_rows(vals, k, payload=None):
    n_rows = vals.shape[0]
    row = lax.broadcasted_iota(jnp.int32, vals.shape, 0)
    out_v, out_i = [], []
    for _ in range(k):
        m = jnp.max(vals, axis=0, keepdims=True)
        first = jnp.min(jnp.where(vals == m, row, n_rows), axis=0, keepdims=True)
        hit = row == first
        out_v.append(m)
        if payload is None:
            out_i.append(first)
        else:
            out_i.append(jnp.max(jnp.where(hit, payload, -1), axis=0, keepdims=True))
        vals = jnp.where(hit, -jnp.inf, vals)
    return jnp.concatenate(out_v, axis=0), jnp.concatenate(out_i, axis=0)


def _retrieve_kernel(oa_ref, ob_ref, x_ref, wo_ref, g_ref, wqt_ref, sk_ref,
                     hp_ref, xn_ref, idx_ref, gate_ref, qt_sc, s1v_sc, s1i_sc, idx_sc, gate_sc):
    half = oa_ref.shape[1]
    y = (jnp.dot(oa_ref[...], wo_ref[0:half, :], preferred_element_type=f32)
         + jnp.dot(ob_ref[...], wo_ref[half:, :], preferred_element_type=f32))
    hp = x_ref[...] + y
    hp_ref[...] = hp
    xn = hp * lax.rsqrt(jnp.mean(hp * hp, axis=-1, keepdims=True) + EPS) * g_ref[...]
    xn_ref[...] = xn
    nt = (((1,), (1,)), ((), ()))
    qt_sc[...] = lax.dot_general(wqt_ref[...], xn.astype(bf16), nt, preferred_element_type=f32).astype(bf16)

    n_half = sk_ref.shape[0]

    def stage1(hc, carry):
        start = pl.multiple_of(hc * PEER_SUB_DIM, PEER_SUB_DIM)
        s = jnp.dot(sk_ref[hc], qt_sc[pl.ds(start, PEER_SUB_DIM), :], preferred_element_type=f32)
        v, i = _top_rows(s, PEER_TOPK)
        s1v_sc[hc] = v
        s1i_sc[hc] = i
        return carry

    lax.fori_loop(0, n_half, stage1, 0)

    tm = x_ref.shape[0]
    sub = lax.broadcasted_iota(jnp.int32, (8, tm), 0)

    def stage2(h, carry):
        v1, i1 = s1v_sc[2 * h], s1i_sc[2 * h]
        v2, i2 = s1v_sc[2 * h + 1], s1i_sc[2 * h + 1]
        cv, ci = [], []
        for i in range(PEER_TOPK):
            nj = PEER_TOPK // (i + 1)
            rows = PEER_TOPK if nj > 8 else 8
            a = v1[i:i + 1, :] + v2[0:rows, :]
            e = i1[i:i + 1, :] * PEER_KEYS + i2[0:rows, :]
            if nj < rows:
                a = jnp.where(sub < nj, a, -jnp.inf)
            cv.append(a)
            ci.append(e)
        top_s, eidx = _top_rows(jnp.concatenate(cv, axis=0), PEER_TOPK, jnp.concatenate(ci, axis=0))
        p = jnp.exp(top_s - top_s[0:1, :])
        g = p / jnp.sum(p, axis=0, keepdims=True)
        r0 = pl.multiple_of(h * PEER_TOPK, PEER_TOPK)
        gate_sc[pl.ds(r0, PEER_TOPK), :] = g
        idx_sc[pl.ds(r0, PEER_TOPK), :] = eidx
        return carry

    lax.fori_loop(0, n_half // 2, stage2, 0)
    idx_ref[...] = pltpu.bitcast(pltpu.bitcast(idx_sc[...], f32).T, jnp.int32)
    gate_ref[...] = gate_sc[...].T


def _retrieve(oa, ob, x, wo_bf, g2, wqt_bf, sk_bf, tm, x_block0=0, ob_block0=0):
    t, half = oa.shape
    d = x.shape[1]
    e = wqt_bf.shape[0]
    n_half = sk_bf.shape[0]
    n_sel = (n_half // 2) * PEER_TOPK
    return pl.pallas_call(
        _retrieve_kernel,
        grid=(t // tm,),
        in_specs=[
            pl.BlockSpec((tm, half), lambda i: (i, 0)),
            pl.BlockSpec((tm, half), lambda i: (ob_block0 + i, 0)),
            pl.BlockSpec((tm, d), lambda i: (x_block0 + i, 0)),
            pl.BlockSpec((2 * half, d), lambda i: (0, 0)),
            pl.BlockSpec((1, d), lambda i: (0, 0)),
            pl.BlockSpec((e, d), lambda i: (0, 0)),
            pl.BlockSpec((n_half, PEER_KEYS, PEER_SUB_DIM), lambda i: (0, 0, 0)),
        ],
        out_specs=[
            pl.BlockSpec((tm, d), lambda i: (i, 0)),
            pl.BlockSpec((tm, d), lambda i: (i, 0)),
            pl.BlockSpec((tm, n_sel), lambda i: (i, 0)),
            pl.BlockSpec((tm, n_sel), lambda i: (i, 0)),
        ],
        out_shape=[
            jax.ShapeDtypeStruct((t, d), f32),
            jax.ShapeDtypeStruct((t, d), f32),
            jax.ShapeDtypeStruct((t, n_sel), jnp.int32),
            jax.ShapeDtypeStruct((t, n_sel), f32),
        ],
        scratch_shapes=[
            pltpu.VMEM((e, tm), bf16),
            pltpu.VMEM((n_half, PEER_TOPK, tm), f32),
            pltpu.VMEM((n_half, PEER_TOPK, tm), jnp.int32),
            pltpu.VMEM((n_sel, tm), jnp.int32),
            pltpu.VMEM((n_sel, tm), f32),
        ],
        compiler_params=_cparams("parallel"),
        name="retrieve",
    )(oa, ob, x, wo_bf, g2, wqt_bf, sk_bf)


def _pack_table(tab):
    half = tab.shape[1] // 2
    lo = lax.bitcast_convert_type(tab[:, :half].astype(bf16), jnp.uint16).astype(jnp.uint32)
    hi = lax.bitcast_convert_type(tab[:, half:].astype(bf16), jnp.uint16).astype(jnp.uint32)
    return lax.bitcast_convert_type(lo | (hi << 16), jnp.int32)


def _sc_layout(n_tok, n_sel, words):
    info = plsc.get_sparse_core_info()
    n_workers = info.num_cores * info.num_subcores
    lanes = info.num_lanes
    block = 8 * lanes
    assert n_tok % (2 * n_workers) == 0 and words % block == 0 and (n_sel // 2) % lanes == 0
    return info, lanes, n_tok // n_workers, n_sel // 2, block


def _selected_dots(table, idx, x):
    n_tok, n_sel = idx.shape
    words = table.shape[1]
    info, lanes, tpw, half_rows, block = _sc_layout(n_tok, n_sel, words)
    idx2 = idx.reshape(2 * n_tok, half_rows)
    mesh = plsc.VectorSubcoreMesh(core_axis_name="c", subcore_axis_name="s")

    @functools.partial(
        pl.kernel, out_type=jax.ShapeDtypeStruct((n_tok, n_sel), f32), mesh=mesh,
        compiler_params=pltpu.CompilerParams(needs_layout_passes=False),
        scratch_types=[
            pltpu.VMEM((2 * tpw, half_rows), jnp.int32),
            pltpu.VMEM((half_rows, words), jnp.int32),
            pltpu.VMEM((half_rows, words), jnp.int32),
            pltpu.VMEM((2 * words,), f32),
            pltpu.VMEM((2 * words,), f32),
            pltpu.VMEM((n_sel // 8, 8 * lanes), f32),
            pltpu.VMEM((tpw, n_sel), f32),
            pltpu.SemaphoreType.DMA, pltpu.SemaphoreType.DMA, pltpu.SemaphoreType.DMA, pltpu.SemaphoreType.DMA,
        ])
    def dots(t_hbm, i_hbm, x_hbm, o_hbm, idx_v, rows_a, rows_b, x_a, x_b, part_v, act_v, sem_a, sem_b, sem_xa, sem_xb):
        wid = lax.axis_index("s") * info.num_cores + lax.axis_index("c")
        base = wid * tpw
        pltpu.sync_copy(i_hbm.at[pl.ds(base * 2, 2 * tpw)], idx_v)

        def accumulate(rows, x_v, half):
            @pl.loop(0, words // block)
            def _(cb):
                xs = [x_v[pl.ds(p * words + cb * block + c * lanes, lanes)] for c in range(8) for p in range(2)]

                @plsc.parallel_loop(0, half_rows)
                def _(row):
                    packed = [rows[row, pl.ds(cb * block + c * lanes, lanes)] for c in range(8)]
                    unp = [plsc.unpack(plsc.bitcast(p, bf16), format=plsc.PackFormat.INTERLEAVED) for p in packed]
                    prods = [unp[c][p] * xs[2 * c + p] for c in range(8) for p in range(2)]
                    while len(prods) > 1:
                        prods = [prods[k] + prods[k + 1] for k in range(0, len(prods), 2)]
                    prow = half * half_rows + row
                    plsc.addupdate(part_v.at[prow // 8, pl.ds((prow % 8) * lanes, lanes)], prods[0])

        def gather(half_index, rows, sem):
            return pltpu.make_async_copy(t_hbm.at[idx_v.at[half_index]], rows, sem)

        def xcopy(t, x_v, sem):
            return pltpu.make_async_copy(x_hbm.at[base + t], x_v, sem)

        def token(t, x_v, sem_x, x_next, sem_xn):
            gather(2 * t + 1, rows_b, sem_b).start()

            @pl.when(t + 1 < tpw)
            def _():
                xcopy(t + 1, x_next, sem_xn).start()

            zero = jnp.zeros((lanes,), f32)
            for r in range(n_sel):
                part_v[r // 8, pl.ds((r % 8) * lanes, lanes)] = zero
            xcopy(t, x_v, sem_x).wait()
            gather(2 * t, rows_a, sem_a).wait()
            accumulate(rows_a, x_v, 0)

            @pl.when(t + 1 < tpw)
            def _():
                gather(2 * t + 2, rows_a, sem_a).start()

            gather(2 * t + 1, rows_b, sem_b).wait()
            accumulate(rows_b, x_v, 1)
            it = lax.iota(jnp.int32, lanes)
            for g in range(n_sel // lanes):
                prow = g * (lanes // 8) + it // 8
                pcol = (it % 8) * lanes
                cols = [plsc.load_gather(part_v, [prow, pcol + l]) for l in range(lanes)]
                while len(cols) > 1:
                    cols = [cols[k] + cols[k + 1] for k in range(0, len(cols), 2)]
                act_v[t, pl.ds(g * lanes, lanes)] = cols[0]

        gather(0, rows_a, sem_a).start()
        xcopy(0, x_a, sem_xa).start()

        @pl.loop(0, tpw // 2)
        def _(i):
            token(2 * i, x_a, sem_xa, x_b, sem_xb)
            token(2 * i + 1, x_b, sem_xb, x_a, sem_xa)

        pltpu.sync_copy(act_v, o_hbm.at[pl.ds(base, tpw)])

    return dots(table, idx2, x)


def _gate_weights_kernel(act_ref, gate_ref, after_ref, w_ref):
    del after_ref
    a = act_ref[...]
    w_ref[...] = gate_ref[...] * (0.5 * a * (1.0 + lax.erf(a * (2.0 ** -0.5))))


def _gate_weights(act, gate, after, tm):
    t, n_sel = act.shape
    spec = pl.BlockSpec((tm, n_sel), lambda i: (i, 0))
    return pl.pallas_call(
        _gate_weights_kernel,
        grid=(t // tm,),
        in_specs=[spec, spec, pl.BlockSpec(memory_space=pl.ANY)],
        out_specs=spec,
        out_shape=jax.ShapeDtypeStruct((t, n_sel), f32),
        compiler_params=_cparams("parallel"),
        name="peer_weights",
    )(act, gate, after)


def _weighted_rows(table, idx, w):
    n_tok, n_sel = w.shape
    words = table.shape[1]
    info, lanes, tpw, half_rows, _ = _sc_layout(n_tok, n_sel, words)
    idx2 = idx.reshape(2 * n_tok, half_rows)
    mesh = plsc.VectorSubcoreMesh(core_axis_name="c", subcore_axis_name="s")

    @functools.partial(
        pl.kernel, out_type=jax.ShapeDtypeStruct((n_tok, 2 * words), f32), mesh=mesh,
        compiler_params=pltpu.CompilerParams(needs_layout_passes=False),
        scratch_types=[
            pltpu.VMEM((2 * tpw, half_rows), jnp.int32),
            pltpu.VMEM((tpw, n_sel), f32),
            pltpu.VMEM((half_rows, words), jnp.int32),
            pltpu.VMEM((half_rows, words), jnp.int32),
            pltpu.VMEM((2 * words,), f32),
            pltpu.VMEM((2 * words,), f32),
            pltpu.SemaphoreType.DMA, pltpu.SemaphoreType.DMA, pltpu.SemaphoreType.DMA, pltpu.SemaphoreType.DMA,
        ])
    def mix(t_hbm, i_hbm, w_hbm, o_hbm, idx_v, w_v, rows_a, rows_b, out_a, out_b, sem_a, sem_b, sem_oa, sem_ob):
        wid = lax.axis_index("s") * info.num_cores + lax.axis_index("c")
        base = wid * tpw
        pltpu.sync_copy(i_hbm.at[pl.ds(base * 2, 2 * tpw)], idx_v)
        pltpu.sync_copy(w_hbm.at[pl.ds(base, tpw)], w_v)

        def accumulate(rows, out_v, t, half, first):
            if first:
                zero = jnp.zeros((lanes,), f32)
                for c in range(2 * words // lanes):
                    out_v[pl.ds(c * lanes, lanes)] = zero

            @pl.loop(0, half_rows // lanes)
            def _(g):
                wv = w_v[t, pl.ds(half * half_rows + g * lanes, lanes)]
                ws = [wv.at[jnp.full((lanes,), r, jnp.int32)].get(mode="promise_in_bounds") for r in range(lanes)]

                @plsc.parallel_loop(0, words // lanes)
                def _(c):
                    packed = [rows[g * lanes + r, pl.ds(c * lanes, lanes)] for r in range(lanes)]
                    unp = [plsc.unpack(plsc.bitcast(p, bf16), format=plsc.PackFormat.INTERLEAVED) for p in packed]
                    lo = [unp[r][0] * ws[r] for r in range(lanes)]
                    hi = [unp[r][1] * ws[r] for r in range(lanes)]
                    while len(lo) > 1:
                        lo = [lo[k] + lo[k + 1] for k in range(0, len(lo), 2)]
                        hi = [hi[k] + hi[k + 1] for k in range(0, len(hi), 2)]
                    plsc.addupdate(out_v.at[pl.ds(c * lanes, lanes)], lo[0])
                    plsc.addupdate(out_v.at[pl.ds(words + c * lanes, lanes)], hi[0])

        def gather(half_index, rows, sem):
            return pltpu.make_async_copy(t_hbm.at[idx_v.at[half_index]], rows, sem)

        def token(t, out_v, sem_o, out_in_flight):
            gather(2 * t + 1, rows_b, sem_b).start()
            gather(2 * t, rows_a, sem_a).wait()

            @pl.when(out_in_flight)
            def _():
                pltpu.make_async_copy(out_v, o_hbm.at[base + t], sem_o).wait()

            accumulate(rows_a, out_v, t, 0, True)

            @pl.when(t + 1 < tpw)
            def _():
                gather(2 * t + 2, rows_a, sem_a).start()

            gather(2 * t + 1, rows_b, sem_b).wait()
            accumulate(rows_b, out_v, t, 1, False)
            pltpu.make_async_copy(out_v, o_hbm.at[base + t], sem_o).start()

        gather(0, rows_a, sem_a).start()

        @pl.loop(0, tpw // 2)
        def _(i):
            token(2 * i, out_a, sem_oa, i > 0)
            token(2 * i + 1, out_b, sem_ob, i > 0)

        pltpu.make_async_copy(out_a, o_hbm.at[base], sem_oa).wait()
        pltpu.make_async_copy(out_b, o_hbm.at[base], sem_ob).wait()

    return mix(table, idx2, w)


def _final_kernel(hp_ref, peer_ref, gf_ref, yin_ref, y_ref, done_ref):
    del yin_ref
    done_ref[...] = jnp.zeros(done_ref.shape, f32)
    h = hp_ref[...] + peer_ref[...]
    y_ref[...] = h * lax.rsqrt(jnp.mean(h * h, axis=-1, keepdims=True) + EPS) * gf_ref[...]


def _final(hp, peer, gf, y_all, y_block0, tm):
    t, d = hp.shape
    return pl.pallas_call(
        _final_kernel,
        grid=(t // tm,),
        in_specs=[
            pl.BlockSpec((tm, d), lambda i: (i, 0)),
            pl.BlockSpec((tm, d), lambda i: (i, 0)),
            pl.BlockSpec((1, d), lambda i: (0, 0)),
            pl.BlockSpec(memory_space=pl.ANY),
        ],
        out_specs=[pl.BlockSpec((tm, d), lambda i: (y_block0 + i, 0)),
                   pl.BlockSpec((8, LANES), lambda i: (0, 0))],
        out_shape=[jax.ShapeDtypeStruct(y_all.shape, f32), jax.ShapeDtypeStruct((8, LANES), f32)],
        input_output_aliases={3: 0},
        compiler_params=_cparams("arbitrary"),
        name="final_norm",
    )(hp, peer, gf, y_all)


def _channel_mixer_front(oa, ob, x, x_block0, ob_block0, wo_bf, g2, wqt_bf, sk_bf, u_pk, tm):
    hp, xn, eidx, gate = _retrieve(oa, ob, x, wo_bf, g2, wqt_bf, sk_bf, tm, x_block0, ob_block0)
    return hp, eidx, gate, _selected_dots(u_pk, eidx, xn)


def _channel_mixer_mix(front, after, v_pk, tm):
    _, eidx, gate, act = front
    return _weighted_rows(v_pk, eidx, _gate_weights(act, gate, after, tm))


def kernel(x_prompt, x_sample, cache_k, cache_v, state_hgrn, norm1, w_in, da_lq1, da_lk1, da_lq2, da_lk2,
           da_out_norm, hg_lb_logits, hg_out_norm, w_out, rel_bias, norm2, peer_w_q, peer_sub_keys,
           peer_u, peer_v, final_norm):
    batch, seq, d = x_prompt.shape
    dbatch, dseq, _ = x_sample.shape
    past = cache_k.shape[2]
    da_heads = cache_k.shape[3]
    hg_heads = state_hgrn.shape[2]
    depth = w_in.shape[0]
    assert depth == 1 and seq % ATT_BLOCK == 0 and seq % CHUNK == 0 and dseq % HG_SUB == 0 and dseq <= CHUNK
    assert past % CHUNK == 0 and (past + dseq - 1) // CHUNK == past // CHUNK

    l = 0
    lam_init = 0.8 - 0.6 * math.exp(-0.3 * l)
    lam = (jnp.exp(jnp.sum(da_lq1[l].astype(f32) * da_lk1[l].astype(f32)))
           - jnp.exp(jnp.sum(da_lq2[l].astype(f32) * da_lk2[l].astype(f32))) + lam_init).reshape(1)
    lb = jnp.cumsum(jax.nn.softmax(hg_lb_logits.astype(f32), axis=0), axis=0)[l].reshape(1, -1)
    da_gain = (jnp.tile(da_out_norm[l].astype(f32), 2) * (1.0 - lam_init)).reshape(1, LANES)
    hg_gain = hg_out_norm[l].astype(f32).reshape(1, HG_DIM)
    g1 = norm1[l].astype(f32).reshape(1, d)
    g2 = norm2[l].astype(f32).reshape(1, d)
    gf = final_norm.astype(f32).reshape(1, d)
    w_in_bf = w_in[l].astype(bf16)
    wkv_t_bf = w_in[l][:, cache_k.shape[3] * cache_k.shape[4]:3 * cache_k.shape[3] * cache_k.shape[4]].T.astype(bf16)
    wo_bf = w_out[l].astype(bf16)
    wqt_bf = peer_w_q[l].T.astype(bf16)
    sk_bf = peer_sub_keys[l].reshape(-1, PEER_KEYS, PEER_SUB_DIM).astype(bf16)
    u_pk = _pack_table(peer_u[l])
    v_pk = _pack_table(peer_v[l])

    bias_tiles = _prompt_bias_tiles(rel_bias)
    q_pos = past + jnp.arange(dseq)
    bias_s = _bias_of(jnp.arange(past + dseq)[None, :] - q_pos[:, None], rel_bias) * LOG2E
    bias_past, bias_new = bias_s[:, :, :past], bias_s[:, :, past:]
    zero_state = jnp.zeros((1, hg_heads, HG_DIM, HG_DIM), f32)

    dh2 = 2 * DA_HEAD_DIM
    wq = da_heads * dh2
    tm = 512
    x_all = x_prompt.reshape(batch * seq, d)
    kt_all = jnp.zeros((batch * wq, seq), f32)
    vt_all = jnp.zeros((batch * wq, seq), f32)
    y_all = jnp.zeros((batch * seq, d), f32)
    nq = seq // ATT_BLOCK
    blocks = seq // tm
    cuts = {0: 4, 1: 2}
    segments = [(b, c * (nq // cuts.get(b, 1)), nq // cuts.get(b, 1))
                for b in range(batch) for c in range(cuts.get(b, 1))]
    assert all(nq % n == 0 and (nq // n * ATT_BLOCK) % (2 * tm) == 0 for n in cuts.values())
    s_p = []
    done = [jnp.zeros((8, LANES), f32)] * 3
    fronts, peers = [], []

    def finish(k, y_all):
        return _final(fronts[k][0][0], peers[k], gf, y_all, fronts[k][1], tm)

    for s, (b, q0, n_q) in enumerate(segments):
        if q0 == 0:
            q, kt, vt, hz, kt_all, vt_all = _inproj_prompt(x_all, g1, w_in_bf, wkv_t_bf, kt_all, vt_all, done[s],
                                                           b, seq, tm)
            ob, s_new = _hgrn(hz, lb, hg_gain, zero_state, 1, seq, CHUNK, 4)
            s_p.append(s_new)
        oa = _attn_prompt(q, kt, vt, bias_tiles, da_gain, lam, q0, n_q)
        if s >= 1:
            peers.append(_channel_mixer_mix(fronts[s - 1][0], oa, v_pk, tm))
        row0 = q0 * ATT_BLOCK // tm
        fronts.append((_channel_mixer_front(oa, ob, x_all, b * blocks + row0, row0, wo_bf, g2, wqt_bf, sk_bf, u_pk,
                                            tm), b * blocks + row0))
        if s >= 2:
            y_all, token = finish(s - 2, y_all)
            done.append(token)
    peers.append(_channel_mixer_mix(fronts[-1][0], fronts[-1][0][0], v_pk, tm))
    for k in (len(segments) - 2, len(segments) - 1):
        y_all, _ = finish(k, y_all)

    xs = x_sample.reshape(dbatch * dseq, d)
    q, k_s, v_s, hz = _inproj_sample(xs, g1, w_in_bf, wq)
    kp_t = jnp.transpose(cache_k[l], (0, 2, 3, 1)).reshape(dbatch * wq, past)
    vp_t = jnp.transpose(cache_v[l], (0, 2, 3, 1)).reshape(dbatch * wq, past)
    oa = _attn_sample(q, k_s, v_s, kp_t, vp_t, bias_past, bias_new, da_gain, lam, dbatch, dseq, past)
    ob, s_s = _hgrn(hz, lb, hg_gain, state_hgrn[l].astype(f32), dbatch, dseq, dseq, 1)
    front = _channel_mixer_front(oa, ob, xs, 0, 0, wo_bf, g2, wqt_bf, sk_bf, u_pk, dbatch * dseq)
    peer = _channel_mixer_mix(front, front[0], v_pk, dbatch * dseq)
    y_s, _ = _final(front[0], peer, gf, jnp.zeros((dbatch * dseq, d), f32), 0, dbatch * dseq)

    y_prompt = y_all.reshape(batch, seq, d)
    y_sample = y_s.reshape(dbatch, dseq, d)
    k_prompt = jnp.transpose(kt_all.reshape(batch, da_heads, dh2, seq), (0, 3, 1, 2))[None]
    v_prompt = jnp.transpose(vt_all.reshape(batch, da_heads, DA_V_DIM, seq), (0, 3, 1, 2))[None]
    state_prompt = jnp.concatenate(s_p, axis=0)[None].astype(state_hgrn.dtype)
    k_sample = k_s.reshape(1, dbatch, dseq, da_heads, dh2)
    v_sample = v_s.reshape(1, dbatch, dseq, da_heads, DA_V_DIM)
    state_sample = s_s[None].astype(state_hgrn.dtype)
    return (y_prompt, y_sample, k_prompt, v_prompt, state_prompt, k_sample, v_sample, state_sample)
```

```python
import functools
import math

import jax
import jax.numpy as jnp
from jax import lax
from jax.experimental import pallas as pl
from jax.experimental.pallas import tpu as pltpu
from jax.experimental.pallas import tpu_sc as plsc

CHUNK = 64
DA_HEAD_DIM = 32
DA_V_DIM = 2 * DA_HEAD_DIM
HG_DIM = 128
REL_BUCKETS = 32
REL_MAX_DIST = 128
PEER_KEYS = 128
PEER_TOPK = 16
PEER_SUB_DIM = 64
EPS = 1e-6
NEG = -1e30
LOG2E = 1.4426950408889634

LANES = 128
ATT_BLOCK = 128
ATT_STEP = 512
HG_SUB = 16
V7X_VMEM_BYTES = 64 * 1024 * 1024
VMEM_LIMIT = V7X_VMEM_BYTES * 7 // 8

f32 = jnp.float32
bf16 = jnp.bfloat16


def _cparams(*sem):
    return pltpu.CompilerParams(dimension_semantics=sem, vmem_limit_bytes=VMEM_LIMIT)


def _inproj_prompt_kernel(x_ref, g_ref, w_ref, wkv_t_ref, kin_ref, vin_ref, after_ref,
                          q_ref, kt_ref, vt_ref, hz_ref, ktf_ref, vtf_ref):
    del kin_ref, vin_ref
    del after_ref
    x = x_ref[...]
    xn = x * lax.rsqrt(jnp.mean(x * x, axis=-1, keepdims=True) + EPS) * g_ref[...]
    xb = xn.astype(bf16)
    wq = q_ref.shape[1]
    nt = (((1,), (1,)), ((), ()))
    q_ref[...] = jnp.dot(xb, w_ref[:, 0:wq], preferred_element_type=f32).astype(bf16)
    kt = lax.dot_general(wkv_t_ref[0:wq, :], xb, nt, preferred_element_type=f32)
    ktf_ref[...] = kt
    kt_ref[...] = kt.astype(bf16)
    vt = lax.dot_general(wkv_t_ref[wq:2 * wq, :], xb, nt, preferred_element_type=f32)
    vtf_ref[...] = vt
    vt_ref[...] = vt.astype(bf16)
    hz_ref[...] = jnp.dot(xb, w_ref[:, 3 * wq:], preferred_element_type=f32)


def _inproj_prompt(x, g, w_bf, wkv_t_bf, kt_all, vt_all, after, b, seq, tm):
    d = x.shape[1]
    e = w_bf.shape[1]
    wq = wkv_t_bf.shape[0] // 2
    steps = seq // tm
    return pl.pallas_call(
        _inproj_prompt_kernel,
        grid=(steps,),
        in_specs=[
            pl.BlockSpec((tm, d), lambda i: (b * steps + i, 0)),
            pl.BlockSpec((1, d), lambda i: (0, 0)),
            pl.BlockSpec((d, e), lambda i: (0, 0)),
            pl.BlockSpec((2 * wq, d), lambda i: (0, 0)),
            pl.BlockSpec(memory_space=pl.ANY),
            pl.BlockSpec(memory_space=pl.ANY),
            pl.BlockSpec(memory_space=pl.ANY),
        ],
        out_specs=[
            pl.BlockSpec((tm, wq), lambda i: (i, 0)),
            pl.BlockSpec((wq, tm), lambda i: (0, i)),
            pl.BlockSpec((wq, tm), lambda i: (0, i)),
            pl.BlockSpec((tm, e - 3 * wq), lambda i: (i, 0)),
            pl.BlockSpec((wq, tm), lambda i: (b, i)),
            pl.BlockSpec((wq, tm), lambda i: (b, i)),
        ],
        out_shape=[
            jax.ShapeDtypeStruct((seq, wq), bf16),
            jax.ShapeDtypeStruct((wq, seq), bf16),
            jax.ShapeDtypeStruct((wq, seq), bf16),
            jax.ShapeDtypeStruct((seq, e - 3 * wq), f32),
            jax.ShapeDtypeStruct(kt_all.shape, f32),
            jax.ShapeDtypeStruct(vt_all.shape, f32),
        ],
        input_output_aliases={4: 4, 5: 5},
        compiler_params=_cparams("parallel"),
        name="inproj",
    )(x, g, w_bf, wkv_t_bf, kt_all, vt_all, after)


def _inproj_sample_kernel(x_ref, g_ref, w_ref, q_ref, k_ref, v_ref, hz_ref):
    x = x_ref[...]
    xn = x * lax.rsqrt(jnp.mean(x * x, axis=-1, keepdims=True) + EPS) * g_ref[...]
    xb = xn.astype(bf16)
    wq = q_ref.shape[1]
    q_ref[...] = jnp.dot(xb, w_ref[:, 0:wq], preferred_element_type=f32).astype(bf16)
    k_ref[...] = jnp.dot(xb, w_ref[:, wq:2 * wq], preferred_element_type=f32)
    v_ref[...] = jnp.dot(xb, w_ref[:, 2 * wq:3 * wq], preferred_element_type=f32)
    hz_ref[...] = jnp.dot(xb, w_ref[:, 3 * wq:], preferred_element_type=f32)


def _inproj_sample(x, g, w_bf, wq):
    t, d = x.shape
    e = w_bf.shape[1]
    return pl.pallas_call(
        _inproj_sample_kernel,
        grid=(1,),
        in_specs=[
            pl.BlockSpec((t, d), lambda i: (0, 0)),
            pl.BlockSpec((1, d), lambda i: (0, 0)),
            pl.BlockSpec((d, e), lambda i: (0, 0)),
        ],
        out_specs=[
            pl.BlockSpec((t, wq), lambda i: (0, 0)),
            pl.BlockSpec((t, wq), lambda i: (0, 0)),
            pl.BlockSpec((t, wq), lambda i: (0, 0)),
            pl.BlockSpec((t, e - 3 * wq), lambda i: (0, 0)),
        ],
        out_shape=[
            jax.ShapeDtypeStruct((t, wq), bf16),
            jax.ShapeDtypeStruct((t, wq), f32),
            jax.ShapeDtypeStruct((t, wq), f32),
            jax.ShapeDtypeStruct((t, e - 3 * wq), f32),
        ],
        compiler_params=_cparams("arbitrary"),
        name="inproj_sample",
    )(x, g, w_bf)


def _rel_bucket(rel):
    nb = REL_BUCKETS // 2
    max_exact = nb // 2
    ret = jnp.where(rel > 0, nb, 0)
    n = jnp.abs(rel)
    nf = jnp.maximum(n, 1).astype(f32)
    large = max_exact + (jnp.log(nf / max_exact) / math.log(REL_MAX_DIST / max_exact)
                         * (nb - max_exact)).astype(jnp.int32)
    large = jnp.minimum(large, nb - 1)
    return ret + jnp.where(n < max_exact, n, large)


def _bias_of(rel, rel_bias):
    bucket = _rel_bucket(rel)
    out = jnp.zeros((rel_bias.shape[1],) + rel.shape, f32)
    for b in range(REL_BUCKETS):
        out = out + jnp.where(bucket[None] == b, rel_bias[b].astype(f32)[:, None, None], 0.0)
    return out


def _attn_prompt_kernel(lam_ref, q_ref, kt_ref, vt_ref, bias_ref, gain_ref, o_ref, m_sc, acc_sc, sa_sc, sb_sc,
                        *, q_block0):
    qb = pl.program_id(1) + q_block0
    tq = q_ref.shape[0]
    seq = kt_ref.shape[1]
    per_step = ATT_STEP // ATT_BLOCK
    lane = lax.broadcasted_iota(jnp.int32, (tq, LANES), 1)
    feat_k = lax.broadcasted_iota(jnp.int32, (LANES, ATT_STEP), 0)
    qs = q_ref[...].astype(f32) * (DA_HEAD_DIM ** -0.5 * LOG2E)
    q4 = jnp.concatenate(
        [jnp.where((lane // DA_HEAD_DIM) == i, qs, 0.0).astype(bf16) for i in range(4)], axis=0)

    m_sc[...] = jnp.full(m_sc.shape, -jnp.inf, f32)
    acc_sc[...] = jnp.zeros(acc_sc.shape, f32)

    n_steps = qb // per_step + 1
    last = n_steps - 1
    r = qb % per_step

    def scores(k):
        start = pl.multiple_of(jnp.minimum(k * ATT_STEP, seq - ATT_STEP), ATT_STEP)
        return jnp.dot(q4, kt_ref[:, pl.ds(start, ATT_STEP)], preferred_element_type=f32)

    def half(k, cur_ref, nxt_ref):
        nxt_ref[...] = scores(k + 1)
        tile = jnp.where(k == last, 1 + r, jnp.where((k == last - 1) & (r == 0), per_step + 1, 0))
        vv = vt_ref[:, pl.ds(pl.multiple_of(k * ATT_STEP, ATT_STEP), ATT_STEP)]
        for hh in range(2):
            bias = bias_ref[0, tile, hh * tq:(hh + 1) * tq, :]
            ps, alphas = [], []
            for c in range(2):
                rows = slice((2 * hh + c) * tq, (2 * hh + c + 1) * tq)
                s = cur_ref[rows, :] + bias
                m_old = m_sc[rows]
                m_new = jnp.maximum(m_old, jnp.max(s, axis=-1, keepdims=True))
                ps.append(jnp.exp2(s - m_new[:, 0:1]).astype(bf16))
                alphas.append(jnp.exp2(m_old - m_new))
                m_sc[rows] = m_new
            rows2 = slice(2 * hh * tq, (2 * hh + 2) * tq)
            vaug = jnp.where((feat_k // DA_V_DIM) == hh, vv, jnp.ones_like(vv))
            acc_sc[rows2] = (acc_sc[rows2] * jnp.concatenate(alphas, axis=0)
                             + lax.dot_general(jnp.concatenate(ps, axis=0), vaug, (((1,), (1,)), ((), ())),
                                               preferred_element_type=f32))

    sa_sc[...] = scores(0)

    def two_steps(kk, carry):
        half(2 * kk, sa_sc, sb_sc)

        @pl.when(2 * kk + 1 < n_steps)
        def _():
            half(2 * kk + 1, sb_sc, sa_sc)

        return carry

    lax.fori_loop(0, (n_steps + 1) // 2, two_steps, 0)

    lam = lam_ref[0]
    outs = []
    for hh in range(2):
        own = (lane // DA_V_DIM) == hh
        a1 = acc_sc[2 * hh * tq:(2 * hh + 1) * tq]
        a2 = acc_sc[(2 * hh + 1) * tq:(2 * hh + 2) * tq]
        l1 = jnp.max(jnp.where(own, 0.0, a1), axis=-1, keepdims=True)
        l2 = jnp.max(jnp.where(own, 0.0, a2), axis=-1, keepdims=True)
        o = a1 / l1 - lam * (a2 / l2)
        ssq = jnp.sum(jnp.where(own, o * o, 0.0), axis=-1, keepdims=True)
        outs.append(o * lax.rsqrt(ssq * (1.0 / DA_V_DIM) + EPS))
    o = jnp.where((lane // DA_V_DIM) == 0, outs[0], outs[1])
    o_ref[...] = (o * gain_ref[...]).astype(o_ref.dtype)


def _attn_prompt(q, kt, vt, bias_tiles, gain, lam, q_block0, n_q):
    seq, w = q.shape
    pairs = w // LANES
    assert seq % ATT_STEP == 0
    n_tiles = ATT_STEP // ATT_BLOCK + 2
    return pl.pallas_call(
        functools.partial(_attn_prompt_kernel, q_block0=q_block0),
        grid=(pairs, n_q),
        in_specs=[
            pl.BlockSpec(memory_space=pltpu.SMEM),
            pl.BlockSpec((ATT_BLOCK, LANES), lambda p, i: (q_block0 + i, p)),
            pl.BlockSpec((LANES, seq), lambda p, i: (p, 0)),
            pl.BlockSpec((LANES, seq), lambda p, i: (p, 0)),
            pl.BlockSpec((1, n_tiles, 2 * ATT_BLOCK, ATT_STEP), lambda p, i: (p, 0, 0, 0)),
            pl.BlockSpec((1, LANES), lambda p, i: (0, 0)),
        ],
        out_specs=pl.BlockSpec((ATT_BLOCK, LANES), lambda p, i: (i, p)),
        out_shape=jax.ShapeDtypeStruct((n_q * ATT_BLOCK, w), bf16),
        scratch_shapes=[
            pltpu.VMEM((4 * ATT_BLOCK, LANES), f32),
            pltpu.VMEM((4 * ATT_BLOCK, LANES), f32),
            pltpu.VMEM((4 * ATT_BLOCK, ATT_STEP), f32),
            pltpu.VMEM((4 * ATT_BLOCK, ATT_STEP), f32),
        ],
        compiler_params=_cparams("parallel", "arbitrary"),
        name="attn_prompt",
    )(lam, q, kt, vt, bias_tiles, gain)


def _prompt_bias_tiles(rel_bias):
    h = rel_bias.shape[1]
    per_step = ATT_STEP // ATT_BLOCK
    i = jnp.arange(ATT_BLOCK)
    rel_diag = i[None, :] - i[:, None]
    rel_prev = rel_diag - ATT_BLOCK
    far = rel_bias[REL_BUCKETS // 2 - 1].astype(f32)
    b_diag = (_bias_of(rel_diag, rel_bias) - far[:, None, None]) * LOG2E
    b_prev = (_bias_of(rel_prev, rel_bias) - far[:, None, None]) * LOG2E
    mask = (i[None, :] // CHUNK) <= (i[:, None] // CHUNK)
    b_diag = jnp.where(mask[None], b_diag, NEG)
    zero = jnp.zeros_like(b_diag)
    dead = jnp.full_like(b_diag, NEG)
    tiles = [jnp.concatenate([zero] * per_step, axis=-1)]
    for r in range(per_step):
        blocks = [zero if j < r - 1 else b_prev if j == r - 1 else b_diag if j == r else dead
                  for j in range(per_step)]
        tiles.append(jnp.concatenate(blocks, axis=-1))
    tiles.append(jnp.concatenate([zero] * (per_step - 1) + [b_prev], axis=-1))
    tiles = jnp.stack(tiles, axis=1)
    tiles = tiles.reshape(h // 2, 2, per_step + 2, ATT_BLOCK, ATT_STEP)
    return jnp.transpose(tiles, (0, 2, 1, 3, 4)).reshape(h // 2, per_step + 2, 2 * ATT_BLOCK, ATT_STEP)


def _attn_sample_kernel(lam_ref, q_ref, kp_ref, vp_ref, kn_ref, vn_ref, bp_ref, bn_ref, gain_ref, o_ref):
    nq = q_ref.shape[0]
    pairs = q_ref.shape[1] // LANES
    lane = lax.broadcasted_iota(jnp.int32, (nq, LANES), 1)
    lam = lam_ref[0]
    nt = (((1,), (1,)), ((), ()))
    for p in range(pairs):
        sl = slice(p * LANES, (p + 1) * LANES)
        qs = q_ref[:, sl].astype(f32) * (DA_HEAD_DIM ** -0.5 * LOG2E)
        kp = kp_ref[sl, :].astype(bf16)
        vp = vp_ref[sl, :].astype(bf16)
        kn = kn_ref[:, sl].astype(bf16)
        vn = vn_ref[:, sl].astype(bf16)
        outs = []
        for hh in range(2):
            h = 2 * p + hh
            own = (lane // DA_V_DIM) == hh
            res = []
            for c in range(2):
                qm = jnp.where((lane // DA_HEAD_DIM) == 2 * hh + c, qs, 0.0).astype(bf16)
                sp = jnp.dot(qm, kp, preferred_element_type=f32) + bp_ref[h]
                sn = lax.dot_general(qm, kn, nt, preferred_element_type=f32) + bn_ref[h]
                m = jnp.maximum(jnp.max(sp, axis=-1, keepdims=True), jnp.max(sn, axis=-1, keepdims=True))
                pp = jnp.exp2(sp - m)
                pn = jnp.exp2(sn - m)
                l = jnp.sum(pp, axis=-1, keepdims=True) + jnp.sum(pn, axis=-1, keepdims=True)
                pv = (lax.dot_general(pp.astype(bf16), vp, nt, preferred_element_type=f32)
                      + jnp.dot(pn.astype(bf16), vn, preferred_element_type=f32))
                res.append(pv / l)
            o = res[0] - lam * res[1]
            ssq = jnp.sum(jnp.where(own, o * o, 0.0), axis=-1, keepdims=True)
            outs.append(o * lax.rsqrt(ssq * (1.0 / DA_V_DIM) + EPS))
        o = jnp.where((lane // DA_V_DIM) == 0, outs[0], outs[1])
        o_ref[:, sl] = (o * gain_ref[...]).astype(o_ref.dtype)


def _attn_sample(q, k_new, v_new, k_past, v_past, bias_past, bias_new, gain, lam, batch, nq, past):
    w = q.shape[1]
    h = bias_past.shape[0]
    return pl.pallas_call(
        _attn_sample_kernel,
        grid=(batch,),
        in_specs=[
            pl.BlockSpec(memory_space=pltpu.SMEM),
            pl.BlockSpec((nq, w), lambda b: (b, 0)),
            pl.BlockSpec((w, past), lambda b: (b, 0)),
            pl.BlockSpec((w, past), lambda b: (b, 0)),
            pl.BlockSpec((nq, w), lambda b: (b, 0)),
            pl.BlockSpec((nq, w), lambda b: (b, 0)),
            pl.BlockSpec((h, nq, past), lambda b: (0, 0, 0)),
            pl.BlockSpec((h, nq, nq), lambda b: (0, 0, 0)),
            pl.BlockSpec((1, LANES), lambda b: (0, 0)),
        ],
        out_specs=pl.BlockSpec((nq, w), lambda b: (b, 0)),
        out_shape=jax.ShapeDtypeStruct((batch * nq, w), bf16),
        compiler_params=_cparams("parallel"),
        name="attn_sample",
    )(lam, q, k_past, v_past, k_new, v_new, bias_past, bias_new, gain)


def _hgrn_chunk(q, f_logit, vv, g_logit, lb, gain, s0, lc):
    n_sub = lc // HG_SUB
    f = lb + (1.0 - lb) * jax.nn.sigmoid(f_logit)
    logf = jnp.log(f)
    k = 1.0 - f
    row = lax.broadcasted_iota(jnp.int32, (lc, lc), 0)
    col = lax.broadcasted_iota(jnp.int32, (lc, lc), 1)
    tril = (col <= row).astype(f32)
    b = jnp.dot(tril, logf, precision=lax.Precision.HIGHEST, preferred_element_type=f32)
    b_last = b[lc - 1:lc, :]

    o = jnp.dot((q * jnp.exp(b)).astype(bf16), s0.astype(bf16), preferred_element_type=f32)
    k_hat = k * jnp.exp(b_last - b)
    bt = b.T
    decay_col = jnp.exp(bt[:, lc - 1:lc])
    s_new = decay_col * s0 + jnp.dot(k_hat.T.astype(bf16), vv.astype(bf16), preferred_element_type=f32)

    if n_sub > 1:
        rows = []
        nt = (((1,), (1,)), ((), ()))
        for i in range(n_sub):
            lo = i * HG_SUB
            if i == 0:
                rows.append(jnp.zeros((HG_SUB, lc), f32))
                continue
            beta = b[lo - 1:lo, :]
            q_t = q[lo:lo + HG_SUB, :] * jnp.exp(b[lo:lo + HG_SUB, :] - beta)
            k_t = k * jnp.exp(jnp.minimum(beta - b, 0.0))
            rows.append(lax.dot_general(q_t.astype(bf16), k_t.astype(bf16), nt, preferred_element_type=f32))
        a_off = jnp.concatenate(rows, axis=0)
        a_off = jnp.where(col < (row // HG_SUB) * HG_SUB, a_off, 0.0)
        o = o + jnp.dot(a_off.astype(bf16), vv.astype(bf16), preferred_element_type=f32)

    ones = jnp.ones((HG_DIM, LANES), bf16)
    srow = lax.broadcasted_iota(jnp.int32, (HG_SUB, HG_DIM), 0)
    orow = lax.broadcasted_iota(jnp.int32, (HG_SUB, HG_DIM), 0)
    blocks = []
    for i in range(n_sub):
        lo = i * HG_SUB
        b_i = b[lo:lo + HG_SUB, :]
        k_i = k[lo:lo + HG_SUB, :]
        v_i = vv[lo:lo + HG_SUB, :]
        q_i = q[lo:lo + HG_SUB, :]
        d_rows = []
        for t in range(HG_SUB):
            arg = jnp.where(srow <= t, b_i[t:t + 1, :] - b_i, -jnp.inf)
            d_rows.append(q_i[t:t + 1, :] * k_i * jnp.exp(arg))
        d3 = jnp.concatenate(d_rows, axis=0)
        a_rep = jnp.dot(d3.astype(bf16), ones, preferred_element_type=f32)
        o_i = jnp.zeros((HG_SUB, HG_DIM), f32)
        for t in range(HG_SUB):
            o_t = jnp.sum(a_rep[t * HG_SUB:(t + 1) * HG_SUB, :] * v_i, axis=0, keepdims=True)
            o_i = jnp.where(orow == t, o_t, o_i)
        blocks.append(o_i)
    o = o + (jnp.concatenate(blocks, axis=0) if n_sub > 1 else blocks[0])

    on = o * lax.rsqrt(jnp.mean(o * o, axis=-1, keepdims=True) + EPS) * gain
    gate = g_logit * jax.nn.sigmoid(g_logit)
    return on * gate, s_new


def _hgrn_kernel(hz_ref, lb_ref, gain_ref, s0_ref, o_ref, s_ref, st_sc, *, lc, heads):
    j = pl.program_id(1)
    w = heads * HG_DIM

    @pl.when(j == 0)
    def _():
        st_sc[...] = s0_ref[0]

    n_chunks = hz_ref.shape[0] // lc
    for c in range(n_chunks):
        rs = slice(c * lc, (c + 1) * lc)
        for h in range(heads):
            cs = slice(h * HG_DIM, (h + 1) * HG_DIM)
            o, s_new = _hgrn_chunk(
                hz_ref[rs, h * HG_DIM:(h + 1) * HG_DIM],
                hz_ref[rs, w + h * HG_DIM:w + (h + 1) * HG_DIM],
                hz_ref[rs, 2 * w + h * HG_DIM:2 * w + (h + 1) * HG_DIM],
                hz_ref[rs, 3 * w + h * HG_DIM:3 * w + (h + 1) * HG_DIM],
                lb_ref[:, cs], gain_ref[...], st_sc[h], lc)
            st_sc[h] = s_new
            o_ref[rs, cs] = o.astype(o_ref.dtype)

    @pl.when(j == pl.num_programs(1) - 1)
    def _():
        s_ref[0] = st_sc[...]


def _hgrn(hz, lb, gain, s0, batch, seq, lc, chunks_per_step):
    t = hz.shape[0]
    w = hz.shape[1] // 4
    heads = w // HG_DIM
    rows = lc * chunks_per_step
    steps = seq // rows
    return pl.pallas_call(
        functools.partial(_hgrn_kernel, lc=lc, heads=heads),
        grid=(batch, steps),
        in_specs=[
            pl.BlockSpec((rows, 4 * w), lambda b, j: (b * steps + j, 0)),
            pl.BlockSpec((1, w), lambda b, j: (0, 0)),
            pl.BlockSpec((1, HG_DIM), lambda b, j: (0, 0)),
            pl.BlockSpec((1, heads, HG_DIM, HG_DIM), lambda b, j: (b, 0, 0, 0)),
        ],
        out_specs=[
            pl.BlockSpec((rows, w), lambda b, j: (b * steps + j, 0)),
            pl.BlockSpec((1, heads, HG_DIM, HG_DIM), lambda b, j: (b, 0, 0, 0)),
        ],
        out_shape=[
            jax.ShapeDtypeStruct((t, w), bf16),
            jax.ShapeDtypeStruct((batch, heads, HG_DIM, HG_DIM), f32),
        ],
        scratch_shapes=[pltpu.VMEM((heads, HG_DIM, HG_DIM), f32)],
        compiler_params=_cparams("parallel", "arbitrary"),
        name="hgrn2",
    )(hz, lb, gain, s0)


def _top_rows(vals, k, payload=None):
    n_rows = vals.shape[0]
    row = lax.broadcasted_iota(jnp.int32, vals.shape, 0)
    out_v, out_i = [], []
    for _ in range(k):
        m = jnp.max(vals, axis=0, keepdims=True)
        first = jnp.min(jnp.where(vals == m, row, n_rows), axis=0, keepdims=True)
        hit = row == first
        out_v.append(m)
        if payload is None:
            out_i.append(first)
        else:
            out_i.append(jnp.max(jnp.where(hit, payload, -1), axis=0, keepdims=True))
        vals = jnp.where(hit, -jnp.inf, vals)
    return jnp.concatenate(out_v, axis=0), jnp.concatenate(out_i, axis=0)


def _retrieve_kernel(oa_ref, ob_ref, x_ref, wo_ref, g_ref, wqt_ref, sk_ref,
                     hp_ref, xn_ref, idx_ref, gate_ref, qt_sc, s1v_sc, s1i_sc, idx_sc, gate_sc):
    half = oa_ref.shape[1]
    y = (jnp.dot(oa_ref[...], wo_ref[0:half, :], preferred_element_type=f32)
         + jnp.dot(ob_ref[...], wo_ref[half:, :], preferred_element_type=f32))
    hp = x_ref[...] + y
    hp_ref[...] = hp
    xn = hp * lax.rsqrt(jnp.mean(hp * hp, axis=-1, keepdims=True) + EPS) * g_ref[...]
    xn_ref[...] = xn
    nt = (((1,), (1,)), ((), ()))
    qt_sc[...] = lax.dot_general(wqt_ref[...], xn.astype(bf16), nt, preferred_element_type=f32).astype(bf16)

    n_half = sk_ref.shape[0]

    def stage1(hc, carry):
        start = pl.multiple_of(hc * PEER_SUB_DIM, PEER_SUB_DIM)
        s = jnp.dot(sk_ref[hc], qt_sc[pl.ds(start, PEER_SUB_DIM), :], preferred_element_type=f32)
        v, i = _top_rows(s, PEER_TOPK)
        s1v_sc[hc] = v
        s1i_sc[hc] = i
        return carry

    lax.fori_loop(0, n_half, stage1, 0)

    tm = x_ref.shape[0]
    sub = lax.broadcasted_iota(jnp.int32, (8, tm), 0)

    def stage2(h, carry):
        v1, i1 = s1v_sc[2 * h], s1i_sc[2 * h]
        v2, i2 = s1v_sc[2 * h + 1], s1i_sc[2 * h + 1]
        cv, ci = [], []
        for i in range(PEER_TOPK):
            nj = PEER_TOPK // (i + 1)
            rows = PEER_TOPK if nj > 8 else 8
            a = v1[i:i + 1, :] + v2[0:rows, :]
            e = i1[i:i + 1, :] * PEER_KEYS + i2[0:rows, :]
            if nj < rows:
                a = jnp.where(sub < nj, a, -jnp.inf)
            cv.append(a)
            ci.append(e)
        top_s, eidx = _top_rows(jnp.concatenate(cv, axis=0), PEER_TOPK, jnp.concatenate(ci, axis=0))
        p = jnp.exp(top_s - top_s[0:1, :])
        g = p / jnp.sum(p, axis=0, keepdims=True)
        r0 = pl.multiple_of(h * PEER_TOPK, PEER_TOPK)
        gate_sc[pl.ds(r0, PEER_TOPK), :] = g
        idx_sc[pl.ds(r0, PEER_TOPK), :] = eidx
        return carry

    lax.fori_loop(0, n_half // 2, stage2, 0)
    idx_ref[...] = pltpu.bitcast(pltpu.bitcast(idx_sc[...], f32).T, jnp.int32)
    gate_ref[...] = gate_sc[...].T


def _retrieve(oa, ob, x, wo_bf, g2, wqt_bf, sk_bf, tm, x_block0=0, ob_block0=0):
    t, half = oa.shape
    d = x.shape[1]
    e = wqt_bf.shape[0]
    n_half = sk_bf.shape[0]
    n_sel = (n_half // 2) * PEER_TOPK
    return pl.pallas_call(
        _retrieve_kernel,
        grid=(t // tm,),
        in_specs=[
            pl.BlockSpec((tm, half), lambda i: (i, 0)),
            pl.BlockSpec((tm, half), lambda i: (ob_block0 + i, 0)),
            pl.BlockSpec((tm, d), lambda i: (x_block0 + i, 0)),
            pl.BlockSpec((2 * half, d), lambda i: (0, 0)),
            pl.BlockSpec((1, d), lambda i: (0, 0)),
            pl.BlockSpec((e, d), lambda i: (0, 0)),
            pl.BlockSpec((n_half, PEER_KEYS, PEER_SUB_DIM), lambda i: (0, 0, 0)),
        ],
        out_specs=[
            pl.BlockSpec((tm, d), lambda i: (i, 0)),
            pl.BlockSpec((tm, d), lambda i: (i, 0)),
            pl.BlockSpec((tm, n_sel), lambda i: (i, 0)),
            pl.BlockSpec((tm, n_sel), lambda i: (i, 0)),
        ],
        out_shape=[
            jax.ShapeDtypeStruct((t, d), f32),
            jax.ShapeDtypeStruct((t, d), f32),
            jax.ShapeDtypeStruct((t, n_sel), jnp.int32),
            jax.ShapeDtypeStruct((t, n_sel), f32),
        ],
        scratch_shapes=[
            pltpu.VMEM((e, tm), bf16),
            pltpu.VMEM((n_half, PEER_TOPK, tm), f32),
            pltpu.VMEM((n_half, PEER_TOPK, tm), jnp.int32),
            pltpu.VMEM((n_sel, tm), jnp.int32),
            pltpu.VMEM((n_sel, tm), f32),
        ],
        compiler_params=_cparams("parallel"),
        name="retrieve",
    )(oa, ob, x, wo_bf, g2, wqt_bf, sk_bf)


def _pack_table(tab):
    half = tab.shape[1] // 2
    lo = lax.bitcast_convert_type(tab[:, :half].astype(bf16), jnp.uint16).astype(jnp.uint32)
    hi = lax.bitcast_convert_type(tab[:, half:].astype(bf16), jnp.uint16).astype(jnp.uint32)
    return lax.bitcast_convert_type(lo | (hi << 16), jnp.int32)


def _sc_layout(n_tok, n_sel, words):
    info = plsc.get_sparse_core_info()
    n_workers = info.num_cores * info.num_subcores
    lanes = info.num_lanes
    block = 8 * lanes
    assert n_tok % (2 * n_workers) == 0 and words % block == 0 and (n_sel // 2) % lanes == 0
    return info, lanes, n_tok // n_workers, n_sel // 2, block


def _selected_dots(table, idx, x):
    n_tok, n_sel = idx.shape
    words = table.shape[1]
    info, lanes, tpw, half_rows, block = _sc_layout(n_tok, n_sel, words)
    idx2 = idx.reshape(2 * n_tok, half_rows)
    mesh = plsc.VectorSubcoreMesh(core_axis_name="c", subcore_axis_name="s")

    @functools.partial(
        pl.kernel, out_type=jax.ShapeDtypeStruct((n_tok, n_sel), f32), mesh=mesh,
        compiler_params=pltpu.CompilerParams(needs_layout_passes=False),
        scratch_types=[
            pltpu.VMEM((2 * tpw, half_rows), jnp.int32),
            pltpu.VMEM((half_rows, words), jnp.int32),
            pltpu.VMEM((half_rows, words), jnp.int32),
            pltpu.VMEM((2 * words,), f32),
            pltpu.VMEM((2 * words,), f32),
            pltpu.VMEM((n_sel // 8, 8 * lanes), f32),
            pltpu.VMEM((tpw, n_sel), f32),
            pltpu.SemaphoreType.DMA, pltpu.SemaphoreType.DMA, pltpu.SemaphoreType.DMA, pltpu.SemaphoreType.DMA,
        ])
    def dots(t_hbm, i_hbm, x_hbm, o_hbm, idx_v, rows_a, rows_b, x_a, x_b, part_v, act_v, sem_a, sem_b, sem_xa, sem_xb):
        wid = lax.axis_index("s") * info.num_cores + lax.axis_index("c")
        base = wid * tpw
        pltpu.sync_copy(i_hbm.at[pl.ds(base * 2, 2 * tpw)], idx_v)

        def accumulate(rows, x_v, half):
            def column_block(cb, first):
                xs = [x_v[pl.ds(p * words + cb * block + c * lanes, lanes)] for c in range(8) for p in range(2)]

                @plsc.parallel_loop(0, half_rows)
                def _(row):
                    packed = [rows[row, pl.ds(cb * block + c * lanes, lanes)] for c in range(8)]
                    unp = [plsc.unpack(plsc.bitcast(p, bf16), format=plsc.PackFormat.INTERLEAVED) for p in packed]
                    prods = [unp[c][p] * xs[2 * c + p] for c in range(8) for p in range(2)]
                    while len(prods) > 1:
                        prods = [prods[k] + prods[k + 1] for k in range(0, len(prods), 2)]
                    prow = half * half_rows + row
                    slot = part_v.at[prow // 8, pl.ds((prow % 8) * lanes, lanes)]
                    if first:
                        slot[...] = prods[0]
                    else:
                        plsc.addupdate(slot, prods[0])

            column_block(0, True)
            pl.loop(1, words // block)(functools.partial(column_block, first=False))

        def gather(half_index, rows, sem):
            return pltpu.make_async_copy(t_hbm.at[idx_v.at[half_index]], rows, sem)

        def xcopy(t, x_v, sem):
            return pltpu.make_async_copy(x_hbm.at[base + t], x_v, sem)

        def token(t, x_v, sem_x, x_next, sem_xn):
            gather(2 * t + 1, rows_b, sem_b).start()

            @pl.when(t + 1 < tpw)
            def _():
                xcopy(t + 1, x_next, sem_xn).start()

            xcopy(t, x_v, sem_x).wait()
            gather(2 * t, rows_a, sem_a).wait()
            accumulate(rows_a, x_v, 0)

            @pl.when(t + 1 < tpw)
            def _():
                gather(2 * t + 2, rows_a, sem_a).start()

            gather(2 * t + 1, rows_b, sem_b).wait()
            accumulate(rows_b, x_v, 1)
            it = lax.iota(jnp.int32, lanes)
            for g in range(n_sel // lanes):
                prow = g * (lanes // 8) + it // 8
                pcol = (it % 8) * lanes
                cols = [plsc.load_gather(part_v, [prow, pcol + l]) for l in range(lanes)]
                while len(cols) > 1:
                    cols = [cols[k] + cols[k + 1] for k in range(0, len(cols), 2)]
                act_v[t, pl.ds(g * lanes, lanes)] = cols[0]

        gather(0, rows_a, sem_a).start()
        xcopy(0, x_a, sem_xa).start()

        @pl.loop(0, tpw // 2)
        def _(i):
            token(2 * i, x_a, sem_xa, x_b, sem_xb)
            token(2 * i + 1, x_b, sem_xb, x_a, sem_xa)

        pltpu.sync_copy(act_v, o_hbm.at[pl.ds(base, tpw)])

    return dots(table, idx2, x)


def _gate_weights_kernel(act_ref, gate_ref, after_ref, w_ref):
    del after_ref
    a = act_ref[...]
    w_ref[...] = gate_ref[...] * (0.5 * a * (1.0 + lax.erf(a * (2.0 ** -0.5))))


def _gate_weights(act, gate, after, tm):
    t, n_sel = act.shape
    spec = pl.BlockSpec((tm, n_sel), lambda i: (i, 0))
    return pl.pallas_call(
        _gate_weights_kernel,
        grid=(t // tm,),
        in_specs=[spec, spec, pl.BlockSpec(memory_space=pl.ANY)],
        out_specs=spec,
        out_shape=jax.ShapeDtypeStruct((t, n_sel), f32),
        compiler_params=_cparams("parallel"),
        name="peer_weights",
    )(act, gate, after)


def _weighted_rows(table, idx, w):
    n_tok, n_sel = w.shape
    words = table.shape[1]
    info, lanes, tpw, half_rows, _ = _sc_layout(n_tok, n_sel, words)
    idx2 = idx.reshape(2 * n_tok, half_rows)
    mesh = plsc.VectorSubcoreMesh(core_axis_name="c", subcore_axis_name="s")

    @functools.partial(
        pl.kernel, out_type=jax.ShapeDtypeStruct((n_tok, 2 * words), f32), mesh=mesh,
        compiler_params=pltpu.CompilerParams(needs_layout_passes=False),
        scratch_types=[
            pltpu.VMEM((2 * tpw, half_rows), jnp.int32),
            pltpu.VMEM((tpw, n_sel), f32),
            pltpu.VMEM((half_rows, words), jnp.int32),
            pltpu.VMEM((half_rows, words), jnp.int32),
            pltpu.VMEM((2 * words,), f32),
            pltpu.VMEM((2 * words,), f32),
            pltpu.SemaphoreType.DMA, pltpu.SemaphoreType.DMA, pltpu.SemaphoreType.DMA, pltpu.SemaphoreType.DMA,
        ])
    def mix(t_hbm, i_hbm, w_hbm, o_hbm, idx_v, w_v, rows_a, rows_b, out_a, out_b, sem_a, sem_b, sem_oa, sem_ob):
        wid = lax.axis_index("s") * info.num_cores + lax.axis_index("c")
        base = wid * tpw
        pltpu.sync_copy(i_hbm.at[pl.ds(base * 2, 2 * tpw)], idx_v)
        pltpu.sync_copy(w_hbm.at[pl.ds(base, tpw)], w_v)

        def accumulate(rows, out_v, t, half, first):
            if first:
                zero = jnp.zeros((lanes,), f32)
                for c in range(2 * words // lanes):
                    out_v[pl.ds(c * lanes, lanes)] = zero

            @pl.loop(0, half_rows // lanes)
            def _(g):
                wv = w_v[t, pl.ds(half * half_rows + g * lanes, lanes)]
                ws = [wv.at[jnp.full((lanes,), r, jnp.int32)].get(mode="promise_in_bounds") for r in range(lanes)]

                @plsc.parallel_loop(0, words // lanes)
                def _(c):
                    packed = [rows[g * lanes + r, pl.ds(c * lanes, lanes)] for r in range(lanes)]
                    unp = [plsc.unpack(plsc.bitcast(p, bf16), format=plsc.PackFormat.INTERLEAVED) for p in packed]
                    lo = [unp[r][0] * ws[r] for r in range(lanes)]
                    hi = [unp[r][1] * ws[r] for r in range(lanes)]
                    while len(lo) > 1:
                        lo = [lo[k] + lo[k + 1] for k in range(0, len(lo), 2)]
                        hi = [hi[k] + hi[k + 1] for k in range(0, len(hi), 2)]
                    plsc.addupdate(out_v.at[pl.ds(c * lanes, lanes)], lo[0])
                    plsc.addupdate(out_v.at[pl.ds(words + c * lanes, lanes)], hi[0])

        def gather(half_index, rows, sem):
            return pltpu.make_async_copy(t_hbm.at[idx_v.at[half_index]], rows, sem)

        def token(t, out_v, sem_o, out_in_flight):
            gather(2 * t + 1, rows_b, sem_b).start()
            gather(2 * t, rows_a, sem_a).wait()

            @pl.when(out_in_flight)
            def _():
                pltpu.make_async_copy(out_v, o_hbm.at[base + t], sem_o).wait()

            accumulate(rows_a, out_v, t, 0, True)

            @pl.when(t + 1 < tpw)
            def _():
                gather(2 * t + 2, rows_a, sem_a).start()

            gather(2 * t + 1, rows_b, sem_b).wait()
            accumulate(rows_b, out_v, t, 1, False)
            pltpu.make_async_copy(out_v, o_hbm.at[base + t], sem_o).start()

        gather(0, rows_a, sem_a).start()

        @pl.loop(0, tpw // 2)
        def _(i):
            token(2 * i, out_a, sem_oa, i > 0)
            token(2 * i + 1, out_b, sem_ob, i > 0)

        pltpu.make_async_copy(out_a, o_hbm.at[base], sem_oa).wait()
        pltpu.make_async_copy(out_b, o_hbm.at[base], sem_ob).wait()

    return mix(table, idx2, w)


def _final_kernel(hp_ref, peer_ref, gf_ref, yin_ref, y_ref, done_ref):
    del yin_ref
    done_ref[...] = jnp.zeros(done_ref.shape, f32)
    h = hp_ref[...] + peer_ref[...]
    y_ref[...] = h * lax.rsqrt(jnp.mean(h * h, axis=-1, keepdims=True) + EPS) * gf_ref[...]


def _final(hp, peer, gf, y_all, y_block0, tm):
    t, d = hp.shape
    return pl.pallas_call(
        _final_kernel,
        grid=(t // tm,),
        in_specs=[
            pl.BlockSpec((tm, d), lambda i: (i, 0)),
            pl.BlockSpec((tm, d), lambda i: (i, 0)),
            pl.BlockSpec((1, d), lambda i: (0, 0)),
            pl.BlockSpec(memory_space=pl.ANY),
        ],
        out_specs=[pl.BlockSpec((tm, d), lambda i: (y_block0 + i, 0)),
                   pl.BlockSpec((8, LANES), lambda i: (0, 0))],
        out_shape=[jax.ShapeDtypeStruct(y_all.shape, f32), jax.ShapeDtypeStruct((8, LANES), f32)],
        input_output_aliases={3: 0},
        compiler_params=_cparams("arbitrary"),
        name="final_norm",
    )(hp, peer, gf, y_all)


def _channel_mixer_front(oa, ob, x, x_block0, ob_block0, wo_bf, g2, wqt_bf, sk_bf, u_pk, tm):
    hp, xn, eidx, gate = _retrieve(oa, ob, x, wo_bf, g2, wqt_bf, sk_bf, tm, x_block0, ob_block0)
    return hp, eidx, gate, _selected_dots(u_pk, eidx, xn)


def _channel_mixer_mix(front, after, v_pk, tm):
    _, eidx, gate, act = front
    return _weighted_rows(v_pk, eidx, _gate_weights(act, gate, after, tm))


def kernel(x_prompt, x_sample, cache_k, cache_v, state_hgrn, norm1, w_in, da_lq1, da_lk1, da_lq2, da_lk2,
           da_out_norm, hg_lb_logits, hg_out_norm, w_out, rel_bias, norm2, peer_w_q, peer_sub_keys,
           peer_u, peer_v, final_norm):
    batch, seq, d = x_prompt.shape
    dbatch, dseq, _ = x_sample.shape
    past = cache_k.shape[2]
    da_heads = cache_k.shape[3]
    hg_heads = state_hgrn.shape[2]
    depth = w_in.shape[0]
    assert depth == 1 and seq % ATT_BLOCK == 0 and seq % CHUNK == 0 and dseq % HG_SUB == 0 and dseq <= CHUNK
    assert past % CHUNK == 0 and (past + dseq - 1) // CHUNK == past // CHUNK

    l = 0
    lam_init = 0.8 - 0.6 * math.exp(-0.3 * l)
    lam = (jnp.exp(jnp.sum(da_lq1[l].astype(f32) * da_lk1[l].astype(f32)))
           - jnp.exp(jnp.sum(da_lq2[l].astype(f32) * da_lk2[l].astype(f32))) + lam_init).reshape(1)
    lb = jnp.cumsum(jax.nn.softmax(hg_lb_logits.astype(f32), axis=0), axis=0)[l].reshape(1, -1)
    da_gain = (jnp.tile(da_out_norm[l].astype(f32), 2) * (1.0 - lam_init)).reshape(1, LANES)
    hg_gain = hg_out_norm[l].astype(f32).reshape(1, HG_DIM)
    g1 = norm1[l].astype(f32).reshape(1, d)
    g2 = norm2[l].astype(f32).reshape(1, d)
    gf = final_norm.astype(f32).reshape(1, d)
    w_in_bf = w_in[l].astype(bf16)
    wkv_t_bf = w_in[l][:, cache_k.shape[3] * cache_k.shape[4]:3 * cache_k.shape[3] * cache_k.shape[4]].T.astype(bf16)
    wo_bf = w_out[l].astype(bf16)
    wqt_bf = peer_w_q[l].T.astype(bf16)
    sk_bf = peer_sub_keys[l].reshape(-1, PEER_KEYS, PEER_SUB_DIM).astype(bf16)
    u_pk = _pack_table(peer_u[l])
    v_pk = _pack_table(peer_v[l])

    bias_tiles = _prompt_bias_tiles(rel_bias)
    q_pos = past + jnp.arange(dseq)
    bias_s = _bias_of(jnp.arange(past + dseq)[None, :] - q_pos[:, None], rel_bias) * LOG2E
    bias_past, bias_new = bias_s[:, :, :past], bias_s[:, :, past:]
    zero_state = jnp.zeros((1, hg_heads, HG_DIM, HG_DIM), f32)

    dh2 = 2 * DA_HEAD_DIM
    wq = da_heads * dh2
    tm = 512
    x_all = x_prompt.reshape(batch * seq, d)
    kt_all = jnp.zeros((batch * wq, seq), f32)
    vt_all = jnp.zeros((batch * wq, seq), f32)
    y_all = jnp.zeros((batch * seq, d), f32)
    nq = seq // ATT_BLOCK
    blocks = seq // tm
    cuts = {0: 4, 1: 2}
    segments = [(b, c * (nq // cuts.get(b, 1)), nq // cuts.get(b, 1))
                for b in range(batch) for c in range(cuts.get(b, 1))]
    assert all(nq % n == 0 and (nq // n * ATT_BLOCK) % (2 * tm) == 0 for n in cuts.values())
    s_p = []
    done = [jnp.zeros((8, LANES), f32)] * 3
    fronts, peers = [], []

    def finish(k, y_all):
        return _final(fronts[k][0][0], peers[k], gf, y_all, fronts[k][1], tm)

    for s, (b, q0, n_q) in enumerate(segments):
        if q0 == 0:
            q, kt, vt, hz, kt_all, vt_all = _inproj_prompt(x_all, g1, w_in_bf, wkv_t_bf, kt_all, vt_all, done[s],
                                                           b, seq, tm)
            ob, s_new = _hgrn(hz, lb, hg_gain, zero_state, 1, seq, CHUNK, 4)
            s_p.append(s_new)
        oa = _attn_prompt(q, kt, vt, bias_tiles, da_gain, lam, q0, n_q)
        if s >= 1:
            peers.append(_channel_mixer_mix(fronts[s - 1][0], oa, v_pk, tm))
        row0 = q0 * ATT_BLOCK // tm
        fronts.append((_channel_mixer_front(oa, ob, x_all, b * blocks + row0, row0, wo_bf, g2, wqt_bf, sk_bf, u_pk,
                                            tm), b * blocks + row0))
        if s >= 2:
            y_all, token = finish(s - 2, y_all)
            done.append(token)
    peers.append(_channel_mixer_mix(fronts[-1][0], fronts[-1][0][0], v_pk, tm))
    for k in (len(segments) - 2, len(segments) - 1):
        y_all, _ = finish(k, y_all)

    xs = x_sample.reshape(dbatch * dseq, d)
    q, k_s, v_s, hz = _inproj_sample(xs, g1, w_in_bf, wq)
    kp_t = jnp.transpose(cache_k[l], (0, 2, 3, 1)).reshape(dbatch * wq, past)
    vp_t = jnp.transpose(cache_v[l], (0, 2, 3, 1)).reshape(dbatch * wq, past)
    oa = _attn_sample(q, k_s, v_s, kp_t, vp_t, bias_past, bias_new, da_gain, lam, dbatch, dseq, past)
    ob, s_s = _hgrn(hz, lb, hg_gain, state_hgrn[l].astype(f32), dbatch, dseq, dseq, 1)
    front = _channel_mixer_front(oa, ob, xs, 0, 0, wo_bf, g2, wqt_bf, sk_bf, u_pk, dbatch * dseq)
    peer = _channel_mixer_mix(front, front[0], v_pk, dbatch * dseq)
    y_s, _ = _final(front[0], peer, gf, jnp.zeros((dbatch * dseq, d), f32), 0, dbatch * dseq)

    y_prompt = y_all.reshape(batch, seq, d)
    y_sample = y_s.reshape(dbatch, dseq, d)
    k_prompt = jnp.transpose(kt_all.reshape(batch, da_heads, dh2, seq), (0, 3, 1, 2))[None]
    v_prompt = jnp.transpose(vt_all.reshape(batch, da_heads, DA_V_DIM, seq), (0, 3, 1, 2))[None]
    state_prompt = jnp.concatenate(s_p, axis=0)[None].astype(state_hgrn.dtype)
    k_sample = k_s.reshape(1, dbatch, dseq, da_heads, dh2)
    v_sample = v_s.reshape(1, dbatch, dseq, da_heads, DA_V_DIM)
    state_sample = s_s[None].astype(state_hgrn.dtype)
    return (y_prompt, y_sample, k_prompt, v_prompt, state_prompt, k_sample, v_sample, state_sample)
```

```python
import functools
import math

import jax
import jax.numpy as jnp
from jax import lax
from jax.experimental import pallas as pl
from jax.experimental.pallas import tpu as pltpu
from jax.experimental.pallas import tpu_sc as plsc

CHUNK = 64
DA_HEAD_DIM = 32
DA_V_DIM = 2 * DA_HEAD_DIM
HG_DIM = 128
REL_BUCKETS = 32
REL_MAX_DIST = 128
PEER_KEYS = 128
PEER_TOPK = 16
PEER_SUB_DIM = 64
EPS = 1e-6
NEG = -1e30
LOG2E = 1.4426950408889634

LANES = 128
ATT_BLOCK = 128
ATT_STEP = 512
HG_SUB = 16
V7X_VMEM_BYTES = 64 * 1024 * 1024
VMEM_LIMIT = V7X_VMEM_BYTES * 7 // 8

f32 = jnp.float32
bf16 = jnp.bfloat16


def _cparams(*sem):
    return pltpu.CompilerParams(dimension_semantics=sem, vmem_limit_bytes=VMEM_LIMIT)


def _inproj_prompt_kernel(x_ref, g_ref, w_ref, wkv_t_ref, kin_ref, vin_ref, after_ref,
                          q_ref, kt_ref, vt_ref, hz_ref, ktf_ref, vtf_ref):
    del kin_ref, vin_ref
    del after_ref
    x = x_ref[...]
    xn = x * lax.rsqrt(jnp.mean(x * x, axis=-1, keepdims=True) + EPS) * g_ref[...]
    xb = xn.astype(bf16)
    wq = q_ref.shape[1]
    nt = (((1,), (1,)), ((), ()))
    q_ref[...] = jnp.dot(xb, w_ref[:, 0:wq], preferred_element_type=f32).astype(bf16)
    kt = lax.dot_general(wkv_t_ref[0:wq, :], xb, nt, preferred_element_type=f32)
    ktf_ref[...] = kt
    kt_ref[...] = kt.astype(bf16)
    vt = lax.dot_general(wkv_t_ref[wq:2 * wq, :], xb, nt, preferred_element_type=f32)
    vtf_ref[...] = vt
    vt_ref[...] = vt.astype(bf16)
    hz_ref[...] = jnp.dot(xb, w_ref[:, 3 * wq:], preferred_element_type=f32)


def _inproj_prompt(x, g, w_bf, wkv_t_bf, kt_all, vt_all, after, b, seq, tm):
    d = x.shape[1]
    e = w_bf.shape[1]
    wq = wkv_t_bf.shape[0] // 2
    steps = seq // tm
    return pl.pallas_call(
        _inproj_prompt_kernel,
        grid=(steps,),
        in_specs=[
            pl.BlockSpec((tm, d), lambda i: (b * steps + i, 0)),
            pl.BlockSpec((1, d), lambda i: (0, 0)),
            pl.BlockSpec((d, e), lambda i: (0, 0)),
            pl.BlockSpec((2 * wq, d), lambda i: (0, 0)),
            pl.BlockSpec(memory_space=pl.ANY),
            pl.BlockSpec(memory_space=pl.ANY),
            pl.BlockSpec(memory_space=pl.ANY),
        ],
        out_specs=[
            pl.BlockSpec((tm, wq), lambda i: (i, 0)),
            pl.BlockSpec((wq, tm), lambda i: (0, i)),
            pl.BlockSpec((wq, tm), lambda i: (0, i)),
            pl.BlockSpec((tm, e - 3 * wq), lambda i: (i, 0)),
            pl.BlockSpec((wq, tm), lambda i: (b, i)),
            pl.BlockSpec((wq, tm), lambda i: (b, i)),
        ],
        out_shape=[
            jax.ShapeDtypeStruct((seq, wq), bf16),
            jax.ShapeDtypeStruct((wq, seq), bf16),
            jax.ShapeDtypeStruct((wq, seq), bf16),
            jax.ShapeDtypeStruct((seq, e - 3 * wq), f32),
            jax.ShapeDtypeStruct(kt_all.shape, f32),
            jax.ShapeDtypeStruct(vt_all.shape, f32),
        ],
        input_output_aliases={4: 4, 5: 5},
        compiler_params=_cparams("parallel"),
        name="inproj",
    )(x, g, w_bf, wkv_t_bf, kt_all, vt_all, after)


def _inproj_sample_kernel(x_ref, g_ref, w_ref, q_ref, k_ref, v_ref, hz_ref):
    x = x_ref[...]
    xn = x * lax.rsqrt(jnp.mean(x * x, axis=-1, keepdims=True) + EPS) * g_ref[...]
    xb = xn.astype(bf16)
    wq = q_ref.shape[1]
    q_ref[...] = jnp.dot(xb, w_ref[:, 0:wq], preferred_element_type=f32).astype(bf16)
    k_ref[...] = jnp.dot(xb, w_ref[:, wq:2 * wq], preferred_element_type=f32)
    v_ref[...] = jnp.dot(xb, w_ref[:, 2 * wq:3 * wq], preferred_element_type=f32)
    hz_ref[...] = jnp.dot(xb, w_ref[:, 3 * wq:], preferred_element_type=f32)


def _inproj_sample(x, g, w_bf, wq):
    t, d = x.shape
    e = w_bf.shape[1]
    return pl.pallas_call(
        _inproj_sample_kernel,
        grid=(1,),
        in_specs=[
            pl.BlockSpec((t, d), lambda i: (0, 0)),
            pl.BlockSpec((1, d), lambda i: (0, 0)),
            pl.BlockSpec((d, e), lambda i: (0, 0)),
        ],
        out_specs=[
            pl.BlockSpec((t, wq), lambda i: (0, 0)),
            pl.BlockSpec((t, wq), lambda i: (0, 0)),
            pl.BlockSpec((t, wq), lambda i: (0, 0)),
            pl.BlockSpec((t, e - 3 * wq), lambda i: (0, 0)),
        ],
        out_shape=[
            jax.ShapeDtypeStruct((t, wq), bf16),
            jax.ShapeDtypeStruct((t, wq), f32),
            jax.ShapeDtypeStruct((t, wq), f32),
            jax.ShapeDtypeStruct((t, e - 3 * wq), f32),
        ],
        compiler_params=_cparams("arbitrary"),
        name="inproj_sample",
    )(x, g, w_bf)


def _rel_bucket(rel):
    nb = REL_BUCKETS // 2
    max_exact = nb // 2
    ret = jnp.where(rel > 0, nb, 0)
    n = jnp.abs(rel)
    nf = jnp.maximum(n, 1).astype(f32)
    large = max_exact + (jnp.log(nf / max_exact) / math.log(REL_MAX_DIST / max_exact)
                         * (nb - max_exact)).astype(jnp.int32)
    large = jnp.minimum(large, nb - 1)
    return ret + jnp.where(n < max_exact, n, large)


def _bias_of(rel, rel_bias):
    bucket = _rel_bucket(rel)
    out = jnp.zeros((rel_bias.shape[1],) + rel.shape, f32)
    for b in range(REL_BUCKETS):
        out = out + jnp.where(bucket[None] == b, rel_bias[b].astype(f32)[:, None, None], 0.0)
    return out


def _attn_prompt_kernel(lam_ref, q_ref, kt_ref, vt_ref, bias_ref, gain_ref, o_ref, m_sc, acc_sc, sa_sc, sb_sc,
                        *, q_block0):
    qb = pl.program_id(1) + q_block0
    tq = q_ref.shape[0]
    seq = kt_ref.shape[1]
    per_step = ATT_STEP // ATT_BLOCK
    lane = lax.broadcasted_iota(jnp.int32, (tq, LANES), 1)
    feat_k = lax.broadcasted_iota(jnp.int32, (LANES, ATT_STEP), 0)
    qs = q_ref[...].astype(f32) * (DA_HEAD_DIM ** -0.5 * LOG2E)
    q4 = jnp.concatenate(
        [jnp.where((lane // DA_HEAD_DIM) == i, qs, 0.0).astype(bf16) for i in range(4)], axis=0)

    m_sc[...] = jnp.full(m_sc.shape, -jnp.inf, f32)
    acc_sc[...] = jnp.zeros(acc_sc.shape, f32)

    n_steps = qb // per_step + 1
    last = n_steps - 1
    r = qb % per_step

    def scores(k):
        start = pl.multiple_of(jnp.minimum(k * ATT_STEP, seq - ATT_STEP), ATT_STEP)
        return jnp.dot(q4, kt_ref[:, pl.ds(start, ATT_STEP)], preferred_element_type=f32)

    def half(k, cur_ref, nxt_ref):
        nxt_ref[...] = scores(k + 1)
        tile = jnp.where(k == last, 1 + r, jnp.where((k == last - 1) & (r == 0), per_step + 1, 0))
        vv = vt_ref[:, pl.ds(pl.multiple_of(k * ATT_STEP, ATT_STEP), ATT_STEP)]
        for hh in range(2):
            bias = bias_ref[0, tile, hh * tq:(hh + 1) * tq, :]
            ps, alphas = [], []
            for c in range(2):
                rows = slice((2 * hh + c) * tq, (2 * hh + c + 1) * tq)
                s = cur_ref[rows, :] + bias
                m_old = m_sc[rows]
                m_new = jnp.maximum(m_old, jnp.max(s, axis=-1, keepdims=True))
                ps.append(jnp.exp2(s - m_new[:, 0:1]).astype(bf16))
                alphas.append(jnp.exp2(m_old - m_new))
                m_sc[rows] = m_new
            rows2 = slice(2 * hh * tq, (2 * hh + 2) * tq)
            vaug = jnp.where((feat_k // DA_V_DIM) == hh, vv, jnp.ones_like(vv))
            acc_sc[rows2] = (acc_sc[rows2] * jnp.concatenate(alphas, axis=0)
                             + lax.dot_general(jnp.concatenate(ps, axis=0), vaug, (((1,), (1,)), ((), ())),
                                               preferred_element_type=f32))

    sa_sc[...] = scores(0)

    def two_steps(kk, carry):
        half(2 * kk, sa_sc, sb_sc)

        @pl.when(2 * kk + 1 < n_steps)
        def _():
            half(2 * kk + 1, sb_sc, sa_sc)

        return carry

    lax.fori_loop(0, (n_steps + 1) // 2, two_steps, 0)

    lam = lam_ref[0]
    outs = []
    for hh in range(2):
        own = (lane // DA_V_DIM) == hh
        a1 = acc_sc[2 * hh * tq:(2 * hh + 1) * tq]
        a2 = acc_sc[(2 * hh + 1) * tq:(2 * hh + 2) * tq]
        l1 = jnp.max(jnp.where(own, 0.0, a1), axis=-1, keepdims=True)
        l2 = jnp.max(jnp.where(own, 0.0, a2), axis=-1, keepdims=True)
        o = a1 / l1 - lam * (a2 / l2)
        ssq = jnp.sum(jnp.where(own, o * o, 0.0), axis=-1, keepdims=True)
        outs.append(o * lax.rsqrt(ssq * (1.0 / DA_V_DIM) + EPS))
    o = jnp.where((lane // DA_V_DIM) == 0, outs[0], outs[1])
    o_ref[...] = (o * gain_ref[...]).astype(o_ref.dtype)


def _attn_prompt(q, kt, vt, bias_tiles, gain, lam, q_block0, n_q):
    seq, w = q.shape
    pairs = w // LANES
    assert seq % ATT_STEP == 0
    n_tiles = ATT_STEP // ATT_BLOCK + 2
    return pl.pallas_call(
        functools.partial(_attn_prompt_kernel, q_block0=q_block0),
        grid=(pairs, n_q),
        in_specs=[
            pl.BlockSpec(memory_space=pltpu.SMEM),
            pl.BlockSpec((ATT_BLOCK, LANES), lambda p, i: (q_block0 + i, p)),
            pl.BlockSpec((LANES, seq), lambda p, i: (p, 0)),
            pl.BlockSpec((LANES, seq), lambda p, i: (p, 0)),
            pl.BlockSpec((1, n_tiles, 2 * ATT_BLOCK, ATT_STEP), lambda p, i: (p, 0, 0, 0)),
            pl.BlockSpec((1, LANES), lambda p, i: (0, 0)),
        ],
        out_specs=pl.BlockSpec((ATT_BLOCK, LANES), lambda p, i: (i, p)),
        out_shape=jax.ShapeDtypeStruct((n_q * ATT_BLOCK, w), bf16),
        scratch_shapes=[
            pltpu.VMEM((4 * ATT_BLOCK, LANES), f32),
            pltpu.VMEM((4 * ATT_BLOCK, LANES), f32),
            pltpu.VMEM((4 * ATT_BLOCK, ATT_STEP), f32),
            pltpu.VMEM((4 * ATT_BLOCK, ATT_STEP), f32),
        ],
        compiler_params=_cparams("parallel", "arbitrary"),
        name="attn_prompt",
    )(lam, q, kt, vt, bias_tiles, gain)


def _prompt_bias_tiles(rel_bias):
    h = rel_bias.shape[1]
    per_step = ATT_STEP // ATT_BLOCK
    i = jnp.arange(ATT_BLOCK)
    rel_diag = i[None, :] - i[:, None]
    rel_prev = rel_diag - ATT_BLOCK
    far = rel_bias[REL_BUCKETS // 2 - 1].astype(f32)
    b_diag = (_bias_of(rel_diag, rel_bias) - far[:, None, None]) * LOG2E
    b_prev = (_bias_of(rel_prev, rel_bias) - far[:, None, None]) * LOG2E
    mask = (i[None, :] // CHUNK) <= (i[:, None] // CHUNK)
    b_diag = jnp.where(mask[None], b_diag, NEG)
    zero = jnp.zeros_like(b_diag)
    dead = jnp.full_like(b_diag, NEG)
    tiles = [jnp.concatenate([zero] * per_step, axis=-1)]
    for r in range(per_step):
        blocks = [zero if j < r - 1 else b_prev if j == r - 1 else b_diag if j == r else dead
                  for j in range(per_step)]
        tiles.append(jnp.concatenate(blocks, axis=-1))
    tiles.append(jnp.concatenate([zero] * (per_step - 1) + [b_prev], axis=-1))
    tiles = jnp.stack(tiles, axis=1)
    tiles = tiles.reshape(h // 2, 2, per_step + 2, ATT_BLOCK, ATT_STEP)
    return jnp.transpose(tiles, (0, 2, 1, 3, 4)).reshape(h // 2, per_step + 2, 2 * ATT_BLOCK, ATT_STEP)


def _attn_sample_kernel(lam_ref, q_ref, kp_ref, vp_ref, kn_ref, vn_ref, bp_ref, bn_ref, gain_ref, o_ref):
    nq = q_ref.shape[0]
    pairs = q_ref.shape[1] // LANES
    lane = lax.broadcasted_iota(jnp.int32, (nq, LANES), 1)
    lam = lam_ref[0]
    nt = (((1,), (1,)), ((), ()))
    for p in range(pairs):
        sl = slice(p * LANES, (p + 1) * LANES)
        qs = q_ref[:, sl].astype(f32) * (DA_HEAD_DIM ** -0.5 * LOG2E)
        kp = kp_ref[sl, :].astype(bf16)
        vp = vp_ref[sl, :].astype(bf16)
        kn = kn_ref[:, sl].astype(bf16)
        vn = vn_ref[:, sl].astype(bf16)
        outs = []
        for hh in range(2):
            h = 2 * p + hh
            own = (lane // DA_V_DIM) == hh
            res = []
            for c in range(2):
                qm = jnp.where((lane // DA_HEAD_DIM) == 2 * hh + c, qs, 0.0).astype(bf16)
                sp = jnp.dot(qm, kp, preferred_element_type=f32) + bp_ref[h]
                sn = lax.dot_general(qm, kn, nt, preferred_element_type=f32) + bn_ref[h]
                m = jnp.maximum(jnp.max(sp, axis=-1, keepdims=True), jnp.max(sn, axis=-1, keepdims=True))
                pp = jnp.exp2(sp - m)
                pn = jnp.exp2(sn - m)
                l = jnp.sum(pp, axis=-1, keepdims=True) + jnp.sum(pn, axis=-1, keepdims=True)
                pv = (lax.dot_general(pp.astype(bf16), vp, nt, preferred_element_type=f32)
                      + jnp.dot(pn.astype(bf16), vn, preferred_element_type=f32))
                res.append(pv / l)
            o = res[0] - lam * res[1]
            ssq = jnp.sum(jnp.where(own, o * o, 0.0), axis=-1, keepdims=True)
            outs.append(o * lax.rsqrt(ssq * (1.0 / DA_V_DIM) + EPS))
        o = jnp.where((lane // DA_V_DIM) == 0, outs[0], outs[1])
        o_ref[:, sl] = (o * gain_ref[...]).astype(o_ref.dtype)


def _attn_sample(q, k_new, v_new, k_past, v_past, bias_past, bias_new, gain, lam, batch, nq, past):
    w = q.shape[1]
    h = bias_past.shape[0]
    return pl.pallas_call(
        _attn_sample_kernel,
        grid=(batch,),
        in_specs=[
            pl.BlockSpec(memory_space=pltpu.SMEM),
            pl.BlockSpec((nq, w), lambda b: (b, 0)),
            pl.BlockSpec((w, past), lambda b: (b, 0)),
            pl.BlockSpec((w, past), lambda b: (b, 0)),
            pl.BlockSpec((nq, w), lambda b: (b, 0)),
            pl.BlockSpec((nq, w), lambda b: (b, 0)),
            pl.BlockSpec((h, nq, past), lambda b: (0, 0, 0)),
            pl.BlockSpec((h, nq, nq), lambda b: (0, 0, 0)),
            pl.BlockSpec((1, LANES), lambda b: (0, 0)),
        ],
        out_specs=pl.BlockSpec((nq, w), lambda b: (b, 0)),
        out_shape=jax.ShapeDtypeStruct((batch * nq, w), bf16),
        compiler_params=_cparams("parallel"),
        name="attn_sample",
    )(lam, q, k_past, v_past, k_new, v_new, bias_past, bias_new, gain)


def _hgrn_chunk(q, f_logit, vv, g_logit, lb, gain, s0, lc):
    n_sub = lc // HG_SUB
    f = lb + (1.0 - lb) * jax.nn.sigmoid(f_logit)
    logf = jnp.log(f)
    k = 1.0 - f
    row = lax.broadcasted_iota(jnp.int32, (lc, lc), 0)
    col = lax.broadcasted_iota(jnp.int32, (lc, lc), 1)
    tril = (col <= row).astype(f32)
    b = jnp.dot(tril, logf, precision=lax.Precision.HIGHEST, preferred_element_type=f32)
    b_last = b[lc - 1:lc, :]

    o = jnp.dot((q * jnp.exp(b)).astype(bf16), s0.astype(bf16), preferred_element_type=f32)
    k_hat = k * jnp.exp(b_last - b)
    bt = b.T
    decay_col = jnp.exp(bt[:, lc - 1:lc])
    s_new = decay_col * s0 + jnp.dot(k_hat.T.astype(bf16), vv.astype(bf16), preferred_element_type=f32)

    if n_sub > 1:
        rows = []
        nt = (((1,), (1,)), ((), ()))
        for i in range(n_sub):
            lo = i * HG_SUB
            if i == 0:
                rows.append(jnp.zeros((HG_SUB, lc), f32))
                continue
            beta = b[lo - 1:lo, :]
            q_t = q[lo:lo + HG_SUB, :] * jnp.exp(b[lo:lo + HG_SUB, :] - beta)
            k_t = k * jnp.exp(jnp.minimum(beta - b, 0.0))
            rows.append(lax.dot_general(q_t.astype(bf16), k_t.astype(bf16), nt, preferred_element_type=f32))
        a_off = jnp.concatenate(rows, axis=0)
        a_off = jnp.where(col < (row // HG_SUB) * HG_SUB, a_off, 0.0)
        o = o + jnp.dot(a_off.astype(bf16), vv.astype(bf16), preferred_element_type=f32)

    ones = jnp.ones((HG_DIM, LANES), bf16)
    srow = lax.broadcasted_iota(jnp.int32, (HG_SUB, HG_DIM), 0)
    orow = lax.broadcasted_iota(jnp.int32, (HG_SUB, HG_DIM), 0)
    blocks = []
    for i in range(n_sub):
        lo = i * HG_SUB
        b_i = b[lo:lo + HG_SUB, :]
        k_i = k[lo:lo + HG_SUB, :]
        v_i = vv[lo:lo + HG_SUB, :]
        q_i = q[lo:lo + HG_SUB, :]
        d_rows = []
        for t in range(HG_SUB):
            arg = jnp.where(srow <= t, b_i[t:t + 1, :] - b_i, -jnp.inf)
            d_rows.append(q_i[t:t + 1, :] * k_i * jnp.exp(arg))
        d3 = jnp.concatenate(d_rows, axis=0)
        a_rep = jnp.dot(d3.astype(bf16), ones, preferred_element_type=f32)
        o_i = jnp.zeros((HG_SUB, HG_DIM), f32)
        for t in range(HG_SUB):
            o_t = jnp.sum(a_rep[t * HG_SUB:(t + 1) * HG_SUB, :] * v_i, axis=0, keepdims=True)
            o_i = jnp.where(orow == t, o_t, o_i)
        blocks.append(o_i)
    o = o + (jnp.concatenate(blocks, axis=0) if n_sub > 1 else blocks[0])

    on = o * lax.rsqrt(jnp.mean(o * o, axis=-1, keepdims=True) + EPS) * gain
    gate = g_logit * jax.nn.sigmoid(g_logit)
    return on * gate, s_new


def _hgrn_kernel(hz_ref, lb_ref, gain_ref, s0_ref, o_ref, s_ref, st_sc, *, lc, heads):
    j = pl.program_id(1)
    w = heads * HG_DIM

    @pl.when(j == 0)
    def _():
        st_sc[...] = s0_ref[0]

    n_chunks = hz_ref.shape[0] // lc
    for c in range(n_chunks):
        rs = slice(c * lc, (c + 1) * lc)
        for h in range(heads):
            cs = slice(h * HG_DIM, (h + 1) * HG_DIM)
            o, s_new = _hgrn_chunk(
                hz_ref[rs, h * HG_DIM:(h + 1) * HG_DIM],
                hz_ref[rs, w + h * HG_DIM:w + (h + 1) * HG_DIM],
                hz_ref[rs, 2 * w + h * HG_DIM:2 * w + (h + 1) * HG_DIM],
                hz_ref[rs, 3 * w + h * HG_DIM:3 * w + (h + 1) * HG_DIM],
                lb_ref[:, cs], gain_ref[...], st_sc[h], lc)
            st_sc[h] = s_new
            o_ref[rs, cs] = o.astype(o_ref.dtype)

    @pl.when(j == pl.num_programs(1) - 1)
    def _():
        s_ref[0] = st_sc[...]


def _hgrn(hz, lb, gain, s0, batch, seq, lc, chunks_per_step):
    t = hz.shape[0]
    w = hz.shape[1] // 4
    heads = w // HG_DIM
    rows = lc * chunks_per_step
    steps = seq // rows
    return pl.pallas_call(
        functools.partial(_hgrn_kernel, lc=lc, heads=heads),
        grid=(batch, steps),
        in_specs=[
            pl.BlockSpec((rows, 4 * w), lambda b, j: (b * steps + j, 0)),
            pl.BlockSpec((1, w), lambda b, j: (0, 0)),
            pl.BlockSpec((1, HG_DIM), lambda b, j: (0, 0)),
            pl.BlockSpec((1, heads, HG_DIM, HG_DIM), lambda b, j: (b, 0, 0, 0)),
        ],
        out_specs=[
            pl.BlockSpec((rows, w), lambda b, j: (b * steps + j, 0)),
            pl.BlockSpec((1, heads, HG_DIM, HG_DIM), lambda b, j: (b, 0, 0, 0)),
        ],
        out_shape=[
            jax.ShapeDtypeStruct((t, w), bf16),
            jax.ShapeDtypeStruct((batch, heads, HG_DIM, HG_DIM), f32),
        ],
        scratch_shapes=[pltpu.VMEM((heads, HG_DIM, HG_DIM), f32)],
        compiler_params=_cparams("parallel", "arbitrary"),
        name="hgrn2",
    )(hz, lb, gain, s0)


def _top_rows(vals, k, payload=None):
    n_rows = vals.shape[0]
    row = lax.broadcasted_iota(jnp.int32, vals.shape, 0)
    out_v, out_i = [], []
    for _ in range(k):
        m = jnp.max(vals, axis=0, keepdims=True)
        first = jnp.min(jnp.where(vals == m, row, n_rows), axis=0, keepdims=True)
        hit = row == first
        out_v.append(m)
        if payload is None:
            out_i.append(first)
        else:
            out_i.append(jnp.max(jnp.where(hit, payload, -1), axis=0, keepdims=True))
        vals = jnp.where(hit, -jnp.inf, vals)
    return jnp.concatenate(out_v, axis=0), jnp.concatenate(out_i, axis=0)


def _retrieve_kernel(oa_ref, ob_ref, x_ref, wo_ref, g_ref, wqt_ref, sk_ref,
                     hp_ref, xn_ref, idx_ref, gate_ref, qt_sc, s1v_sc, s1i_sc, idx_sc, gate_sc):
    half = oa_ref.shape[1]
    y = (jnp.dot(oa_ref[...], wo_ref[0:half, :], preferred_element_type=f32)
         + jnp.dot(ob_ref[...], wo_ref[half:, :], preferred_element_type=f32))
    hp = x_ref[...] + y
    hp_ref[...] = hp
    xn = hp * lax.rsqrt(jnp.mean(hp * hp, axis=-1, keepdims=True) + EPS) * g_ref[...]
    xn_ref[...] = xn
    nt = (((1,), (1,)), ((), ()))
    qt_sc[...] = lax.dot_general(wqt_ref[...], xn.astype(bf16), nt, preferred_element_type=f32).astype(bf16)

    n_half = sk_ref.shape[0]

    def stage1(hc, carry):
        start = pl.multiple_of(hc * PEER_SUB_DIM, PEER_SUB_DIM)
        s = jnp.dot(sk_ref[hc], qt_sc[pl.ds(start, PEER_SUB_DIM), :], preferred_element_type=f32)
        v, i = _top_rows(s, PEER_TOPK)
        s1v_sc[hc] = v
        s1i_sc[hc] = i
        return carry

    lax.fori_loop(0, n_half, stage1, 0)

    tm = x_ref.shape[0]
    sub = lax.broadcasted_iota(jnp.int32, (8, tm), 0)

    def stage2(h, carry):
        v1, i1 = s1v_sc[2 * h], s1i_sc[2 * h]
        v2, i2 = s1v_sc[2 * h + 1], s1i_sc[2 * h + 1]
        cv, ci = [], []
        for i in range(PEER_TOPK):
            nj = PEER_TOPK // (i + 1)
            rows = PEER_TOPK if nj > 8 else 8
            a = v1[i:i + 1, :] + v2[0:rows, :]
            e = i1[i:i + 1, :] * PEER_KEYS + i2[0:rows, :]
            if nj < rows:
                a = jnp.where(sub < nj, a, -jnp.inf)
            cv.append(a)
            ci.append(e)
        top_s, eidx = _top_rows(jnp.concatenate(cv, axis=0), PEER_TOPK, jnp.concatenate(ci, axis=0))
        p = jnp.exp(top_s - top_s[0:1, :])
        g = p / jnp.sum(p, axis=0, keepdims=True)
        r0 = pl.multiple_of(h * PEER_TOPK, PEER_TOPK)
        gate_sc[pl.ds(r0, PEER_TOPK), :] = g
        idx_sc[pl.ds(r0, PEER_TOPK), :] = eidx
        return carry

    lax.fori_loop(0, n_half // 2, stage2, 0)
    idx_ref[...] = pltpu.bitcast(pltpu.bitcast(idx_sc[...], f32).T, jnp.int32)
    gate_ref[...] = gate_sc[...].T


def _retrieve(oa, ob, x, wo_bf, g2, wqt_bf, sk_bf, tm, x_block0=0, ob_block0=0):
    t, half = oa.shape
    d = x.shape[1]
    e = wqt_bf.shape[0]
    n_half = sk_bf.shape[0]
    n_sel = (n_half // 2) * PEER_TOPK
    return pl.pallas_call(
        _retrieve_kernel,
        grid=(t // tm,),
        in_specs=[
            pl.BlockSpec((tm, half), lambda i: (i, 0)),
            pl.BlockSpec((tm, half), lambda i: (ob_block0 + i, 0)),
            pl.BlockSpec((tm, d), lambda i: (x_block0 + i, 0)),
            pl.BlockSpec((2 * half, d), lambda i: (0, 0)),
            pl.BlockSpec((1, d), lambda i: (0, 0)),
            pl.BlockSpec((e, d), lambda i: (0, 0)),
            pl.BlockSpec((n_half, PEER_KEYS, PEER_SUB_DIM), lambda i: (0, 0, 0)),
        ],
        out_specs=[
            pl.BlockSpec((tm, d), lambda i: (i, 0)),
            pl.BlockSpec((tm, d), lambda i: (i, 0)),
            pl.BlockSpec((tm, n_sel), lambda i: (i, 0)),
            pl.BlockSpec((tm, n_sel), lambda i: (i, 0)),
        ],
        out_shape=[
            jax.ShapeDtypeStruct((t, d), f32),
            jax.ShapeDtypeStruct((t, d), f32),
            jax.ShapeDtypeStruct((t, n_sel), jnp.int32),
            jax.ShapeDtypeStruct((t, n_sel), f32),
        ],
        scratch_shapes=[
            pltpu.VMEM((e, tm), bf16),
            pltpu.VMEM((n_half, PEER_TOPK, tm), f32),
            pltpu.VMEM((n_half, PEER_TOPK, tm), jnp.int32),
            pltpu.VMEM((n_sel, tm), jnp.int32),
            pltpu.VMEM((n_sel, tm), f32),
        ],
        compiler_params=_cparams("parallel"),
        name="retrieve",
    )(oa, ob, x, wo_bf, g2, wqt_bf, sk_bf)


def _pack_table(tab):
    half = tab.shape[1] // 2
    lo = lax.bitcast_convert_type(tab[:, :half].astype(bf16), jnp.uint16).astype(jnp.uint32)
    hi = lax.bitcast_convert_type(tab[:, half:].astype(bf16), jnp.uint16).astype(jnp.uint32)
    return lax.bitcast_convert_type(lo | (hi << 16), jnp.int32)


def _sc_layout(n_tok, n_sel, words):
    info = plsc.get_sparse_core_info()
    n_workers = info.num_cores * info.num_subcores
    lanes = info.num_lanes
    block = 8 * lanes
    assert n_tok % (2 * n_workers) == 0 and words % block == 0 and (n_sel // 2) % lanes == 0
    return info, lanes, n_tok // n_workers, n_sel // 2, block


def _selected_dots(table, idx, x):
    n_tok, n_sel = idx.shape
    words = table.shape[1]
    info, lanes, tpw, half_rows, block = _sc_layout(n_tok, n_sel, words)
    idx2 = idx.reshape(2 * n_tok, half_rows)
    mesh = plsc.VectorSubcoreMesh(core_axis_name="c", subcore_axis_name="s")

    @functools.partial(
        pl.kernel, out_type=jax.ShapeDtypeStruct((n_tok, n_sel), f32), mesh=mesh,
        compiler_params=pltpu.CompilerParams(needs_layout_passes=False),
        scratch_types=[
            pltpu.VMEM((2 * tpw, half_rows), jnp.int32),
            pltpu.VMEM((half_rows, words), jnp.int32),
            pltpu.VMEM((half_rows, words), jnp.int32),
            pltpu.VMEM((2 * words,), f32),
            pltpu.VMEM((2 * words,), f32),
            pltpu.VMEM((n_sel // 8, 8 * lanes), f32),
            pltpu.VMEM((tpw, n_sel), f32),
            pltpu.SemaphoreType.DMA, pltpu.SemaphoreType.DMA, pltpu.SemaphoreType.DMA, pltpu.SemaphoreType.DMA,
        ])
    def dots(t_hbm, i_hbm, x_hbm, o_hbm, idx_v, rows_a, rows_b, x_a, x_b, part_v, act_v, sem_a, sem_b, sem_xa, sem_xb):
        wid = lax.axis_index("s") * info.num_cores + lax.axis_index("c")
        base = wid * tpw
        pltpu.sync_copy(i_hbm.at[pl.ds(base * 2, 2 * tpw)], idx_v)

        def accumulate(rows, x_v, half):
            def column_block(cb, first):
                xs = [x_v[pl.ds(p * words + cb * block + c * lanes, lanes)] for c in range(8) for p in range(2)]

                @plsc.parallel_loop(0, half_rows)
                def _(row):
                    packed = [rows[row, pl.ds(cb * block + c * lanes, lanes)] for c in range(8)]
                    unp = [plsc.unpack(plsc.bitcast(p, bf16), format=plsc.PackFormat.INTERLEAVED) for p in packed]
                    prods = [unp[c][p] * xs[2 * c + p] for c in range(8) for p in range(2)]
                    while len(prods) > 1:
                        prods = [prods[k] + prods[k + 1] for k in range(0, len(prods), 2)]
                    prow = half * half_rows + row
                    slot = part_v.at[prow // 8, pl.ds((prow % 8) * lanes, lanes)]
                    if first:
                        slot[...] = prods[0]
                    else:
                        plsc.addupdate(slot, prods[0])

            column_block(0, True)
            pl.loop(1, words // block)(functools.partial(column_block, first=False))

        def gather(half_index, rows, sem):
            return pltpu.make_async_copy(t_hbm.at[idx_v.at[half_index]], rows, sem)

        def xcopy(t, x_v, sem):
            return pltpu.make_async_copy(x_hbm.at[base + t], x_v, sem)

        def token(t, x_v, sem_x, x_next, sem_xn):
            gather(2 * t + 1, rows_b, sem_b).start()

            @pl.when(t + 1 < tpw)
            def _():
                xcopy(t + 1, x_next, sem_xn).start()

            xcopy(t, x_v, sem_x).wait()
            gather(2 * t, rows_a, sem_a).wait()
            accumulate(rows_a, x_v, 0)

            @pl.when(t + 1 < tpw)
            def _():
                gather(2 * t + 2, rows_a, sem_a).start()

            gather(2 * t + 1, rows_b, sem_b).wait()
            accumulate(rows_b, x_v, 1)
            it = lax.iota(jnp.int32, lanes)
            for g in range(n_sel // lanes):
                prow = g * (lanes // 8) + it // 8
                pcol = (it % 8) * lanes
                cols = [plsc.load_gather(part_v, [prow, pcol + l]) for l in range(lanes)]
                while len(cols) > 1:
                    cols = [cols[k] + cols[k + 1] for k in range(0, len(cols), 2)]
                act_v[t, pl.ds(g * lanes, lanes)] = cols[0]

        gather(0, rows_a, sem_a).start()
        xcopy(0, x_a, sem_xa).start()

        @pl.loop(0, tpw // 2)
        def _(i):
            token(2 * i, x_a, sem_xa, x_b, sem_xb)
            token(2 * i + 1, x_b, sem_xb, x_a, sem_xa)

        pltpu.sync_copy(act_v, o_hbm.at[pl.ds(base, tpw)])

    return dots(table, idx2, x)


def _gate_weights_kernel(act_ref, gate_ref, after_ref, w_ref):
    del after_ref
    a = act_ref[...]
    w_ref[...] = gate_ref[...] * (0.5 * a * (1.0 + lax.erf(a * (2.0 ** -0.5))))


def _gate_weights(act, gate, after, tm):
    t, n_sel = act.shape
    spec = pl.BlockSpec((tm, n_sel), lambda i: (i, 0))
    return pl.pallas_call(
        _gate_weights_kernel,
        grid=(t // tm,),
        in_specs=[spec, spec, pl.BlockSpec(memory_space=pl.ANY)],
        out_specs=spec,
        out_shape=jax.ShapeDtypeStruct((t, n_sel), f32),
        compiler_params=_cparams("parallel"),
        name="peer_weights",
    )(act, gate, after)


def _weighted_rows(table, idx, w):
    n_tok, n_sel = w.shape
    words = table.shape[1]
    info, lanes, tpw, half_rows, _ = _sc_layout(n_tok, n_sel, words)
    idx2 = idx.reshape(2 * n_tok, half_rows)
    mesh = plsc.VectorSubcoreMesh(core_axis_name="c", subcore_axis_name="s")

    @functools.partial(
        pl.kernel, out_type=jax.ShapeDtypeStruct((n_tok, 2 * words), f32), mesh=mesh,
        compiler_params=pltpu.CompilerParams(needs_layout_passes=False),
        scratch_types=[
            pltpu.VMEM((2 * tpw, half_rows), jnp.int32),
            pltpu.VMEM((tpw, n_sel), f32),
            pltpu.VMEM((half_rows, words), jnp.int32),
            pltpu.VMEM((half_rows, words), jnp.int32),
            pltpu.VMEM((2 * words,), f32),
            pltpu.VMEM((2 * words,), f32),
            pltpu.SemaphoreType.DMA, pltpu.SemaphoreType.DMA, pltpu.SemaphoreType.DMA, pltpu.SemaphoreType.DMA,
        ])
    def mix(t_hbm, i_hbm, w_hbm, o_hbm, idx_v, w_v, rows_a, rows_b, out_a, out_b, sem_a, sem_b, sem_oa, sem_ob):
        wid = lax.axis_index("s") * info.num_cores + lax.axis_index("c")
        base = wid * tpw
        pltpu.sync_copy(i_hbm.at[pl.ds(base * 2, 2 * tpw)], idx_v)
        pltpu.sync_copy(w_hbm.at[pl.ds(base, tpw)], w_v)

        def accumulate(rows, out_v, t, half, first):
            def row_group(g, store):
                wv = w_v[t, pl.ds(half * half_rows + g * lanes, lanes)]
                ws = [wv.at[jnp.full((lanes,), r, jnp.int32)].get(mode="promise_in_bounds") for r in range(lanes)]

                @plsc.parallel_loop(0, words // lanes)
                def _(c):
                    packed = [rows[g * lanes + r, pl.ds(c * lanes, lanes)] for r in range(lanes)]
                    unp = [plsc.unpack(plsc.bitcast(p, bf16), format=plsc.PackFormat.INTERLEAVED) for p in packed]
                    lo = [unp[r][0] * ws[r] for r in range(lanes)]
                    hi = [unp[r][1] * ws[r] for r in range(lanes)]
                    while len(lo) > 1:
                        lo = [lo[k] + lo[k + 1] for k in range(0, len(lo), 2)]
                        hi = [hi[k] + hi[k + 1] for k in range(0, len(hi), 2)]
                    if store:
                        out_v[pl.ds(c * lanes, lanes)] = lo[0]
                        out_v[pl.ds(words + c * lanes, lanes)] = hi[0]
                    else:
                        plsc.addupdate(out_v.at[pl.ds(c * lanes, lanes)], lo[0])
                        plsc.addupdate(out_v.at[pl.ds(words + c * lanes, lanes)], hi[0])

            if first:
                row_group(0, True)
            pl.loop(1 if first else 0, half_rows // lanes)(functools.partial(row_group, store=False))

        def gather(half_index, rows, sem):
            return pltpu.make_async_copy(t_hbm.at[idx_v.at[half_index]], rows, sem)

        def token(t, out_v, sem_o, out_in_flight):
            gather(2 * t + 1, rows_b, sem_b).start()
            gather(2 * t, rows_a, sem_a).wait()

            @pl.when(out_in_flight)
            def _():
                pltpu.make_async_copy(out_v, o_hbm.at[base + t], sem_o).wait()

            accumulate(rows_a, out_v, t, 0, True)

            @pl.when(t + 1 < tpw)
            def _():
                gather(2 * t + 2, rows_a, sem_a).start()

            gather(2 * t + 1, rows_b, sem_b).wait()
            accumulate(rows_b, out_v, t, 1, False)
            pltpu.make_async_copy(out_v, o_hbm.at[base + t], sem_o).start()

        gather(0, rows_a, sem_a).start()

        @pl.loop(0, tpw // 2)
        def _(i):
            token(2 * i, out_a, sem_oa, i > 0)
            token(2 * i + 1, out_b, sem_ob, i > 0)

        pltpu.make_async_copy(out_a, o_hbm.at[base], sem_oa).wait()
        pltpu.make_async_copy(out_b, o_hbm.at[base], sem_ob).wait()

    return mix(table, idx2, w)


def _final_kernel(hp_ref, peer_ref, gf_ref, yin_ref, y_ref, done_ref):
    del yin_ref
    done_ref[...] = jnp.zeros(done_ref.shape, f32)
    h = hp_ref[...] + peer_ref[...]
    y_ref[...] = h * lax.rsqrt(jnp.mean(h * h, axis=-1, keepdims=True) + EPS) * gf_ref[...]


def _final(hp, peer, gf, y_all, y_block0, tm):
    t, d = hp.shape
    return pl.pallas_call(
        _final_kernel,
        grid=(t // tm,),
        in_specs=[
            pl.BlockSpec((tm, d), lambda i: (i, 0)),
            pl.BlockSpec((tm, d), lambda i: (i, 0)),
            pl.BlockSpec((1, d), lambda i: (0, 0)),
            pl.BlockSpec(memory_space=pl.ANY),
        ],
        out_specs=[pl.BlockSpec((tm, d), lambda i: (y_block0 + i, 0)),
                   pl.BlockSpec((8, LANES), lambda i: (0, 0))],
        out_shape=[jax.ShapeDtypeStruct(y_all.shape, f32), jax.ShapeDtypeStruct((8, LANES), f32)],
        input_output_aliases={3: 0},
        compiler_params=_cparams("arbitrary"),
        name="final_norm",
    )(hp, peer, gf, y_all)


def _channel_mixer_front(oa, ob, x, x_block0, ob_block0, wo_bf, g2, wqt_bf, sk_bf, u_pk, tm):
    hp, xn, eidx, gate = _retrieve(oa, ob, x, wo_bf, g2, wqt_bf, sk_bf, tm, x_block0, ob_block0)
    return hp, eidx, gate, _selected_dots(u_pk, eidx, xn)


def _channel_mixer_mix(front, after, v_pk, tm):
    _, eidx, gate, act = front
    return _weighted_rows(v_pk, eidx, _gate_weights(act, gate, after, tm))


def kernel(x_prompt, x_sample, cache_k, cache_v, state_hgrn, norm1, w_in, da_lq1, da_lk1, da_lq2, da_lk2,
           da_out_norm, hg_lb_logits, hg_out_norm, w_out, rel_bias, norm2, peer_w_q, peer_sub_keys,
           peer_u, peer_v, final_norm):
    batch, seq, d = x_prompt.shape
    dbatch, dseq, _ = x_sample.shape
    past = cache_k.shape[2]
    da_heads = cache_k.shape[3]
    hg_heads = state_hgrn.shape[2]
    depth = w_in.shape[0]
    assert depth == 1 and seq % ATT_BLOCK == 0 and seq % CHUNK == 0 and dseq % HG_SUB == 0 and dseq <= CHUNK
    assert past % CHUNK == 0 and (past + dseq - 1) // CHUNK == past // CHUNK

    l = 0
    lam_init = 0.8 - 0.6 * math.exp(-0.3 * l)
    lam = (jnp.exp(jnp.sum(da_lq1[l].astype(f32) * da_lk1[l].astype(f32)))
           - jnp.exp(jnp.sum(da_lq2[l].astype(f32) * da_lk2[l].astype(f32))) + lam_init).reshape(1)
    lb = jnp.cumsum(jax.nn.softmax(hg_lb_logits.astype(f32), axis=0), axis=0)[l].reshape(1, -1)
    da_gain = (jnp.tile(da_out_norm[l].astype(f32), 2) * (1.0 - lam_init)).reshape(1, LANES)
    hg_gain = hg_out_norm[l].astype(f32).reshape(1, HG_DIM)
    g1 = norm1[l].astype(f32).reshape(1, d)
    g2 = norm2[l].astype(f32).reshape(1, d)
    gf = final_norm.astype(f32).reshape(1, d)
    w_in_bf = w_in[l].astype(bf16)
    wkv_t_bf = w_in[l][:, cache_k.shape[3] * cache_k.shape[4]:3 * cache_k.shape[3] * cache_k.shape[4]].T.astype(bf16)
    wo_bf = w_out[l].astype(bf16)
    wqt_bf = peer_w_q[l].T.astype(bf16)
    sk_bf = peer_sub_keys[l].reshape(-1, PEER_KEYS, PEER_SUB_DIM).astype(bf16)
    u_pk = _pack_table(peer_u[l])
    v_pk = _pack_table(peer_v[l])

    bias_tiles = _prompt_bias_tiles(rel_bias)
    q_pos = past + jnp.arange(dseq)
    bias_s = _bias_of(jnp.arange(past + dseq)[None, :] - q_pos[:, None], rel_bias) * LOG2E
    bias_past, bias_new = bias_s[:, :, :past], bias_s[:, :, past:]
    zero_state = jnp.zeros((1, hg_heads, HG_DIM, HG_DIM), f32)

    dh2 = 2 * DA_HEAD_DIM
    wq = da_heads * dh2
    tm = 512
    x_all = x_prompt.reshape(batch * seq, d)
    kt_all = jnp.zeros((batch * wq, seq), f32)
    vt_all = jnp.zeros((batch * wq, seq), f32)
    y_all = jnp.zeros((batch * seq, d), f32)
    nq = seq // ATT_BLOCK
    blocks = seq // tm
    cuts = {0: 4, 1: 2}
    segments = [(b, c * (nq // cuts.get(b, 1)), nq // cuts.get(b, 1))
                for b in range(batch) for c in range(cuts.get(b, 1))]
    assert all(nq % n == 0 and (nq // n * ATT_BLOCK) % (2 * tm) == 0 for n in cuts.values())
    s_p = []
    done = [jnp.zeros((8, LANES), f32)] * 3
    fronts, peers = [], []

    def finish(k, y_all):
        return _final(fronts[k][0][0], peers[k], gf, y_all, fronts[k][1], tm)

    for s, (b, q0, n_q) in enumerate(segments):
        if q0 == 0:
            q, kt, vt, hz, kt_all, vt_all = _inproj_prompt(x_all, g1, w_in_bf, wkv_t_bf, kt_all, vt_all, done[s],
                                                           b, seq, tm)
            ob, s_new = _hgrn(hz, lb, hg_gain, zero_state, 1, seq, CHUNK, 4)
            s_p.append(s_new)
        oa = _attn_prompt(q, kt, vt, bias_tiles, da_gain, lam, q0, n_q)
        if s >= 1:
            peers.append(_channel_mixer_mix(fronts[s - 1][0], oa, v_pk, tm))
        row0 = q0 * ATT_BLOCK // tm
        fronts.append((_channel_mixer_front(oa, ob, x_all, b * blocks + row0, row0, wo_bf, g2, wqt_bf, sk_bf, u_pk,
                                            tm), b * blocks + row0))
        if s >= 2:
            y_all, token = finish(s - 2, y_all)
            done.append(token)
    peers.append(_channel_mixer_mix(fronts[-1][0], fronts[-1][0][0], v_pk, tm))
    for k in (len(segments) - 2, len(segments) - 1):
        y_all, _ = finish(k, y_all)

    xs = x_sample.reshape(dbatch * dseq, d)
    q, k_s, v_s, hz = _inproj_sample(xs, g1, w_in_bf, wq)
    kp_t = jnp.transpose(cache_k[l], (0, 2, 3, 1)).reshape(dbatch * wq, past)
    vp_t = jnp.transpose(cache_v[l], (0, 2, 3, 1)).reshape(dbatch * wq, past)
    oa = _attn_sample(q, k_s, v_s, kp_t, vp_t, bias_past, bias_new, da_gain, lam, dbatch, dseq, past)
    ob, s_s = _hgrn(hz, lb, hg_gain, state_hgrn[l].astype(f32), dbatch, dseq, dseq, 1)
    front = _channel_mixer_front(oa, ob, xs, 0, 0, wo_bf, g2, wqt_bf, sk_bf, u_pk, dbatch * dseq)
    peer = _channel_mixer_mix(front, front[0], v_pk, dbatch * dseq)
    y_s, _ = _final(front[0], peer, gf, jnp.zeros((dbatch * dseq, d), f32), 0, dbatch * dseq)

    y_prompt = y_all.reshape(batch, seq, d)
    y_sample = y_s.reshape(dbatch, dseq, d)
    k_prompt = jnp.transpose(kt_all.reshape(batch, da_heads, dh2, seq), (0, 3, 1, 2))[None]
    v_prompt = jnp.transpose(vt_all.reshape(batch, da_heads, DA_V_DIM, seq), (0, 3, 1, 2))[None]
    state_prompt = jnp.concatenate(s_p, axis=0)[None].astype(state_hgrn.dtype)
    k_sample = k_s.reshape(1, dbatch, dseq, da_heads, dh2)
    v_sample = v_s.reshape(1, dbatch, dseq, da_heads, DA_V_DIM)
    state_sample = s_s[None].astype(state_hgrn.dtype)
    return (y_prompt, y_sample, k_prompt, v_prompt, state_prompt, k_sample, v_sample, state_sample)
```
